```python
import jax, jax.numpy as jnp
from jax import lax
import numpy as np

D_MODEL = 1024
BATCH = 16
SEQ = 2048
DEPTH = 1

CTX_LEN = 256
GRID_W = 64
D_CONV = 1024
CONV_W = 3
N_HEADS = 8
DK = 64
DV = 128
D_MLSTM = N_HEADS * DV
D_FF = 4 * D_MODEL
CHUNK = 128
N_DIR = 2
N_BRANCH = 2
EPS = 1e-6

STATE_SPLITS = (N_HEADS * DK, D_MLSTM, N_DIR * N_HEADS, N_DIR * N_HEADS)
REST_SPLITS = (N_HEADS * DK, D_MLSTM, D_CONV, D_CONV, D_CONV, N_BRANCH * D_MODEL)
COL_SPLITS = STATE_SPLITS + REST_SPLITS
D_IN = sum(COL_SPLITS)
N_STATE_COLS = sum(STATE_SPLITS)
COL_OFFSETS = tuple(np.cumsum(COL_SPLITS)[:-1].tolist())
STATE_OFFSETS = tuple(np.cumsum(STATE_SPLITS)[:-1].tolist())
F_GATE_OFFSET = N_HEADS * DK + D_MLSTM + N_DIR * N_HEADS

kernel_name = "hybrid_conv_mlstm_prefix_dit"


def rmsnorm(x, w):
    xf = x.astype(jnp.float32)
    y = xf * lax.rsqrt(jnp.mean(xf * xf, axis=-1, keepdims=True) + EPS)
    return y.astype(x.dtype) * w


def modulate(x, w, shift, scale):
    return rmsnorm(x, w) * (1 + scale) + shift


def adaln(cvec, w_mod, b_mod):
    return jnp.split(jax.nn.silu(cvec) @ w_mod + b_mod, 6, axis=-1)


def ffn(h, w1, w2):
    return jnp.square(jax.nn.relu(h @ w1)) @ w2


def conv3(u, w, axis):
    n = u.shape[axis]
    pad = [(0, 0)] * u.ndim
    pad[axis] = (1, 1)
    p = jnp.pad(u, pad)
    return (w[0] * lax.slice_in_dim(p, 0, n, axis=axis)
            + w[1] * lax.slice_in_dim(p, 1, n + 1, axis=axis)
            + w[2] * lax.slice_in_dim(p, 2, n + 2, axis=axis))


def heads(a, d):
    b, t, _ = a.shape
    return a.reshape(b, t, -1, d).transpose(0, 2, 1, 3)


def dir_shared(a):
    return jnp.stack([a, a[..., ::-1, :]])


def dir_gates(g):
    b, t, _ = g.shape
    g = g.reshape(b, t, N_DIR, N_HEADS).transpose(2, 0, 3, 1)
    return jnp.stack([g[0], g[1][..., ::-1]])


def mlstm_dir_inputs(k, v, ig, fg):
    kd = dir_shared(heads(k, DK)).astype(jnp.float32) * (DK ** -0.5)
    vd = dir_shared(heads(v, DV)).astype(jnp.float32)
    igd = dir_gates(ig).astype(jnp.float32)
    lfd = jax.nn.log_sigmoid(dir_gates(fg).astype(jnp.float32))
    return kd, vd, igd, lfd


def mlstm_chunked(q, k, v, ig, lf, C0, n0, m0):
    lead = q.shape[:-2]
    t = q.shape[-2]
    nc = t // CHUNK

    def feat_chunks(a):
        return jnp.moveaxis(a.reshape(*lead, nc, CHUNK, a.shape[-1]), -3, 0)

    def gate_chunks(a):
        return jnp.moveaxis(a.reshape(*lead, nc, CHUNK), -2, 0)

    causal_in_chunk = jnp.tril(jnp.ones((CHUNK, CHUNK), dtype=bool))

    def step(carry, inp):
        C, n, m = carry
        qc, kc, vc, ic, fc = inp
        b = jnp.cumsum(fc, axis=-1)
        log_d = jnp.where(causal_in_chunk,
                          b[..., :, None] - b[..., None, :] + ic[..., None, :], -jnp.inf)
        inter = b + m[..., None]
        m_t = jnp.maximum(jnp.max(log_d, axis=-1), inter)
        s = jnp.einsum('...tk,...sk->...ts', qc, kc) * jnp.exp(log_d - m_t[..., None])
        w_inter = jnp.exp(inter - m_t)
        num = (jnp.einsum('...ts,...sv->...tv', s, vc)
               + w_inter[..., None] * jnp.einsum('...vk,...tk->...tv', C, qc))
        den = jnp.sum(s, axis=-1) + w_inter * jnp.einsum('...k,...tk->...t', n, qc)
        h = num / jnp.maximum(jnp.abs(den), jnp.exp(-m_t))[..., None]
        b_last = b[..., -1]
        g = b_last[..., None] - b + ic
        m_new = jnp.maximum(b_last + m, jnp.max(g, axis=-1))
        wk = jnp.exp(g - m_new[..., None])
        decay = jnp.exp(b_last + m - m_new)
        C_new = decay[..., None, None] * C + jnp.einsum('...s,...sv,...sk->...vk', wk, vc, kc)
        n_new = decay[..., None] * n + jnp.einsum('...s,...sk->...k', wk, kc)
        return (C_new, n_new, m_new), h

    state, hs = lax.scan(step, (C0, n0, m0),
                         (feat_chunks(q), feat_chunks(k), feat_chunks(v), gate_chunks(ig), gate_chunks(lf)))
    hs = jnp.moveaxis(hs, 0, -3).reshape(*lead, t, v.shape[-1])
    return hs, state


def mlstm_state(k, v, ig, lf):
    b = jnp.cumsum(lf, axis=-1)
    b_last = b[..., -1]
    g = b_last[..., None] - b + ig
    m = jnp.maximum(b_last, jnp.max(g, axis=-1))
    w = jnp.exp(g - m[..., None])
    C = jnp.einsum('...s,...sv,...sk->...vk', w, v, k)
    n = jnp.einsum('...s,...sk->...k', w, k)
    return C, n, m


def context_state(hc, w_in, b_in):
    z = hc @ w_in[:, :N_STATE_COLS] + b_in[:N_STATE_COLS]
    k, v, ig, fg = jnp.split(z, STATE_OFFSETS, axis=-1)
    return mlstm_state(*mlstm_dir_inputs(k, v, ig, fg))


def token_mixers(h, w_in, b_in, conv_w, mlstm_norm_w, w_conv_out, w_mlstm_out, w_out, state0, grid):
    bsz, t, _ = h.shape
    k, v, ig, fg, q, o, xin, gate_c, gate_b, merge = jnp.split(h @ w_in + b_in, COL_OFFSETS, axis=-1)
    u = gate_c * xin
    if grid:
        rows = t // GRID_W
        a = conv3(u.reshape(bsz, rows, GRID_W, D_CONV), conv_w, axis=2).reshape(bsz, t, D_CONV)
    else:
        a = conv3(u, conv_w, axis=1)
    y_a = (gate_b * a) @ w_conv_out
    kd, vd, igd, lfd = mlstm_dir_inputs(k, v, ig, fg)
    qd = dir_shared(heads(q, DK)).astype(jnp.float32)
    hd, state = mlstm_chunked(qd, kd, vd, igd, lfd, *state0)
    hs = (hd[0] + hd[1][..., ::-1, :]).transpose(0, 2, 1, 3)
    hs = hs * lax.rsqrt(jnp.mean(hs * hs, axis=-1, keepdims=True) + EPS)
    hs = hs.reshape(bsz, t, D_MLSTM).astype(h.dtype) * mlstm_norm_w
    y_b = (hs * jax.nn.sigmoid(o)) @ w_mlstm_out
    g_a, g_b = jnp.split(jax.nn.sigmoid(merge), N_BRANCH, axis=-1)
    return (g_a * y_a + g_b * y_b) @ w_out, state


def setup_inputs(seed: int = 0) -> dict:
    key = jax.random.key(seed)
    ks = jax.random.split(key, 20)

    def nrm(k, shape, s):
        return jax.random.normal(k, shape, jnp.float32) * s

    b_in = nrm(ks[7], (DEPTH, D_IN), 0.02)
    b_in = b_in.at[:, F_GATE_OFFSET:F_GATE_OFFSET + N_DIR * N_HEADS].add(
        jnp.tile(jnp.linspace(3.0, 6.0, N_HEADS), N_DIR))
    return {
        "x": nrm(ks[0], (BATCH, SEQ, D_MODEL), 1.0),
        "c": nrm(ks[1], (BATCH, D_MODEL), 1.0),
        "ctx": nrm(ks[2], (BATCH, CTX_LEN, D_MODEL), 1.0),
        "c_ctx": nrm(ks[3], (D_MODEL,), 1.0),
        "w_mod": nrm(ks[4], (DEPTH, D_MODEL, 6 * D_MODEL), 0.5 * D_MODEL ** -0.5),
        "b_mod": nrm(ks[5], (DEPTH, 6 * D_MODEL), 0.02),
        "norm1_w": 1.0 + nrm(ks[6], (DEPTH, D_MODEL), 0.02),
        "w_in": nrm(ks[8], (DEPTH, D_MODEL, D_IN), D_MODEL ** -0.5),
        "b_in": b_in,
        "conv_w": nrm(ks[9], (DEPTH, CONV_W, D_CONV), 0.5),
        "mlstm_norm_w": 1.0 + nrm(ks[10], (DEPTH, D_MLSTM), 0.02),
        "w_conv_out": nrm(ks[11], (DEPTH, D_CONV, D_MODEL), D_CONV ** -0.5),
        "w_mlstm_out": nrm(ks[12], (DEPTH, D_MLSTM, D_MODEL), D_MLSTM ** -0.5),
        "w_out": nrm(ks[13], (DEPTH, D_MODEL, D_MODEL), D_MODEL ** -0.5),
        "norm2_w": 1.0 + nrm(ks[14], (DEPTH, D_MODEL), 0.02),
        "w_ff1": nrm(ks[15], (DEPTH, D_MODEL, D_FF), D_MODEL ** -0.5),
        "w_ff2": nrm(ks[16], (DEPTH, D_FF, D_MODEL), D_FF ** -0.5),
        "final_norm_w": 1.0 + nrm(ks[17], (D_MODEL,), 0.02),
    }


def reference(x, c, ctx, c_ctx, w_mod, b_mod, norm1_w, w_in, b_in, conv_w, mlstm_norm_w,
              w_conv_out, w_mlstm_out, w_out, norm2_w, w_ff1, w_ff2, final_norm_w):
    for l in range(DEPTH):
        sh1, sc1, g1, sh2, sc2, g2 = [m[:, None, :] for m in adaln(c, w_mod[l], b_mod[l])]
        csh1, csc1, cg1, csh2, csc2, cg2 = adaln(c_ctx, w_mod[l], b_mod[l])
        hc = modulate(ctx, norm1_w[l], csh1, csc1)
        if l + 1 < DEPTH:
            bsz = ctx.shape[0]
            zero_state = (jnp.zeros((N_DIR, bsz, N_HEADS, DV, DK), jnp.float32),
                          jnp.zeros((N_DIR, bsz, N_HEADS, DK), jnp.float32),
                          jnp.zeros((N_DIR, bsz, N_HEADS), jnp.float32))
            out_c, state = token_mixers(hc, w_in[l], b_in[l], conv_w[l], mlstm_norm_w[l], w_conv_out[l],
                                        w_mlstm_out[l], w_out[l], zero_state, grid=False)
            ctx_next = ctx + cg1 * out_c
            ctx_next = ctx_next + cg2 * ffn(modulate(ctx_next, norm2_w[l], csh2, csc2), w_ff1[l], w_ff2[l])
        else:
            state = context_state(hc, w_in[l], b_in[l])
            ctx_next = ctx
        h = modulate(x, norm1_w[l], sh1, sc1)
        out, _ = token_mixers(h, w_in[l], b_in[l], conv_w[l], mlstm_norm_w[l], w_conv_out[l],
                              w_mlstm_out[l], w_out[l], state, grid=True)
        x = x + g1 * out
        x = x + g2 * ffn(modulate(x, norm2_w[l], sh2, sc2), w_ff1[l], w_ff2[l])
        ctx = ctx_next
    return rmsnorm(x, final_norm_w)
```

```python
import functools

import jax
import jax.numpy as jnp
import numpy as np
from jax import lax
from jax.experimental import pallas as pl
from jax.experimental.pallas import tpu as pltpu

D_MODEL = 1024
CTX_LEN = 256
GRID_W = 64
D_CONV = 1024
N_HEADS = 8
DK = 64
DV = 128
D_MLSTM = N_HEADS * DV
D_QK = N_HEADS * DK
D_FF = 4 * D_MODEL
N_DIR = 2
N_GATE = 2 * N_DIR * N_HEADS
EPS = 1e-6

CHUNK = 128
HEADS_PER_STEP = 2
N_HEAD_GROUPS = N_HEADS // HEADS_PER_STEP
GATES_PER_GROUP = N_GATE // N_HEAD_GROUPS
TOKEN_TILE = 512
FF_SPLIT = 4

VMEM_LIMIT_BYTES = 56 * 1024 * 1024

_REF_K, _REF_V, _REF_IG, _REF_FG, _REF_REST = 0, 512, 1536, 1552, 1568
D_IN = 8224
_K0, _V0, _Q0, _O0, _XIN0, _GC0, _GB0, _MA0, _MB0, _G0 = (
    0, 512, 1536, 2048, 3072, 4096, 5120, 6144, 7168, 8192)

F32 = jnp.float32
BF16 = jnp.bfloat16


def _sigmoid(x):
    return 1.0 / (1.0 + jnp.exp(-x))


def _log_sigmoid(x):
    return jnp.minimum(x, 0.0) - jnp.log(1.0 + jnp.exp(-jnp.abs(x)))


def _split3(x):
    hi = x.astype(BF16)
    r1 = x - hi.astype(F32)
    mid = r1.astype(BF16)
    lo = (r1 - mid.astype(F32)).astype(BF16)
    return hi, mid, lo


def _cumsum_rows(tri, x):
    hi, mid, lo = _split3(x)
    dot = functools.partial(jnp.dot, preferred_element_type=F32)
    return dot(tri, hi) + dot(tri, mid) + dot(tri, lo)


def _lower_tri(n, dtype):
    r = lax.broadcasted_iota(jnp.int32, (n, n), 0)
    c = lax.broadcasted_iota(jnp.int32, (n, n), 1)
    return jnp.where(r >= c, 1.0, 0.0).astype(dtype)


def _modulated_norm(x, norm_w, shift, scale):
    y = x * lax.rsqrt(jnp.mean(x * x, axis=-1, keepdims=True) + EPS)
    return (y * norm_w) * (1.0 + scale) + shift


def _resident(shape):
    nd = len(shape)
    return pl.BlockSpec(shape, lambda *_: (0,) * nd, pipeline_mode=pl.Buffered(1))


def _adaln_body(c_ref, w_ref, b_ref, o_ref):
    c = c_ref[...]
    s = c * _sigmoid(c)
    o_ref[...] = jnp.dot(s, w_ref[...], preferred_element_type=F32,
                         precision=lax.Precision.HIGHEST) + b_ref[...]


def adaln_call(cvecs, w_mod, b_mod):
    n = cvecs.shape[0]
    n_out = w_mod.shape[1]
    tile = 1024
    return pl.pallas_call(
        _adaln_body,
        grid=(n_out // tile,),
        in_specs=[pl.BlockSpec((n, D_MODEL), lambda j: (0, 0)),
                  pl.BlockSpec((D_MODEL, tile), lambda j: (0, j)),
                  pl.BlockSpec((1, tile), lambda j: (0, j))],
        out_specs=pl.BlockSpec((n, tile), lambda j: (0, j)),
        out_shape=jax.ShapeDtypeStruct((n, n_out), F32),
        name="adaln",
    )(cvecs, w_mod, b_mod.reshape(1, n_out))


def _ctx_body(ctx_ref, sh_ref, sc_ref, nw_ref, w_ref, b_ref, ct_ref, n_ref, m_ref):
    x = ctx_ref[0]
    h = _modulated_norm(x, nw_ref[...], sh_ref[...], sc_ref[...]).astype(BF16)
    z = jnp.dot(h, w_ref[...], preferred_element_type=F32) + b_ref[...]
    k = z[:, 0:D_QK] * (DK ** -0.5)
    v = z[:, D_QK:D_QK + D_MLSTM]
    ig = z[:, D_QK + D_MLSTM:D_QK + D_MLSTM + N_DIR * N_HEADS]
    fg = z[:, D_QK + D_MLSTM + N_DIR * N_HEADS:]
    lf = _log_sigmoid(fg)
    b = _cumsum_rows(_lower_tri(CTX_LEN, BF16), lf)
    tot = b[CTX_LEN - 1:CTX_LEN, :]
    e = b - lf
    g = jnp.concatenate([tot[:, :N_HEADS] - b[:, :N_HEADS] + ig[:, :N_HEADS],
                         e[:, N_HEADS:] + ig[:, N_HEADS:]], axis=1)
    m = jnp.maximum(tot, jnp.max(g, axis=0, keepdims=True))
    w = jnp.exp(g - m)
    for d in range(N_DIR):
        for hh in range(N_HEADS):
            col = d * N_HEADS + hh
            wc = w[:, col:col + 1]
            kh = k[:, hh * DK:(hh + 1) * DK]
            vh = v[:, hh * DV:(hh + 1) * DV]
            wv = (wc * vh).astype(BF16)
            ct_ref[0, d, hh] = lax.dot_general(kh.astype(BF16), wv, (((0,), (0,)), ((), ())),
                                               preferred_element_type=F32)
            n_ref[0, d, hh] = jnp.sum(wc * kh, axis=0, keepdims=True)
            m_ref[0, d, hh] = m[:, col:col + 1]


def ctx_call(ctx, shift, scale, norm_w, w_state, b_state):
    bsz = ctx.shape[0]
    n_state = w_state.shape[1]
    return pl.pallas_call(
        _ctx_body,
        grid=(bsz,),
        in_specs=[pl.BlockSpec((1, CTX_LEN, D_MODEL), lambda i: (i, 0, 0)),
                  pl.BlockSpec((1, D_MODEL), lambda i: (0, 0)),
                  pl.BlockSpec((1, D_MODEL), lambda i: (0, 0)),
                  pl.BlockSpec((1, D_MODEL), lambda i: (0, 0)),
                  pl.BlockSpec((D_MODEL, n_state), lambda i: (0, 0)),
                  pl.BlockSpec((1, n_state), lambda i: (0, 0))],
        out_specs=[pl.BlockSpec((1, N_DIR, N_HEADS, DK, DV), lambda i: (i, 0, 0, 0, 0)),
                   pl.BlockSpec((1, N_DIR, N_HEADS, 1, DK), lambda i: (i, 0, 0, 0, 0)),
                   pl.BlockSpec((1, N_DIR, N_HEADS, 1, 1), lambda i: (i, 0, 0, 0, 0))],
        out_shape=[jax.ShapeDtypeStruct((bsz, N_DIR, N_HEADS, DK, DV), F32),
                   jax.ShapeDtypeStruct((bsz, N_DIR, N_HEADS, 1, DK), F32),
                   jax.ShapeDtypeStruct((bsz, N_DIR, N_HEADS, 1, 1), F32)],
        compiler_params=pltpu.CompilerParams(dimension_semantics=("arbitrary",),
                                             vmem_limit_bytes=VMEM_LIMIT_BYTES),
        name="ctx_state",
    )(ctx, shift, scale, norm_w, w_state, b_state)


def _inproj_body(x_ref, sh_ref, sc_ref, nw_ref, w_ref, b_ref, cw_ref, wco_ref,
                 k_ref, v_ref, q_ref, so_ref, g_ref, ya_ref, gb_ref):
    x = x_ref[0]
    hb = _modulated_norm(x, nw_ref[...], sh_ref[0], sc_ref[0]).astype(BF16)

    def seg(a, b):
        return jnp.dot(hb, w_ref[:, a:b], preferred_element_type=F32) + b_ref[:, a:b]

    k_ref[0] = (seg(_K0, _V0) * (DK ** -0.5)).astype(BF16)
    v_ref[0] = seg(_V0, _Q0).astype(BF16)
    q_ref[0] = seg(_Q0, _O0).astype(BF16)
    so_ref[0] = _sigmoid(seg(_O0, _XIN0)).astype(BF16)
    g_ref[0] = seg(_G0, _G0 + N_GATE)

    u = seg(_XIN0, _GC0) * seg(_GC0, _GB0)
    tile = u.shape[0]
    col = jnp.bitwise_and(lax.broadcasted_iota(jnp.int32, (tile, 1), 0), GRID_W - 1)
    u_prev = jnp.where(col != 0, pltpu.roll(u, 1, axis=0), 0.0)
    u_next = jnp.where(col != GRID_W - 1, pltpu.roll(u, tile - 1, axis=0), 0.0)
    a = cw_ref[0:1, :] * u_prev + cw_ref[1:2, :] * u + cw_ref[2:3, :] * u_next
    ya = jnp.dot((seg(_GB0, _MA0) * a).astype(BF16), wco_ref[...], preferred_element_type=F32)
    ya_ref[0] = (_sigmoid(seg(_MA0, _MB0)) * ya).astype(BF16)
    gb_ref[0] = _sigmoid(seg(_MB0, _G0)).astype(BF16)


def inproj_call(x, shift, scale, norm_w, w_main, b_main, conv_w, w_conv_out):
    bsz, t, _ = x.shape
    tile = TOKEN_TILE
    tok = lambda width: pl.BlockSpec((1, tile, width), lambda i, j: (i, j, 0))
    row = pl.BlockSpec((1, 1, D_MODEL), lambda i, j: (i, 0, 0))
    widths = (D_QK, D_MLSTM, D_QK, D_MLSTM, N_GATE, D_MODEL, D_MODEL)
    dtypes = (BF16, BF16, BF16, BF16, F32, BF16, BF16)
    return pl.pallas_call(
        _inproj_body,
        grid=(bsz, t // tile),
        in_specs=[tok(D_MODEL), row, row,
                  _resident((1, D_MODEL)), _resident(w_main.shape), _resident(b_main.shape),
                  _resident(conv_w.shape), _resident(w_conv_out.shape)],
        out_specs=[tok(w) for w in widths],
        out_shape=[jax.ShapeDtypeStruct((bsz, t, w), dt) for w, dt in zip(widths, dtypes)],
        compiler_params=pltpu.CompilerParams(dimension_semantics=("arbitrary", "arbitrary"),
                                             vmem_limit_bytes=VMEM_LIMIT_BYTES),
        name="inproj_conv",
    )(x, shift, scale, norm_w, w_main, b_main, conv_w, w_conv_out)


def _chunk_gates(g_ref, tri, rows):
    g = g_ref[0, 0, rows, :]
    nh = N_DIR * HEADS_PER_STEP
    ig = g[:, 0:nh]
    lf = _log_sigmoid(g[:, nh:2 * nh])
    b = _cumsum_rows(tri, lf)
    tot = b[CHUNK - 1:CHUNK, :]
    e = b - lf
    hp = HEADS_PER_STEP
    col_t = jnp.concatenate([b[:, :hp], -e[:, hp:]], axis=1)
    key_t = jnp.concatenate([ig[:, :hp] - b[:, :hp], ig[:, hp:] + e[:, hp:]], axis=1)
    return col_t, key_t, tot


def _state_update(ct_old, n_old, m_old, kh, vh, g_col, tot):
    m_new = jnp.maximum(tot + m_old, jnp.max(g_col, axis=0, keepdims=True))
    wk = jnp.exp(g_col - m_new)
    decay = jnp.exp(tot + m_old - m_new)
    wv = (wk * vh.astype(F32)).astype(BF16)
    ct_new = decay * ct_old + lax.dot_general(kh, wv, (((0,), (0,)), ((), ())),
                                              preferred_element_type=F32)
    n_new = decay * n_old + jnp.sum(wk * kh.astype(F32), axis=0, keepdims=True)
    return ct_new, n_new, m_new


def _mlstm_body(q_ref, k_ref, v_ref, so_ref, g_ref, ct0_ref, n0_ref, m0_ref, nw_ref, o_ref,
                ctb_ref, nb_ref, mb_ref, ctf_ref, nf_ref, mf_ref, *, n_chunks):
    hp = HEADS_PER_STEP
    tri = _lower_tri(CHUNK, BF16)
    r_i = lax.broadcasted_iota(jnp.int32, (CHUNK, CHUNK), 0)
    c_i = lax.broadcasted_iota(jnp.int32, (CHUNK, CHUNK), 1)
    causal = r_i >= c_i
    anticausal = r_i <= c_i
    eye = r_i == c_i
    lane = lax.broadcasted_iota(jnp.int32, (CHUNK, 2 * DK), 1)

    def to_row(colvec):
        return jnp.sum(jnp.where(eye, colvec, 0.0), axis=0, keepdims=True)

    for j in range(hp):
        ctf_ref[j] = ct0_ref[0, 1, j]
        nf_ref[j] = n0_ref[0, 1, j]
        mf_ref[j] = m0_ref[0, 1, j]

    def bwd_step(i, carry):
        c = n_chunks - 1 - i
        rows = pl.ds(pl.multiple_of(c * CHUNK, CHUNK), CHUNK)
        _, key_t, tot = _chunk_gates(g_ref, tri, rows)
        k2 = k_ref[0, rows, :]
        for j in range(hp):
            ct_old, n_old, m_old = ctf_ref[j], nf_ref[j], mf_ref[j]
            ctb_ref[c, j] = ct_old
            nb_ref[c, j] = n_old
            mb_ref[c, j] = m_old
            kh = k2[:, j * DK:(j + 1) * DK]
            vh = v_ref[0, rows, j * DV:(j + 1) * DV]
            ct_new, n_new, m_new = _state_update(ct_old, n_old, m_old, kh, vh,
                                                 key_t[:, hp + j:hp + j + 1], tot[:, hp + j:hp + j + 1])
            ctf_ref[j] = ct_new
            nf_ref[j] = n_new
            mf_ref[j] = m_new
        return carry

    lax.fori_loop(0, n_chunks, bwd_step, 0)

    for j in range(hp):
        ctf_ref[j] = ct0_ref[0, 0, j]
        nf_ref[j] = n0_ref[0, 0, j]
        mf_ref[j] = m0_ref[0, 0, j]

    def direction_terms(qk, qf, col, key_row, mask, inter, n_state):
        log_d = jnp.where(mask, col + key_row, -jnp.inf)
        m_t = jnp.maximum(jnp.max(log_d, axis=-1, keepdims=True), inter)
        s = qk * jnp.exp(log_d - m_t)
        w_inter = jnp.exp(inter - m_t)
        den = jnp.sum(s, axis=-1, keepdims=True) + w_inter * jnp.sum(qf * n_state, axis=-1, keepdims=True)
        r = 1.0 / jnp.maximum(jnp.abs(den), jnp.exp(-m_t))
        return s * r, w_inter * r

    def fwd_step(c, carry):
        rows = pl.ds(pl.multiple_of(c * CHUNK, CHUNK), CHUNK)
        col_t, key_t, tot = _chunk_gates(g_ref, tri, rows)
        q2 = q_ref[0, rows, :]
        k2 = k_ref[0, rows, :]
        q2f = q2.astype(F32)
        q2f_swapped = pltpu.roll(q2f, DK, axis=1)
        for j in range(hp):
            kh = k2[:, j * DK:(j + 1) * DK]
            qh = q2[:, j * DK:(j + 1) * DK]
            qhf = q2f[:, j * DK:(j + 1) * DK]
            vh = v_ref[0, rows, j * DV:(j + 1) * DV]
            ctf, nf, mf = ctf_ref[j], nf_ref[j], mf_ref[j]
            ctb, nb, mb = ctb_ref[c, j], nb_ref[c, j], mb_ref[c, j]
            qk = lax.dot_general(qh, kh, (((1,), (1,)), ((), ())), preferred_element_type=F32)
            col_f = col_t[:, j:j + 1]
            col_b = col_t[:, hp + j:hp + j + 1]
            p_f, a_f = direction_terms(qk, qhf, col_f, to_row(key_t[:, j:j + 1]), causal,
                                       col_f + mf, nf)
            inter_b = tot[:, hp + j:hp + j + 1] + col_b + mb
            p_b, a_b = direction_terms(qk, qhf, col_b, to_row(key_t[:, hp + j:hp + j + 1]), anticausal,
                                       inter_b, nb)
            if j == 0:
                qa = jnp.where(lane < DK, q2f * a_f, q2f_swapped * a_b)
            else:
                qa = jnp.where(lane < DK, q2f_swapped * a_f, q2f * a_b)
            lhs = jnp.concatenate([(p_f + p_b).astype(BF16), qa.astype(BF16)], axis=1)
            rhs = jnp.concatenate([vh, ctf.astype(BF16), ctb.astype(BF16)], axis=0)
            h = jnp.dot(lhs, rhs, preferred_element_type=F32)
            hn = h * lax.rsqrt(jnp.mean(h * h, axis=-1, keepdims=True) + EPS)
            hn = hn * nw_ref[:, j * DV:(j + 1) * DV]
            o_ref[0, rows, j * DV:(j + 1) * DV] = (
                hn * so_ref[0, rows, j * DV:(j + 1) * DV].astype(F32)).astype(BF16)
            g_col = tot[:, j:j + 1] + key_t[:, j:j + 1]
            ct_new, n_new, m_new = _state_update(ctf, nf, mf, kh, vh, g_col, tot[:, j:j + 1])
            ctf_ref[j] = ct_new
            nf_ref[j] = n_new
            mf_ref[j] = m_new
        return carry

    lax.fori_loop(0, n_chunks, fwd_step, 0)


def mlstm_call(q, k, v, so, gates, ct0, n0, m0, norm_w):
    bsz, t, _ = q.shape
    n_chunks = t // CHUNK
    hp = HEADS_PER_STEP
    seq = lambda width: pl.BlockSpec((1, t, width), lambda i, j: (i, 0, j))
    state = lambda *tail: pl.BlockSpec((1, N_DIR, hp) + tail, lambda i, j: (i, 0, j, 0, 0))
    return pl.pallas_call(
        functools.partial(_mlstm_body, n_chunks=n_chunks),
        grid=(bsz, N_HEAD_GROUPS),
        in_specs=[seq(hp * DK), seq(hp * DK), seq(hp * DV), seq(hp * DV),
                  pl.BlockSpec((1, 1, t, GATES_PER_GROUP), lambda i, j: (i, j, 0, 0)),
                  state(DK, DV), state(1, DK), state(1, 1),
                  pl.BlockSpec((1, hp * DV), lambda i, j: (0, j))],
        out_specs=seq(hp * DV),
        out_shape=jax.ShapeDtypeStruct((bsz, t, D_MLSTM), BF16),
        scratch_shapes=[pltpu.VMEM((n_chunks, hp, DK, DV), F32),
                        pltpu.VMEM((n_chunks, hp, 1, DK), F32),
                        pltpu.VMEM((n_chunks, hp, 1, 1), F32),
                        pltpu.VMEM((hp, DK, DV), F32),
                        pltpu.VMEM((hp, 1, DK), F32),
                        pltpu.VMEM((hp, 1, 1), F32)],
        compiler_params=pltpu.CompilerParams(dimension_semantics=("arbitrary", "arbitrary"),
                                             vmem_limit_bytes=VMEM_LIMIT_BYTES),
        name="mlstm",
    )(q, k, v, so, gates, ct0, n0, m0, norm_w)


def _out_body(x_ref, hs_ref, ya_ref, gb_ref, g1_ref, sh2_ref, sc2_ref, g2_ref, nw2_ref, fnw_ref,
              wmo_ref, wo_ref, w1_ref, w2_ref, o_ref):
    dot = functools.partial(jnp.dot, preferred_element_type=F32)
    yb = dot(hs_ref[0], wmo_ref[...])
    merged = ya_ref[0].astype(F32) + gb_ref[0].astype(F32) * yb
    x1 = x_ref[0] + g1_ref[0] * dot(merged.astype(BF16), wo_ref[...])
    hm = _modulated_norm(x1, nw2_ref[...], sh2_ref[0], sc2_ref[0]).astype(BF16)
    step = D_FF // FF_SPLIT
    ff = None
    for s in range(FF_SPLIT):
        a = jnp.maximum(dot(hm, w1_ref[:, s * step:(s + 1) * step]), 0.0)
        part = dot((a * a).astype(BF16), w2_ref[s * step:(s + 1) * step, :])
        ff = part if ff is None else ff + part
    x2 = x1 + g2_ref[0] * ff
    y = x2 * lax.rsqrt(jnp.mean(x2 * x2, axis=-1, keepdims=True) + EPS)
    o_ref[0] = y * fnw_ref[...]


def out_call(x, hs, ya, gb, g1, sh2, sc2, g2, norm2_w, final_norm_w, w_mlstm_out, w_out, w_ff1, w_ff2):
    bsz, t, _ = x.shape
    tile = TOKEN_TILE
    tok = pl.BlockSpec((1, tile, D_MODEL), lambda i, j: (i, j, 0))
    row = pl.BlockSpec((1, 1, D_MODEL), lambda i, j: (i, 0, 0))
    return pl.pallas_call(
        _out_body,
        grid=(bsz, t // tile),
        in_specs=[tok, tok, tok, tok, row, row, row, row,
                  _resident((1, D_MODEL)), _resident((1, D_MODEL)),
                  _resident(w_mlstm_out.shape), _resident(w_out.shape),
                  _resident(w_ff1.shape), _resident(w_ff2.shape)],
        out_specs=tok,
        out_shape=jax.ShapeDtypeStruct((bsz, t, D_MODEL), F32),
        compiler_params=pltpu.CompilerParams(dimension_semantics=("arbitrary", "arbitrary"),
                                             vmem_limit_bytes=VMEM_LIMIT_BYTES),
        name="merge_out_mlp",
    )(x, hs, ya, gb, g1, sh2, sc2, g2, norm2_w, final_norm_w, w_mlstm_out, w_out, w_ff1, w_ff2)


def _gate_columns():
    cols = []
    for grp in range(N_HEAD_GROUPS):
        for base in (_REF_IG, _REF_FG):
            for d in range(N_DIR):
                for j in range(HEADS_PER_STEP):
                    cols.append(base + d * N_HEADS + grp * HEADS_PER_STEP + j)
    return np.asarray(cols, dtype=np.int32)


_MAIN_COLUMNS = np.concatenate([np.arange(_REF_K, _REF_IG, dtype=np.int32),
                                np.arange(_REF_REST, D_IN, dtype=np.int32),
                                _gate_columns()])


def _layer(x, ctx, mod, mod_ctx, norm1_w, w_in, b_in, conv_w, mlstm_norm_w, w_conv_out, w_mlstm_out,
           w_out, norm2_w, w_ff1, w_ff2, final_norm_w):
    bsz, t, _ = x.shape
    sh1, sc1, g1, sh2, sc2, g2 = [m.reshape(bsz, 1, D_MODEL) for m in jnp.split(mod, 6, axis=-1)]
    csh1, csc1 = mod_ctx[:, :D_MODEL], mod_ctx[:, D_MODEL:2 * D_MODEL]
    nw1 = norm1_w.reshape(1, D_MODEL)

    ct0, n0, m0 = ctx_call(ctx, csh1, csc1, nw1, w_in[:, :_REF_REST].astype(BF16),
                           b_in[:_REF_REST].reshape(1, _REF_REST))

    w_main = jnp.take(w_in, _MAIN_COLUMNS, axis=1).astype(BF16)
    b_main = jnp.take(b_in, _MAIN_COLUMNS).reshape(1, D_IN)
    k, v, q, so, gates, ya, gb = inproj_call(x, sh1, sc1, nw1, w_main, b_main, conv_w,
                                             w_conv_out.astype(BF16))
    gates = gates.reshape(bsz, t, N_HEAD_GROUPS, GATES_PER_GROUP).transpose(0, 2, 1, 3)
    hs = mlstm_call(q, k, v, so, gates, ct0, n0, m0, mlstm_norm_w.reshape(1, D_MLSTM))
    return out_call(x, hs, ya, gb, g1, sh2, sc2, g2, norm2_w.reshape(1, D_MODEL),
                    final_norm_w.reshape(1, D_MODEL), w_mlstm_out.astype(BF16), w_out.astype(BF16),
                    w_ff1.astype(BF16), w_ff2.astype(BF16))


def kernel(x, c, ctx, c_ctx, w_mod, b_mod, norm1_w, w_in, b_in, conv_w, mlstm_norm_w, w_conv_out,
           w_mlstm_out, w_out, norm2_w, w_ff1, w_ff2, final_norm_w):
    depth = w_mod.shape[0]
    assert depth == 1, "the context stream is only advanced through its mLSTM state (single layer)"
    bsz = x.shape[0]
    cvecs = jnp.concatenate([c, c_ctx[None, :]], axis=0)
    mod_all = adaln_call(cvecs, w_mod[0], b_mod[0])
    return _layer(x, ctx, mod_all[:bsz], mod_all[bsz:], norm1_w[0], w_in[0], b_in[0], conv_w[0],
                  mlstm_norm_w[0], w_conv_out[0], w_mlstm_out[0], w_out[0], norm2_w[0], w_ff1[0],
                  w_ff2[0], final_norm_w)
```

```python
import functools

import jax
import jax.numpy as jnp
from jax import lax
from jax.experimental import pallas as pl
from jax.experimental.pallas import tpu as pltpu

D_MODEL = 1024
CTX_LEN = 256
GRID_W = 64
D_CONV = 1024
N_HEADS = 8
DK = 64
DV = 128
D_MLSTM = N_HEADS * DV
D_QK = N_HEADS * DK
D_FF = 4 * D_MODEL
N_DIR = 2
N_GATE = 2 * N_DIR * N_HEADS
EPS = 1e-6

CHUNK = 128
HEADS_PER_STEP = 2
N_HEAD_GROUPS = N_HEADS // HEADS_PER_STEP
GATES_PER_GROUP = N_GATE // N_HEAD_GROUPS
BF16_SUBLANES = 16
V_AUG = DV + BF16_SUBLANES
TOKEN_TILE = 512
FF_SPLIT = 4

VMEM_LIMIT_BYTES = 56 * 1024 * 1024

_REF_K, _REF_V, _REF_IG, _REF_FG, _REF_Q, _REF_O = 0, 512, 1536, 1552, 1568, 2080
D_IN = 8224
_K0, _O0, _XIN0, _GC0, _GB0, _MA0, _MB0, _G0 = 0, 512, 1536, 2560, 3584, 4608, 5632, 6656
D_MAIN = _G0 + N_GATE

F32 = jnp.float32
BF16 = jnp.bfloat16


def _sigmoid(x):
    return 1.0 / (1.0 + jnp.exp(-x))


def _log_sigmoid(x):
    return jnp.minimum(x, 0.0) - jnp.log(1.0 + jnp.exp(-jnp.abs(x)))


def _split3(x):
    hi = x.astype(BF16)
    r1 = x - hi.astype(F32)
    mid = r1.astype(BF16)
    lo = (r1 - mid.astype(F32)).astype(BF16)
    return hi, mid, lo


def _cumsum_rows(tri, x):
    hi, mid, lo = _split3(x)
    dot = functools.partial(jnp.dot, preferred_element_type=F32)
    return dot(tri, hi) + dot(tri, mid) + dot(tri, lo)


def _cumsum_lanes(x, tri_t):
    hi, mid, lo = _split3(x)
    dot = functools.partial(jnp.dot, preferred_element_type=F32)
    return dot(hi, tri_t) + dot(mid, tri_t) + dot(lo, tri_t)


def _tri(n, dtype, lower):
    r = lax.broadcasted_iota(jnp.int32, (n, n), 0)
    c = lax.broadcasted_iota(jnp.int32, (n, n), 1)
    return jnp.where((r >= c) if lower else (r <= c), 1.0, 0.0).astype(dtype)


def _modulated_norm(x, norm_w, shift, scale):
    y = x * lax.rsqrt(jnp.mean(x * x, axis=-1, keepdims=True) + EPS)
    return (y * norm_w) * (1.0 + scale) + shift


def _resident(shape):
    nd = len(shape)
    return pl.BlockSpec(shape, lambda *_: (0,) * nd, pipeline_mode=pl.Buffered(1))


def _adaln_body(c_ref, w_ref, b_ref, o_ref):
    c = c_ref[...]
    s = c * _sigmoid(c)
    o_ref[...] = jnp.dot(s, w_ref[...], preferred_element_type=F32,
                         precision=lax.Precision.HIGHEST) + b_ref[...]


def adaln_call(cvecs, w_mod, b_mod):
    n = cvecs.shape[0]
    n_out = w_mod.shape[1]
    tile = 1024
    return pl.pallas_call(
        _adaln_body,
        grid=(n_out // tile,),
        in_specs=[pl.BlockSpec((n, D_MODEL), lambda j: (0, 0)),
                  pl.BlockSpec((D_MODEL, tile), lambda j: (0, j)),
                  pl.BlockSpec((1, tile), lambda j: (0, j))],
        out_specs=pl.BlockSpec((n, tile), lambda j: (0, j)),
        out_shape=jax.ShapeDtypeStruct((n, n_out), F32),
        name="adaln",
    )(cvecs, w_mod, b_mod.reshape(1, n_out))


def _ctx_body(ctx_ref, sh_ref, sc_ref, nw_ref, w_ref, b_ref, c_ref, m_ref):
    x = ctx_ref[0]
    h = _modulated_norm(x, nw_ref[...], sh_ref[...], sc_ref[...]).astype(BF16)
    z = jnp.dot(h, w_ref[...], preferred_element_type=F32) + b_ref[...]
    k = z[:, _REF_K:_REF_V] * (DK ** -0.5)
    v = z[:, _REF_V:_REF_IG]
    ig = z[:, _REF_IG:_REF_FG]
    fg = z[:, _REF_FG:_REF_Q]
    lf = _log_sigmoid(fg)
    b = _cumsum_rows(_tri(CTX_LEN, BF16, lower=True), lf)
    tot = b[CTX_LEN - 1:CTX_LEN, :]
    e = b - lf
    g = jnp.concatenate([tot[:, :N_HEADS] - b[:, :N_HEADS] + ig[:, :N_HEADS],
                         e[:, N_HEADS:] + ig[:, N_HEADS:]], axis=1)
    m = jnp.maximum(tot, jnp.max(g, axis=0, keepdims=True))
    w = jnp.exp(g - m)
    pad = jnp.zeros((V_AUG - DV - 1, HEADS_PER_STEP * DK), F32)
    for d in range(N_DIR):
        for grp in range(N_HEAD_GROUPS):
            c_parts, n_parts = [], []
            for j in range(HEADS_PER_STEP):
                hh = grp * HEADS_PER_STEP + j
                col = d * N_HEADS + hh
                wc = w[:, col:col + 1]
                kh = k[:, hh * DK:(hh + 1) * DK]
                wv = (wc * v[:, hh * DV:(hh + 1) * DV]).astype(BF16)
                c_parts.append(lax.dot_general(wv, kh.astype(BF16), (((0,), (0,)), ((), ())),
                                               preferred_element_type=F32))
                n_parts.append(jnp.sum(wc * kh, axis=0, keepdims=True))
                m_ref[0, d, hh] = m[:, col:col + 1]
            c_ref[0, d, grp] = jnp.concatenate(
                [jnp.concatenate(c_parts, axis=1), jnp.concatenate(n_parts, axis=1), pad], axis=0)


def ctx_call(ctx, shift, scale, norm_w, w_state, b_state):
    bsz = ctx.shape[0]
    n_state = w_state.shape[1]
    return pl.pallas_call(
        _ctx_body,
        grid=(bsz,),
        in_specs=[pl.BlockSpec((1, CTX_LEN, D_MODEL), lambda i: (i, 0, 0)),
                  pl.BlockSpec((1, D_MODEL), lambda i: (0, 0)),
                  pl.BlockSpec((1, D_MODEL), lambda i: (0, 0)),
                  pl.BlockSpec((1, D_MODEL), lambda i: (0, 0)),
                  pl.BlockSpec((D_MODEL, n_state), lambda i: (0, 0)),
                  pl.BlockSpec((1, n_state), lambda i: (0, 0))],
        out_specs=[pl.BlockSpec((1, N_DIR, N_HEAD_GROUPS, V_AUG, HEADS_PER_STEP * DK),
                                lambda i: (i, 0, 0, 0, 0)),
                   pl.BlockSpec((1, N_DIR, N_HEADS, 1, 1), lambda i: (i, 0, 0, 0, 0))],
        out_shape=[jax.ShapeDtypeStruct((bsz, N_DIR, N_HEAD_GROUPS, V_AUG, HEADS_PER_STEP * DK), F32),
                   jax.ShapeDtypeStruct((bsz, N_DIR, N_HEADS, 1, 1), F32)],
        compiler_params=pltpu.CompilerParams(dimension_semantics=("arbitrary",),
                                             vmem_limit_bytes=VMEM_LIMIT_BYTES),
        name="ctx_state",
    )(ctx, shift, scale, norm_w, w_state, b_state)


def _inproj_body(x_ref, sh_ref, sc_ref, nw_ref, w_ref, b_ref, wqv_ref, bqv_ref, cw_ref, wco_ref,
                 k_ref, qt_ref, vt_ref, so_ref, g_ref, ya_ref, gb_ref):
    x = x_ref[0]
    hb = _modulated_norm(x, nw_ref[...], sh_ref[0], sc_ref[0]).astype(BF16)
    tile = hb.shape[0]

    def seg(a, b):
        return jnp.dot(hb, w_ref[:, a:b], preferred_element_type=F32) + b_ref[:, a:b]

    k_ref[0] = (seg(_K0, _O0) * (DK ** -0.5)).astype(BF16)
    so_ref[0] = _sigmoid(seg(_O0, _XIN0)).astype(BF16)
    g_ref[0] = seg(_G0, _G0 + N_GATE)

    zt = lax.dot_general(wqv_ref[...], hb, (((1,), (1,)), ((), ())),
                         preferred_element_type=F32) + bqv_ref[...]
    for i in range(tile // CHUNK):
        qt_ref[0, i] = zt[:D_QK, i * CHUNK:(i + 1) * CHUNK].astype(BF16)
        vt_ref[0, i] = zt[D_QK:, i * CHUNK:(i + 1) * CHUNK].astype(BF16)

    u = seg(_XIN0, _GC0) * seg(_GC0, _GB0)
    col = jnp.bitwise_and(lax.broadcasted_iota(jnp.int32, (tile, 1), 0), GRID_W - 1)
    u_prev = jnp.where(col != 0, pltpu.roll(u, 1, axis=0), 0.0)
    u_next = jnp.where(col != GRID_W - 1, pltpu.roll(u, tile - 1, axis=0), 0.0)
    a = cw_ref[0:1, :] * u_prev + cw_ref[1:2, :] * u + cw_ref[2:3, :] * u_next
    ya = jnp.dot((seg(_GB0, _MA0) * a).astype(BF16), wco_ref[...], preferred_element_type=F32)
    ya_ref[0] = (_sigmoid(seg(_MA0, _MB0)) * ya).astype(BF16)
    gb_ref[0] = _sigmoid(seg(_MB0, _G0)).astype(BF16)


def inproj_call(x, shift, scale, norm_w, w_main, b_main, w_qvt, b_qvt, conv_w, w_conv_out):
    bsz, t, _ = x.shape
    tile = TOKEN_TILE
    tok = lambda width: pl.BlockSpec((1, tile, width), lambda i, j: (i, j, 0))
    tok_t = lambda rows: pl.BlockSpec((1, tile // CHUNK, rows, CHUNK), lambda i, j: (i, j, 0, 0))
    row = pl.BlockSpec((1, 1, D_MODEL), lambda i, j: (i, 0, 0))
    seq = lambda width, dt: jax.ShapeDtypeStruct((bsz, t, width), dt)
    seq_t = lambda rows: jax.ShapeDtypeStruct((bsz, t // CHUNK, rows, CHUNK), BF16)
    return pl.pallas_call(
        _inproj_body,
        grid=(bsz, t // tile),
        in_specs=[tok(D_MODEL), row, row,
                  _resident((1, D_MODEL)), _resident(w_main.shape), _resident(b_main.shape),
                  _resident(w_qvt.shape), _resident(b_qvt.shape),
                  _resident(conv_w.shape), _resident(w_conv_out.shape)],
        out_specs=[tok(D_QK), tok_t(D_QK), tok_t(D_MLSTM), tok(D_MLSTM), tok(N_GATE),
                   tok(D_MODEL), tok(D_MODEL)],
        out_shape=[seq(D_QK, BF16), seq_t(D_QK), seq_t(D_MLSTM), seq(D_MLSTM, BF16), seq(N_GATE, F32),
                   seq(D_MODEL, BF16), seq(D_MODEL, BF16)],
        compiler_params=pltpu.CompilerParams(dimension_semantics=("arbitrary", "arbitrary"),
                                             vmem_limit_bytes=VMEM_LIMIT_BYTES),
        name="inproj_conv",
    )(x, shift, scale, norm_w, w_main, b_main, w_qvt, b_qvt, conv_w, w_conv_out)


def _chunk_gates(g8, tri_t):
    ns = N_DIR * HEADS_PER_STEP
    lf = _log_sigmoid(g8)
    b = _cumsum_lanes(lf, tri_t)[ns:]
    lf = lf[ns:]
    ig = g8[:ns]
    tot = b[:, CHUNK - 1:CHUNK]
    e = b - lf
    fwd = lax.broadcasted_iota(jnp.int32, (ns, 1), 0) < HEADS_PER_STEP
    col = jnp.where(fwd, b, -e)
    key = ig + jnp.where(fwd, -b, e)
    return col, key, tot


def _state_step(cs_ref, ms_ref, vts, k_bd, g_rows, tots):
    lane = lax.broadcasted_iota(jnp.int32, (1, HEADS_PER_STEP * DK), 1)
    lhs, decay_row = [], None
    for j in range(HEADS_PER_STEP):
        m_old = ms_ref[j]
        m_new = jnp.maximum(tots[j] + m_old, jnp.max(g_rows[j], axis=1, keepdims=True))
        wk = jnp.exp(g_rows[j] - m_new)
        decay = jnp.exp(tots[j] + m_old - m_new)
        ms_ref[j] = m_new
        lhs.append((vts[j].astype(F32) * wk).astype(BF16))
        decay_row = decay if decay_row is None else jnp.where(lane < j * DK, decay_row, decay)
    upd = jnp.dot(jnp.concatenate(lhs, axis=1), k_bd, preferred_element_type=F32)
    cs_ref[...] = decay_row * cs_ref[...] + upd


def _mlstm_body(qt_ref, k_ref, vt_ref, so_ref, g_ref, c0_ref, m0_ref, nw_ref, o_ref,
                cb_ref, mb_ref, cs_ref, ms_ref, *, n_chunks):
    hp = HEADS_PER_STEP
    tri_t = _tri(CHUNK, BF16, lower=False)
    s_i = lax.broadcasted_iota(jnp.int32, (CHUNK, CHUNK), 0)
    t_i = lax.broadcasted_iota(jnp.int32, (CHUNK, CHUNK), 1)
    masks = (s_i <= t_i, s_i >= t_i)
    ones_rows = jnp.where(lax.broadcasted_iota(jnp.int32, (V_AUG - DV, CHUNK), 0) == 0,
                          1.0, 0.0).astype(BF16)
    kq_lane = lax.broadcasted_iota(jnp.int32, (CHUNK, hp * DK), 1)
    kq_row = lax.broadcasted_iota(jnp.int32, (hp * DK, CHUNK), 0)

    def load_chunk(c):
        rows = pl.ds(pl.multiple_of(c * CHUNK, CHUNK), CHUNK)
        k2 = k_ref[0, rows, :]
        zero = jnp.zeros_like(k2)
        k_bd = jnp.concatenate([jnp.where(kq_lane < DK, k2, zero),
                                jnp.where(kq_lane >= DK, k2, zero)], axis=0)
        vts = [jnp.concatenate([vt_ref[0, c, j * DV:(j + 1) * DV, :], ones_rows], axis=0)
               for j in range(hp)]
        return rows, k2, k_bd, vts

    cs_ref[...] = c0_ref[0, 1, 0]
    for j in range(hp):
        ms_ref[j] = m0_ref[0, 1, j]

    def bwd_step(i, carry):
        c = n_chunks - 1 - i
        _, _, k_bd, vts = load_chunk(c)
        _, key, tot = _chunk_gates(g_ref[0, 0, c], tri_t)
        cb_ref[c] = cs_ref[...].astype(BF16)
        for j in range(hp):
            mb_ref[c, j] = ms_ref[j]
        _state_step(cs_ref, ms_ref, vts, k_bd,
                    [key[hp + j:hp + j + 1] for j in range(hp)],
                    [tot[hp + j:hp + j + 1] for j in range(hp)])
        return carry

    lax.fori_loop(0, n_chunks, bwd_step, 0)

    cs_ref[...] = c0_ref[0, 0, 0]
    for j in range(hp):
        ms_ref[j] = m0_ref[0, 0, j]

    def fwd_step(c, carry):
        rows, k2, k_bd, vts = load_chunk(c)
        col, key, tot = _chunk_gates(g_ref[0, 0, c], tri_t)
        qt2 = qt_ref[0, c]
        zero = jnp.zeros_like(qt2)
        q_bd = jnp.concatenate([jnp.where(kq_row < DK, qt2, zero),
                                jnp.where(kq_row >= DK, qt2, zero)], axis=1)
        lhs = jnp.concatenate([k2, cs_ref[...].astype(BF16), cb_ref[c]], axis=0)
        r_all = jnp.dot(lhs, q_bd, preferred_element_type=F32)
        for j in range(hp):
            cols = slice(j * CHUNK, (j + 1) * CHUNK)
            qk_t = r_all[:CHUNK, cols]
            cq = (r_all[CHUNK:CHUNK + V_AUG, cols], r_all[CHUNK + V_AUG:, cols])
            m_prev = (ms_ref[j], mb_ref[c, j])
            s_t, m_t, w_inter = [], [], []
            for d in range(N_DIR):
                sidx = d * hp + j
                col_row = col[sidx:sidx + 1]
                key_bc = jnp.transpose(jnp.broadcast_to(key[sidx:sidx + 1], (CHUNK, CHUNK)))
                inter = col_row + m_prev[d] if d == 0 else tot[sidx:sidx + 1] + col_row + m_prev[d]
                log_d = jnp.where(masks[d], key_bc + col_row, -jnp.inf)
                mt = jnp.maximum(jnp.max(log_d, axis=0, keepdims=True), inter)
                s_t.append((qk_t * jnp.exp(log_d - mt)).astype(BF16))
                m_t.append(mt)
                w_inter.append(jnp.exp(inter - mt))
            n_all = jnp.dot(vts[j], jnp.concatenate(s_t, axis=1), preferred_element_type=F32)
            h_t = None
            for d in range(N_DIR):
                num = n_all[:, d * CHUNK:(d + 1) * CHUNK]
                den = num[DV:DV + 1] + w_inter[d] * cq[d][DV:DV + 1]
                r = 1.0 / jnp.maximum(jnp.abs(den), jnp.exp(-m_t[d]))
                part = num[:DV] * r + cq[d][:DV] * (w_inter[d] * r)
                h_t = part if h_t is None else h_t + part
            hn_t = h_t * lax.rsqrt(jnp.mean(h_t * h_t, axis=0, keepdims=True) + EPS)
            hn = jnp.transpose(hn_t) * nw_ref[:, j * DV:(j + 1) * DV]
            o_ref[0, rows, j * DV:(j + 1) * DV] = (
                hn * so_ref[0, rows, j * DV:(j + 1) * DV].astype(F32)).astype(BF16)
        _state_step(cs_ref, ms_ref, vts, k_bd,
                    [tot[j:j + 1] + key[j:j + 1] for j in range(hp)],
                    [tot[j:j + 1] for j in range(hp)])
        return carry

    lax.fori_loop(0, n_chunks, fwd_step, 0)


def mlstm_call(qt, k, vt, so, gates, c0, m0, norm_w):
    bsz, t, _ = k.shape
    n_chunks = t // CHUNK
    hp = HEADS_PER_STEP
    seq = lambda width: pl.BlockSpec((1, t, width), lambda i, j: (i, 0, j))
    seq_t = lambda rows: pl.BlockSpec((1, n_chunks, rows, CHUNK), lambda i, j: (i, 0, j, 0))
    return pl.pallas_call(
        functools.partial(_mlstm_body, n_chunks=n_chunks),
        grid=(bsz, N_HEAD_GROUPS),
        in_specs=[seq_t(hp * DK), seq(hp * DK), seq_t(hp * DV), seq(hp * DV),
                  pl.BlockSpec((1, 1, n_chunks, GATES_PER_GROUP, CHUNK), lambda i, j: (i, j, 0, 0, 0)),
                  pl.BlockSpec((1, N_DIR, 1, V_AUG, hp * DK), lambda i, j: (i, 0, j, 0, 0)),
                  pl.BlockSpec((1, N_DIR, hp, 1, 1), lambda i, j: (i, 0, j, 0, 0)),
                  pl.BlockSpec((1, hp * DV), lambda i, j: (0, j))],
        out_specs=seq(hp * DV),
        out_shape=jax.ShapeDtypeStruct((bsz, t, D_MLSTM), BF16),
        scratch_shapes=[pltpu.VMEM((n_chunks, V_AUG, hp * DK), BF16),
                        pltpu.VMEM((n_chunks, hp, 1, 1), F32),
                        pltpu.VMEM((V_AUG, hp * DK), F32),
                        pltpu.VMEM((hp, 1, 1), F32)],
        compiler_params=pltpu.CompilerParams(dimension_semantics=("arbitrary", "arbitrary"),
                                             vmem_limit_bytes=VMEM_LIMIT_BYTES),
        name="mlstm",
    )(qt, k, vt, so, gates, c0, m0, norm_w)


def _out_body(x_ref, hs_ref, ya_ref, gb_ref, g1_ref, sh2_ref, sc2_ref, g2_ref, nw2_ref, fnw_ref,
              wmo_ref, wo_ref, w1_ref, w2_ref, o_ref):
    dot = functools.partial(jnp.dot, preferred_element_type=F32)
    yb = dot(hs_ref[0], wmo_ref[...])
    merged = ya_ref[0].astype(F32) + gb_ref[0].astype(F32) * yb
    x1 = x_ref[0] + g1_ref[0] * dot(merged.astype(BF16), wo_ref[...])
    hm = _modulated_norm(x1, nw2_ref[...], sh2_ref[0], sc2_ref[0]).astype(BF16)
    step = D_FF // FF_SPLIT
    ff = None
    for s in range(FF_SPLIT):
        a = jnp.maximum(dot(hm, w1_ref[:, s * step:(s + 1) * step]), 0.0)
        part = dot((a * a).astype(BF16), w2_ref[s * step:(s + 1) * step, :])
        ff = part if ff is None else ff + part
    x2 = x1 + g2_ref[0] * ff
    y = x2 * lax.rsqrt(jnp.mean(x2 * x2, axis=-1, keepdims=True) + EPS)
    o_ref[0] = y * fnw_ref[...]


def out_call(x, hs, ya, gb, g1, sh2, sc2, g2, norm2_w, final_norm_w, w_mlstm_out, w_out, w_ff1, w_ff2):
    bsz, t, _ = x.shape
    tile = TOKEN_TILE
    tok = pl.BlockSpec((1, tile, D_MODEL), lambda i, j: (i, j, 0))
    row = pl.BlockSpec((1, 1, D_MODEL), lambda i, j: (i, 0, 0))
    return pl.pallas_call(
        _out_body,
        grid=(bsz, t // tile),
        in_specs=[tok, tok, tok, tok, row, row, row, row,
                  _resident((1, D_MODEL)), _resident((1, D_MODEL)),
                  _resident(w_mlstm_out.shape), _resident(w_out.shape),
                  _resident(w_ff1.shape), _resident(w_ff2.shape)],
        out_specs=tok,
        out_shape=jax.ShapeDtypeStruct((bsz, t, D_MODEL), F32),
        compiler_params=pltpu.CompilerParams(dimension_semantics=("arbitrary", "arbitrary"),
                                             vmem_limit_bytes=VMEM_LIMIT_BYTES),
        name="merge_out_mlp",
    )(x, hs, ya, gb, g1, sh2, sc2, g2, norm2_w, final_norm_w, w_mlstm_out, w_out, w_ff1, w_ff2)


def _group_gates(g):
    lead = g.shape[:-1]
    g = g.reshape(*lead, 2, N_DIR, N_HEAD_GROUPS, HEADS_PER_STEP)
    g = jnp.moveaxis(g, -2, -4)
    return g.reshape(*lead, N_GATE)


def _layer(x, ctx, mod, mod_ctx, norm1_w, w_in, b_in, conv_w, mlstm_norm_w, w_conv_out, w_mlstm_out,
           w_out, norm2_w, w_ff1, w_ff2, final_norm_w):
    bsz, t, _ = x.shape
    sh1, sc1, g1, sh2, sc2, g2 = [m.reshape(bsz, 1, D_MODEL) for m in jnp.split(mod, 6, axis=-1)]
    csh1, csc1 = mod_ctx[:, :D_MODEL], mod_ctx[:, D_MODEL:2 * D_MODEL]
    nw1 = norm1_w.reshape(1, D_MODEL)

    c0, m0 = ctx_call(ctx, csh1, csc1, nw1, w_in[:, :_REF_Q].astype(BF16), b_in[:_REF_Q].reshape(1, _REF_Q))

    w_main = jnp.concatenate([w_in[:, _REF_K:_REF_V], w_in[:, _REF_O:],
                              _group_gates(w_in[:, _REF_IG:_REF_Q])], axis=1).astype(BF16)
    b_main = jnp.concatenate([b_in[_REF_K:_REF_V], b_in[_REF_O:],
                              _group_gates(b_in[_REF_IG:_REF_Q])]).reshape(1, D_MAIN)
    w_qvt = jnp.concatenate([w_in[:, _REF_Q:_REF_O], w_in[:, _REF_V:_REF_IG]], axis=1).T.astype(BF16)
    b_qvt = jnp.concatenate([b_in[_REF_Q:_REF_O], b_in[_REF_V:_REF_IG]]).reshape(D_QK + D_MLSTM, 1)
    k, qt, vt, so, gates, ya, gb = inproj_call(x, sh1, sc1, nw1, w_main, b_main, w_qvt, b_qvt, conv_w,
                                               w_conv_out.astype(BF16))
    gates = gates.reshape(bsz, t // CHUNK, CHUNK, N_HEAD_GROUPS, GATES_PER_GROUP).transpose(0, 3, 1, 4, 2)
    hs = mlstm_call(qt, k, vt, so, gates, c0, m0, mlstm_norm_w.reshape(1, D_MLSTM))
    return out_call(x, hs, ya, gb, g1, sh2, sc2, g2, norm2_w.reshape(1, D_MODEL),
                    final_norm_w.reshape(1, D_MODEL), w_mlstm_out.astype(BF16), w_out.astype(BF16),
                    w_ff1.astype(BF16), w_ff2.astype(BF16))


def kernel(x, c, ctx, c_ctx, w_mod, b_mod, norm1_w, w_in, b_in, conv_w, mlstm_norm_w, w_conv_out,
           w_mlstm_out, w_out, norm2_w, w_ff1, w_ff2, final_norm_w):
    depth = w_mod.shape[0]
    assert depth == 1, "the context stream is only advanced through its mLSTM state (single layer)"
    bsz = x.shape[0]
    cvecs = jnp.concatenate([c, c_ctx[None, :]], axis=0)
    mod_all = adaln_call(cvecs, w_mod[0], b_mod[0])
    return _layer(x, ctx, mod_all[:bsz], mod_all[bsz:], norm1_w[0], w_in[0], b_in[0], conv_w[0],
                  mlstm_norm_w[0], w_conv_out[0], w_mlstm_out[0], w_out[0], norm2_w[0], w_ff1[0],
                  w_ff2[0], final_norm_w)
```

```python
import functools

import jax
import jax.numpy as jnp
from jax import lax
from jax.experimental import pallas as pl
from jax.experimental.pallas import tpu as pltpu

D_MODEL = 1024
CTX_LEN = 256
GRID_W = 64
D_CONV = 1024
N_HEADS = 8
DK = 64
DV = 128
D_MLSTM = N_HEADS * DV
D_QK = N_HEADS * DK
D_FF = 4 * D_MODEL
N_DIR = 2
N_GATE = 2 * N_DIR * N_HEADS
EPS = 1e-6

CHUNK = 128
HEADS_PER_STEP = 2
N_HEAD_GROUPS = N_HEADS // HEADS_PER_STEP
GATES_PER_GROUP = N_GATE // N_HEAD_GROUPS
BF16_SUBLANES = 16
V_AUG = DV + BF16_SUBLANES
CHUNK_UNROLL = 2
TOKEN_TILE = 512
FF_SPLIT = 4

VMEM_LIMIT_BYTES = 56 * 1024 * 1024

_REF_K, _REF_V, _REF_IG, _REF_FG, _REF_Q, _REF_O = 0, 512, 1536, 1552, 1568, 2080
D_IN = 8224
_K0, _O0, _XIN0, _GC0, _GB0, _MA0, _MB0, _G0 = 0, 512, 1536, 2560, 3584, 4608, 5632, 6656
D_MAIN = _G0 + N_GATE

F32 = jnp.float32
BF16 = jnp.bfloat16


def _sigmoid(x):
    return 1.0 / (1.0 + jnp.exp(-x))


def _log_sigmoid(x):
    return jnp.minimum(x, 0.0) - jnp.log(1.0 + jnp.exp(-jnp.abs(x)))


def _split3(x):
    hi = x.astype(BF16)
    r1 = x - hi.astype(F32)
    mid = r1.astype(BF16)
    lo = (r1 - mid.astype(F32)).astype(BF16)
    return hi, mid, lo


def _cumsum_rows(tri, x):
    hi, mid, lo = _split3(x)
    dot = functools.partial(jnp.dot, preferred_element_type=F32)
    return dot(tri, hi) + dot(tri, mid) + dot(tri, lo)


def _cumsum_lanes(x, tri_t):
    hi, mid, lo = _split3(x)
    dot = functools.partial(jnp.dot, preferred_element_type=F32)
    return dot(hi, tri_t) + dot(mid, tri_t) + dot(lo, tri_t)


def _tri(n, dtype, lower):
    r = lax.broadcasted_iota(jnp.int32, (n, n), 0)
    c = lax.broadcasted_iota(jnp.int32, (n, n), 1)
    return jnp.where((r >= c) if lower else (r <= c), 1.0, 0.0).astype(dtype)


def _modulated_norm(x, norm_w, shift, scale):
    y = x * lax.rsqrt(jnp.mean(x * x, axis=-1, keepdims=True) + EPS)
    return (y * norm_w) * (1.0 + scale) + shift


def _resident(shape):
    nd = len(shape)
    return pl.BlockSpec(shape, lambda *_: (0,) * nd, pipeline_mode=pl.Buffered(1))


def _adaln_body(c_ref, w_ref, b_ref, o_ref):
    c = c_ref[...]
    s = c * _sigmoid(c)
    o_ref[...] = jnp.dot(s, w_ref[...], preferred_element_type=F32,
                         precision=lax.Precision.HIGHEST) + b_ref[...]


def adaln_call(cvecs, w_mod, b_mod):
    n = cvecs.shape[0]
    n_out = w_mod.shape[1]
    tile = 1024
    return pl.pallas_call(
        _adaln_body,
        grid=(n_out // tile,),
        in_specs=[pl.BlockSpec((n, D_MODEL), lambda j: (0, 0)),
                  pl.BlockSpec((D_MODEL, tile), lambda j: (0, j)),
                  pl.BlockSpec((1, tile), lambda j: (0, j))],
        out_specs=pl.BlockSpec((n, tile), lambda j: (0, j)),
        out_shape=jax.ShapeDtypeStruct((n, n_out), F32),
        name="adaln",
    )(cvecs, w_mod, b_mod.reshape(1, n_out))


def _ctx_body(ctx_ref, sh_ref, sc_ref, nw_ref, w_ref, b_ref, c_ref, m_ref):
    x = ctx_ref[0]
    h = _modulated_norm(x, nw_ref[...], sh_ref[...], sc_ref[...]).astype(BF16)
    z = jnp.dot(h, w_ref[...], preferred_element_type=F32) + b_ref[...]
    k = z[:, _REF_K:_REF_V] * (DK ** -0.5)
    v = z[:, _REF_V:_REF_IG]
    ig = z[:, _REF_IG:_REF_FG]
    fg = z[:, _REF_FG:_REF_Q]
    lf = _log_sigmoid(fg)
    b = _cumsum_rows(_tri(CTX_LEN, BF16, lower=True), lf)
    tot = b[CTX_LEN - 1:CTX_LEN, :]
    e = b - lf
    g = jnp.concatenate([tot[:, :N_HEADS] - b[:, :N_HEADS] + ig[:, :N_HEADS],
                         e[:, N_HEADS:] + ig[:, N_HEADS:]], axis=1)
    m = jnp.maximum(tot, jnp.max(g, axis=0, keepdims=True))
    w = jnp.exp(g - m)
    pad = jnp.zeros((V_AUG - DV - 1, HEADS_PER_STEP * DK), F32)
    for grp in range(N_HEAD_GROUPS):
        m_rows = [jnp.zeros((GATES_PER_GROUP - N_DIR * HEADS_PER_STEP, 1), F32)]
        for d in range(N_DIR):
            c_parts, n_parts = [], []
            for j in range(HEADS_PER_STEP):
                hh = grp * HEADS_PER_STEP + j
                col = d * N_HEADS + hh
                wc = w[:, col:col + 1]
                kh = k[:, hh * DK:(hh + 1) * DK]
                wv = (wc * v[:, hh * DV:(hh + 1) * DV]).astype(BF16)
                c_parts.append(lax.dot_general(wv, kh.astype(BF16), (((0,), (0,)), ((), ())),
                                               preferred_element_type=F32))
                n_parts.append(jnp.sum(wc * kh, axis=0, keepdims=True))
                m_rows.append(m[:, col:col + 1])
            c_ref[0, d, grp] = jnp.concatenate(
                [jnp.concatenate(c_parts, axis=1), jnp.concatenate(n_parts, axis=1), pad], axis=0)
        m_ref[0, grp] = jnp.concatenate(m_rows, axis=0)


def ctx_call(ctx, shift, scale, norm_w, w_state, b_state):
    bsz = ctx.shape[0]
    n_state = w_state.shape[1]
    return pl.pallas_call(
        _ctx_body,
        grid=(bsz,),
        in_specs=[pl.BlockSpec((1, CTX_LEN, D_MODEL), lambda i: (i, 0, 0)),
                  pl.BlockSpec((1, D_MODEL), lambda i: (0, 0)),
                  pl.BlockSpec((1, D_MODEL), lambda i: (0, 0)),
                  pl.BlockSpec((1, D_MODEL), lambda i: (0, 0)),
                  pl.BlockSpec((D_MODEL, n_state), lambda i: (0, 0)),
                  pl.BlockSpec((1, n_state), lambda i: (0, 0))],
        out_specs=[pl.BlockSpec((1, N_DIR, N_HEAD_GROUPS, V_AUG, HEADS_PER_STEP * DK),
                                lambda i: (i, 0, 0, 0, 0)),
                   pl.BlockSpec((1, N_HEAD_GROUPS, GATES_PER_GROUP, 1), lambda i: (i, 0, 0, 0))],
        out_shape=[jax.ShapeDtypeStruct((bsz, N_DIR, N_HEAD_GROUPS, V_AUG, HEADS_PER_STEP * DK), F32),
                   jax.ShapeDtypeStruct((bsz, N_HEAD_GROUPS, GATES_PER_GROUP, 1), F32)],
        compiler_params=pltpu.CompilerParams(dimension_semantics=("arbitrary",),
                                             vmem_limit_bytes=VMEM_LIMIT_BYTES),
        name="ctx_state",
    )(ctx, shift, scale, norm_w, w_state, b_state)


def _inproj_body(x_ref, sh_ref, sc_ref, nw_ref, w_ref, b_ref, wqv_ref, bqv_ref, cw_ref, wco_ref,
                 k_ref, qt_ref, vt_ref, so_ref, g_ref, ya_ref, gb_ref):
    x = x_ref[0]
    hb = _modulated_norm(x, nw_ref[...], sh_ref[0], sc_ref[0]).astype(BF16)
    tile = hb.shape[0]

    def seg(a, b):
        return jnp.dot(hb, w_ref[:, a:b], preferred_element_type=F32) + b_ref[:, a:b]

    k_ref[0] = (seg(_K0, _O0) * (DK ** -0.5)).astype(BF16)
    so_ref[0] = _sigmoid(seg(_O0, _XIN0)).astype(BF16)
    g_ref[0] = seg(_G0, _G0 + N_GATE)

    zt = lax.dot_general(wqv_ref[...], hb, (((1,), (1,)), ((), ())),
                         preferred_element_type=F32) + bqv_ref[...]
    for i in range(tile // CHUNK):
        qt_ref[0, i] = zt[:D_QK, i * CHUNK:(i + 1) * CHUNK].astype(BF16)
        vt_ref[0, i] = zt[D_QK:, i * CHUNK:(i + 1) * CHUNK].astype(BF16)

    u = seg(_XIN0, _GC0) * seg(_GC0, _GB0)
    col = jnp.bitwise_and(lax.broadcasted_iota(jnp.int32, (tile, 1), 0), GRID_W - 1)
    u_prev = jnp.where(col != 0, pltpu.roll(u, 1, axis=0), 0.0)
    u_next = jnp.where(col != GRID_W - 1, pltpu.roll(u, tile - 1, axis=0), 0.0)
    a = cw_ref[0:1, :] * u_prev + cw_ref[1:2, :] * u + cw_ref[2:3, :] * u_next
    ya = jnp.dot((seg(_GB0, _MA0) * a).astype(BF16), wco_ref[...], preferred_element_type=F32)
    ya_ref[0] = (_sigmoid(seg(_MA0, _MB0)) * ya).astype(BF16)
    gb_ref[0] = _sigmoid(seg(_MB0, _G0)).astype(BF16)


def inproj_call(x, shift, scale, norm_w, w_main, b_main, w_qvt, b_qvt, conv_w, w_conv_out):
    bsz, t, _ = x.shape
    tile = TOKEN_TILE
    tok = lambda width: pl.BlockSpec((1, tile, width), lambda i, j: (i, j, 0))
    tok_t = lambda rows: pl.BlockSpec((1, tile // CHUNK, rows, CHUNK), lambda i, j: (i, j, 0, 0))
    row = pl.BlockSpec((1, 1, D_MODEL), lambda i, j: (i, 0, 0))
    seq = lambda width, dt: jax.ShapeDtypeStruct((bsz, t, width), dt)
    seq_t = lambda rows: jax.ShapeDtypeStruct((bsz, t // CHUNK, rows, CHUNK), BF16)
    return pl.pallas_call(
        _inproj_body,
        grid=(bsz, t // tile),
        in_specs=[tok(D_MODEL), row, row,
                  _resident((1, D_MODEL)), _resident(w_main.shape), _resident(b_main.shape),
                  _resident(w_qvt.shape), _resident(b_qvt.shape),
                  _resident(conv_w.shape), _resident(w_conv_out.shape)],
        out_specs=[tok(D_QK), tok_t(D_QK), tok_t(D_MLSTM), tok(D_MLSTM), tok(N_GATE),
                   tok(D_MODEL), tok(D_MODEL)],
        out_shape=[seq(D_QK, BF16), seq_t(D_QK), seq_t(D_MLSTM), seq(D_MLSTM, BF16), seq(N_GATE, F32),
                   seq(D_MODEL, BF16), seq(D_MODEL, BF16)],
        compiler_params=pltpu.CompilerParams(dimension_semantics=("arbitrary", "arbitrary"),
                                             vmem_limit_bytes=VMEM_LIMIT_BYTES),
        name="inproj_conv",
    )(x, shift, scale, norm_w, w_main, b_main, w_qvt, b_qvt, conv_w, w_conv_out)


SERIES_ROW0 = GATES_PER_GROUP // 2
_COL, _KEY, _INTER, _WK, _DECAY = range(5)


def _gate_tables(g_ref, m0_ref, tab_ref, n_chunks):
    rows = n_chunks * GATES_PER_GROUP
    g = g_ref[0, 0].reshape(rows, CHUNK)
    lf = _log_sigmoid(g)
    b = _cumsum_lanes(lf, _tri(CHUNK, BF16, lower=False))
    ig = pltpu.roll(g, SERIES_ROW0, axis=0)
    r_in_tile = jnp.bitwise_and(lax.broadcasted_iota(jnp.int32, (rows, 1), 0), GATES_PER_GROUP - 1)
    fwd = r_in_tile < SERIES_ROW0 + HEADS_PER_STEP
    e = b - lf
    tot = b[:, CHUNK - 1:CHUNK]
    col = jnp.where(fwd, b, -e)
    key = ig + jnp.where(fwd, -b, e)
    g_end = jnp.where(fwd, tot + key, key)
    g_max = jnp.max(g_end, axis=1, keepdims=True)

    fwd8 = fwd[:GATES_PER_GROUP]
    tile = lambda a, c: a[c * GATES_PER_GROUP:(c + 1) * GATES_PER_GROUP]
    m = m0_ref[0, 0]
    m_in_steps, m_out_steps = [], []
    for i in range(n_chunks):
        back = n_chunks - 1 - i
        m_in_steps.append(m)
        m = jnp.maximum(jnp.where(fwd8, tile(tot, i), tile(tot, back)) + m,
                        jnp.where(fwd8, tile(g_max, i), tile(g_max, back)))
        m_out_steps.append(m)
    by_chunk = lambda steps: jnp.concatenate(
        [jnp.where(fwd8, steps[c], steps[n_chunks - 1 - c]) for c in range(n_chunks)], axis=0)
    m_in, m_out = by_chunk(m_in_steps), by_chunk(m_out_steps)

    decay = jnp.broadcast_to(jnp.exp(tot + m_in - m_out), (rows, CHUNK))
    lane = lax.broadcasted_iota(jnp.int32, (rows, CHUNK), 1)
    tables = {_COL: col, _KEY: key, _INTER: jnp.where(fwd, col, tot + col) + m_in,
              _WK: jnp.exp(g_end - m_out),
              _DECAY: jnp.where(lane < DK, decay, pltpu.roll(decay, rows - 1, axis=0))}
    for idx, a in tables.items():
        tab_ref[idx] = a.reshape(n_chunks, GATES_PER_GROUP, CHUNK)


def _mlstm_body(qt_ref, k_ref, vt_ref, so_ref, g_ref, c0_ref, m0_ref, nw_ref, o_ref,
                tab_ref, u_ref, s_ref, *, n_chunks):
    hp = HEADS_PER_STEP
    s_i = lax.broadcasted_iota(jnp.int32, (CHUNK, CHUNK), 0)
    t_i = lax.broadcasted_iota(jnp.int32, (CHUNK, CHUNK), 1)
    masks = (s_i <= t_i, s_i >= t_i)
    ones_rows = jnp.where(lax.broadcasted_iota(jnp.int32, (V_AUG - DV, CHUNK), 0) == 0,
                          1.0, 0.0).astype(BF16)
    kq_lane = lax.broadcasted_iota(jnp.int32, (CHUNK, hp * DK), 1)
    kq_row = lax.broadcasted_iota(jnp.int32, (hp * DK, CHUNK), 0)
    series = lambda tab8, d, j: tab8[SERIES_ROW0 + d * hp + j:SERIES_ROW0 + d * hp + j + 1]

    def chunk_rows(c):
        return pl.ds(pl.multiple_of(c * CHUNK, CHUNK), CHUNK)

    def values_t(c, j):
        return jnp.concatenate([vt_ref[0, c, j * DV:(j + 1) * DV, :], ones_rows], axis=0)

    _gate_tables(g_ref, m0_ref, tab_ref, n_chunks)

    def increment_step(c, carry):
        k2 = k_ref[0, chunk_rows(c), :]
        zero = jnp.zeros_like(k2)
        k_bd = jnp.concatenate([jnp.where(kq_lane < DK, k2, zero),
                                jnp.where(kq_lane >= DK, k2, zero)], axis=0)
        wk8 = tab_ref[_WK, c]
        vf = [values_t(c, j).astype(F32) for j in range(hp)]
        lhs = jnp.concatenate(
            [jnp.concatenate([(vf[j] * series(wk8, d, j)).astype(BF16) for j in range(hp)], axis=1)
             for d in range(N_DIR)], axis=0)
        u_ref[c] = jnp.dot(lhs, k_bd, preferred_element_type=F32)
        return carry

    lax.fori_loop(0, n_chunks, increment_step, 0, unroll=CHUNK_UNROLL)

    def scan_step(i, carry):
        cf, cb = carry
        back = n_chunks - 1 - i
        s_ref[i, :V_AUG] = cf.astype(BF16)
        s_ref[back, V_AUG:] = cb.astype(BF16)
        cf = tab_ref[_DECAY, i][SERIES_ROW0:SERIES_ROW0 + 1] * cf + u_ref[i, :V_AUG]
        cb = tab_ref[_DECAY, back][SERIES_ROW0 + hp:SERIES_ROW0 + hp + 1] * cb + u_ref[back, V_AUG:]
        return cf, cb

    lax.fori_loop(0, n_chunks, scan_step, (c0_ref[0, 0, 0], c0_ref[0, 1, 0]))

    def output_step(c, carry):
        rows = chunk_rows(c)
        qt2 = qt_ref[0, c]
        zero = jnp.zeros_like(qt2)
        q_bd = jnp.concatenate([jnp.where(kq_row < DK, qt2, zero),
                                jnp.where(kq_row >= DK, qt2, zero)], axis=1)
        lhs = jnp.concatenate([k_ref[0, rows, :], s_ref[c]], axis=0)
        r_all = jnp.dot(lhs, q_bd, preferred_element_type=F32)
        col8, key8, inter8 = tab_ref[_COL, c], tab_ref[_KEY, c], tab_ref[_INTER, c]
        for j in range(hp):
            cols = slice(j * CHUNK, (j + 1) * CHUNK)
            qk_t = r_all[:CHUNK, cols]
            s_t, m_t, w_inter = [], [], []
            for d in range(N_DIR):
                key_bc = jnp.transpose(jnp.broadcast_to(series(key8, d, j), (CHUNK, CHUNK)))
                inter = series(inter8, d, j)
                log_d = jnp.where(masks[d], key_bc + series(col8, d, j), -jnp.inf)
                mt = jnp.maximum(jnp.max(log_d, axis=0, keepdims=True), inter)
                s_t.append((qk_t * jnp.exp(log_d - mt)).astype(BF16))
                m_t.append(mt)
                w_inter.append(jnp.exp(inter - mt))
            n_all = jnp.dot(values_t(c, j), jnp.concatenate(s_t, axis=1), preferred_element_type=F32)
            h_t = None
            for d in range(N_DIR):
                num = n_all[:, d * CHUNK:(d + 1) * CHUNK]
                cq = r_all[CHUNK + d * V_AUG:CHUNK + (d + 1) * V_AUG, cols]
                den = num[DV:DV + 1] + w_inter[d] * cq[DV:DV + 1]
                r = 1.0 / jnp.maximum(jnp.abs(den), jnp.exp(-m_t[d]))
                part = num[:DV] * r + cq[:DV] * (w_inter[d] * r)
                h_t = part if h_t is None else h_t + part
            hn_t = h_t * lax.rsqrt(jnp.mean(h_t * h_t, axis=0, keepdims=True) + EPS)
            hn = jnp.transpose(hn_t) * nw_ref[:, j * DV:(j + 1) * DV]
            o_ref[0, rows, j * DV:(j + 1) * DV] = (
                hn * so_ref[0, rows, j * DV:(j + 1) * DV].astype(F32)).astype(BF16)
        return carry

    lax.fori_loop(0, n_chunks, output_step, 0, unroll=CHUNK_UNROLL)


def mlstm_call(qt, k, vt, so, gates, c0, m0, norm_w):
    bsz, t, _ = k.shape
    n_chunks = t // CHUNK
    hp = HEADS_PER_STEP
    seq = lambda width: pl.BlockSpec((1, t, width), lambda i, j: (i, 0, j))
    seq_t = lambda rows: pl.BlockSpec((1, n_chunks, rows, CHUNK), lambda i, j: (i, 0, j, 0))
    return pl.pallas_call(
        functools.partial(_mlstm_body, n_chunks=n_chunks),
        grid=(bsz, N_HEAD_GROUPS),
        in_specs=[seq_t(hp * DK), seq(hp * DK), seq_t(hp * DV), seq(hp * DV),
                  pl.BlockSpec((1, 1, n_chunks, GATES_PER_GROUP, CHUNK), lambda i, j: (i, j, 0, 0, 0)),
                  pl.BlockSpec((1, N_DIR, 1, V_AUG, hp * DK), lambda i, j: (i, 0, j, 0, 0)),
                  pl.BlockSpec((1, 1, GATES_PER_GROUP, 1), lambda i, j: (i, j, 0, 0)),
                  pl.BlockSpec((1, hp * DV), lambda i, j: (0, j))],
        out_specs=seq(hp * DV),
        out_shape=jax.ShapeDtypeStruct((bsz, t, D_MLSTM), BF16),
        scratch_shapes=[pltpu.VMEM((5, n_chunks, GATES_PER_GROUP, CHUNK), F32),
                        pltpu.VMEM((n_chunks, N_DIR * V_AUG, hp * DK), F32),
                        pltpu.VMEM((n_chunks, N_DIR * V_AUG, hp * DK), BF16)],
        compiler_params=pltpu.CompilerParams(dimension_semantics=("arbitrary", "arbitrary"),
                                             vmem_limit_bytes=VMEM_LIMIT_BYTES),
        name="mlstm",
    )(qt, k, vt, so, gates, c0, m0, norm_w)


def _out_body(x_ref, hs_ref, ya_ref, gb_ref, g1_ref, sh2_ref, sc2_ref, g2_ref, nw2_ref, fnw_ref,
              wmo_ref, wo_ref, w1_ref, w2_ref, o_ref):
    dot = functools.partial(jnp.dot, preferred_element_type=F32)
    yb = dot(hs_ref[0], wmo_ref[...])
    merged = ya_ref[0].astype(F32) + gb_ref[0].astype(F32) * yb
    x1 = x_ref[0] + g1_ref[0] * dot(merged.astype(BF16), wo_ref[...])
    hm = _modulated_norm(x1, nw2_ref[...], sh2_ref[0], sc2_ref[0]).astype(BF16)
    step = D_FF // FF_SPLIT
    ff = None
    for s in range(FF_SPLIT):
        a = jnp.maximum(dot(hm, w1_ref[:, s * step:(s + 1) * step]), 0.0)
        part = dot((a * a).astype(BF16), w2_ref[s * step:(s + 1) * step, :])
        ff = part if ff is None else ff + part
    x2 = x1 + g2_ref[0] * ff
    y = x2 * lax.rsqrt(jnp.mean(x2 * x2, axis=-1, keepdims=True) + EPS)
    o_ref[0] = y * fnw_ref[...]


def out_call(x, hs, ya, gb, g1, sh2, sc2, g2, norm2_w, final_norm_w, w_mlstm_out, w_out, w_ff1, w_ff2):
    bsz, t, _ = x.shape
    tile = TOKEN_TILE
    tok = pl.BlockSpec((1, tile, D_MODEL), lambda i, j: (i, j, 0))
    row = pl.BlockSpec((1, 1, D_MODEL), lambda i, j: (i, 0, 0))
    return pl.pallas_call(
        _out_body,
        grid=(bsz, t // tile),
        in_specs=[tok, tok, tok, tok, row, row, row, row,
                  _resident((1, D_MODEL)), _resident((1, D_MODEL)),
                  _resident(w_mlstm_out.shape), _resident(w_out.shape),
                  _resident(w_ff1.shape), _resident(w_ff2.shape)],
        out_specs=tok,
        out_shape=jax.ShapeDtypeStruct((bsz, t, D_MODEL), F32),
        compiler_params=pltpu.CompilerParams(dimension_semantics=("arbitrary", "arbitrary"),
                                             vmem_limit_bytes=VMEM_LIMIT_BYTES),
        name="merge_out_mlp",
    )(x, hs, ya, gb, g1, sh2, sc2, g2, norm2_w, final_norm_w, w_mlstm_out, w_out, w_ff1, w_ff2)


def _group_gates(g):
    lead = g.shape[:-1]
    g = g.reshape(*lead, 2, N_DIR, N_HEAD_GROUPS, HEADS_PER_STEP)
    g = jnp.moveaxis(g, -2, -4)
    return g.reshape(*lead, N_GATE)


def _layer(x, ctx, mod, mod_ctx, norm1_w, w_in, b_in, conv_w, mlstm_norm_w, w_conv_out, w_mlstm_out,
           w_out, norm2_w, w_ff1, w_ff2, final_norm_w):
    bsz, t, _ = x.shape
    sh1, sc1, g1, sh2, sc2, g2 = [m.reshape(bsz, 1, D_MODEL) for m in jnp.split(mod, 6, axis=-1)]
    csh1, csc1 = mod_ctx[:, :D_MODEL], mod_ctx[:, D_MODEL:2 * D_MODEL]
    nw1 = norm1_w.reshape(1, D_MODEL)

    c0, m0 = ctx_call(ctx, csh1, csc1, nw1, w_in[:, :_REF_Q].astype(BF16), b_in[:_REF_Q].reshape(1, _REF_Q))

    w_main = jnp.concatenate([w_in[:, _REF_K:_REF_V], w_in[:, _REF_O:],
                              _group_gates(w_in[:, _REF_IG:_REF_Q])], axis=1).astype(BF16)
    b_main = jnp.concatenate([b_in[_REF_K:_REF_V], b_in[_REF_O:],
                              _group_gates(b_in[_REF_IG:_REF_Q])]).reshape(1, D_MAIN)
    w_qvt = jnp.concatenate([w_in[:, _REF_Q:_REF_O], w_in[:, _REF_V:_REF_IG]], axis=1).T.astype(BF16)
    b_qvt = jnp.concatenate([b_in[_REF_Q:_REF_O], b_in[_REF_V:_REF_IG]]).reshape(D_QK + D_MLSTM, 1)
    k, qt, vt, so, gates, ya, gb = inproj_call(x, sh1, sc1, nw1, w_main, b_main, w_qvt, b_qvt, conv_w,
                                               w_conv_out.astype(BF16))
    gates = gates.reshape(bsz, t // CHUNK, CHUNK, N_HEAD_GROUPS, GATES_PER_GROUP).transpose(0, 3, 1, 4, 2)
    hs = mlstm_call(qt, k, vt, so, gates, c0, m0, mlstm_norm_w.reshape(1, D_MLSTM))
    return out_call(x, hs, ya, gb, g1, sh2, sc2, g2, norm2_w.reshape(1, D_MODEL),
                    final_norm_w.reshape(1, D_MODEL), w_mlstm_out.astype(BF16), w_out.astype(BF16),
                    w_ff1.astype(BF16), w_ff2.astype(BF16))


def kernel(x, c, ctx, c_ctx, w_mod, b_mod, norm1_w, w_in, b_in, conv_w, mlstm_norm_w, w_conv_out,
           w_mlstm_out, w_out, norm2_w, w_ff1, w_ff2, final_norm_w):
    depth = w_mod.shape[0]
    assert depth == 1, "the context stream is only advanced through its mLSTM state (single layer)"
    bsz = x.shape[0]
    cvecs = jnp.concatenate([c, c_ctx[None, :]], axis=0)
    mod_all = adaln_call(cvecs, w_mod[0], b_mod[0])
    return _layer(x, ctx, mod_all[:bsz], mod_all[bsz:], norm1_w[0], w_in[0], b_in[0], conv_w[0],
                  mlstm_norm_w[0], w_conv_out[0], w_mlstm_out[0], w_out[0], norm2_w[0], w_ff1[0],
                  w_ff2[0], final_norm_w)
```

```python
import functools

import jax
import jax.numpy as jnp
from jax import lax
from jax.experimental import pallas as pl
from jax.experimental.pallas import tpu as pltpu

D_MODEL = 1024
CTX_LEN = 256
GRID_W = 64
D_CONV = 1024
N_HEADS = 8
DK = 64
DV = 128
D_MLSTM = N_HEADS * DV
D_QK = N_HEADS * DK
D_FF = 4 * D_MODEL
N_DIR = 2
N_GATE = 2 * N_DIR * N_HEADS
EPS = 1e-6

CHUNK = 128
HEADS_PER_STEP = 2
N_HEAD_GROUPS = N_HEADS // HEADS_PER_STEP
GATES_PER_GROUP = N_GATE // N_HEAD_GROUPS
BF16_SUBLANES = 16
V_AUG = DV + BF16_SUBLANES
CHUNK_UNROLL = 4
TOKEN_TILE = 512
FF_SPLIT = 4

VMEM_LIMIT_BYTES = 56 * 1024 * 1024

_REF_K, _REF_V, _REF_IG, _REF_FG, _REF_Q, _REF_O = 0, 512, 1536, 1552, 1568, 2080
D_IN = 8224
_K0, _O0, _XIN0, _GC0, _GB0, _MA0, _MB0, _G0 = 0, 512, 1536, 2560, 3584, 4608, 5632, 6656
D_MAIN = _G0 + N_GATE

F32 = jnp.float32
BF16 = jnp.bfloat16


def _sigmoid(x):
    return 1.0 / (1.0 + jnp.exp(-x))


def _log_sigmoid(x):
    return jnp.minimum(x, 0.0) - jnp.log(1.0 + jnp.exp(-jnp.abs(x)))


def _split3(x):
    hi = x.astype(BF16)
    r1 = x - hi.astype(F32)
    mid = r1.astype(BF16)
    lo = (r1 - mid.astype(F32)).astype(BF16)
    return hi, mid, lo


def _cumsum_rows(tri, x):
    hi, mid, lo = _split3(x)
    dot = functools.partial(jnp.dot, preferred_element_type=F32)
    return dot(tri, hi) + dot(tri, mid) + dot(tri, lo)


def _cumsum_lanes(x, tri_t):
    hi, mid, lo = _split3(x)
    dot = functools.partial(jnp.dot, preferred_element_type=F32)
    return dot(hi, tri_t) + dot(mid, tri_t) + dot(lo, tri_t)


def _tri(n, dtype, lower):
    r = lax.broadcasted_iota(jnp.int32, (n, n), 0)
    c = lax.broadcasted_iota(jnp.int32, (n, n), 1)
    return jnp.where((r >= c) if lower else (r <= c), 1.0, 0.0).astype(dtype)


def _modulated_norm(x, norm_w, shift, scale):
    y = x * lax.rsqrt(jnp.mean(x * x, axis=-1, keepdims=True) + EPS)
    return (y * norm_w) * (1.0 + scale) + shift


def _resident(shape):
    nd = len(shape)
    return pl.BlockSpec(shape, lambda *_: (0,) * nd, pipeline_mode=pl.Buffered(1))


def _adaln_body(c_ref, w_ref, b_ref, o_ref):
    c = c_ref[...]
    s = c * _sigmoid(c)
    o_ref[...] = jnp.dot(s, w_ref[...], preferred_element_type=F32,
                         precision=lax.Precision.HIGHEST) + b_ref[...]


def adaln_call(cvecs, w_mod, b_mod):
    n = cvecs.shape[0]
    n_out = w_mod.shape[1]
    tile = 1024
    return pl.pallas_call(
        _adaln_body,
        grid=(n_out // tile,),
        in_specs=[pl.BlockSpec((n, D_MODEL), lambda j: (0, 0)),
                  pl.BlockSpec((D_MODEL, tile), lambda j: (0, j)),
                  pl.BlockSpec((1, tile), lambda j: (0, j))],
        out_specs=pl.BlockSpec((n, tile), lambda j: (0, j)),
        out_shape=jax.ShapeDtypeStruct((n, n_out), F32),
        name="adaln",
    )(cvecs, w_mod, b_mod.reshape(1, n_out))


def _ctx_body(ctx_ref, sh_ref, sc_ref, nw_ref, w_ref, b_ref, c_ref, m_ref):
    x = ctx_ref[0]
    h = _modulated_norm(x, nw_ref[...], sh_ref[...], sc_ref[...]).astype(BF16)
    z = jnp.dot(h, w_ref[...], preferred_element_type=F32) + b_ref[...]
    k = z[:, _REF_K:_REF_V] * (DK ** -0.5)
    v = z[:, _REF_V:_REF_IG]
    ig = z[:, _REF_IG:_REF_FG]
    fg = z[:, _REF_FG:_REF_Q]
    lf = _log_sigmoid(fg)
    b = _cumsum_rows(_tri(CTX_LEN, BF16, lower=True), lf)
    tot = b[CTX_LEN - 1:CTX_LEN, :]
    e = b - lf
    g = jnp.concatenate([tot[:, :N_HEADS] - b[:, :N_HEADS] + ig[:, :N_HEADS],
                         e[:, N_HEADS:] + ig[:, N_HEADS:]], axis=1)
    m = jnp.maximum(tot, jnp.max(g, axis=0, keepdims=True))
    w = jnp.exp(g - m)
    pad = jnp.zeros((V_AUG - DV - 1, HEADS_PER_STEP * DK), F32)
    for grp in range(N_HEAD_GROUPS):
        m_rows = [jnp.zeros((GATES_PER_GROUP - N_DIR * HEADS_PER_STEP, 1), F32)]
        for d in range(N_DIR):
            c_parts, n_parts = [], []
            for j in range(HEADS_PER_STEP):
                hh = grp * HEADS_PER_STEP + j
                col = d * N_HEADS + hh
                wc = w[:, col:col + 1]
                kh = k[:, hh * DK:(hh + 1) * DK]
                wv = (wc * v[:, hh * DV:(hh + 1) * DV]).astype(BF16)
                c_parts.append(lax.dot_general(wv, kh.astype(BF16), (((0,), (0,)), ((), ())),
                                               preferred_element_type=F32))
                n_parts.append(jnp.sum(wc * kh, axis=0, keepdims=True))
                m_rows.append(m[:, col:col + 1])
            c_ref[0, d, grp] = jnp.concatenate(
                [jnp.concatenate(c_parts, axis=1), jnp.concatenate(n_parts, axis=1), pad], axis=0)
        m_ref[0, grp] = jnp.concatenate(m_rows, axis=0)


def ctx_call(ctx, shift, scale, norm_w, w_state, b_state):
    bsz = ctx.shape[0]
    n_state = w_state.shape[1]
    return pl.pallas_call(
        _ctx_body,
        grid=(bsz,),
        in_specs=[pl.BlockSpec((1, CTX_LEN, D_MODEL), lambda i: (i, 0, 0)),
                  pl.BlockSpec((1, D_MODEL), lambda i: (0, 0)),
                  pl.BlockSpec((1, D_MODEL), lambda i: (0, 0)),
                  pl.BlockSpec((1, D_MODEL), lambda i: (0, 0)),
                  pl.BlockSpec((D_MODEL, n_state), lambda i: (0, 0)),
                  pl.BlockSpec((1, n_state), lambda i: (0, 0))],
        out_specs=[pl.BlockSpec((1, N_DIR, N_HEAD_GROUPS, V_AUG, HEADS_PER_STEP * DK),
                                lambda i: (i, 0, 0, 0, 0)),
                   pl.BlockSpec((1, N_HEAD_GROUPS, GATES_PER_GROUP, 1), lambda i: (i, 0, 0, 0))],
        out_shape=[jax.ShapeDtypeStruct((bsz, N_DIR, N_HEAD_GROUPS, V_AUG, HEADS_PER_STEP * DK), F32),
                   jax.ShapeDtypeStruct((bsz, N_HEAD_GROUPS, GATES_PER_GROUP, 1), F32)],
        compiler_params=pltpu.CompilerParams(dimension_semantics=("arbitrary",),
                                             vmem_limit_bytes=VMEM_LIMIT_BYTES),
        name="ctx_state",
    )(ctx, shift, scale, norm_w, w_state, b_state)


def _inproj_body(x_ref, sh_ref, sc_ref, nw_ref, w_ref, b_ref, wqv_ref, bqv_ref, cw_ref, wco_ref,
                 k_ref, qt_ref, vt_ref, so_ref, g_ref, ya_ref, gb_ref):
    x = x_ref[0]
    hb = _modulated_norm(x, nw_ref[...], sh_ref[0], sc_ref[0]).astype(BF16)
    tile = hb.shape[0]

    def seg(a, b):
        return jnp.dot(hb, w_ref[:, a:b], preferred_element_type=F32) + b_ref[:, a:b]

    k_ref[0] = (seg(_K0, _O0) * (DK ** -0.5)).astype(BF16)
    so_ref[0] = _sigmoid(seg(_O0, _XIN0)).astype(BF16)
    g_ref[0] = seg(_G0, _G0 + N_GATE)

    zt = lax.dot_general(wqv_ref[...], hb, (((1,), (1,)), ((), ())),
                         preferred_element_type=F32) + bqv_ref[...]
    for i in range(tile // CHUNK):
        qt_ref[0, i] = zt[:D_QK, i * CHUNK:(i + 1) * CHUNK].astype(BF16)
        vt_ref[0, i] = zt[D_QK:, i * CHUNK:(i + 1) * CHUNK].astype(BF16)

    u = seg(_XIN0, _GC0) * seg(_GC0, _GB0)
    col = jnp.bitwise_and(lax.broadcasted_iota(jnp.int32, (tile, 1), 0), GRID_W - 1)
    u_prev = jnp.where(col != 0, pltpu.roll(u, 1, axis=0), 0.0)
    u_next = jnp.where(col != GRID_W - 1, pltpu.roll(u, tile - 1, axis=0), 0.0)
    a = cw_ref[0:1, :] * u_prev + cw_ref[1:2, :] * u + cw_ref[2:3, :] * u_next
    ya = jnp.dot((seg(_GB0, _MA0) * a).astype(BF16), wco_ref[...], preferred_element_type=F32)
    ya_ref[0] = (_sigmoid(seg(_MA0, _MB0)) * ya).astype(BF16)
    gb_ref[0] = _sigmoid(seg(_MB0, _G0)).astype(BF16)


def inproj_call(x, shift, scale, norm_w, w_main, b_main, w_qvt, b_qvt, conv_w, w_conv_out):
    bsz, t, _ = x.shape
    tile = TOKEN_TILE
    tok = lambda width: pl.BlockSpec((1, tile, width), lambda i, j: (i, j, 0))
    tok_t = lambda rows: pl.BlockSpec((1, tile // CHUNK, rows, CHUNK), lambda i, j: (i, j, 0, 0))
    row = pl.BlockSpec((1, 1, D_MODEL), lambda i, j: (i, 0, 0))
    seq = lambda width, dt: jax.ShapeDtypeStruct((bsz, t, width), dt)
    seq_t = lambda rows: jax.ShapeDtypeStruct((bsz, t // CHUNK, rows, CHUNK), BF16)
    return pl.pallas_call(
        _inproj_body,
        grid=(bsz, t // tile),
        in_specs=[tok(D_MODEL), row, row,
                  _resident((1, D_MODEL)), _resident(w_main.shape), _resident(b_main.shape),
                  _resident(w_qvt.shape), _resident(b_qvt.shape),
                  _resident(conv_w.shape), _resident(w_conv_out.shape)],
        out_specs=[tok(D_QK), tok_t(D_QK), tok_t(D_MLSTM), tok(D_MLSTM), tok(N_GATE),
                   tok(D_MODEL), tok(D_MODEL)],
        out_shape=[seq(D_QK, BF16), seq_t(D_QK), seq_t(D_MLSTM), seq(D_MLSTM, BF16), seq(N_GATE, F32),
                   seq(D_MODEL, BF16), seq(D_MODEL, BF16)],
        compiler_params=pltpu.CompilerParams(dimension_semantics=("arbitrary", "arbitrary"),
                                             vmem_limit_bytes=VMEM_LIMIT_BYTES),
        name="inproj_conv",
    )(x, shift, scale, norm_w, w_main, b_main, w_qvt, b_qvt, conv_w, w_conv_out)


SERIES_ROW0 = GATES_PER_GROUP // 2
N_SERIES = N_DIR * HEADS_PER_STEP
N_SPLIT = 3
LOG2E = 1.4426950408889634
_WK, _DECAY, _WINTER, _EXPNEG, _KEYS = 0, 1, 2, 3, 4
_COLM = _KEYS + N_SPLIT
N_TABLES = _COLM + N_SPLIT * N_SERIES


def _running_max_lanes(x, reverse):
    n = x.shape[1]
    lane = lax.broadcasted_iota(jnp.int32, x.shape, 1)
    shift = 1
    while shift < n:
        if reverse:
            moved = jnp.where(lane < n - shift, pltpu.roll(x, n - shift, axis=1), -jnp.inf)
        else:
            moved = jnp.where(lane >= shift, pltpu.roll(x, shift, axis=1), -jnp.inf)
        x = jnp.maximum(x, moved)
        shift *= 2
    return x


def _gate_tables(g_ref, m0_ref, tab_ref, n_chunks):
    rows = n_chunks * GATES_PER_GROUP
    g = g_ref[0, 0].reshape(rows, CHUNK)
    lf = _log_sigmoid(g)
    b = _cumsum_lanes(lf, _tri(CHUNK, BF16, lower=False))
    ig = pltpu.roll(g, SERIES_ROW0, axis=0)
    r_in_tile = jnp.bitwise_and(lax.broadcasted_iota(jnp.int32, (rows, 1), 0), GATES_PER_GROUP - 1)
    fwd = r_in_tile < SERIES_ROW0 + HEADS_PER_STEP
    e = b - lf
    tot = b[:, CHUNK - 1:CHUNK]
    col = jnp.where(fwd, b, -e)
    key = ig + jnp.where(fwd, -b, e)
    g_end = jnp.where(fwd, tot + key, key)
    g_max = jnp.max(g_end, axis=1, keepdims=True)

    fwd8 = fwd[:GATES_PER_GROUP]
    tile = lambda a, c: a[c * GATES_PER_GROUP:(c + 1) * GATES_PER_GROUP]
    m = m0_ref[0, 0]
    m_in_steps, m_out_steps = [], []
    for i in range(n_chunks):
        back = n_chunks - 1 - i
        m_in_steps.append(m)
        m = jnp.maximum(jnp.where(fwd8, tile(tot, i), tile(tot, back)) + m,
                        jnp.where(fwd8, tile(g_max, i), tile(g_max, back)))
        m_out_steps.append(m)
    by_chunk = lambda steps: jnp.concatenate(
        [jnp.where(fwd8, steps[c], steps[n_chunks - 1 - c]) for c in range(n_chunks)], axis=0)
    m_in, m_out = by_chunk(m_in_steps), by_chunk(m_out_steps)

    decay = jnp.broadcast_to(jnp.exp(tot + m_in - m_out), (rows, CHUNK))
    lane = lax.broadcasted_iota(jnp.int32, (rows, CHUNK), 1)
    inter = jnp.where(fwd, col, tot + col) + m_in
    key_max = jnp.where(fwd, _running_max_lanes(key, reverse=False), _running_max_lanes(key, reverse=True))
    m_t = jnp.maximum(col + key_max, inter)
    tables = {_WK: jnp.exp(g_end - m_out),
              _DECAY: jnp.where(lane < DK, decay, pltpu.roll(decay, rows - 1, axis=0)),
              _WINTER: jnp.exp(inter - m_t), _EXPNEG: jnp.exp(-m_t)}
    for x, part in enumerate(_split3(key * LOG2E)):
        tables[_KEYS + x] = part.astype(F32)
    for x, part in enumerate(_split3((col - m_t) * LOG2E)):
        for sidx in range(N_SERIES):
            tables[_COLM + x * N_SERIES + sidx] = jnp.where(r_in_tile == SERIES_ROW0 + sidx,
                                                            part.astype(F32), 0.0)
    for idx, a in tables.items():
        tab_ref[idx] = a.reshape(n_chunks, GATES_PER_GROUP, CHUNK)


def _mlstm_body(qt_ref, k_ref, vt_ref, so_ref, g_ref, c0_ref, m0_ref, nw_ref, o_ref,
                tab_ref, u_ref, s_ref, st_ref, *, n_chunks):
    hp = HEADS_PER_STEP
    s_i = lax.broadcasted_iota(jnp.int32, (CHUNK, CHUNK), 0)
    t_i = lax.broadcasted_iota(jnp.int32, (CHUNK, CHUNK), 1)
    masks = (s_i <= t_i, s_i >= t_i)
    ones_rows = jnp.where(lax.broadcasted_iota(jnp.int32, (V_AUG - DV, CHUNK), 0) == 0,
                          1.0, 0.0).astype(BF16)
    kq_lane = lax.broadcasted_iota(jnp.int32, (CHUNK, hp * DK), 1)
    kq_row = lax.broadcasted_iota(jnp.int32, (hp * DK, CHUNK), 0)
    series = lambda tab8, d, j: tab8[SERIES_ROW0 + d * hp + j:SERIES_ROW0 + d * hp + j + 1]
    split_rows = N_SPLIT * GATES_PER_GROUP
    ones_split = jnp.ones((split_rows, CHUNK), F32)
    pick_r = jnp.bitwise_and(lax.broadcasted_iota(jnp.int32, (split_rows, N_DIR * CHUNK), 0),
                             GATES_PER_GROUP - 1)
    pick_d = jnp.where(lax.broadcasted_iota(jnp.int32, (split_rows, N_DIR * CHUNK), 1) >= CHUNK, 1, 0)
    pick_series = [jnp.where(pick_r == SERIES_ROW0 + pick_d * hp + j, 1.0, 0.0) for j in range(hp)]

    def chunk_rows(c):
        return pl.ds(pl.multiple_of(c * CHUNK, CHUNK), CHUNK)

    def values_t(c, j):
        return jnp.concatenate([vt_ref[0, c, j * DV:(j + 1) * DV, :], ones_rows], axis=0)

    _gate_tables(g_ref, m0_ref, tab_ref, n_chunks)

    def block_diag_q(qt2):
        zero = jnp.zeros_like(qt2)
        return jnp.concatenate([jnp.where(kq_row < DK, qt2, zero),
                                jnp.where(kq_row >= DK, qt2, zero)], axis=1)

    def chunk_step(c, carry):
        k2 = k_ref[0, chunk_rows(c), :]
        zero = jnp.zeros_like(k2)
        k_bd = jnp.concatenate([jnp.where(kq_lane < DK, k2, zero),
                                jnp.where(kq_lane >= DK, k2, zero)], axis=0)
        wk8 = tab_ref[_WK, c]
        vf = [values_t(c, j).astype(F32) for j in range(hp)]
        lhs = jnp.concatenate(
            [jnp.concatenate([(vf[j] * series(wk8, d, j)).astype(BF16) for j in range(hp)], axis=1)
             for d in range(N_DIR)], axis=0)
        u_ref[c] = jnp.dot(lhs, k_bd, preferred_element_type=F32)

        qk_t = jnp.dot(k2, block_diag_q(qt_ref[0, c]), preferred_element_type=F32)
        key_side = jnp.concatenate([tab_ref[_KEYS + x, c] for x in range(N_SPLIT)] + [ones_split],
                                   axis=0).astype(BF16)
        for j in range(hp):
            query_side = jnp.concatenate(
                [pick_series[j]] +
                [jnp.concatenate([tab_ref[_COLM + x * N_SERIES + d * hp + j, c] for d in range(N_DIR)], axis=1)
                 for x in range(N_SPLIT)], axis=0).astype(BF16)
            log_d = lax.dot_general(key_side, query_side, (((0,), (0,)), ((), ())),
                                    preferred_element_type=F32)
            for d in range(N_DIR):
                decay = jnp.exp2(jnp.where(masks[d], log_d[:, d * CHUNK:(d + 1) * CHUNK], -jnp.inf))
                st_ref[c, d, j] = (qk_t[:, j * CHUNK:(j + 1) * CHUNK] * decay).astype(BF16)
        return carry

    lax.fori_loop(0, n_chunks, chunk_step, 0, unroll=CHUNK_UNROLL)

    def scan_step(i, carry):
        cf, cb = carry
        back = n_chunks - 1 - i
        s_ref[i, :V_AUG] = cf.astype(BF16)
        s_ref[back, V_AUG:] = cb.astype(BF16)
        cf = tab_ref[_DECAY, i][SERIES_ROW0:SERIES_ROW0 + 1] * cf + u_ref[i, :V_AUG]
        cb = tab_ref[_DECAY, back][SERIES_ROW0 + hp:SERIES_ROW0 + hp + 1] * cb + u_ref[back, V_AUG:]
        return cf, cb

    lax.fori_loop(0, n_chunks, scan_step, (c0_ref[0, 0, 0], c0_ref[0, 1, 0]))

    def output_step(c, carry):
        rows = chunk_rows(c)
        qt2 = qt_ref[0, c].astype(F32)
        w_inter8, exp_neg8 = tab_ref[_WINTER, c], tab_ref[_EXPNEG, c]
        v_pair = jnp.concatenate([values_t(c, j) for j in range(hp)], axis=1)
        no_scores = jnp.zeros((CHUNK, CHUNK), BF16)
        h_t = [None] * hp
        for d in range(N_DIR):
            w_rows = jnp.where(kq_row < DK, series(w_inter8, d, 0), series(w_inter8, d, 1))
            rhs = jnp.concatenate(
                [jnp.concatenate([st_ref[c, d, 0], no_scores], axis=1),
                 jnp.concatenate([no_scores, st_ref[c, d, 1]], axis=1),
                 block_diag_q((qt2 * w_rows).astype(BF16))], axis=0)
            lhs = jnp.concatenate([v_pair, s_ref[c, d * V_AUG:(d + 1) * V_AUG]], axis=1)
            n_all = jnp.dot(lhs, rhs, preferred_element_type=F32)
            for j in range(hp):
                num = n_all[:, j * CHUNK:(j + 1) * CHUNK]
                r = 1.0 / jnp.maximum(jnp.abs(num[DV:DV + 1]), series(exp_neg8, d, j))
                part = num[:DV] * r
                h_t[j] = part if h_t[j] is None else h_t[j] + part
        for j in range(hp):
            hn_t = h_t[j] * lax.rsqrt(jnp.mean(h_t[j] * h_t[j], axis=0, keepdims=True) + EPS)
            hn = jnp.transpose(hn_t) * nw_ref[:, j * DV:(j + 1) * DV]
            o_ref[0, rows, j * DV:(j + 1) * DV] = (
                hn * so_ref[0, rows, j * DV:(j + 1) * DV].astype(F32)).astype(BF16)
        return carry

    lax.fori_loop(0, n_chunks, output_step, 0, unroll=CHUNK_UNROLL)


def mlstm_call(qt, k, vt, so, gates, c0, m0, norm_w):
    bsz, t, _ = k.shape
    n_chunks = t // CHUNK
    hp = HEADS_PER_STEP
    seq = lambda width: pl.BlockSpec((1, t, width), lambda i, j: (i, 0, j))
    seq_t = lambda rows: pl.BlockSpec((1, n_chunks, rows, CHUNK), lambda i, j: (i, 0, j, 0))
    return pl.pallas_call(
        functools.partial(_mlstm_body, n_chunks=n_chunks),
        grid=(bsz, N_HEAD_GROUPS),
        in_specs=[seq_t(hp * DK), seq(hp * DK), seq_t(hp * DV), seq(hp * DV),
                  pl.BlockSpec((1, 1, n_chunks, GATES_PER_GROUP, CHUNK), lambda i, j: (i, j, 0, 0, 0)),
                  pl.BlockSpec((1, N_DIR, 1, V_AUG, hp * DK), lambda i, j: (i, 0, j, 0, 0)),
                  pl.BlockSpec((1, 1, GATES_PER_GROUP, 1), lambda i, j: (i, j, 0, 0)),
                  pl.BlockSpec((1, hp * DV), lambda i, j: (0, j))],
        out_specs=seq(hp * DV),
        out_shape=jax.ShapeDtypeStruct((bsz, t, D_MLSTM), BF16),
        scratch_shapes=[pltpu.VMEM((N_TABLES, n_chunks, GATES_PER_GROUP, CHUNK), F32),
                        pltpu.VMEM((n_chunks, N_DIR * V_AUG, hp * DK), F32),
                        pltpu.VMEM((n_chunks, N_DIR * V_AUG, hp * DK), BF16),
                        pltpu.VMEM((n_chunks, N_DIR, hp, CHUNK, CHUNK), BF16)],
        compiler_params=pltpu.CompilerParams(dimension_semantics=("arbitrary", "arbitrary"),
                                             vmem_limit_bytes=VMEM_LIMIT_BYTES),
        name="mlstm",
    )(qt, k, vt, so, gates, c0, m0, norm_w)


def _out_body(x_ref, hs_ref, ya_ref, gb_ref, g1_ref, sh2_ref, sc2_ref, g2_ref, nw2_ref, fnw_ref,
              wmo_ref, wo_ref, w1_ref, w2_ref, o_ref):
    dot = functools.partial(jnp.dot, preferred_element_type=F32)
    yb = dot(hs_ref[0], wmo_ref[...])
    merged = ya_ref[0].astype(F32) + gb_ref[0].astype(F32) * yb
    x1 = x_ref[0] + g1_ref[0] * dot(merged.astype(BF16), wo_ref[...])
    hm = _modulated_norm(x1, nw2_ref[...], sh2_ref[0], sc2_ref[0]).astype(BF16)
    step = D_FF // FF_SPLIT
    ff = None
    for s in range(FF_SPLIT):
        a = jnp.maximum(dot(hm, w1_ref[:, s * step:(s + 1) * step]), 0.0)
        part = dot((a * a).astype(BF16), w2_ref[s * step:(s + 1) * step, :])
        ff = part if ff is None else ff + part
    x2 = x1 + g2_ref[0] * ff
    y = x2 * lax.rsqrt(jnp.mean(x2 * x2, axis=-1, keepdims=True) + EPS)
    o_ref[0] = y * fnw_ref[...]


def out_call(x, hs, ya, gb, g1, sh2, sc2, g2, norm2_w, final_norm_w, w_mlstm_out, w_out, w_ff1, w_ff2):
    bsz, t, _ = x.shape
    tile = TOKEN_TILE
    tok = pl.BlockSpec((1, tile, D_MODEL), lambda i, j: (i, j, 0))
    row = pl.BlockSpec((1, 1, D_MODEL), lambda i, j: (i, 0, 0))
    return pl.pallas_call(
        _out_body,
        grid=(bsz, t // tile),
        in_specs=[tok, tok, tok, tok, row, row, row, row,
                  _resident((1, D_MODEL)), _resident((1, D_MODEL)),
                  _resident(w_mlstm_out.shape), _resident(w_out.shape),
                  _resident(w_ff1.shape), _resident(w_ff2.shape)],
        out_specs=tok,
        out_shape=jax.ShapeDtypeStruct((bsz, t, D_MODEL), F32),
        compiler_params=pltpu.CompilerParams(dimension_semantics=("arbitrary", "arbitrary"),
                                             vmem_limit_bytes=VMEM_LIMIT_BYTES),
        name="merge_out_mlp",
    )(x, hs, ya, gb, g1, sh2, sc2, g2, norm2_w, final_norm_w, w_mlstm_out, w_out, w_ff1, w_ff2)


def _group_gates(g):
    lead = g.shape[:-1]
    g = g.reshape(*lead, 2, N_DIR, N_HEAD_GROUPS, HEADS_PER_STEP)
    g = jnp.moveaxis(g, -2, -4)
    return g.reshape(*lead, N_GATE)


def _layer(x, ctx, mod, mod_ctx, norm1_w, w_in, b_in, conv_w, mlstm_norm_w, w_conv_out, w_mlstm_out,
           w_out, norm2_w, w_ff1, w_ff2, final_norm_w):
    bsz, t, _ = x.shape
    sh1, sc1, g1, sh2, sc2, g2 = [m.reshape(bsz, 1, D_MODEL) for m in jnp.split(mod, 6, axis=-1)]
    csh1, csc1 = mod_ctx[:, :D_MODEL], mod_ctx[:, D_MODEL:2 * D_MODEL]
    nw1 = norm1_w.reshape(1, D_MODEL)

    c0, m0 = ctx_call(ctx, csh1, csc1, nw1, w_in[:, :_REF_Q].astype(BF16), b_in[:_REF_Q].reshape(1, _REF_Q))

    w_main = jnp.concatenate([w_in[:, _REF_K:_REF_V], w_in[:, _REF_O:],
                              _group_gates(w_in[:, _REF_IG:_REF_Q])], axis=1).astype(BF16)
    b_main = jnp.concatenate([b_in[_REF_K:_REF_V], b_in[_REF_O:],
                              _group_gates(b_in[_REF_IG:_REF_Q])]).reshape(1, D_MAIN)
    w_qvt = jnp.concatenate([w_in[:, _REF_Q:_REF_O], w_in[:, _REF_V:_REF_IG]], axis=1).T.astype(BF16)
    b_qvt = jnp.concatenate([b_in[_REF_Q:_REF_O], b_in[_REF_V:_REF_IG]]).reshape(D_QK + D_MLSTM, 1)
    k, qt, vt, so, gates, ya, gb = inproj_call(x, sh1, sc1, nw1, w_main, b_main, w_qvt, b_qvt, conv_w,
                                               w_conv_out.astype(BF16))
    gates = gates.reshape(bsz, t // CHUNK, CHUNK, N_HEAD_GROUPS, GATES_PER_GROUP).transpose(0, 3, 1, 4, 2)
    hs = mlstm_call(qt, k, vt, so, gates, c0, m0, mlstm_norm_w.reshape(1, D_MLSTM))
    return out_call(x, hs, ya, gb, g1, sh2, sc2, g2, norm2_w.reshape(1, D_MODEL),
                    final_norm_w.reshape(1, D_MODEL), w_mlstm_out.astype(BF16), w_out.astype(BF16),
                    w_ff1.astype(BF16), w_ff2.astype(BF16))


def kernel(x, c, ctx, c_ctx, w_mod, b_mod, norm1_w, w_in, b_in, conv_w, mlstm_norm_w, w_conv_out,
           w_mlstm_out, w_out, norm2_w, w_ff1, w_ff2, final_norm_w):
    depth = w_mod.shape[0]
    assert depth == 1, "the context stream is only advanced through its mLSTM state (single layer)"
    bsz = x.shape[0]
    cvecs = jnp.concatenate([c, c_ctx[None, :]], axis=0)
    mod_all = adaln_call(cvecs, w_mod[0], b_mod[0])
    return _layer(x, ctx, mod_all[:bsz], mod_all[bsz:], norm1_w[0], w_in[0], b_in[0], conv_w[0],
                  mlstm_norm_w[0], w_conv_out[0], w_mlstm_out[0], w_out[0], norm2_w[0], w_ff1[0],
                  w_ff2[0], final_norm_w)
```

```python
import functools

import jax
import jax.numpy as jnp
from jax import lax
from jax.experimental import pallas as pl
from jax.experimental.pallas import tpu as pltpu

D_MODEL = 1024
CTX_LEN = 256
GRID_W = 64
D_CONV = 1024
N_HEADS = 8
DK = 64
DV = 128
D_MLSTM = N_HEADS * DV
D_QK = N_HEADS * DK
D_FF = 4 * D_MODEL
N_DIR = 2
N_GATE = 2 * N_DIR * N_HEADS
EPS = 1e-6

CHUNK = 128
HEADS_PER_STEP = 2
N_HEAD_GROUPS = N_HEADS // HEADS_PER_STEP
GATES_PER_GROUP = N_GATE // N_HEAD_GROUPS
BF16_SUBLANES = 16
V_AUG = DV + BF16_SUBLANES
CHUNK_UNROLL = 4
TOKEN_TILE = 512
FF_SPLIT = 4

VMEM_LIMIT_BYTES = 56 * 1024 * 1024

_REF_K, _REF_V, _REF_IG, _REF_FG, _REF_Q, _REF_O = 0, 512, 1536, 1552, 1568, 2080
D_IN = 8224
_O0, _XIN0, _GC0, _GB0, _MA0, _MB0, _G0 = 0, 1024, 2048, 3072, 4096, 5120, 6144

F32 = jnp.float32
BF16 = jnp.bfloat16


def _sigmoid(x):
    return 1.0 / (1.0 + jnp.exp(-x))


def _log_sigmoid(x):
    return jnp.minimum(x, 0.0) - jnp.log(1.0 + jnp.exp(-jnp.abs(x)))


def _split3(x):
    hi = x.astype(BF16)
    r1 = x - hi.astype(F32)
    mid = r1.astype(BF16)
    lo = (r1 - mid.astype(F32)).astype(BF16)
    return hi, mid, lo


def _cumsum_rows(tri, x):
    hi, mid, lo = _split3(x)
    dot = functools.partial(jnp.dot, preferred_element_type=F32)
    return dot(tri, hi) + dot(tri, mid) + dot(tri, lo)


def _cumsum_lanes(x, tri_t):
    hi, mid, lo = _split3(x)
    dot = functools.partial(jnp.dot, preferred_element_type=F32)
    return dot(hi, tri_t) + dot(mid, tri_t) + dot(lo, tri_t)


def _tri(n, dtype, lower):
    r = lax.broadcasted_iota(jnp.int32, (n, n), 0)
    c = lax.broadcasted_iota(jnp.int32, (n, n), 1)
    return jnp.where((r >= c) if lower else (r <= c), 1.0, 0.0).astype(dtype)


def _modulated_norm(x, norm_w, shift, scale):
    y = x * lax.rsqrt(jnp.mean(x * x, axis=-1, keepdims=True) + EPS)
    return (y * norm_w) * (1.0 + scale) + shift


def _resident(shape):
    nd = len(shape)
    return pl.BlockSpec(shape, lambda *_: (0,) * nd, pipeline_mode=pl.Buffered(1))


def _adaln_body(c_ref, w_ref, b_ref, o_ref):
    c = c_ref[...]
    s = c * _sigmoid(c)
    o_ref[...] = jnp.dot(s, w_ref[...], preferred_element_type=F32,
                         precision=lax.Precision.HIGHEST) + b_ref[...]


def adaln_call(cvecs, w_mod, b_mod):
    n = cvecs.shape[0]
    n_out = w_mod.shape[1]
    tile = 1024
    return pl.pallas_call(
        _adaln_body,
        grid=(n_out // tile,),
        in_specs=[pl.BlockSpec((n, D_MODEL), lambda j: (0, 0)),
                  pl.BlockSpec((D_MODEL, tile), lambda j: (0, j)),
                  pl.BlockSpec((1, tile), lambda j: (0, j))],
        out_specs=pl.BlockSpec((n, tile), lambda j: (0, j)),
        out_shape=jax.ShapeDtypeStruct((n, n_out), F32),
        name="adaln",
    )(cvecs, w_mod, b_mod.reshape(1, n_out))


def _ctx_body(ctx_ref, sh_ref, sc_ref, nw_ref, w_ref, b_ref, c_ref, m_ref):
    x = ctx_ref[0]
    h = _modulated_norm(x, nw_ref[...], sh_ref[...], sc_ref[...]).astype(BF16)
    z = jnp.dot(h, w_ref[...], preferred_element_type=F32) + b_ref[...]
    k = z[:, _REF_K:_REF_V] * (DK ** -0.5)
    v = z[:, _REF_V:_REF_IG]
    ig = z[:, _REF_IG:_REF_FG]
    fg = z[:, _REF_FG:_REF_Q]
    lf = _log_sigmoid(fg)
    b = _cumsum_rows(_tri(CTX_LEN, BF16, lower=True), lf)
    tot = b[CTX_LEN - 1:CTX_LEN, :]
    e = b - lf
    g = jnp.concatenate([tot[:, :N_HEADS] - b[:, :N_HEADS] + ig[:, :N_HEADS],
                         e[:, N_HEADS:] + ig[:, N_HEADS:]], axis=1)
    m = jnp.maximum(tot, jnp.max(g, axis=0, keepdims=True))
    w = jnp.exp(g - m)
    pad = jnp.zeros((V_AUG - DV - 1, HEADS_PER_STEP * DK), F32)
    m_rows = []
    for grp in range(N_HEAD_GROUPS):
        m_rows.append(jnp.zeros((GATES_PER_GROUP - N_DIR * HEADS_PER_STEP, 1), F32))
        for d in range(N_DIR):
            c_parts, n_parts = [], []
            for j in range(HEADS_PER_STEP):
                hh = grp * HEADS_PER_STEP + j
                col = d * N_HEADS + hh
                wc = w[:, col:col + 1]
                kh = k[:, hh * DK:(hh + 1) * DK]
                wv = (wc * v[:, hh * DV:(hh + 1) * DV]).astype(BF16)
                c_parts.append(lax.dot_general(wv, kh.astype(BF16), (((0,), (0,)), ((), ())),
                                               preferred_element_type=F32))
                n_parts.append(jnp.sum(wc * kh, axis=0, keepdims=True))
                m_rows.append(m[:, col:col + 1])
            c_ref[0, d, grp] = jnp.concatenate(
                [jnp.concatenate(c_parts, axis=1), jnp.concatenate(n_parts, axis=1), pad], axis=0)
    m_ref[0] = jnp.concatenate(m_rows, axis=0)


def ctx_call(ctx, shift, scale, norm_w, w_state, b_state):
    bsz = ctx.shape[0]
    n_state = w_state.shape[1]
    return pl.pallas_call(
        _ctx_body,
        grid=(bsz,),
        in_specs=[pl.BlockSpec((1, CTX_LEN, D_MODEL), lambda i: (i, 0, 0)),
                  pl.BlockSpec((1, D_MODEL), lambda i: (0, 0)),
                  pl.BlockSpec((1, D_MODEL), lambda i: (0, 0)),
                  pl.BlockSpec((1, D_MODEL), lambda i: (0, 0)),
                  pl.BlockSpec((D_MODEL, n_state), lambda i: (0, 0)),
                  pl.BlockSpec((1, n_state), lambda i: (0, 0))],
        out_specs=[pl.BlockSpec((1, N_DIR, N_HEAD_GROUPS, V_AUG, HEADS_PER_STEP * DK),
                                lambda i: (i, 0, 0, 0, 0)),
                   pl.BlockSpec((1, N_GATE, 1), lambda i: (i, 0, 0))],
        out_shape=[jax.ShapeDtypeStruct((bsz, N_DIR, N_HEAD_GROUPS, V_AUG, HEADS_PER_STEP * DK), F32),
                   jax.ShapeDtypeStruct((bsz, N_GATE, 1), F32)],
        compiler_params=pltpu.CompilerParams(dimension_semantics=("arbitrary",),
                                             vmem_limit_bytes=VMEM_LIMIT_BYTES),
        name="ctx_state",
    )(ctx, shift, scale, norm_w, w_state, b_state)


def _inproj_body(x_ref, sh_ref, sc_ref, nw_ref, wk_ref, bk_ref, w_ref, b_ref, wt_ref, bt_ref, cw_ref, wco_ref,
                 k_ref, qt_ref, vt_ref, gt_ref, so_ref, ya_ref, gb_ref):
    x = x_ref[0]
    hb = _modulated_norm(x, nw_ref[...], sh_ref[0], sc_ref[0]).astype(BF16)
    tile = hb.shape[0]

    def seg(a, b):
        return jnp.dot(hb, w_ref[:, a:b], preferred_element_type=F32) + b_ref[:, a:b]

    k = jnp.dot(hb, wk_ref[...], preferred_element_type=F32) + bk_ref[...]
    k_ref[0] = (k * (DK ** -0.5)).astype(BF16)
    so_ref[0] = _sigmoid(seg(_O0, _XIN0)).astype(BF16)

    zt = lax.dot_general(wt_ref[...], hb, (((1,), (1,)), ((), ())),
                         preferred_element_type=F32) + bt_ref[...]
    for i in range(tile // CHUNK):
        lanes = slice(i * CHUNK, (i + 1) * CHUNK)
        qt_ref[0, i] = zt[:D_QK, lanes].astype(BF16)
        vt_ref[0, i] = zt[D_QK:D_QK + D_MLSTM, lanes].astype(BF16)
        gt_ref[0, i] = zt[D_QK + D_MLSTM:, lanes]

    u = seg(_XIN0, _GC0) * seg(_GC0, _GB0)
    col = jnp.bitwise_and(lax.broadcasted_iota(jnp.int32, (tile, 1), 0), GRID_W - 1)
    u_prev = jnp.where(col != 0, pltpu.roll(u, 1, axis=0), 0.0)
    u_next = jnp.where(col != GRID_W - 1, pltpu.roll(u, tile - 1, axis=0), 0.0)
    a = cw_ref[0:1, :] * u_prev + cw_ref[1:2, :] * u + cw_ref[2:3, :] * u_next
    ya = jnp.dot((seg(_GB0, _MA0) * a).astype(BF16), wco_ref[...], preferred_element_type=F32)
    ya_ref[0] = (_sigmoid(seg(_MA0, _MB0)) * ya).astype(BF16)
    gb_ref[0] = _sigmoid(seg(_MB0, _G0)).astype(BF16)


def inproj_call(x, shift, scale, norm_w, w_k, b_k, w_rest, b_rest, w_t, b_t, conv_w, w_conv_out):
    bsz, t, _ = x.shape
    tile = TOKEN_TILE
    tok = lambda width: pl.BlockSpec((1, tile, width), lambda i, j: (i, j, 0))
    tok_t = lambda rows: pl.BlockSpec((1, tile // CHUNK, rows, CHUNK), lambda i, j: (i, j, 0, 0))
    row = pl.BlockSpec((1, 1, D_MODEL), lambda i, j: (i, 0, 0))
    seq = lambda width: jax.ShapeDtypeStruct((bsz, t, width), BF16)
    seq_t = lambda rows, dt: jax.ShapeDtypeStruct((bsz, t // CHUNK, rows, CHUNK), dt)
    resident = [norm_w, w_k, b_k, w_rest, b_rest, w_t, b_t, conv_w, w_conv_out]
    return pl.pallas_call(
        _inproj_body,
        grid=(bsz, t // tile),
        in_specs=[tok(D_MODEL), row, row] + [_resident(a.shape) for a in resident],
        out_specs=[tok(D_QK), tok_t(D_QK), tok_t(D_MLSTM), tok_t(N_GATE), tok(D_MLSTM),
                   tok(D_MODEL), tok(D_MODEL)],
        out_shape=[seq(D_QK), seq_t(D_QK, BF16), seq_t(D_MLSTM, BF16), seq_t(N_GATE, F32), seq(D_MLSTM),
                   seq(D_MODEL), seq(D_MODEL)],
        compiler_params=pltpu.CompilerParams(dimension_semantics=("arbitrary", "arbitrary"),
                                             vmem_limit_bytes=VMEM_LIMIT_BYTES),
        name="inproj_conv",
    )(x, shift, scale, *resident)


SERIES_ROW0 = GATES_PER_GROUP // 2
N_SERIES = N_DIR * HEADS_PER_STEP
N_SPLIT = 3
LOG2E = 1.4426950408889634
_WK, _DECAY, _WINTER, _EXPNEG, _KEYS = 0, 1, 2, 3, 4
_COLM = _KEYS + N_SPLIT
N_TABLES = _COLM + N_SPLIT * N_SERIES


def _running_max_rows(x, reverse):
    n = x.shape[0]
    row = lax.broadcasted_iota(jnp.int32, x.shape, 0)
    shift = 1
    while shift < n:
        if reverse:
            moved = jnp.where(row < n - shift, pltpu.roll(x, n - shift, axis=0), -jnp.inf)
        else:
            moved = jnp.where(row >= shift, pltpu.roll(x, shift, axis=0), -jnp.inf)
        x = jnp.maximum(x, moved)
        shift *= 2
    return x


def _gate_tables(g_ref, m0_ref, tab_ref, n_chunks):
    step_rows = N_HEAD_GROUPS * GATES_PER_GROUP
    rows = n_chunks * step_rows
    g = g_ref[0].reshape(rows, CHUNK)
    lf = _log_sigmoid(g)
    b = _cumsum_lanes(lf, _tri(CHUNK, BF16, lower=False))
    ig = pltpu.roll(g, SERIES_ROW0, axis=0)
    in_tile = lambda idx: jnp.bitwise_and(idx, GATES_PER_GROUP - 1)
    r_in_tile = in_tile(lax.broadcasted_iota(jnp.int32, (rows, 1), 0))
    fwd = r_in_tile < SERIES_ROW0 + HEADS_PER_STEP
    e = b - lf
    tot = b[:, CHUNK - 1:CHUNK]
    col = jnp.where(fwd, b, -e)
    key = ig + jnp.where(fwd, -b, e)
    g_end = jnp.where(fwd, tot + key, key)
    g_max = jnp.max(g_end, axis=1, keepdims=True)

    fwd_step = fwd[:step_rows]
    tile = lambda a, c: a[c * step_rows:(c + 1) * step_rows]
    m = m0_ref[0]
    m_in_steps, m_out_steps = [], []
    for i in range(n_chunks):
        back = n_chunks - 1 - i
        m_in_steps.append(m)
        m = jnp.maximum(jnp.where(fwd_step, tile(tot, i), tile(tot, back)) + m,
                        jnp.where(fwd_step, tile(g_max, i), tile(g_max, back)))
        m_out_steps.append(m)
    by_chunk = lambda steps: jnp.concatenate(
        [jnp.where(fwd_step, steps[c], steps[n_chunks - 1 - c]) for c in range(n_chunks)], axis=0)
    m_in, m_out = by_chunk(m_in_steps), by_chunk(m_out_steps)

    decay = jnp.broadcast_to(jnp.exp(tot + m_in - m_out), (rows, CHUNK))
    lane = lax.broadcasted_iota(jnp.int32, (rows, CHUNK), 1)
    inter = jnp.where(fwd, col, tot + col) + m_in
    key_t = jnp.transpose(key)
    fwd_lane = in_tile(lax.broadcasted_iota(jnp.int32, (1, rows), 1)) < SERIES_ROW0 + HEADS_PER_STEP
    key_max = jnp.transpose(jnp.where(fwd_lane, _running_max_rows(key_t, reverse=False),
                                      _running_max_rows(key_t, reverse=True)))
    m_t = jnp.maximum(col + key_max, inter)
    tables = {_WK: jnp.exp(g_end - m_out),
              _DECAY: jnp.where(lane < DK, decay, pltpu.roll(decay, rows - 1, axis=0)),
              _WINTER: jnp.exp(inter - m_t), _EXPNEG: jnp.exp(-m_t)}
    for x, part in enumerate(_split3(key * LOG2E)):
        tables[_KEYS + x] = part.astype(F32)
    for x, part in enumerate(_split3((col - m_t) * LOG2E)):
        for sidx in range(N_SERIES):
            tables[_COLM + x * N_SERIES + sidx] = jnp.where(r_in_tile == SERIES_ROW0 + sidx,
                                                            part.astype(F32), 0.0)
    for idx, a in tables.items():
        tab_ref[idx] = a.reshape(n_chunks, N_HEAD_GROUPS, GATES_PER_GROUP, CHUNK)


def _mlstm_body(qt_ref, k_ref, vt_ref, so_ref, g_ref, c0_ref, m0_ref, nw_ref, o_ref,
                tab_ref, u_ref, s_ref, st_ref, *, n_chunks):
    hp = HEADS_PER_STEP
    s_i = lax.broadcasted_iota(jnp.int32, (CHUNK, CHUNK), 0)
    t_i = lax.broadcasted_iota(jnp.int32, (CHUNK, CHUNK), 1)
    masks = (s_i <= t_i, s_i >= t_i)
    ones_rows = jnp.where(lax.broadcasted_iota(jnp.int32, (V_AUG - DV, CHUNK), 0) == 0,
                          1.0, 0.0).astype(BF16)
    kq_lane = lax.broadcasted_iota(jnp.int32, (CHUNK, hp * DK), 1)
    kq_row = lax.broadcasted_iota(jnp.int32, (hp * DK, CHUNK), 0)
    series = lambda tab8, d, j: tab8[SERIES_ROW0 + d * hp + j:SERIES_ROW0 + d * hp + j + 1]
    split_rows = N_SPLIT * GATES_PER_GROUP
    ones_split = jnp.ones((split_rows, CHUNK), F32)
    pick_r = jnp.bitwise_and(lax.broadcasted_iota(jnp.int32, (split_rows, N_DIR * CHUNK), 0),
                             GATES_PER_GROUP - 1)
    pick_d = jnp.where(lax.broadcasted_iota(jnp.int32, (split_rows, N_DIR * CHUNK), 1) >= CHUNK, 1, 0)
    pick_series = [jnp.where(pick_r == SERIES_ROW0 + pick_d * hp + j, 1.0, 0.0) for j in range(hp)]

    def chunk_rows(c):
        return pl.ds(pl.multiple_of(c * CHUNK, CHUNK), CHUNK)

    def values_t(c, j):
        return jnp.concatenate([vt_ref[0, c, j * DV:(j + 1) * DV, :], ones_rows], axis=0)

    grp = pl.program_id(1)

    @pl.when(grp == 0)
    def _():
        _gate_tables(g_ref, m0_ref, tab_ref, n_chunks)

    def block_diag_q(qt2):
        zero = jnp.zeros_like(qt2)
        return jnp.concatenate([jnp.where(kq_row < DK, qt2, zero),
                                jnp.where(kq_row >= DK, qt2, zero)], axis=1)

    def chunk_step(c, carry):
        k2 = k_ref[0, chunk_rows(c), :]
        zero = jnp.zeros_like(k2)
        k_bd = jnp.concatenate([jnp.where(kq_lane < DK, k2, zero),
                                jnp.where(kq_lane >= DK, k2, zero)], axis=0)
        wk8 = tab_ref[_WK, c, grp]
        vf = [values_t(c, j).astype(F32) for j in range(hp)]
        lhs = jnp.concatenate(
            [jnp.concatenate([(vf[j] * series(wk8, d, j)).astype(BF16) for j in range(hp)], axis=1)
             for d in range(N_DIR)], axis=0)
        u_ref[c] = jnp.dot(lhs, k_bd, preferred_element_type=F32)

        qk_t = jnp.dot(k2, block_diag_q(qt_ref[0, c]), preferred_element_type=F32)
        key_side = jnp.concatenate([tab_ref[_KEYS + x, c, grp] for x in range(N_SPLIT)] + [ones_split],
                                   axis=0).astype(BF16)
        for j in range(hp):
            query_side = jnp.concatenate(
                [pick_series[j]] +
                [jnp.concatenate([tab_ref[_COLM + x * N_SERIES + d * hp + j, c, grp] for d in range(N_DIR)],
                                 axis=1)
                 for x in range(N_SPLIT)], axis=0).astype(BF16)
            log_d = lax.dot_general(key_side, query_side, (((0,), (0,)), ((), ())),
                                    preferred_element_type=F32)
            for d in range(N_DIR):
                decay = jnp.exp2(jnp.where(masks[d], log_d[:, d * CHUNK:(d + 1) * CHUNK], -jnp.inf))
                st_ref[c, d, j] = (qk_t[:, j * CHUNK:(j + 1) * CHUNK] * decay).astype(BF16)
        return carry

    lax.fori_loop(0, n_chunks, chunk_step, 0, unroll=CHUNK_UNROLL)

    def scan_step(i, carry):
        cf, cb = carry
        back = n_chunks - 1 - i
        s_ref[i, :V_AUG] = cf.astype(BF16)
        s_ref[back, V_AUG:] = cb.astype(BF16)
        cf = tab_ref[_DECAY, i, grp][SERIES_ROW0:SERIES_ROW0 + 1] * cf + u_ref[i, :V_AUG]
        cb = tab_ref[_DECAY, back, grp][SERIES_ROW0 + hp:SERIES_ROW0 + hp + 1] * cb + u_ref[back, V_AUG:]
        return cf, cb

    lax.fori_loop(0, n_chunks, scan_step, (c0_ref[0, 0, 0], c0_ref[0, 1, 0]))

    def output_step(c, carry):
        rows = chunk_rows(c)
        qt2 = qt_ref[0, c].astype(F32)
        w_inter8, exp_neg8 = tab_ref[_WINTER, c, grp], tab_ref[_EXPNEG, c, grp]
        v_pair = jnp.concatenate([values_t(c, j) for j in range(hp)], axis=1)
        no_scores = jnp.zeros((CHUNK, CHUNK), BF16)
        h_t = [None] * hp
        for d in range(N_DIR):
            w_rows = jnp.where(kq_row < DK, series(w_inter8, d, 0), series(w_inter8, d, 1))
            rhs = jnp.concatenate(
                [jnp.concatenate([st_ref[c, d, 0], no_scores], axis=1),
                 jnp.concatenate([no_scores, st_ref[c, d, 1]], axis=1),
                 block_diag_q((qt2 * w_rows).astype(BF16))], axis=0)
            lhs = jnp.concatenate([v_pair, s_ref[c, d * V_AUG:(d + 1) * V_AUG]], axis=1)
            n_all = jnp.dot(lhs, rhs, preferred_element_type=F32)
            for j in range(hp):
                num = n_all[:, j * CHUNK:(j + 1) * CHUNK]
                r = 1.0 / jnp.maximum(jnp.abs(num[DV:DV + 1]), series(exp_neg8, d, j))
                part = num[:DV] * r
                h_t[j] = part if h_t[j] is None else h_t[j] + part
        for j in range(hp):
            hn_t = h_t[j] * lax.rsqrt(jnp.mean(h_t[j] * h_t[j], axis=0, keepdims=True) + EPS)
            hn = jnp.transpose(hn_t) * nw_ref[:, j * DV:(j + 1) * DV]
            o_ref[0, rows, j * DV:(j + 1) * DV] = (
                hn * so_ref[0, rows, j * DV:(j + 1) * DV].astype(F32)).astype(BF16)
        return carry

    lax.fori_loop(0, n_chunks, output_step, 0, unroll=CHUNK_UNROLL)


def mlstm_call(qt, k, vt, so, gates, c0, m0, norm_w):
    bsz, t, _ = k.shape
    n_chunks = t // CHUNK
    hp = HEADS_PER_STEP
    seq = lambda width: pl.BlockSpec((1, t, width), lambda i, j: (i, 0, j))
    seq_t = lambda rows: pl.BlockSpec((1, n_chunks, rows, CHUNK), lambda i, j: (i, 0, j, 0))
    return pl.pallas_call(
        functools.partial(_mlstm_body, n_chunks=n_chunks),
        grid=(bsz, N_HEAD_GROUPS),
        in_specs=[seq_t(hp * DK), seq(hp * DK), seq_t(hp * DV), seq(hp * DV),
                  pl.BlockSpec((1, n_chunks, N_GATE, CHUNK), lambda i, j: (i, 0, 0, 0)),
                  pl.BlockSpec((1, N_DIR, 1, V_AUG, hp * DK), lambda i, j: (i, 0, j, 0, 0)),
                  pl.BlockSpec((1, N_GATE, 1), lambda i, j: (i, 0, 0)),
                  pl.BlockSpec((1, hp * DV), lambda i, j: (0, j))],
        out_specs=seq(hp * DV),
        out_shape=jax.ShapeDtypeStruct((bsz, t, D_MLSTM), BF16),
        scratch_shapes=[pltpu.VMEM((N_TABLES, n_chunks, N_HEAD_GROUPS, GATES_PER_GROUP, CHUNK), F32),
                        pltpu.VMEM((n_chunks, N_DIR * V_AUG, hp * DK), F32),
                        pltpu.VMEM((n_chunks, N_DIR * V_AUG, hp * DK), BF16),
                        pltpu.VMEM((n_chunks, N_DIR, hp, CHUNK, CHUNK), BF16)],
        compiler_params=pltpu.CompilerParams(dimension_semantics=("arbitrary", "arbitrary"),
                                             vmem_limit_bytes=VMEM_LIMIT_BYTES),
        name="mlstm",
    )(qt, k, vt, so, gates, c0, m0, norm_w)


def _out_body(x_ref, hs_ref, ya_ref, gb_ref, g1_ref, sh2_ref, sc2_ref, g2_ref, nw2_ref, fnw_ref,
              wmo_ref, wo_ref, w1_ref, w2_ref, o_ref):
    dot = functools.partial(jnp.dot, preferred_element_type=F32)
    yb = dot(hs_ref[0], wmo_ref[...])
    merged = ya_ref[0].astype(F32) + gb_ref[0].astype(F32) * yb
    x1 = x_ref[0] + g1_ref[0] * dot(merged.astype(BF16), wo_ref[...])
    hm = _modulated_norm(x1, nw2_ref[...], sh2_ref[0], sc2_ref[0]).astype(BF16)
    step = D_FF // FF_SPLIT
    ff = None
    for s in range(FF_SPLIT):
        a = jnp.maximum(dot(hm, w1_ref[:, s * step:(s + 1) * step]), 0.0)
        part = dot((a * a).astype(BF16), w2_ref[s * step:(s + 1) * step, :])
        ff = part if ff is None else ff + part
    x2 = x1 + g2_ref[0] * ff
    y = x2 * lax.rsqrt(jnp.mean(x2 * x2, axis=-1, keepdims=True) + EPS)
    o_ref[0] = y * fnw_ref[...]


def out_call(x, hs, ya, gb, g1, sh2, sc2, g2, norm2_w, final_norm_w, w_mlstm_out, w_out, w_ff1, w_ff2):
    bsz, t, _ = x.shape
    tile = TOKEN_TILE
    tok = pl.BlockSpec((1, tile, D_MODEL), lambda i, j: (i, j, 0))
    row = pl.BlockSpec((1, 1, D_MODEL), lambda i, j: (i, 0, 0))
    return pl.pallas_call(
        _out_body,
        grid=(bsz, t // tile),
        in_specs=[tok, tok, tok, tok, row, row, row, row,
                  _resident((1, D_MODEL)), _resident((1, D_MODEL)),
                  _resident(w_mlstm_out.shape), _resident(w_out.shape),
                  _resident(w_ff1.shape), _resident(w_ff2.shape)],
        out_specs=tok,
        out_shape=jax.ShapeDtypeStruct((bsz, t, D_MODEL), F32),
        compiler_params=pltpu.CompilerParams(dimension_semantics=("arbitrary", "arbitrary"),
                                             vmem_limit_bytes=VMEM_LIMIT_BYTES),
        name="merge_out_mlp",
    )(x, hs, ya, gb, g1, sh2, sc2, g2, norm2_w, final_norm_w, w_mlstm_out, w_out, w_ff1, w_ff2)


def _group_gates(g):
    lead = g.shape[:-1]
    g = g.reshape(*lead, 2, N_DIR, N_HEAD_GROUPS, HEADS_PER_STEP)
    g = jnp.moveaxis(g, -2, -4)
    return g.reshape(*lead, N_GATE)


def _layer(x, ctx, mod, mod_ctx, norm1_w, w_in, b_in, conv_w, mlstm_norm_w, w_conv_out, w_mlstm_out,
           w_out, norm2_w, w_ff1, w_ff2, final_norm_w):
    bsz, t, _ = x.shape
    sh1, sc1, g1, sh2, sc2, g2 = [m.reshape(bsz, 1, D_MODEL) for m in jnp.split(mod, 6, axis=-1)]
    csh1, csc1 = mod_ctx[:, :D_MODEL], mod_ctx[:, D_MODEL:2 * D_MODEL]
    nw1 = norm1_w.reshape(1, D_MODEL)

    c0, m0 = ctx_call(ctx, csh1, csc1, nw1, w_in[:, :_REF_Q].astype(BF16), b_in[:_REF_Q].reshape(1, _REF_Q))

    w_t = jnp.concatenate([w_in[:, _REF_Q:_REF_O], w_in[:, _REF_V:_REF_IG],
                           _group_gates(w_in[:, _REF_IG:_REF_Q])], axis=1).T.astype(BF16)
    b_t = jnp.concatenate([b_in[_REF_Q:_REF_O], b_in[_REF_V:_REF_IG],
                           _group_gates(b_in[_REF_IG:_REF_Q])]).reshape(-1, 1)
    k, qt, vt, gates, so, ya, gb = inproj_call(
        x, sh1, sc1, nw1, w_in[:, _REF_K:_REF_V].astype(BF16), b_in[_REF_K:_REF_V].reshape(1, -1),
        w_in[:, _REF_O:].astype(BF16), b_in[_REF_O:].reshape(1, -1), w_t, b_t, conv_w,
        w_conv_out.astype(BF16))
    hs = mlstm_call(qt, k, vt, so, gates, c0, m0, mlstm_norm_w.reshape(1, D_MLSTM))
    return out_call(x, hs, ya, gb, g1, sh2, sc2, g2, norm2_w.reshape(1, D_MODEL),
                    final_norm_w.reshape(1, D_MODEL), w_mlstm_out.astype(BF16), w_out.astype(BF16),
                    w_ff1.astype(BF16), w_ff2.astype(BF16))


def kernel(x, c, ctx, c_ctx, w_mod, b_mod, norm1_w, w_in, b_in, conv_w, mlstm_norm_w, w_conv_out,
           w_mlstm_out, w_out, norm2_w, w_ff1, w_ff2, final_norm_w):
    depth = w_mod.shape[0]
    assert depth == 1, "the context stream is only advanced through its mLSTM state (single layer)"
    bsz = x.shape[0]
    cvecs = jnp.concatenate([c, c_ctx[None, :]], axis=0)
    mod_all = adaln_call(cvecs, w_mod[0], b_mod[0])
    return _layer(x, ctx, mod_all[:bsz], mod_all[bsz:], norm1_w[0], w_in[0], b_in[0], conv_w[0],
                  mlstm_norm_w[0], w_conv_out[0], w_mlstm_out[0], w_out[0], norm2_w[0], w_ff1[0],
                  w_ff2[0], final_norm_w)
```

```python
import functools

import jax
import jax.numpy as jnp
from jax import lax
from jax.experimental import pallas as pl
from jax.experimental.pallas import tpu as pltpu

D_MODEL = 1024
CTX_LEN = 256
GRID_W = 64
D_CONV = 1024
N_HEADS = 8
DK = 64
DV = 128
D_MLSTM = N_HEADS * DV
D_QK = N_HEADS * DK
D_FF = 4 * D_MODEL
N_DIR = 2
N_GATE = 2 * N_DIR * N_HEADS
EPS = 1e-6

CHUNK = 128
HEADS_PER_STEP = 2
N_HEAD_GROUPS = N_HEADS // HEADS_PER_STEP
GATES_PER_GROUP = N_GATE // N_HEAD_GROUPS
BF16_SUBLANES = 16
V_AUG = DV + BF16_SUBLANES
CHUNK_UNROLL = 16
TOKEN_TILE = 512
FF_SPLIT = 4

VMEM_LIMIT_BYTES = 56 * 1024 * 1024

_REF_K, _REF_V, _REF_IG, _REF_FG, _REF_Q, _REF_O = 0, 512, 1536, 1552, 1568, 2080
D_IN = 8224
_O0, _XIN0, _GC0, _GB0, _MA0, _MB0, _G0 = 0, 1024, 2048, 3072, 4096, 5120, 6144

F32 = jnp.float32
BF16 = jnp.bfloat16


def _sigmoid(x):
    return 1.0 / (1.0 + jnp.exp(-x))


def _log_sigmoid(x):
    return jnp.minimum(x, 0.0) - jnp.log(1.0 + jnp.exp(-jnp.abs(x)))


def _split3(x):
    hi = x.astype(BF16)
    r1 = x - hi.astype(F32)
    mid = r1.astype(BF16)
    lo = (r1 - mid.astype(F32)).astype(BF16)
    return hi, mid, lo


def _cumsum_rows(tri, x):
    hi, mid, lo = _split3(x)
    dot = functools.partial(jnp.dot, preferred_element_type=F32)
    return dot(tri, hi) + dot(tri, mid) + dot(tri, lo)


def _cumsum_lanes(x, tri_t):
    hi, mid, lo = _split3(x)
    dot = functools.partial(jnp.dot, preferred_element_type=F32)
    return dot(hi, tri_t) + dot(mid, tri_t) + dot(lo, tri_t)


def _tri(n, dtype, lower):
    r = lax.broadcasted_iota(jnp.int32, (n, n), 0)
    c = lax.broadcasted_iota(jnp.int32, (n, n), 1)
    return jnp.where((r >= c) if lower else (r <= c), 1.0, 0.0).astype(dtype)


def _modulated_norm(x, norm_w, shift, scale):
    y = x * lax.rsqrt(jnp.mean(x * x, axis=-1, keepdims=True) + EPS)
    return (y * norm_w) * (1.0 + scale) + shift


def _resident(shape):
    nd = len(shape)
    return pl.BlockSpec(shape, lambda *_: (0,) * nd, pipeline_mode=pl.Buffered(1))


def _adaln_body(c_ref, w_ref, b_ref, o_ref):
    c = c_ref[...]
    s = c * _sigmoid(c)
    o_ref[...] = jnp.dot(s, w_ref[...], preferred_element_type=F32,
                         precision=lax.Precision.HIGHEST) + b_ref[...]


def adaln_call(cvecs, w_mod, b_mod):
    n = cvecs.shape[0]
    n_out = w_mod.shape[1]
    tile = 1024
    return pl.pallas_call(
        _adaln_body,
        grid=(n_out // tile,),
        in_specs=[pl.BlockSpec((n, D_MODEL), lambda j: (0, 0)),
                  pl.BlockSpec((D_MODEL, tile), lambda j: (0, j)),
                  pl.BlockSpec((1, tile), lambda j: (0, j))],
        out_specs=pl.BlockSpec((n, tile), lambda j: (0, j)),
        out_shape=jax.ShapeDtypeStruct((n, n_out), F32),
        name="adaln",
    )(cvecs, w_mod, b_mod.reshape(1, n_out))


def _ctx_body(ctx_ref, sh_ref, sc_ref, nw_ref, w_ref, b_ref, c_ref, m_ref):
    x = ctx_ref[0]
    h = _modulated_norm(x, nw_ref[...], sh_ref[...], sc_ref[...]).astype(BF16)
    z = jnp.dot(h, w_ref[...], preferred_element_type=F32) + b_ref[...]
    k = z[:, _REF_K:_REF_V] * (DK ** -0.5)
    v = z[:, _REF_V:_REF_IG]
    ig = z[:, _REF_IG:_REF_FG]
    fg = z[:, _REF_FG:_REF_Q]
    lf = _log_sigmoid(fg)
    b = _cumsum_rows(_tri(CTX_LEN, BF16, lower=True), lf)
    tot = b[CTX_LEN - 1:CTX_LEN, :]
    e = b - lf
    g = jnp.concatenate([tot[:, :N_HEADS] - b[:, :N_HEADS] + ig[:, :N_HEADS],
                         e[:, N_HEADS:] + ig[:, N_HEADS:]], axis=1)
    m = jnp.maximum(tot, jnp.max(g, axis=0, keepdims=True))
    w = jnp.exp(g - m)
    pad = jnp.zeros((V_AUG - DV - 1, HEADS_PER_STEP * DK), F32)
    m_rows = []
    for grp in range(N_HEAD_GROUPS):
        m_rows.append(jnp.zeros((GATES_PER_GROUP - N_DIR * HEADS_PER_STEP, 1), F32))
        for d in range(N_DIR):
            c_parts, n_parts = [], []
            for j in range(HEADS_PER_STEP):
                hh = grp * HEADS_PER_STEP + j
                col = d * N_HEADS + hh
                wc = w[:, col:col + 1]
                kh = k[:, hh * DK:(hh + 1) * DK]
                wv = (wc * v[:, hh * DV:(hh + 1) * DV]).astype(BF16)
                c_parts.append(lax.dot_general(wv, kh.astype(BF16), (((0,), (0,)), ((), ())),
                                               preferred_element_type=F32))
                n_parts.append(jnp.sum(wc * kh, axis=0, keepdims=True))
                m_rows.append(m[:, col:col + 1])
            c_ref[0, d, grp] = jnp.concatenate(
                [jnp.concatenate(c_parts, axis=1), jnp.concatenate(n_parts, axis=1), pad], axis=0)
    m_ref[0] = jnp.concatenate(m_rows, axis=0)


def ctx_call(ctx, shift, scale, norm_w, w_state, b_state):
    bsz = ctx.shape[0]
    n_state = w_state.shape[1]
    return pl.pallas_call(
        _ctx_body,
        grid=(bsz,),
        in_specs=[pl.BlockSpec((1, CTX_LEN, D_MODEL), lambda i: (i, 0, 0)),
                  pl.BlockSpec((1, D_MODEL), lambda i: (0, 0)),
                  pl.BlockSpec((1, D_MODEL), lambda i: (0, 0)),
                  pl.BlockSpec((1, D_MODEL), lambda i: (0, 0)),
                  pl.BlockSpec((D_MODEL, n_state), lambda i: (0, 0)),
                  pl.BlockSpec((1, n_state), lambda i: (0, 0))],
        out_specs=[pl.BlockSpec((1, N_DIR, N_HEAD_GROUPS, V_AUG, HEADS_PER_STEP * DK),
                                lambda i: (i, 0, 0, 0, 0)),
                   pl.BlockSpec((1, N_GATE, 1), lambda i: (i, 0, 0))],
        out_shape=[jax.ShapeDtypeStruct((bsz, N_DIR, N_HEAD_GROUPS, V_AUG, HEADS_PER_STEP * DK), F32),
                   jax.ShapeDtypeStruct((bsz, N_GATE, 1), F32)],
        compiler_params=pltpu.CompilerParams(dimension_semantics=("arbitrary",),
                                             vmem_limit_bytes=VMEM_LIMIT_BYTES),
        name="ctx_state",
    )(ctx, shift, scale, norm_w, w_state, b_state)


def _inproj_body(x_ref, sh_ref, sc_ref, nw_ref, wk_ref, bk_ref, w_ref, b_ref, wt_ref, bt_ref, cw_ref, wco_ref,
                 k_ref, qt_ref, vt_ref, gt_ref, so_ref, ya_ref, gb_ref):
    x = x_ref[0]
    hb = _modulated_norm(x, nw_ref[...], sh_ref[0], sc_ref[0]).astype(BF16)
    tile = hb.shape[0]

    def seg(a, b):
        return jnp.dot(hb, w_ref[:, a:b], preferred_element_type=F32) + b_ref[:, a:b]

    k = jnp.dot(hb, wk_ref[...], preferred_element_type=F32) + bk_ref[...]
    k_ref[0] = (k * (DK ** -0.5)).astype(BF16)
    so_ref[0] = _sigmoid(seg(_O0, _XIN0)).astype(BF16)

    zt = lax.dot_general(wt_ref[...], hb, (((1,), (1,)), ((), ())),
                         preferred_element_type=F32) + bt_ref[...]
    for i in range(tile // CHUNK):
        lanes = slice(i * CHUNK, (i + 1) * CHUNK)
        qt_ref[0, i] = zt[:D_QK, lanes].astype(BF16)
        vt_ref[0, i] = zt[D_QK:D_QK + D_MLSTM, lanes].astype(BF16)
        gt_ref[0, i] = zt[D_QK + D_MLSTM:, lanes]

    u = seg(_XIN0, _GC0) * seg(_GC0, _GB0)
    col = jnp.bitwise_and(lax.broadcasted_iota(jnp.int32, (tile, 1), 0), GRID_W - 1)
    u_prev = jnp.where(col != 0, pltpu.roll(u, 1, axis=0), 0.0)
    u_next = jnp.where(col != GRID_W - 1, pltpu.roll(u, tile - 1, axis=0), 0.0)
    a = cw_ref[0:1, :] * u_prev + cw_ref[1:2, :] * u + cw_ref[2:3, :] * u_next
    ya = jnp.dot((seg(_GB0, _MA0) * a).astype(BF16), wco_ref[...], preferred_element_type=F32)
    ya_ref[0] = (_sigmoid(seg(_MA0, _MB0)) * ya).astype(BF16)
    gb_ref[0] = _sigmoid(seg(_MB0, _G0)).astype(BF16)


def inproj_call(x, shift, scale, norm_w, w_k, b_k, w_rest, b_rest, w_t, b_t, conv_w, w_conv_out):
    bsz, t, _ = x.shape
    tile = TOKEN_TILE
    tok = lambda width: pl.BlockSpec((1, tile, width), lambda i, j: (i, j, 0))
    tok_t = lambda rows: pl.BlockSpec((1, tile // CHUNK, rows, CHUNK), lambda i, j: (i, j, 0, 0))
    row = pl.BlockSpec((1, 1, D_MODEL), lambda i, j: (i, 0, 0))
    seq = lambda width: jax.ShapeDtypeStruct((bsz, t, width), BF16)
    seq_t = lambda rows, dt: jax.ShapeDtypeStruct((bsz, t // CHUNK, rows, CHUNK), dt)
    resident = [norm_w, w_k, b_k, w_rest, b_rest, w_t, b_t, conv_w, w_conv_out]
    return pl.pallas_call(
        _inproj_body,
        grid=(bsz, t // tile),
        in_specs=[tok(D_MODEL), row, row] + [_resident(a.shape) for a in resident],
        out_specs=[tok(D_QK), tok_t(D_QK), tok_t(D_MLSTM), tok_t(N_GATE), tok(D_MLSTM),
                   tok(D_MODEL), tok(D_MODEL)],
        out_shape=[seq(D_QK), seq_t(D_QK, BF16), seq_t(D_MLSTM, BF16), seq_t(N_GATE, F32), seq(D_MLSTM),
                   seq(D_MODEL), seq(D_MODEL)],
        compiler_params=pltpu.CompilerParams(dimension_semantics=("arbitrary", "arbitrary"),
                                             vmem_limit_bytes=VMEM_LIMIT_BYTES),
        name="inproj_conv",
    )(x, shift, scale, *resident)


SERIES_ROW0 = GATES_PER_GROUP // 2
N_SERIES = N_DIR * HEADS_PER_STEP
N_SPLIT = 3
LOG2E = 1.4426950408889634
_WK, _DECAY, _WINTER, _EXPNEG, _KEYS = 0, 1, 2, 3, 4
_COLM = _KEYS + N_SPLIT
N_TABLES = _COLM + N_SPLIT * N_SERIES


def _running_max_rows(x, reverse):
    n = x.shape[0]
    row = lax.broadcasted_iota(jnp.int32, x.shape, 0)
    shift = 1
    while shift < n:
        if reverse:
            moved = jnp.where(row < n - shift, pltpu.roll(x, n - shift, axis=0), -jnp.inf)
        else:
            moved = jnp.where(row >= shift, pltpu.roll(x, shift, axis=0), -jnp.inf)
        x = jnp.maximum(x, moved)
        shift *= 2
    return x


def _gate_tables(g_ref, m0_ref, tab_ref, n_chunks):
    step_rows = N_HEAD_GROUPS * GATES_PER_GROUP
    rows = n_chunks * step_rows
    g = g_ref[0].reshape(rows, CHUNK)
    lf = _log_sigmoid(g)
    b = _cumsum_lanes(lf, _tri(CHUNK, BF16, lower=False))
    ig = pltpu.roll(g, SERIES_ROW0, axis=0)
    in_tile = lambda idx: jnp.bitwise_and(idx, GATES_PER_GROUP - 1)
    r_in_tile = in_tile(lax.broadcasted_iota(jnp.int32, (rows, 1), 0))
    fwd = r_in_tile < SERIES_ROW0 + HEADS_PER_STEP
    e = b - lf
    tot = b[:, CHUNK - 1:CHUNK]
    col = jnp.where(fwd, b, -e)
    key = ig + jnp.where(fwd, -b, e)
    g_end = jnp.where(fwd, tot + key, key)
    g_max = jnp.max(g_end, axis=1, keepdims=True)

    fwd_step = fwd[:step_rows]
    tile = lambda a, c: a[c * step_rows:(c + 1) * step_rows]
    m = m0_ref[0]
    m_in_steps, m_out_steps = [], []
    for i in range(n_chunks):
        back = n_chunks - 1 - i
        m_in_steps.append(m)
        m = jnp.maximum(jnp.where(fwd_step, tile(tot, i), tile(tot, back)) + m,
                        jnp.where(fwd_step, tile(g_max, i), tile(g_max, back)))
        m_out_steps.append(m)
    by_chunk = lambda steps: jnp.concatenate(
        [jnp.where(fwd_step, steps[c], steps[n_chunks - 1 - c]) for c in range(n_chunks)], axis=0)
    m_in, m_out = by_chunk(m_in_steps), by_chunk(m_out_steps)

    decay = jnp.broadcast_to(jnp.exp(tot + m_in - m_out), (rows, CHUNK))
    lane = lax.broadcasted_iota(jnp.int32, (rows, CHUNK), 1)
    inter = jnp.where(fwd, col, tot + col) + m_in
    key_t = jnp.transpose(key)
    fwd_lane = in_tile(lax.broadcasted_iota(jnp.int32, (1, rows), 1)) < SERIES_ROW0 + HEADS_PER_STEP
    key_max = jnp.transpose(jnp.where(fwd_lane, _running_max_rows(key_t, reverse=False),
                                      _running_max_rows(key_t, reverse=True)))
    m_t = jnp.maximum(col + key_max, inter)
    tables = {_WK: jnp.exp(g_end - m_out),
              _DECAY: jnp.where(lane < DK, decay, pltpu.roll(decay, rows - 1, axis=0)),
              _WINTER: jnp.exp(inter - m_t), _EXPNEG: jnp.exp(-m_t)}
    for x, part in enumerate(_split3(key * LOG2E)):
        tables[_KEYS + x] = part.astype(F32)
    for x, part in enumerate(_split3((col - m_t) * LOG2E)):
        for sidx in range(N_SERIES):
            tables[_COLM + x * N_SERIES + sidx] = jnp.where(r_in_tile == SERIES_ROW0 + sidx,
                                                            part.astype(F32), 0.0)
    for idx, a in tables.items():
        tab_ref[idx] = a.reshape(n_chunks, N_HEAD_GROUPS, GATES_PER_GROUP, CHUNK)


def _mlstm_body(qt_ref, k_ref, vt_ref, so_ref, g_ref, c0_ref, m0_ref, nw_ref, o_ref,
                tab_ref, u_ref, s_ref, st_ref, *, n_chunks):
    hp = HEADS_PER_STEP
    s_i = lax.broadcasted_iota(jnp.int32, (CHUNK, CHUNK), 0)
    t_i = lax.broadcasted_iota(jnp.int32, (CHUNK, CHUNK), 1)
    masks = (s_i <= t_i, s_i >= t_i)
    ones_rows = jnp.where(lax.broadcasted_iota(jnp.int32, (V_AUG - DV, CHUNK), 0) == 0,
                          1.0, 0.0).astype(BF16)
    kq_lane = lax.broadcasted_iota(jnp.int32, (CHUNK, hp * DK), 1)
    kq_row = lax.broadcasted_iota(jnp.int32, (hp * DK, CHUNK), 0)
    series = lambda tab8, d, j: tab8[SERIES_ROW0 + d * hp + j:SERIES_ROW0 + d * hp + j + 1]
    split_rows = N_SPLIT * GATES_PER_GROUP
    ones_split = jnp.ones((split_rows, CHUNK), F32)
    pick_r = jnp.bitwise_and(lax.broadcasted_iota(jnp.int32, (split_rows, N_DIR * CHUNK), 0),
                             GATES_PER_GROUP - 1)
    pick_d = jnp.where(lax.broadcasted_iota(jnp.int32, (split_rows, N_DIR * CHUNK), 1) >= CHUNK, 1, 0)
    pick_series = [jnp.where(pick_r == SERIES_ROW0 + pick_d * hp + j, 1.0, 0.0) for j in range(hp)]

    def chunk_rows(c):
        return pl.ds(pl.multiple_of(c * CHUNK, CHUNK), CHUNK)

    def values_t(c, j):
        return jnp.concatenate([vt_ref[0, c, j * DV:(j + 1) * DV, :], ones_rows], axis=0)

    grp = pl.program_id(1)

    @pl.when(grp == 0)
    def _():
        _gate_tables(g_ref, m0_ref, tab_ref, n_chunks)

    def block_diag_q(qt2):
        zero = jnp.zeros_like(qt2)
        return jnp.concatenate([jnp.where(kq_row < DK, qt2, zero),
                                jnp.where(kq_row >= DK, qt2, zero)], axis=1)

    def chunk_step(c, carry):
        k2 = k_ref[0, chunk_rows(c), :]
        zero = jnp.zeros_like(k2)
        k_bd = jnp.concatenate([jnp.where(kq_lane < DK, k2, zero),
                                jnp.where(kq_lane >= DK, k2, zero)], axis=0)
        wk8 = tab_ref[_WK, c, grp]
        vf = [values_t(c, j).astype(F32) for j in range(hp)]
        lhs = jnp.concatenate(
            [jnp.concatenate([(vf[j] * series(wk8, d, j)).astype(BF16) for j in range(hp)], axis=1)
             for d in range(N_DIR)], axis=0)
        u_ref[c] = jnp.dot(lhs, k_bd, preferred_element_type=F32)

        qk_t = jnp.dot(k2, block_diag_q(qt_ref[0, c]), preferred_element_type=F32)
        key_side = jnp.concatenate([tab_ref[_KEYS + x, c, grp] for x in range(N_SPLIT)] + [ones_split],
                                   axis=0).astype(BF16)
        for j in range(hp):
            query_side = jnp.concatenate(
                [pick_series[j]] +
                [jnp.concatenate([tab_ref[_COLM + x * N_SERIES + d * hp + j, c, grp] for d in range(N_DIR)],
                                 axis=1)
                 for x in range(N_SPLIT)], axis=0).astype(BF16)
            log_d = lax.dot_general(key_side, query_side, (((0,), (0,)), ((), ())),
                                    preferred_element_type=F32)
            for d in range(N_DIR):
                decay = jnp.exp2(jnp.where(masks[d], log_d[:, d * CHUNK:(d + 1) * CHUNK], -jnp.inf))
                st_ref[c, d, j] = (qk_t[:, j * CHUNK:(j + 1) * CHUNK] * decay).astype(BF16)
        return carry

    lax.fori_loop(0, n_chunks, chunk_step, 0, unroll=CHUNK_UNROLL)

    def scan_step(i, carry):
        cf, cb = carry
        back = n_chunks - 1 - i
        s_ref[i, :V_AUG] = cf.astype(BF16)
        s_ref[back, V_AUG:] = cb.astype(BF16)
        cf = tab_ref[_DECAY, i, grp][SERIES_ROW0:SERIES_ROW0 + 1] * cf + u_ref[i, :V_AUG]
        cb = tab_ref[_DECAY, back, grp][SERIES_ROW0 + hp:SERIES_ROW0 + hp + 1] * cb + u_ref[back, V_AUG:]
        return cf, cb

    lax.fori_loop(0, n_chunks, scan_step, (c0_ref[0, 0, 0], c0_ref[0, 1, 0]))

    def output_step(c, carry):
        rows = chunk_rows(c)
        qt2 = qt_ref[0, c].astype(F32)
        w_inter8, exp_neg8 = tab_ref[_WINTER, c, grp], tab_ref[_EXPNEG, c, grp]
        v_pair = jnp.concatenate([values_t(c, j) for j in range(hp)], axis=1)
        no_scores = jnp.zeros((CHUNK, CHUNK), BF16)
        h_t = [None] * hp
        for d in range(N_DIR):
            w_rows = jnp.where(kq_row < DK, series(w_inter8, d, 0), series(w_inter8, d, 1))
            rhs = jnp.concatenate(
                [jnp.concatenate([st_ref[c, d, 0], no_scores], axis=1),
                 jnp.concatenate([no_scores, st_ref[c, d, 1]], axis=1),
                 block_diag_q((qt2 * w_rows).astype(BF16))], axis=0)
            lhs = jnp.concatenate([v_pair, s_ref[c, d * V_AUG:(d + 1) * V_AUG]], axis=1)
            n_all = jnp.dot(lhs, rhs, preferred_element_type=F32)
            for j in range(hp):
                num = n_all[:, j * CHUNK:(j + 1) * CHUNK]
                r = 1.0 / jnp.maximum(jnp.abs(num[DV:DV + 1]), series(exp_neg8, d, j))
                part = num[:DV] * r
                h_t[j] = part if h_t[j] is None else h_t[j] + part
        for j in range(hp):
            hn_t = h_t[j] * lax.rsqrt(jnp.mean(h_t[j] * h_t[j], axis=0, keepdims=True) + EPS)
            hn = jnp.transpose(hn_t) * nw_ref[:, j * DV:(j + 1) * DV]
            o_ref[0, rows, j * DV:(j + 1) * DV] = (
                hn * so_ref[0, rows, j * DV:(j + 1) * DV].astype(F32)).astype(BF16)
        return carry

    lax.fori_loop(0, n_chunks, output_step, 0, unroll=CHUNK_UNROLL)


def mlstm_call(qt, k, vt, so, gates, c0, m0, norm_w):
    bsz, t, _ = k.shape
    n_chunks = t // CHUNK
    hp = HEADS_PER_STEP
    seq = lambda width: pl.BlockSpec((1, t, width), lambda i, j: (i, 0, j))
    seq_t = lambda rows: pl.BlockSpec((1, n_chunks, rows, CHUNK), lambda i, j: (i, 0, j, 0))
    return pl.pallas_call(
        functools.partial(_mlstm_body, n_chunks=n_chunks),
        grid=(bsz, N_HEAD_GROUPS),
        in_specs=[seq_t(hp * DK), seq(hp * DK), seq_t(hp * DV), seq(hp * DV),
                  pl.BlockSpec((1, n_chunks, N_GATE, CHUNK), lambda i, j: (i, 0, 0, 0)),
                  pl.BlockSpec((1, N_DIR, 1, V_AUG, hp * DK), lambda i, j: (i, 0, j, 0, 0)),
                  pl.BlockSpec((1, N_GATE, 1), lambda i, j: (i, 0, 0)),
                  pl.BlockSpec((1, hp * DV), lambda i, j: (0, j))],
        out_specs=seq(hp * DV),
        out_shape=jax.ShapeDtypeStruct((bsz, t, D_MLSTM), BF16),
        scratch_shapes=[pltpu.VMEM((N_TABLES, n_chunks, N_HEAD_GROUPS, GATES_PER_GROUP, CHUNK), F32),
                        pltpu.VMEM((n_chunks, N_DIR * V_AUG, hp * DK), F32),
                        pltpu.VMEM((n_chunks, N_DIR * V_AUG, hp * DK), BF16),
                        pltpu.VMEM((n_chunks, N_DIR, hp, CHUNK, CHUNK), BF16)],
        compiler_params=pltpu.CompilerParams(dimension_semantics=("arbitrary", "arbitrary"),
                                             vmem_limit_bytes=VMEM_LIMIT_BYTES),
        name="mlstm",
    )(qt, k, vt, so, gates, c0, m0, norm_w)


def _out_body(x_ref, hs_ref, ya_ref, gb_ref, g1_ref, sh2_ref, sc2_ref, g2_ref, nw2_ref, fnw_ref,
              wmo_ref, wo_ref, w1_ref, w2_ref, o_ref):
    dot = functools.partial(jnp.dot, preferred_element_type=F32)
    yb = dot(hs_ref[0], wmo_ref[...])
    merged = ya_ref[0].astype(F32) + gb_ref[0].astype(F32) * yb
    x1 = x_ref[0] + g1_ref[0] * dot(merged.astype(BF16), wo_ref[...])
    hm = _modulated_norm(x1, nw2_ref[...], sh2_ref[0], sc2_ref[0]).astype(BF16)
    step = D_FF // FF_SPLIT
    ff = None
    for s in range(FF_SPLIT):
        a = jnp.maximum(dot(hm, w1_ref[:, s * step:(s + 1) * step]), 0.0)
        part = dot((a * a).astype(BF16), w2_ref[s * step:(s + 1) * step, :])
        ff = part if ff is None else ff + part
    x2 = x1 + g2_ref[0] * ff
    y = x2 * lax.rsqrt(jnp.mean(x2 * x2, axis=-1, keepdims=True) + EPS)
    o_ref[0] = y * fnw_ref[...]


def out_call(x, hs, ya, gb, g1, sh2, sc2, g2, norm2_w, final_norm_w, w_mlstm_out, w_out, w_ff1, w_ff2):
    bsz, t, _ = x.shape
    tile = TOKEN_TILE
    tok = pl.BlockSpec((1, tile, D_MODEL), lambda i, j: (i, j, 0))
    row = pl.BlockSpec((1, 1, D_MODEL), lambda i, j: (i, 0, 0))
    return pl.pallas_call(
        _out_body,
        grid=(bsz, t // tile),
        in_specs=[tok, tok, tok, tok, row, row, row, row,
                  _resident((1, D_MODEL)), _resident((1, D_MODEL)),
                  _resident(w_mlstm_out.shape), _resident(w_out.shape),
                  _resident(w_ff1.shape), _resident(w_ff2.shape)],
        out_specs=tok,
        out_shape=jax.ShapeDtypeStruct((bsz, t, D_MODEL), F32),
        compiler_params=pltpu.CompilerParams(dimension_semantics=("arbitrary", "arbitrary"),
                                             vmem_limit_bytes=VMEM_LIMIT_BYTES),
        name="merge_out_mlp",
    )(x, hs, ya, gb, g1, sh2, sc2, g2, norm2_w, final_norm_w, w_mlstm_out, w_out, w_ff1, w_ff2)


def _group_gates(g):
    lead = g.shape[:-1]
    g = g.reshape(*lead, 2, N_DIR, N_HEAD_GROUPS, HEADS_PER_STEP)
    g = jnp.moveaxis(g, -2, -4)
    return g.reshape(*lead, N_GATE)


def _layer(x, ctx, mod, mod_ctx, norm1_w, w_in, b_in, conv_w, mlstm_norm_w, w_conv_out, w_mlstm_out,
           w_out, norm2_w, w_ff1, w_ff2, final_norm_w):
    bsz, t, _ = x.shape
    sh1, sc1, g1, sh2, sc2, g2 = [m.reshape(bsz, 1, D_MODEL) for m in jnp.split(mod, 6, axis=-1)]
    csh1, csc1 = mod_ctx[:, :D_MODEL], mod_ctx[:, D_MODEL:2 * D_MODEL]
    nw1 = norm1_w.reshape(1, D_MODEL)

    c0, m0 = ctx_call(ctx, csh1, csc1, nw1, w_in[:, :_REF_Q].astype(BF16), b_in[:_REF_Q].reshape(1, _REF_Q))

    w_t = jnp.concatenate([w_in[:, _REF_Q:_REF_O], w_in[:, _REF_V:_REF_IG],
                           _group_gates(w_in[:, _REF_IG:_REF_Q])], axis=1).T.astype(BF16)
    b_t = jnp.concatenate([b_in[_REF_Q:_REF_O], b_in[_REF_V:_REF_IG],
                           _group_gates(b_in[_REF_IG:_REF_Q])]).reshape(-1, 1)
    k, qt, vt, gates, so, ya, gb = inproj_call(
        x, sh1, sc1, nw1, w_in[:, _REF_K:_REF_V].astype(BF16), b_in[_REF_K:_REF_V].reshape(1, -1),
        w_in[:, _REF_O:].astype(BF16), b_in[_REF_O:].reshape(1, -1), w_t, b_t, conv_w,
        w_conv_out.astype(BF16))
    hs = mlstm_call(qt, k, vt, so, gates, c0, m0, mlstm_norm_w.reshape(1, D_MLSTM))
    return out_call(x, hs, ya, gb, g1, sh2, sc2, g2, norm2_w.reshape(1, D_MODEL),
                    final_norm_w.reshape(1, D_MODEL), w_mlstm_out.astype(BF16), w_out.astype(BF16),
                    w_ff1.astype(BF16), w_ff2.astype(BF16))


def kernel(x, c, ctx, c_ctx, w_mod, b_mod, norm1_w, w_in, b_in, conv_w, mlstm_norm_w, w_conv_out,
           w_mlstm_out, w_out, norm2_w, w_ff1, w_ff2, final_norm_w):
    depth = w_mod.shape[0]
    assert depth == 1, "the context stream is only advanced through its mLSTM state (single layer)"
    bsz = x.shape[0]
    cvecs = jnp.concatenate([c, c_ctx[None, :]], axis=0)
    mod_all = adaln_call(cvecs, w_mod[0], b_mod[0])
    return _layer(x, ctx, mod_all[:bsz], mod_all[bsz:], norm1_w[0], w_in[0], b_in[0], conv_w[0],
                  mlstm_norm_w[0], w_conv_out[0], w_mlstm_out[0], w_out[0], norm2_w[0], w_ff1[0],
                  w_ff2[0], final_norm_w)
```

```python
import functools

import jax
import jax.numpy as jnp
from jax import lax
from jax.experimental import pallas as pl
from jax.experimental.pallas import tpu as pltpu

D_MODEL = 1024
CTX_LEN = 256
GRID_W = 64
D_CONV = 1024
N_HEADS = 8
DK = 64
DV = 128
D_MLSTM = N_HEADS * DV
D_QK = N_HEADS * DK
D_FF = 4 * D_MODEL
N_DIR = 2
N_GATE = 2 * N_DIR * N_HEADS
EPS = 1e-6

CHUNK = 128
HEADS_PER_STEP = 2
assert HEADS_PER_STEP == 2 and HEADS_PER_STEP * DK == 128, "a head pair shares one 128-lane tile"
N_HEAD_GROUPS = N_HEADS // HEADS_PER_STEP
GATES_PER_GROUP = N_GATE // N_HEAD_GROUPS
BF16_SUBLANES = 16
V_AUG = DV + BF16_SUBLANES
CHUNK_UNROLL = 16
TOKEN_TILE = 512
FF_SPLIT = 4

VMEM_LIMIT_BYTES = 56 * 1024 * 1024

_REF_K, _REF_V, _REF_IG, _REF_FG, _REF_Q, _REF_O = 0, 512, 1536, 1552, 1568, 2080
D_IN = 8224
_O0, _XIN0, _GC0, _GB0, _MA0, _MB0, _G0 = 0, 1024, 2048, 3072, 4096, 5120, 6144

F32 = jnp.float32
BF16 = jnp.bfloat16


def _sigmoid(x):
    return 1.0 / (1.0 + jnp.exp(-x))


def _log_sigmoid(x):
    return jnp.minimum(x, 0.0) - jnp.log(1.0 + jnp.exp(-jnp.abs(x)))


def _split3(x):
    hi = x.astype(BF16)
    r1 = x - hi.astype(F32)
    mid = r1.astype(BF16)
    lo = (r1 - mid.astype(F32)).astype(BF16)
    return hi, mid, lo


def _cumsum_rows(tri, x):
    hi, mid, lo = _split3(x)
    dot = functools.partial(jnp.dot, preferred_element_type=F32)
    return dot(tri, hi) + dot(tri, mid) + dot(tri, lo)


def _cumsum_lanes(x, tri_t):
    hi, mid, lo = _split3(x)
    dot = functools.partial(jnp.dot, preferred_element_type=F32)
    return dot(hi, tri_t) + dot(mid, tri_t) + dot(lo, tri_t)


def _tri(n, dtype, lower):
    r = lax.broadcasted_iota(jnp.int32, (n, n), 0)
    c = lax.broadcasted_iota(jnp.int32, (n, n), 1)
    return jnp.where((r >= c) if lower else (r <= c), 1.0, 0.0).astype(dtype)


def _modulated_norm(x, norm_w, shift, scale):
    y = x * lax.rsqrt(jnp.mean(x * x, axis=-1, keepdims=True) + EPS)
    return (y * norm_w) * (1.0 + scale) + shift


def _resident(shape):
    nd = len(shape)
    return pl.BlockSpec(shape, lambda *_: (0,) * nd, pipeline_mode=pl.Buffered(1))


def _adaln_body(c_ref, w_ref, b_ref, o_ref):
    c = c_ref[...]
    s = c * _sigmoid(c)
    o_ref[...] = jnp.dot(s, w_ref[...], preferred_element_type=F32,
                         precision=lax.Precision.HIGHEST) + b_ref[...]


def adaln_call(cvecs, w_mod, b_mod):
    n = cvecs.shape[0]
    n_out = w_mod.shape[1]
    tile = 1024
    return pl.pallas_call(
        _adaln_body,
        grid=(n_out // tile,),
        in_specs=[pl.BlockSpec((n, D_MODEL), lambda j: (0, 0)),
                  pl.BlockSpec((D_MODEL, tile), lambda j: (0, j)),
                  pl.BlockSpec((1, tile), lambda j: (0, j))],
        out_specs=pl.BlockSpec((n, tile), lambda j: (0, j)),
        out_shape=jax.ShapeDtypeStruct((n, n_out), F32),
        name="adaln",
    )(cvecs, w_mod, b_mod.reshape(1, n_out))


def _split_cast_body(w_ref, state_ref, rest_ref):
    w = w_ref[...]
    state_ref[...] = w[:, :_REF_Q].astype(BF16)
    rest_ref[...] = w[:, _REF_O:].astype(BF16)


def split_cast_call(w_in):
    rows = 128
    widths = (_REF_Q, D_IN - _REF_O)
    return pl.pallas_call(
        _split_cast_body,
        grid=(D_MODEL // rows,),
        in_specs=[pl.BlockSpec((rows, D_IN), lambda i: (i, 0))],
        out_specs=[pl.BlockSpec((rows, w), lambda i: (i, 0)) for w in widths],
        out_shape=[jax.ShapeDtypeStruct((D_MODEL, w), BF16) for w in widths],
        name="w_in_split",
    )(w_in)


def _ctx_body(ctx_ref, sh_ref, sc_ref, nw_ref, w_ref, b_ref, c_ref, m_ref):
    x = ctx_ref[0]
    h = _modulated_norm(x, nw_ref[...], sh_ref[...], sc_ref[...]).astype(BF16)
    z = jnp.dot(h, w_ref[...], preferred_element_type=F32) + b_ref[...]
    k = z[:, _REF_K:_REF_V] * (DK ** -0.5)
    v = z[:, _REF_V:_REF_IG]
    ig = z[:, _REF_IG:_REF_FG]
    fg = z[:, _REF_FG:_REF_Q]
    lf = _log_sigmoid(fg)
    b = _cumsum_rows(_tri(CTX_LEN, BF16, lower=True), lf)
    tot = b[CTX_LEN - 1:CTX_LEN, :]
    e = b - lf
    g = jnp.concatenate([tot[:, :N_HEADS] - b[:, :N_HEADS] + ig[:, :N_HEADS],
                         e[:, N_HEADS:] + ig[:, N_HEADS:]], axis=1)
    m = jnp.maximum(tot, jnp.max(g, axis=0, keepdims=True))
    w = jnp.exp(g - m)
    pad = jnp.zeros((V_AUG - DV - 1, HEADS_PER_STEP * DK), F32)
    m_rows = []
    for grp in range(N_HEAD_GROUPS):
        m_rows.append(jnp.zeros((GATES_PER_GROUP - N_DIR * HEADS_PER_STEP, 1), F32))
        for d in range(N_DIR):
            c_parts, n_parts = [], []
            for j in range(HEADS_PER_STEP):
                hh = grp * HEADS_PER_STEP + j
                col = d * N_HEADS + hh
                wc = w[:, col:col + 1]
                kh = k[:, hh * DK:(hh + 1) * DK]
                wv = (wc * v[:, hh * DV:(hh + 1) * DV]).astype(BF16)
                c_parts.append(lax.dot_general(wv, kh.astype(BF16), (((0,), (0,)), ((), ())),
                                               preferred_element_type=F32))
                n_parts.append(jnp.sum(wc * kh, axis=0, keepdims=True))
                m_rows.append(m[:, col:col + 1])
            c_ref[0, d, grp] = jnp.concatenate(
                [jnp.concatenate(c_parts, axis=1), jnp.concatenate(n_parts, axis=1), pad], axis=0)
    m_ref[0] = jnp.concatenate(m_rows, axis=0)


def ctx_call(ctx, shift, scale, norm_w, w_state, b_state):
    bsz = ctx.shape[0]
    n_state = w_state.shape[1]
    return pl.pallas_call(
        _ctx_body,
        grid=(bsz,),
        in_specs=[pl.BlockSpec((1, CTX_LEN, D_MODEL), lambda i: (i, 0, 0)),
                  pl.BlockSpec((1, D_MODEL), lambda i: (0, 0)),
                  pl.BlockSpec((1, D_MODEL), lambda i: (0, 0)),
                  pl.BlockSpec((1, D_MODEL), lambda i: (0, 0)),
                  pl.BlockSpec((D_MODEL, n_state), lambda i: (0, 0)),
                  pl.BlockSpec((1, n_state), lambda i: (0, 0))],
        out_specs=[pl.BlockSpec((1, N_DIR, N_HEAD_GROUPS, V_AUG, HEADS_PER_STEP * DK),
                                lambda i: (i, 0, 0, 0, 0)),
                   pl.BlockSpec((1, N_GATE, 1), lambda i: (i, 0, 0))],
        out_shape=[jax.ShapeDtypeStruct((bsz, N_DIR, N_HEAD_GROUPS, V_AUG, HEADS_PER_STEP * DK), F32),
                   jax.ShapeDtypeStruct((bsz, N_GATE, 1), F32)],
        compiler_params=pltpu.CompilerParams(dimension_semantics=("arbitrary",),
                                             vmem_limit_bytes=VMEM_LIMIT_BYTES),
        name="ctx_state",
    )(ctx, shift, scale, norm_w, w_state, b_state)


def _inproj_body(x_ref, sh_ref, sc_ref, nw_ref, wk_ref, bk_ref, w_ref, b_ref, wt_ref, bt_ref, cw_ref, wco_ref,
                 k_ref, qt_ref, vt_ref, gt_ref, so_ref, ya_ref, gb_ref):
    x = x_ref[0]
    hb = _modulated_norm(x, nw_ref[...], sh_ref[0], sc_ref[0]).astype(BF16)
    tile = hb.shape[0]

    def seg(a, b):
        return jnp.dot(hb, w_ref[:, a:b], preferred_element_type=F32) + b_ref[:, a:b]

    k = jnp.dot(hb, wk_ref[:, _REF_K:_REF_V], preferred_element_type=F32) + bk_ref[...]
    k_ref[0] = (k * (DK ** -0.5)).astype(BF16)
    so_ref[0] = _sigmoid(seg(_O0, _XIN0)).astype(BF16)

    zt = lax.dot_general(wt_ref[...], hb, (((1,), (1,)), ((), ())),
                         preferred_element_type=F32) + bt_ref[...]
    for i in range(tile // CHUNK):
        lanes = slice(i * CHUNK, (i + 1) * CHUNK)
        qt_ref[0, i] = zt[:D_QK, lanes].astype(BF16)
        vt_ref[0, i] = zt[D_QK:D_QK + D_MLSTM, lanes].astype(BF16)
        gt_ref[0, i] = zt[D_QK + D_MLSTM:, lanes]

    u = seg(_XIN0, _GC0) * seg(_GC0, _GB0)
    col = jnp.bitwise_and(lax.broadcasted_iota(jnp.int32, (tile, 1), 0), GRID_W - 1)
    u_prev = jnp.where(col != 0, pltpu.roll(u, 1, axis=0), 0.0)
    u_next = jnp.where(col != GRID_W - 1, pltpu.roll(u, tile - 1, axis=0), 0.0)
    a = cw_ref[0:1, :] * u_prev + cw_ref[1:2, :] * u + cw_ref[2:3, :] * u_next
    ya = jnp.dot((seg(_GB0, _MA0) * a).astype(BF16), wco_ref[...], preferred_element_type=F32)
    ya_ref[0] = (_sigmoid(seg(_MA0, _MB0)) * ya).astype(BF16)
    gb_ref[0] = _sigmoid(seg(_MB0, _G0)).astype(BF16)


def inproj_call(x, shift, scale, norm_w, w_k, b_k, w_rest, b_rest, w_t, b_t, conv_w, w_conv_out):
    bsz, t, _ = x.shape
    tile = TOKEN_TILE
    tok = lambda width: pl.BlockSpec((1, tile, width), lambda i, j: (i, j, 0))
    tok_t = lambda rows: pl.BlockSpec((1, tile // CHUNK, rows, CHUNK), lambda i, j: (i, j, 0, 0))
    row = pl.BlockSpec((1, 1, D_MODEL), lambda i, j: (i, 0, 0))
    seq = lambda width: jax.ShapeDtypeStruct((bsz, t, width), BF16)
    seq_t = lambda rows, dt: jax.ShapeDtypeStruct((bsz, t // CHUNK, rows, CHUNK), dt)
    resident = [norm_w, w_k, b_k, w_rest, b_rest, w_t, b_t, conv_w, w_conv_out]
    return pl.pallas_call(
        _inproj_body,
        grid=(bsz, t // tile),
        in_specs=[tok(D_MODEL), row, row] + [_resident(a.shape) for a in resident],
        out_specs=[tok(D_QK), tok_t(D_QK), tok_t(D_MLSTM), tok_t(N_GATE), tok(D_MLSTM),
                   tok(D_MODEL), tok(D_MODEL)],
        out_shape=[seq(D_QK), seq_t(D_QK, BF16), seq_t(D_MLSTM, BF16), seq_t(N_GATE, F32), seq(D_MLSTM),
                   seq(D_MODEL), seq(D_MODEL)],
        compiler_params=pltpu.CompilerParams(dimension_semantics=("arbitrary", "arbitrary"),
                                             vmem_limit_bytes=VMEM_LIMIT_BYTES),
        name="inproj_conv",
    )(x, shift, scale, *resident)


SERIES_ROW0 = GATES_PER_GROUP // 2
N_SERIES = N_DIR * HEADS_PER_STEP
N_SPLIT = 3
LOG2E = 1.4426950408889634
_WK, _DECAY, _WINTER, _EXPNEG, _KEYS = 0, 1, 2, 3, 4
_COLM = _KEYS + N_SPLIT
N_TABLES = _COLM + N_SPLIT * N_SERIES


def _running_max_rows(x, reverse):
    n = x.shape[0]
    row = lax.broadcasted_iota(jnp.int32, x.shape, 0)
    shift = 1
    while shift < n:
        if reverse:
            moved = jnp.where(row < n - shift, pltpu.roll(x, n - shift, axis=0), -jnp.inf)
        else:
            moved = jnp.where(row >= shift, pltpu.roll(x, shift, axis=0), -jnp.inf)
        x = jnp.maximum(x, moved)
        shift *= 2
    return x


def _gate_tables(g_ref, m0_ref, tab_ref, n_chunks):
    step_rows = N_HEAD_GROUPS * GATES_PER_GROUP
    rows = n_chunks * step_rows
    g = g_ref[0].reshape(rows, CHUNK)
    lf = _log_sigmoid(g)
    b = _cumsum_lanes(lf, _tri(CHUNK, BF16, lower=False))
    ig = pltpu.roll(g, SERIES_ROW0, axis=0)
    in_tile = lambda idx: jnp.bitwise_and(idx, GATES_PER_GROUP - 1)
    r_in_tile = in_tile(lax.broadcasted_iota(jnp.int32, (rows, 1), 0))
    fwd = r_in_tile < SERIES_ROW0 + HEADS_PER_STEP
    e = b - lf
    tot = b[:, CHUNK - 1:CHUNK]
    col = jnp.where(fwd, b, -e)
    key = ig + jnp.where(fwd, -b, e)
    g_end = jnp.where(fwd, tot + key, key)
    g_max = jnp.max(g_end, axis=1, keepdims=True)

    fwd_step = fwd[:step_rows]
    tile = lambda a, c: a[c * step_rows:(c + 1) * step_rows]
    m = m0_ref[0]
    m_in_steps, m_out_steps = [], []
    for i in range(n_chunks):
        back = n_chunks - 1 - i
        m_in_steps.append(m)
        m = jnp.maximum(jnp.where(fwd_step, tile(tot, i), tile(tot, back)) + m,
                        jnp.where(fwd_step, tile(g_max, i), tile(g_max, back)))
        m_out_steps.append(m)
    by_chunk = lambda steps: jnp.concatenate(
        [jnp.where(fwd_step, steps[c], steps[n_chunks - 1 - c]) for c in range(n_chunks)], axis=0)
    m_in, m_out = by_chunk(m_in_steps), by_chunk(m_out_steps)

    decay = jnp.broadcast_to(jnp.exp(tot + m_in - m_out), (rows, CHUNK))
    lane = lax.broadcasted_iota(jnp.int32, (rows, CHUNK), 1)
    inter = jnp.where(fwd, col, tot + col) + m_in
    key_t = jnp.transpose(key)
    fwd_lane = in_tile(lax.broadcasted_iota(jnp.int32, (1, rows), 1)) < SERIES_ROW0 + HEADS_PER_STEP
    key_max = jnp.transpose(jnp.where(fwd_lane, _running_max_rows(key_t, reverse=False),
                                      _running_max_rows(key_t, reverse=True)))
    m_t = jnp.maximum(col + key_max, inter)
    tables = {_WK: jnp.exp(g_end - m_out),
              _DECAY: jnp.where(lane < DK, decay, pltpu.roll(decay, rows - 1, axis=0)),
              _WINTER: jnp.exp(inter - m_t), _EXPNEG: jnp.exp(-m_t)}
    for x, part in enumerate(_split3(key * LOG2E)):
        tables[_KEYS + x] = part.astype(F32)
    for x, part in enumerate(_split3((col - m_t) * LOG2E)):
        for sidx in range(N_SERIES):
            tables[_COLM + x * N_SERIES + sidx] = jnp.where(r_in_tile == SERIES_ROW0 + sidx,
                                                            part.astype(F32), 0.0)
    for idx, a in tables.items():
        tab_ref[idx] = a.reshape(n_chunks, N_HEAD_GROUPS, GATES_PER_GROUP, CHUNK)


def _mlstm_body(qt_ref, k_ref, vt_ref, so_ref, g_ref, c0_ref, m0_ref, nw_ref, o_ref,
                tab_ref, u_ref, s_ref, st_ref, *, n_chunks):
    hp = HEADS_PER_STEP
    s_i = lax.broadcasted_iota(jnp.int32, (CHUNK, CHUNK), 0)
    t_i = lax.broadcasted_iota(jnp.int32, (CHUNK, CHUNK), 1)
    masks = (s_i <= t_i, s_i >= t_i)
    ones_rows = jnp.where(lax.broadcasted_iota(jnp.int32, (V_AUG - DV, CHUNK), 0) == 0,
                          1.0, 0.0).astype(BF16)
    kq_lane = lax.broadcasted_iota(jnp.int32, (CHUNK, hp * DK), 1)
    kq_row = lax.broadcasted_iota(jnp.int32, (hp * DK, CHUNK), 0)
    state_lane = lax.broadcasted_iota(jnp.int32, (V_AUG, hp * DK), 1)
    series = lambda tab8, d, j: tab8[SERIES_ROW0 + d * hp + j:SERIES_ROW0 + d * hp + j + 1]
    split_rows = N_SPLIT * GATES_PER_GROUP
    ones_split = jnp.ones((split_rows, CHUNK), F32)
    pick_r = jnp.bitwise_and(lax.broadcasted_iota(jnp.int32, (split_rows, N_DIR * CHUNK), 0),
                             GATES_PER_GROUP - 1)
    pick_d = jnp.where(lax.broadcasted_iota(jnp.int32, (split_rows, N_DIR * CHUNK), 1) >= CHUNK, 1, 0)
    pick_series = [jnp.where(pick_r == SERIES_ROW0 + pick_d * hp + j, 1.0, 0.0) for j in range(hp)]

    def chunk_rows(c):
        return pl.ds(pl.multiple_of(c * CHUNK, CHUNK), CHUNK)

    def values_t(c, j):
        return jnp.concatenate([vt_ref[0, c, j * DV:(j + 1) * DV, :], ones_rows], axis=0)

    grp = pl.program_id(1)

    @pl.when(grp == 0)
    def _():
        _gate_tables(g_ref, m0_ref, tab_ref, n_chunks)

    def block_diag_q(qt2):
        zero = jnp.zeros_like(qt2)
        return jnp.concatenate([jnp.where(kq_row < DK, qt2, zero),
                                jnp.where(kq_row >= DK, qt2, zero)], axis=1)

    def chunk_step(c, carry):
        k2 = k_ref[0, chunk_rows(c), :]
        zero = jnp.zeros_like(k2)
        k_bd = jnp.concatenate([jnp.where(kq_lane < DK, k2, zero),
                                jnp.where(kq_lane >= DK, k2, zero)], axis=0)
        wk8 = tab_ref[_WK, c, grp]
        vf = [values_t(c, j).astype(F32) for j in range(hp)]
        lhs = jnp.concatenate(
            [jnp.concatenate([(vf[j] * series(wk8, d, j)).astype(BF16) for j in range(hp)], axis=1)
             for d in range(N_DIR)], axis=0)
        u_ref[c] = jnp.dot(lhs, k_bd, preferred_element_type=F32)

        qk_t = jnp.dot(k2, block_diag_q(qt_ref[0, c]), preferred_element_type=F32)
        key_side = jnp.concatenate([tab_ref[_KEYS + x, c, grp] for x in range(N_SPLIT)] + [ones_split],
                                   axis=0).astype(BF16)
        for j in range(hp):
            query_side = jnp.concatenate(
                [pick_series[j]] +
                [jnp.concatenate([tab_ref[_COLM + x * N_SERIES + d * hp + j, c, grp] for d in range(N_DIR)],
                                 axis=1)
                 for x in range(N_SPLIT)], axis=0).astype(BF16)
            log_d = lax.dot_general(key_side, query_side, (((0,), (0,)), ((), ())),
                                    preferred_element_type=F32)
            for d in range(N_DIR):
                decay = jnp.exp2(jnp.where(masks[d], log_d[:, d * CHUNK:(d + 1) * CHUNK], -jnp.inf))
                st_ref[c, d, j] = (qk_t[:, j * CHUNK:(j + 1) * CHUNK] * decay).astype(BF16)
        return carry

    lax.fori_loop(0, n_chunks, chunk_step, 0, unroll=CHUNK_UNROLL)

    def scan_step(i, carry):
        cf, cb = carry
        back = n_chunks - 1 - i
        s_ref[i, 0, 0] = cf.astype(BF16)
        s_ref[i, 0, 1] = pltpu.roll(cf, DK, axis=1).astype(BF16)
        s_ref[back, 1, 0] = cb.astype(BF16)
        s_ref[back, 1, 1] = pltpu.roll(cb, DK, axis=1).astype(BF16)
        cf = tab_ref[_DECAY, i, grp][SERIES_ROW0:SERIES_ROW0 + 1] * cf + u_ref[i, :V_AUG]
        cb = tab_ref[_DECAY, back, grp][SERIES_ROW0 + hp:SERIES_ROW0 + hp + 1] * cb + u_ref[back, V_AUG:]
        return cf, cb

    lax.fori_loop(0, n_chunks, scan_step, (c0_ref[0, 0, 0], c0_ref[0, 1, 0]), unroll=CHUNK_UNROLL)

    def output_step(c, carry):
        rows = chunk_rows(c)
        qt2 = qt_ref[0, c].astype(F32)
        w_inter8, exp_neg8 = tab_ref[_WINTER, c, grp], tab_ref[_EXPNEG, c, grp]
        no_query = jnp.zeros((DK, CHUNK), BF16)
        for j in range(hp):
            state = jnp.where(state_lane < DK, s_ref[c, 0, j], s_ref[c, 1, 1 - j])
            q_h = qt2[j * DK:(j + 1) * DK]
            qw = [(q_h * series(w_inter8, d, j)).astype(BF16) for d in range(N_DIR)]
            rhs = jnp.concatenate(
                [jnp.concatenate([st_ref[c, d, j] for d in range(N_DIR)], axis=1),
                 jnp.concatenate([qw[0], no_query], axis=1),
                 jnp.concatenate([no_query, qw[1]], axis=1)], axis=0)
            lhs = jnp.concatenate([values_t(c, j), state], axis=1)
            n_all = jnp.dot(lhs, rhs, preferred_element_type=F32)
            h_t = None
            for d in range(N_DIR):
                num = n_all[:, d * CHUNK:(d + 1) * CHUNK]
                r = 1.0 / jnp.maximum(jnp.abs(num[DV:DV + 1]), series(exp_neg8, d, j))
                part = num[:DV] * r
                h_t = part if h_t is None else h_t + part
            hn_t = h_t * lax.rsqrt(jnp.mean(h_t * h_t, axis=0, keepdims=True) + EPS)
            hn = jnp.transpose(hn_t) * nw_ref[:, j * DV:(j + 1) * DV]
            o_ref[0, rows, j * DV:(j + 1) * DV] = (
                hn * so_ref[0, rows, j * DV:(j + 1) * DV].astype(F32)).astype(BF16)
        return carry

    lax.fori_loop(0, n_chunks, output_step, 0, unroll=CHUNK_UNROLL)


def mlstm_call(qt, k, vt, so, gates, c0, m0, norm_w):
    bsz, t, _ = k.shape
    n_chunks = t // CHUNK
    hp = HEADS_PER_STEP
    seq = lambda width: pl.BlockSpec((1, t, width), lambda i, j: (i, 0, j))
    seq_t = lambda rows: pl.BlockSpec((1, n_chunks, rows, CHUNK), lambda i, j: (i, 0, j, 0))
    return pl.pallas_call(
        functools.partial(_mlstm_body, n_chunks=n_chunks),
        grid=(bsz, N_HEAD_GROUPS),
        in_specs=[seq_t(hp * DK), seq(hp * DK), seq_t(hp * DV), seq(hp * DV),
                  pl.BlockSpec((1, n_chunks, N_GATE, CHUNK), lambda i, j: (i, 0, 0, 0)),
                  pl.BlockSpec((1, N_DIR, 1, V_AUG, hp * DK), lambda i, j: (i, 0, j, 0, 0)),
                  pl.BlockSpec((1, N_GATE, 1), lambda i, j: (i, 0, 0)),
                  pl.BlockSpec((1, hp * DV), lambda i, j: (0, j))],
        out_specs=seq(hp * DV),
        out_shape=jax.ShapeDtypeStruct((bsz, t, D_MLSTM), BF16),
        scratch_shapes=[pltpu.VMEM((N_TABLES, n_chunks, N_HEAD_GROUPS, GATES_PER_GROUP, CHUNK), F32),
                        pltpu.VMEM((n_chunks, N_DIR * V_AUG, hp * DK), F32),
                        pltpu.VMEM((n_chunks, N_DIR, 2, V_AUG, hp * DK), BF16),
                        pltpu.VMEM((n_chunks, N_DIR, hp, CHUNK, CHUNK), BF16)],
        compiler_params=pltpu.CompilerParams(dimension_semantics=("arbitrary", "arbitrary"),
                                             vmem_limit_bytes=VMEM_LIMIT_BYTES),
        name="mlstm",
    )(qt, k, vt, so, gates, c0, m0, norm_w)


def _out_body(x_ref, hs_ref, ya_ref, gb_ref, g1_ref, sh2_ref, sc2_ref, g2_ref, nw2_ref, fnw_ref,
              wmo_ref, wo_ref, w1_ref, w2_ref, o_ref):
    dot = functools.partial(jnp.dot, preferred_element_type=F32)
    yb = dot(hs_ref[0], wmo_ref[...])
    merged = ya_ref[0].astype(F32) + gb_ref[0].astype(F32) * yb
    x1 = x_ref[0] + g1_ref[0] * dot(merged.astype(BF16), wo_ref[...])
    hm = _modulated_norm(x1, nw2_ref[...], sh2_ref[0], sc2_ref[0]).astype(BF16)
    step = D_FF // FF_SPLIT
    ff = None
    for s in range(FF_SPLIT):
        a = jnp.maximum(dot(hm, w1_ref[:, s * step:(s + 1) * step]), 0.0)
        part = dot((a * a).astype(BF16), w2_ref[s * step:(s + 1) * step, :])
        ff = part if ff is None else ff + part
    x2 = x1 + g2_ref[0] * ff
    y = x2 * lax.rsqrt(jnp.mean(x2 * x2, axis=-1, keepdims=True) + EPS)
    o_ref[0] = y * fnw_ref[...]


def out_call(x, hs, ya, gb, g1, sh2, sc2, g2, norm2_w, final_norm_w, w_mlstm_out, w_out, w_ff1, w_ff2):
    bsz, t, _ = x.shape
    tile = TOKEN_TILE
    tok = pl.BlockSpec((1, tile, D_MODEL), lambda i, j: (i, j, 0))
    row = pl.BlockSpec((1, 1, D_MODEL), lambda i, j: (i, 0, 0))
    return pl.pallas_call(
        _out_body,
        grid=(bsz, t // tile),
        in_specs=[tok, tok, tok, tok, row, row, row, row,
                  _resident((1, D_MODEL)), _resident((1, D_MODEL)),
                  _resident(w_mlstm_out.shape), _resident(w_out.shape),
                  _resident(w_ff1.shape), _resident(w_ff2.shape)],
        out_specs=tok,
        out_shape=jax.ShapeDtypeStruct((bsz, t, D_MODEL), F32),
        compiler_params=pltpu.CompilerParams(dimension_semantics=("arbitrary", "arbitrary"),
                                             vmem_limit_bytes=VMEM_LIMIT_BYTES),
        name="merge_out_mlp",
    )(x, hs, ya, gb, g1, sh2, sc2, g2, norm2_w, final_norm_w, w_mlstm_out, w_out, w_ff1, w_ff2)


def _group_gates(g):
    lead = g.shape[:-1]
    g = g.reshape(*lead, 2, N_DIR, N_HEAD_GROUPS, HEADS_PER_STEP)
    g = jnp.moveaxis(g, -2, -4)
    return g.reshape(*lead, N_GATE)


def _layer(x, ctx, mod, mod_ctx, norm1_w, w_in, b_in, conv_w, mlstm_norm_w, w_conv_out, w_mlstm_out,
           w_out, norm2_w, w_ff1, w_ff2, final_norm_w):
    bsz, t, _ = x.shape
    sh1, sc1, g1, sh2, sc2, g2 = [m.reshape(bsz, 1, D_MODEL) for m in jnp.split(mod, 6, axis=-1)]
    csh1, csc1 = mod_ctx[:, :D_MODEL], mod_ctx[:, D_MODEL:2 * D_MODEL]
    nw1 = norm1_w.reshape(1, D_MODEL)

    w_state, w_rest = split_cast_call(w_in)
    c0, m0 = ctx_call(ctx, csh1, csc1, nw1, w_state, b_in[:_REF_Q].reshape(1, _REF_Q))

    w_t = jnp.concatenate([w_in[:, _REF_Q:_REF_O], w_in[:, _REF_V:_REF_IG],
                           _group_gates(w_in[:, _REF_IG:_REF_Q])], axis=1).T.astype(BF16)
    b_t = jnp.concatenate([b_in[_REF_Q:_REF_O], b_in[_REF_V:_REF_IG],
                           _group_gates(b_in[_REF_IG:_REF_Q])]).reshape(-1, 1)
    k, qt, vt, gates, so, ya, gb = inproj_call(
        x, sh1, sc1, nw1, w_state, b_in[_REF_K:_REF_V].reshape(1, -1),
        w_rest, b_in[_REF_O:].reshape(1, -1), w_t, b_t, conv_w,
        w_conv_out.astype(BF16))
    hs = mlstm_call(qt, k, vt, so, gates, c0, m0, mlstm_norm_w.reshape(1, D_MLSTM))
    return out_call(x, hs, ya, gb, g1, sh2, sc2, g2, norm2_w.reshape(1, D_MODEL),
                    final_norm_w.reshape(1, D_MODEL), w_mlstm_out.astype(BF16), w_out.astype(BF16),
                    w_ff1.astype(BF16), w_ff2.astype(BF16))


def kernel(x, c, ctx, c_ctx, w_mod, b_mod, norm1_w, w_in, b_in, conv_w, mlstm_norm_w, w_conv_out,
           w_mlstm_out, w_out, norm2_w, w_ff1, w_ff2, final_norm_w):
    depth = w_mod.shape[0]
    assert depth == 1, "the context stream is only advanced through its mLSTM state (single layer)"
    bsz = x.shape[0]
    cvecs = jnp.concatenate([c, c_ctx[None, :]], axis=0)
    mod_all = adaln_call(cvecs, w_mod[0], b_mod[0])
    return _layer(x, ctx, mod_all[:bsz], mod_all[bsz:], norm1_w[0], w_in[0], b_in[0], conv_w[0],
                  mlstm_norm_w[0], w_conv_out[0], w_mlstm_out[0], w_out[0], norm2_w[0], w_ff1[0],
                  w_ff2[0], final_norm_w)
```

```python
import functools

import jax
import jax.numpy as jnp
from jax import lax
from jax.experimental import pallas as pl
from jax.experimental.pallas import tpu as pltpu

D_MODEL = 1024
CTX_LEN = 256
GRID_W = 64
D_CONV = 1024
N_HEADS = 8
DK = 64
DV = 128
D_MLSTM = N_HEADS * DV
D_QK = N_HEADS * DK
D_FF = 4 * D_MODEL
N_DIR = 2
N_GATE = 2 * N_DIR * N_HEADS
EPS = 1e-6

CHUNK = 128
HEADS_PER_STEP = 2
assert HEADS_PER_STEP == 2 and HEADS_PER_STEP * DK == 128, "a head pair shares one 128-lane tile"
N_HEAD_GROUPS = N_HEADS // HEADS_PER_STEP
GATES_PER_GROUP = N_GATE // N_HEAD_GROUPS
SUBLANES = 8
BF16_SUBLANES = 16
V_AUG = DV + BF16_SUBLANES
CHUNK_UNROLL = 16
TOKEN_TILE = 512
FF_SPLIT = 4

VMEM_LIMIT_BYTES = 56 * 1024 * 1024

_REF_K, _REF_V, _REF_IG, _REF_FG, _REF_Q, _REF_O = 0, 512, 1536, 1552, 1568, 2080
D_IN = 8224
_K0, _O0, _XIN0, _GC0, _GB0, _MA0, _MB0, _G0 = 0, 512, 1536, 2560, 3584, 4608, 5632, 6656

F32 = jnp.float32
BF16 = jnp.bfloat16


def _sigmoid(x):
    return 1.0 / (1.0 + jnp.exp(-x))


def _log_sigmoid(x):
    return jnp.minimum(x, 0.0) - jnp.log(1.0 + jnp.exp(-jnp.abs(x)))


def _split3(x):
    hi = x.astype(BF16)
    r1 = x - hi.astype(F32)
    mid = r1.astype(BF16)
    lo = (r1 - mid.astype(F32)).astype(BF16)
    return hi, mid, lo


def _cumsum_lanes(x, tri_t):
    hi, mid, lo = _split3(x)
    dot = functools.partial(jnp.dot, preferred_element_type=F32)
    return dot(hi, tri_t) + dot(mid, tri_t) + dot(lo, tri_t)


def _tri(n, dtype, lower):
    r = lax.broadcasted_iota(jnp.int32, (n, n), 0)
    c = lax.broadcasted_iota(jnp.int32, (n, n), 1)
    return jnp.where((r >= c) if lower else (r <= c), 1.0, 0.0).astype(dtype)


def _modulated_norm(x, norm_w, shift, scale):
    y = x * lax.rsqrt(jnp.mean(x * x, axis=-1, keepdims=True) + EPS)
    return (y * norm_w) * (1.0 + scale) + shift


def _resident(shape):
    nd = len(shape)
    return pl.BlockSpec(shape, lambda *_: (0,) * nd, pipeline_mode=pl.Buffered(1))


def _adaln_body(c_ref, w_ref, b_ref, o_ref):
    c = c_ref[...]
    s = c * _sigmoid(c)
    o_ref[...] = jnp.dot(s, w_ref[...], preferred_element_type=F32,
                         precision=lax.Precision.HIGHEST) + b_ref[...]


def adaln_call(cvecs, w_mod, b_mod):
    n = cvecs.shape[0]
    n_out = w_mod.shape[1]
    tile = 1024
    return pl.pallas_call(
        _adaln_body,
        grid=(n_out // tile,),
        in_specs=[pl.BlockSpec((n, D_MODEL), lambda j: (0, 0)),
                  pl.BlockSpec((D_MODEL, tile), lambda j: (0, j)),
                  pl.BlockSpec((1, tile), lambda j: (0, j))],
        out_specs=pl.BlockSpec((n, tile), lambda j: (0, j)),
        out_shape=jax.ShapeDtypeStruct((n, n_out), F32),
        name="adaln",
    )(cvecs, w_mod, b_mod.reshape(1, n_out))


W_BLOCK = 512


def _transpose_cast_body(w_ref, o_ref):
    o_ref[...] = jnp.transpose(w_ref[...]).astype(BF16)


def _row_cast_body(w_ref, o_ref):
    o_ref[...] = w_ref[...].astype(BF16)


def projection_weights(w_in_t):
    n_rest = (D_IN - _REF_O) // W_BLOCK
    blk = W_BLOCK // SUBLANES
    w_n = pl.pallas_call(
        _transpose_cast_body,
        grid=(1 + n_rest,),
        in_specs=[pl.BlockSpec((pl.Element(W_BLOCK), pl.Element(D_MODEL)),
                               lambda i: (SUBLANES * jnp.where(i == 0, _REF_K // SUBLANES,
                                                               _REF_O // SUBLANES + (i - 1) * blk), 0))],
        out_specs=pl.BlockSpec((D_MODEL, W_BLOCK), lambda i: (0, i)),
        out_shape=jax.ShapeDtypeStruct((D_MODEL, (1 + n_rest) * W_BLOCK), BF16),
        name="w_in_normal",
    )(w_in_t)
    n_v = D_MLSTM // W_BLOCK
    w_vq_t = pl.pallas_call(
        _row_cast_body,
        grid=(n_v + D_QK // W_BLOCK,),
        in_specs=[pl.BlockSpec((pl.Element(W_BLOCK), pl.Element(D_MODEL)),
                               lambda i: (SUBLANES * jnp.where(i < n_v, _REF_V // SUBLANES + i * blk,
                                                               _REF_Q // SUBLANES + (i - n_v) * blk), 0))],
        out_specs=pl.BlockSpec((W_BLOCK, D_MODEL), lambda i: (i, 0)),
        out_shape=jax.ShapeDtypeStruct((D_MLSTM + D_QK, D_MODEL), BF16),
        name="w_in_transposed",
    )(w_in_t)
    return w_n, w_vq_t


def _ctx_body(ctx_ref, sh_ref, sc_ref, nw_ref, wk_ref, bk_ref, wv_ref, bv_ref, wg_ref, bg_ref, c_ref, m_ref):
    hp = HEADS_PER_STEP
    hb = _modulated_norm(ctx_ref[0], nw_ref[...], sh_ref[...], sc_ref[...]).astype(BF16)
    nt = functools.partial(lax.dot_general, dimension_numbers=(((1,), (1,)), ((), ())),
                           preferred_element_type=F32)
    k = ((jnp.dot(hb, wk_ref[...], preferred_element_type=F32) + bk_ref[...]) * (DK ** -0.5)).astype(BF16)
    vt = (nt(wv_ref[...], hb) + bv_ref[...]).astype(BF16)
    g = nt(wg_ref[...], hb) + bg_ref[...]

    lf = _log_sigmoid(g)
    b = _cumsum_lanes(lf, _tri(CTX_LEN, BF16, lower=False))
    ig = pltpu.roll(g, GATES_PER_GROUP // 2, axis=0)
    r_in_tile = jnp.bitwise_and(lax.broadcasted_iota(jnp.int32, (N_GATE, 1), 0), GATES_PER_GROUP - 1)
    fwd = r_in_tile < GATES_PER_GROUP // 2 + hp
    tot = b[:, CTX_LEN - 1:CTX_LEN]
    g_end = ig + jnp.where(fwd, tot - b, b - lf)
    m = jnp.maximum(tot, jnp.max(g_end, axis=1, keepdims=True))
    m_ref[0] = m
    wk = jnp.exp(g_end - m)

    ones_rows = jnp.where(lax.broadcasted_iota(jnp.int32, (V_AUG - DV, CTX_LEN), 0) == 0, 1.0, 0.0)
    lane = lax.broadcasted_iota(jnp.int32, (CTX_LEN, hp * DK), 1)
    for grp in range(N_HEAD_GROUPS):
        k2 = k[:, grp * hp * DK:(grp + 1) * hp * DK]
        zero = jnp.zeros_like(k2)
        k_bd = jnp.concatenate([jnp.where(lane < DK, k2, zero), jnp.where(lane >= DK, k2, zero)], axis=0)
        vf = [jnp.concatenate([vt[(grp * hp + j) * DV:(grp * hp + j + 1) * DV].astype(F32), ones_rows], axis=0)
              for j in range(hp)]
        for d in range(N_DIR):
            row0 = grp * GATES_PER_GROUP + GATES_PER_GROUP // 2 + d * hp
            lhs = jnp.concatenate([(vf[j] * wk[row0 + j:row0 + j + 1]).astype(BF16) for j in range(hp)], axis=1)
            c_ref[0, d, grp] = jnp.dot(lhs, k_bd, preferred_element_type=F32)


def ctx_call(ctx, shift, scale, norm_w, w_n, b_k, w_vq_t, b_v, w_g_t, b_g):
    bsz = ctx.shape[0]
    const = lambda shape: pl.BlockSpec(shape, lambda i: (0,) * len(shape))
    return pl.pallas_call(
        _ctx_body,
        grid=(bsz,),
        in_specs=[pl.BlockSpec((1, CTX_LEN, D_MODEL), lambda i: (i, 0, 0)),
                  const((1, D_MODEL)), const((1, D_MODEL)), const((1, D_MODEL)),
                  const((D_MODEL, D_QK)), const((1, D_QK)),
                  const((D_MLSTM, D_MODEL)), const((D_MLSTM, 1)),
                  const((N_GATE, D_MODEL)), const((N_GATE, 1))],
        out_specs=[pl.BlockSpec((1, N_DIR, N_HEAD_GROUPS, V_AUG, HEADS_PER_STEP * DK),
                                lambda i: (i, 0, 0, 0, 0)),
                   pl.BlockSpec((1, N_GATE, 1), lambda i: (i, 0, 0))],
        out_shape=[jax.ShapeDtypeStruct((bsz, N_DIR, N_HEAD_GROUPS, V_AUG, HEADS_PER_STEP * DK), F32),
                   jax.ShapeDtypeStruct((bsz, N_GATE, 1), F32)],
        compiler_params=pltpu.CompilerParams(dimension_semantics=("arbitrary",),
                                             vmem_limit_bytes=VMEM_LIMIT_BYTES),
        name="ctx_state",
    )(ctx, shift, scale, norm_w, w_n, b_k, w_vq_t, b_v, w_g_t, b_g)


def _inproj_body(x_ref, sh_ref, sc_ref, nw_ref, w_ref, b_ref, wt_ref, bt_ref, wg_ref, bg_ref, cw_ref, wco_ref,
                 k_ref, qt_ref, vt_ref, gt_ref, so_ref, ya_ref, gb_ref):
    x = x_ref[0]
    hb = _modulated_norm(x, nw_ref[...], sh_ref[0], sc_ref[0]).astype(BF16)
    tile = hb.shape[0]

    def seg(a, b):
        return jnp.dot(hb, w_ref[:, a:b], preferred_element_type=F32) + b_ref[:, a:b]

    k_ref[0] = (seg(_K0, _O0) * (DK ** -0.5)).astype(BF16)
    so_ref[0] = _sigmoid(seg(_O0, _XIN0)).astype(BF16)

    nt = functools.partial(lax.dot_general, dimension_numbers=(((1,), (1,)), ((), ())),
                           preferred_element_type=F32)
    zt = nt(wt_ref[...], hb) + bt_ref[...]
    gt = nt(wg_ref[...], hb) + bg_ref[...]
    for i in range(tile // CHUNK):
        lanes = slice(i * CHUNK, (i + 1) * CHUNK)
        vt_ref[0, i] = zt[:D_MLSTM, lanes].astype(BF16)
        qt_ref[0, i] = zt[D_MLSTM:, lanes].astype(BF16)
        gt_ref[0, i] = gt[:, lanes]

    u = seg(_XIN0, _GC0) * seg(_GC0, _GB0)
    col = jnp.bitwise_and(lax.broadcasted_iota(jnp.int32, (tile, 1), 0), GRID_W - 1)
    u_prev = jnp.where(col != 0, pltpu.roll(u, 1, axis=0), 0.0)
    u_next = jnp.where(col != GRID_W - 1, pltpu.roll(u, tile - 1, axis=0), 0.0)
    a = cw_ref[0:1, :] * u_prev + cw_ref[1:2, :] * u + cw_ref[2:3, :] * u_next
    ya = jnp.dot((seg(_GB0, _MA0) * a).astype(BF16), wco_ref[...], preferred_element_type=F32)
    ya_ref[0] = (_sigmoid(seg(_MA0, _MB0)) * ya).astype(BF16)
    gb_ref[0] = _sigmoid(seg(_MB0, _G0)).astype(BF16)


def inproj_call(x, shift, scale, norm_w, w_n, b_n, w_vq_t, b_vq, w_g_t, b_g, conv_w, w_conv_out):
    bsz, t, _ = x.shape
    tile = TOKEN_TILE
    tok = lambda width: pl.BlockSpec((1, tile, width), lambda i, j: (i, j, 0))
    tok_t = lambda rows: pl.BlockSpec((1, tile // CHUNK, rows, CHUNK), lambda i, j: (i, j, 0, 0))
    row = pl.BlockSpec((1, 1, D_MODEL), lambda i, j: (i, 0, 0))
    seq = lambda width: jax.ShapeDtypeStruct((bsz, t, width), BF16)
    seq_t = lambda rows, dt: jax.ShapeDtypeStruct((bsz, t // CHUNK, rows, CHUNK), dt)
    resident = [norm_w, w_n, b_n, w_vq_t, b_vq, w_g_t, b_g, conv_w, w_conv_out]
    return pl.pallas_call(
        _inproj_body,
        grid=(bsz, t // tile),
        in_specs=[tok(D_MODEL), row, row] + [_resident(a.shape) for a in resident],
        out_specs=[tok(D_QK), tok_t(D_QK), tok_t(D_MLSTM), tok_t(N_GATE), tok(D_MLSTM),
                   tok(D_MODEL), tok(D_MODEL)],
        out_shape=[seq(D_QK), seq_t(D_QK, BF16), seq_t(D_MLSTM, BF16), seq_t(N_GATE, F32), seq(D_MLSTM),
                   seq(D_MODEL), seq(D_MODEL)],
        compiler_params=pltpu.CompilerParams(dimension_semantics=("arbitrary", "arbitrary"),
                                             vmem_limit_bytes=VMEM_LIMIT_BYTES),
        name="inproj_conv",
    )(x, shift, scale, *resident)


SERIES_ROW0 = GATES_PER_GROUP // 2
N_SERIES = N_DIR * HEADS_PER_STEP
N_SPLIT = 3
LOG2E = 1.4426950408889634
_WK, _DECAY, _WINTER, _EXPNEG, _KEYS = 0, 1, 2, 3, 4
_COLM = _KEYS + N_SPLIT
N_TABLES = _COLM + N_SPLIT * N_SERIES


def _running_max_rows(x, reverse):
    n = x.shape[0]
    row = lax.broadcasted_iota(jnp.int32, x.shape, 0)
    shift = 1
    while shift < n:
        if reverse:
            moved = jnp.where(row < n - shift, pltpu.roll(x, n - shift, axis=0), -jnp.inf)
        else:
            moved = jnp.where(row >= shift, pltpu.roll(x, shift, axis=0), -jnp.inf)
        x = jnp.maximum(x, moved)
        shift *= 2
    return x


def _gate_tables(g_ref, m0_ref, tab_ref, n_chunks):
    step_rows = N_HEAD_GROUPS * GATES_PER_GROUP
    rows = n_chunks * step_rows
    g = g_ref[0].reshape(rows, CHUNK)
    lf = _log_sigmoid(g)
    b = _cumsum_lanes(lf, _tri(CHUNK, BF16, lower=False))
    ig = pltpu.roll(g, SERIES_ROW0, axis=0)
    in_tile = lambda idx: jnp.bitwise_and(idx, GATES_PER_GROUP - 1)
    r_in_tile = in_tile(lax.broadcasted_iota(jnp.int32, (rows, 1), 0))
    fwd = r_in_tile < SERIES_ROW0 + HEADS_PER_STEP
    e = b - lf
    tot = b[:, CHUNK - 1:CHUNK]
    col = jnp.where(fwd, b, -e)
    key = ig + jnp.where(fwd, -b, e)
    g_end = jnp.where(fwd, tot + key, key)
    g_max = jnp.max(g_end, axis=1, keepdims=True)

    fwd_step = fwd[:step_rows]
    tile = lambda a, c: a[c * step_rows:(c + 1) * step_rows]
    m = m0_ref[0]
    m_in_steps, m_out_steps = [], []
    for i in range(n_chunks):
        back = n_chunks - 1 - i
        m_in_steps.append(m)
        m = jnp.maximum(jnp.where(fwd_step, tile(tot, i), tile(tot, back)) + m,
                        jnp.where(fwd_step, tile(g_max, i), tile(g_max, back)))
        m_out_steps.append(m)
    by_chunk = lambda steps: jnp.concatenate(
        [jnp.where(fwd_step, steps[c], steps[n_chunks - 1 - c]) for c in range(n_chunks)], axis=0)
    m_in, m_out = by_chunk(m_in_steps), by_chunk(m_out_steps)

    decay = jnp.broadcast_to(jnp.exp(tot + m_in - m_out), (rows, CHUNK))
    lane = lax.broadcasted_iota(jnp.int32, (rows, CHUNK), 1)
    inter = jnp.where(fwd, col, tot + col) + m_in
    key_t = jnp.transpose(key)
    fwd_lane = in_tile(lax.broadcasted_iota(jnp.int32, (1, rows), 1)) < SERIES_ROW0 + HEADS_PER_STEP
    key_max = jnp.transpose(jnp.where(fwd_lane, _running_max_rows(key_t, reverse=False),
                                      _running_max_rows(key_t, reverse=True)))
    m_t = jnp.maximum(col + key_max, inter)
    tables = {_WK: jnp.exp(g_end - m_out),
              _DECAY: jnp.where(lane < DK, decay, pltpu.roll(decay, rows - 1, axis=0)),
              _WINTER: jnp.exp(inter - m_t), _EXPNEG: jnp.exp(-m_t)}
    for x, part in enumerate(_split3(key * LOG2E)):
        tables[_KEYS + x] = part.astype(F32)
    for x, part in enumerate(_split3((col - m_t) * LOG2E)):
        for sidx in range(N_SERIES):
            tables[_COLM + x * N_SERIES + sidx] = jnp.where(r_in_tile == SERIES_ROW0 + sidx,
                                                            part.astype(F32), 0.0)
    for idx, a in tables.items():
        tab_ref[idx] = a.reshape(n_chunks, N_HEAD_GROUPS, GATES_PER_GROUP, CHUNK)


def _mlstm_body(qt_ref, k_ref, vt_ref, so_ref, g_ref, c0_ref, m0_ref, nw_ref, o_ref,
                tab_ref, u_ref, s_ref, st_ref, *, n_chunks):
    hp = HEADS_PER_STEP
    s_i = lax.broadcasted_iota(jnp.int32, (CHUNK, CHUNK), 0)
    t_i = lax.broadcasted_iota(jnp.int32, (CHUNK, CHUNK), 1)
    masks = (s_i <= t_i, s_i >= t_i)
    ones_rows = jnp.where(lax.broadcasted_iota(jnp.int32, (V_AUG - DV, CHUNK), 0) == 0,
                          1.0, 0.0).astype(BF16)
    kq_lane = lax.broadcasted_iota(jnp.int32, (CHUNK, hp * DK), 1)
    kq_row = lax.broadcasted_iota(jnp.int32, (hp * DK, CHUNK), 0)
    state_lane = lax.broadcasted_iota(jnp.int32, (V_AUG, hp * DK), 1)
    series = lambda tab8, d, j: tab8[SERIES_ROW0 + d * hp + j:SERIES_ROW0 + d * hp + j + 1]
    split_rows = N_SPLIT * GATES_PER_GROUP
    ones_split = jnp.ones((split_rows, CHUNK), F32)
    pick_r = jnp.bitwise_and(lax.broadcasted_iota(jnp.int32, (split_rows, N_DIR * CHUNK), 0),
                             GATES_PER_GROUP - 1)
    pick_d = jnp.where(lax.broadcasted_iota(jnp.int32, (split_rows, N_DIR * CHUNK), 1) >= CHUNK, 1, 0)
    pick_series = [jnp.where(pick_r == SERIES_ROW0 + pick_d * hp + j, 1.0, 0.0) for j in range(hp)]

    def chunk_rows(c):
        return pl.ds(pl.multiple_of(c * CHUNK, CHUNK), CHUNK)

    def values_t(c, j):
        return jnp.concatenate([vt_ref[0, c, j * DV:(j + 1) * DV, :], ones_rows], axis=0)

    grp = pl.program_id(1)

    @pl.when(grp == 0)
    def _():
        _gate_tables(g_ref, m0_ref, tab_ref, n_chunks)

    def block_diag_q(qt2):
        zero = jnp.zeros_like(qt2)
        return jnp.concatenate([jnp.where(kq_row < DK, qt2, zero),
                                jnp.where(kq_row >= DK, qt2, zero)], axis=1)

    def chunk_step(c, carry):
        k2 = k_ref[0, chunk_rows(c), :]
        zero = jnp.zeros_like(k2)
        k_bd = jnp.concatenate([jnp.where(kq_lane < DK, k2, zero),
                                jnp.where(kq_lane >= DK, k2, zero)], axis=0)
        wk8 = tab_ref[_WK, c, grp]
        vf = [values_t(c, j).astype(F32) for j in range(hp)]
        lhs = jnp.concatenate(
            [jnp.concatenate([(vf[j] * series(wk8, d, j)).astype(BF16) for j in range(hp)], axis=1)
             for d in range(N_DIR)], axis=0)
        u_ref[c] = jnp.dot(lhs, k_bd, preferred_element_type=F32)

        qk_t = jnp.dot(k2, block_diag_q(qt_ref[0, c]), preferred_element_type=F32)
        key_side = jnp.concatenate([tab_ref[_KEYS + x, c, grp] for x in range(N_SPLIT)] + [ones_split],
                                   axis=0).astype(BF16)
        for j in range(hp):
            query_side = jnp.concatenate(
                [pick_series[j]] +
                [jnp.concatenate([tab_ref[_COLM + x * N_SERIES + d * hp + j, c, grp] for d in range(N_DIR)],
                                 axis=1)
                 for x in range(N_SPLIT)], axis=0).astype(BF16)
            log_d = lax.dot_general(key_side, query_side, (((0,), (0,)), ((), ())),
                                    preferred_element_type=F32)
            for d in range(N_DIR):
                decay = jnp.exp2(jnp.where(masks[d], log_d[:, d * CHUNK:(d + 1) * CHUNK], -jnp.inf))
                st_ref[c, d, j] = (qk_t[:, j * CHUNK:(j + 1) * CHUNK] * decay).astype(BF16)
        return carry

    lax.fori_loop(0, n_chunks, chunk_step, 0, unroll=CHUNK_UNROLL)

    def scan_step(i, carry):
        cf, cb = carry
        back = n_chunks - 1 - i
        s_ref[i, 0, 0] = cf.astype(BF16)
        s_ref[i, 0, 1] = pltpu.roll(cf, DK, axis=1).astype(BF16)
        s_ref[back, 1, 0] = cb.astype(BF16)
        s_ref[back, 1, 1] = pltpu.roll(cb, DK, axis=1).astype(BF16)
        cf = tab_ref[_DECAY, i, grp][SERIES_ROW0:SERIES_ROW0 + 1] * cf + u_ref[i, :V_AUG]
        cb = tab_ref[_DECAY, back, grp][SERIES_ROW0 + hp:SERIES_ROW0 + hp + 1] * cb + u_ref[back, V_AUG:]
        return cf, cb

    lax.fori_loop(0, n_chunks, scan_step, (c0_ref[0, 0, 0], c0_ref[0, 1, 0]), unroll=CHUNK_UNROLL)

    def output_step(c, carry):
        rows = chunk_rows(c)
        qt2 = qt_ref[0, c].astype(F32)
        w_inter8, exp_neg8 = tab_ref[_WINTER, c, grp], tab_ref[_EXPNEG, c, grp]
        no_query = jnp.zeros((DK, CHUNK), BF16)
        for j in range(hp):
            state = jnp.where(state_lane < DK, s_ref[c, 0, j], s_ref[c, 1, 1 - j])
            q_h = qt2[j * DK:(j + 1) * DK]
            qw = [(q_h * series(w_inter8, d, j)).astype(BF16) for d in range(N_DIR)]
            rhs = jnp.concatenate(
                [jnp.concatenate([st_ref[c, d, j] for d in range(N_DIR)], axis=1),
                 jnp.concatenate([qw[0], no_query], axis=1),
                 jnp.concatenate([no_query, qw[1]], axis=1)], axis=0)
            lhs = jnp.concatenate([values_t(c, j), state], axis=1)
            n_all = jnp.dot(lhs, rhs, preferred_element_type=F32)
            h_t = None
            for d in range(N_DIR):
                num = n_all[:, d * CHUNK:(d + 1) * CHUNK]
                r = 1.0 / jnp.maximum(jnp.abs(num[DV:DV + 1]), series(exp_neg8, d, j))
                part = num[:DV] * r
                h_t = part if h_t is None else h_t + part
            hn_t = h_t * lax.rsqrt(jnp.mean(h_t * h_t, axis=0, keepdims=True) + EPS)
            hn = jnp.transpose(hn_t) * nw_ref[:, j * DV:(j + 1) * DV]
            o_ref[0, rows, j * DV:(j + 1) * DV] = (
                hn * so_ref[0, rows, j * DV:(j + 1) * DV].astype(F32)).astype(BF16)
        return carry

    lax.fori_loop(0, n_chunks, output_step, 0, unroll=CHUNK_UNROLL)


def mlstm_call(qt, k, vt, so, gates, c0, m0, norm_w):
    bsz, t, _ = k.shape
    n_chunks = t // CHUNK
    hp = HEADS_PER_STEP
    seq = lambda width: pl.BlockSpec((1, t, width), lambda i, j: (i, 0, j))
    seq_t = lambda rows: pl.BlockSpec((1, n_chunks, rows, CHUNK), lambda i, j: (i, 0, j, 0))
    return pl.pallas_call(
        functools.partial(_mlstm_body, n_chunks=n_chunks),
        grid=(bsz, N_HEAD_GROUPS),
        in_specs=[seq_t(hp * DK), seq(hp * DK), seq_t(hp * DV), seq(hp * DV),
                  pl.BlockSpec((1, n_chunks, N_GATE, CHUNK), lambda i, j: (i, 0, 0, 0)),
                  pl.BlockSpec((1, N_DIR, 1, V_AUG, hp * DK), lambda i, j: (i, 0, j, 0, 0)),
                  pl.BlockSpec((1, N_GATE, 1), lambda i, j: (i, 0, 0)),
                  pl.BlockSpec((1, hp * DV), lambda i, j: (0, j))],
        out_specs=seq(hp * DV),
        out_shape=jax.ShapeDtypeStruct((bsz, t, D_MLSTM), BF16),
        scratch_shapes=[pltpu.VMEM((N_TABLES, n_chunks, N_HEAD_GROUPS, GATES_PER_GROUP, CHUNK), F32),
                        pltpu.VMEM((n_chunks, N_DIR * V_AUG, hp * DK), F32),
                        pltpu.VMEM((n_chunks, N_DIR, 2, V_AUG, hp * DK), BF16),
                        pltpu.VMEM((n_chunks, N_DIR, hp, CHUNK, CHUNK), BF16)],
        compiler_params=pltpu.CompilerParams(dimension_semantics=("arbitrary", "arbitrary"),
                                             vmem_limit_bytes=VMEM_LIMIT_BYTES),
        name="mlstm",
    )(qt, k, vt, so, gates, c0, m0, norm_w)


def _out_body(x_ref, hs_ref, ya_ref, gb_ref, g1_ref, sh2_ref, sc2_ref, g2_ref, nw2_ref, fnw_ref,
              wmo_ref, wo_ref, w1_ref, w2_ref, o_ref):
    dot = functools.partial(jnp.dot, preferred_element_type=F32)
    yb = dot(hs_ref[0], wmo_ref[...])
    merged = ya_ref[0].astype(F32) + gb_ref[0].astype(F32) * yb
    x1 = x_ref[0] + g1_ref[0] * dot(merged.astype(BF16), wo_ref[...])
    hm = _modulated_norm(x1, nw2_ref[...], sh2_ref[0], sc2_ref[0]).astype(BF16)
    step = D_FF // FF_SPLIT
    ff = None
    for s in range(FF_SPLIT):
        a = jnp.maximum(dot(hm, w1_ref[:, s * step:(s + 1) * step]), 0.0)
        part = dot((a * a).astype(BF16), w2_ref[s * step:(s + 1) * step, :])
        ff = part if ff is None else ff + part
    x2 = x1 + g2_ref[0] * ff
    y = x2 * lax.rsqrt(jnp.mean(x2 * x2, axis=-1, keepdims=True) + EPS)
    o_ref[0] = y * fnw_ref[...]


def out_call(x, hs, ya, gb, g1, sh2, sc2, g2, norm2_w, final_norm_w, w_mlstm_out, w_out, w_ff1, w_ff2):
    bsz, t, _ = x.shape
    tile = TOKEN_TILE
    tok = pl.BlockSpec((1, tile, D_MODEL), lambda i, j: (i, j, 0))
    row = pl.BlockSpec((1, 1, D_MODEL), lambda i, j: (i, 0, 0))
    return pl.pallas_call(
        _out_body,
        grid=(bsz, t // tile),
        in_specs=[tok, tok, tok, tok, row, row, row, row,
                  _resident((1, D_MODEL)), _resident((1, D_MODEL)),
                  _resident(w_mlstm_out.shape), _resident(w_out.shape),
                  _resident(w_ff1.shape), _resident(w_ff2.shape)],
        out_specs=tok,
        out_shape=jax.ShapeDtypeStruct((bsz, t, D_MODEL), F32),
        compiler_params=pltpu.CompilerParams(dimension_semantics=("arbitrary", "arbitrary"),
                                             vmem_limit_bytes=VMEM_LIMIT_BYTES),
        name="merge_out_mlp",
    )(x, hs, ya, gb, g1, sh2, sc2, g2, norm2_w, final_norm_w, w_mlstm_out, w_out, w_ff1, w_ff2)


def _group_gates(g):
    lead = g.shape[:-1]
    g = g.reshape(*lead, 2, N_DIR, N_HEAD_GROUPS, HEADS_PER_STEP)
    g = jnp.moveaxis(g, -2, -4)
    return g.reshape(*lead, N_GATE)


def _layer(x, ctx, mod, mod_ctx, norm1_w, w_in, b_in, conv_w, mlstm_norm_w, w_conv_out, w_mlstm_out,
           w_out, norm2_w, w_ff1, w_ff2, final_norm_w):
    bsz, t, _ = x.shape
    sh1, sc1, g1, sh2, sc2, g2 = [m.reshape(bsz, 1, D_MODEL) for m in jnp.split(mod, 6, axis=-1)]
    csh1, csc1 = mod_ctx[:, :D_MODEL], mod_ctx[:, D_MODEL:2 * D_MODEL]
    nw1 = norm1_w.reshape(1, D_MODEL)

    w_in_t = w_in.T
    w_n, w_vq_t = projection_weights(w_in_t)
    w_g_t = _group_gates(w_in_t[_REF_IG:_REF_Q].T).T.astype(BF16)
    b_n = jnp.concatenate([b_in[_REF_K:_REF_V], b_in[_REF_O:]]).reshape(1, -1)
    b_vq = jnp.concatenate([b_in[_REF_V:_REF_IG], b_in[_REF_Q:_REF_O]]).reshape(-1, 1)
    b_g = _group_gates(b_in[_REF_IG:_REF_Q]).reshape(-1, 1)

    c0, m0 = ctx_call(ctx, csh1, csc1, nw1, w_n, b_n, w_vq_t, b_vq, w_g_t, b_g)
    k, qt, vt, gates, so, ya, gb = inproj_call(x, sh1, sc1, nw1, w_n, b_n, w_vq_t, b_vq, w_g_t, b_g, conv_w,
                                               w_conv_out.astype(BF16))
    hs = mlstm_call(qt, k, vt, so, gates, c0, m0, mlstm_norm_w.reshape(1, D_MLSTM))
    return out_call(x, hs, ya, gb, g1, sh2, sc2, g2, norm2_w.reshape(1, D_MODEL),
                    final_norm_w.reshape(1, D_MODEL), w_mlstm_out.astype(BF16), w_out.astype(BF16),
                    w_ff1.astype(BF16), w_ff2.astype(BF16))


def kernel(x, c, ctx, c_ctx, w_mod, b_mod, norm1_w, w_in, b_in, conv_w, mlstm_norm_w, w_conv_out,
           w_mlstm_out, w_out, norm2_w, w_ff1, w_ff2, final_norm_w):
    depth = w_mod.shape[0]
    assert depth == 1, "the context stream is only advanced through its mLSTM state (single layer)"
    bsz = x.shape[0]
    cvecs = jnp.concatenate([c, c_ctx[None, :]], axis=0)
    mod_all = adaln_call(cvecs, w_mod[0], b_mod[0])
    return _layer(x, ctx, mod_all[:bsz], mod_all[bsz:], norm1_w[0], w_in[0], b_in[0], conv_w[0],
                  mlstm_norm_w[0], w_conv_out[0], w_mlstm_out[0], w_out[0], norm2_w[0], w_ff1[0],
                  w_ff2[0], final_norm_w)
```

```python
import functools

import jax
import jax.numpy as jnp
from jax import lax
from jax.experimental import pallas as pl
from jax.experimental.pallas import tpu as pltpu

D_MODEL = 1024
CTX_LEN = 256
GRID_W = 64
D_CONV = 1024
N_HEADS = 8
DK = 64
DV = 128
D_MLSTM = N_HEADS * DV
D_QK = N_HEADS * DK
D_FF = 4 * D_MODEL
N_DIR = 2
N_GATE = 2 * N_DIR * N_HEADS
EPS = 1e-6

CHUNK = 128
HEADS_PER_STEP = 2
assert HEADS_PER_STEP == 2 and HEADS_PER_STEP * DK == 128, "a head pair shares one 128-lane tile"
N_HEAD_GROUPS = N_HEADS // HEADS_PER_STEP
GATES_PER_GROUP = N_GATE // N_HEAD_GROUPS
SUBLANES = 8
BF16_SUBLANES = 16
V_AUG = DV + BF16_SUBLANES
CHUNK_UNROLL = 16
TOKEN_TILE = 512
FF_SPLIT = 4

VMEM_LIMIT_BYTES = 56 * 1024 * 1024

_REF_K, _REF_V, _REF_IG, _REF_FG, _REF_Q, _REF_O = 0, 512, 1536, 1552, 1568, 2080
D_IN = 8224
_K0, _O0, _XIN0, _GC0, _GB0, _MA0, _MB0, _G0 = 0, 512, 1536, 2560, 3584, 4608, 5632, 6656
_T_V, _T_Q, _T_GATES, _T_END, _T_ROWS = 0, 1024, 1536, 1568, 2048
_SH1, _SC1, _G1, _SH2, _SC2, _G2 = (i * D_MODEL for i in range(6))

F32 = jnp.float32
BF16 = jnp.bfloat16


def _sigmoid(x):
    return 1.0 / (1.0 + jnp.exp(-x))


def _log_sigmoid(x):
    return jnp.minimum(x, 0.0) - jnp.log(1.0 + jnp.exp(-jnp.abs(x)))


def _split3(x):
    hi = x.astype(BF16)
    r1 = x - hi.astype(F32)
    mid = r1.astype(BF16)
    lo = (r1 - mid.astype(F32)).astype(BF16)
    return hi, mid, lo


def _cumsum_lanes(x, tri_t):
    hi, mid, lo = _split3(x)
    dot = functools.partial(jnp.dot, preferred_element_type=F32)
    return dot(hi, tri_t) + dot(mid, tri_t) + dot(lo, tri_t)


def _tri(n, dtype, lower):
    r = lax.broadcasted_iota(jnp.int32, (n, n), 0)
    c = lax.broadcasted_iota(jnp.int32, (n, n), 1)
    return jnp.where((r >= c) if lower else (r <= c), 1.0, 0.0).astype(dtype)


def _modulated_norm(x, norm_w, shift, scale):
    y = x * lax.rsqrt(jnp.mean(x * x, axis=-1, keepdims=True) + EPS)
    return (y * norm_w) * (1.0 + scale) + shift


def _resident(shape):
    nd = len(shape)
    return pl.BlockSpec(shape, lambda *_: (0,) * nd, pipeline_mode=pl.Buffered(1))


def _adaln_body(c_ref, w_ref, b_ref, o_ref):
    c = c_ref[...]
    s = c * _sigmoid(c)
    o_ref[...] = jnp.dot(s, w_ref[...], preferred_element_type=F32,
                         precision=lax.Precision.HIGHEST) + b_ref[...]


def adaln_call(cvecs, w_mod, b_mod):
    n = cvecs.shape[0]
    n_out = w_mod.shape[1]
    tile = 1024
    return pl.pallas_call(
        _adaln_body,
        grid=(n_out // tile,),
        in_specs=[pl.BlockSpec((n, D_MODEL), lambda j: (0, 0)),
                  pl.BlockSpec((D_MODEL, tile), lambda j: (0, j)),
                  pl.BlockSpec((1, tile), lambda j: (0, j))],
        out_specs=pl.BlockSpec((n, tile), lambda j: (0, j)),
        out_shape=jax.ShapeDtypeStruct((n, n_out), F32),
        name="adaln",
    )(cvecs, w_mod, b_mod.reshape(1, n_out))


W_BLOCK = 512


def _transpose_cast_body(w_ref, o_ref):
    o_ref[...] = jnp.transpose(w_ref[...]).astype(BF16)


def _row_cast_body(starts_ref, w_ref, perm_ref, o_ref):
    del starts_ref
    o_ref[...] = w_ref[...].astype(BF16)

    @pl.when(pl.program_id(0) == pl.num_programs(0) - 1)
    def _():
        gates = w_ref[0:N_GATE, :].astype(BF16)
        o_ref[0:N_GATE, :] = jnp.dot(perm_ref[...], gates, preferred_element_type=F32).astype(BF16)


def _group_gates(g):
    lead = g.shape[:-1]
    g = g.reshape(*lead, 2, N_DIR, N_HEAD_GROUPS, HEADS_PER_STEP)
    g = jnp.moveaxis(g, -2, -4)
    return g.reshape(*lead, N_GATE)


def projection_weights(w_in_t):
    n_rest = (D_IN - _REF_O) // W_BLOCK
    blk = W_BLOCK // SUBLANES
    w_n = pl.pallas_call(
        _transpose_cast_body,
        grid=(1 + n_rest,),
        in_specs=[pl.BlockSpec((pl.Element(W_BLOCK), pl.Element(D_MODEL)),
                               lambda i: (SUBLANES * jnp.where(i == 0, _REF_K // SUBLANES,
                                                               _REF_O // SUBLANES + (i - 1) * blk), 0))],
        out_specs=pl.BlockSpec((D_MODEL, W_BLOCK), lambda i: (0, i)),
        out_shape=jax.ShapeDtypeStruct((D_MODEL, (1 + n_rest) * W_BLOCK), BF16),
        name="w_in_normal",
    )(w_in_t)
    starts = jnp.asarray([_REF_V // SUBLANES, _REF_V // SUBLANES + blk, _REF_Q // SUBLANES, _REF_IG // SUBLANES],
                         jnp.int32)
    gate_perm = jnp.eye(N_GATE, dtype=BF16)[_group_gates(jnp.arange(N_GATE))]
    w_t = pl.pallas_call(
        _row_cast_body,
        grid_spec=pltpu.PrefetchScalarGridSpec(
            num_scalar_prefetch=1,
            grid=(_T_ROWS // W_BLOCK,),
            in_specs=[pl.BlockSpec((pl.Element(W_BLOCK), pl.Element(D_MODEL)),
                                   lambda i, starts_ref: (SUBLANES * starts_ref[i], 0)),
                      pl.BlockSpec((N_GATE, N_GATE), lambda i, starts_ref: (0, 0))],
            out_specs=pl.BlockSpec((W_BLOCK, D_MODEL), lambda i, starts_ref: (i, 0))),
        out_shape=jax.ShapeDtypeStruct((_T_ROWS, D_MODEL), BF16),
        name="w_in_transposed",
    )(starts, w_in_t, gate_perm)
    return w_n, w_t


def _ctx_body(ctx_ref, mod_ref, nw_ref, wk_ref, bk_ref, wv_ref, bv_ref, wg_ref, bg_ref, c_ref, m_ref):
    hp = HEADS_PER_STEP
    last = mod_ref.shape[0] - 1
    hb = _modulated_norm(ctx_ref[0], nw_ref[...], mod_ref[last:, _SH1:_SC1], mod_ref[last:, _SC1:_G1]).astype(BF16)
    nt = functools.partial(lax.dot_general, dimension_numbers=(((1,), (1,)), ((), ())),
                           preferred_element_type=F32)
    k = ((jnp.dot(hb, wk_ref[...], preferred_element_type=F32) + bk_ref[...]) * (DK ** -0.5)).astype(BF16)
    vt = (nt(wv_ref[...], hb) + bv_ref[...]).astype(BF16)
    g = nt(wg_ref[...], hb) + bg_ref[...]

    lf = _log_sigmoid(g)
    b = _cumsum_lanes(lf, _tri(CTX_LEN, BF16, lower=False))
    ig = pltpu.roll(g, GATES_PER_GROUP // 2, axis=0)
    r_in_tile = jnp.bitwise_and(lax.broadcasted_iota(jnp.int32, (N_GATE, 1), 0), GATES_PER_GROUP - 1)
    fwd = r_in_tile < GATES_PER_GROUP // 2 + hp
    tot = b[:, CTX_LEN - 1:CTX_LEN]
    g_end = ig + jnp.where(fwd, tot - b, b - lf)
    m = jnp.maximum(tot, jnp.max(g_end, axis=1, keepdims=True))
    m_ref[0] = m
    wk = jnp.exp(g_end - m)

    ones_rows = jnp.where(lax.broadcasted_iota(jnp.int32, (V_AUG - DV, CTX_LEN), 0) == 0, 1.0, 0.0)
    lane = lax.broadcasted_iota(jnp.int32, (CTX_LEN, hp * DK), 1)
    for grp in range(N_HEAD_GROUPS):
        k2 = k[:, grp * hp * DK:(grp + 1) * hp * DK]
        zero = jnp.zeros_like(k2)
        k_bd = jnp.concatenate([jnp.where(lane < DK, k2, zero), jnp.where(lane >= DK, k2, zero)], axis=0)
        vf = [jnp.concatenate([vt[(grp * hp + j) * DV:(grp * hp + j + 1) * DV].astype(F32), ones_rows], axis=0)
              for j in range(hp)]
        for d in range(N_DIR):
            row0 = grp * GATES_PER_GROUP + GATES_PER_GROUP // 2 + d * hp
            lhs = jnp.concatenate([(vf[j] * wk[row0 + j:row0 + j + 1]).astype(BF16) for j in range(hp)], axis=1)
            c_ref[0, d, grp] = jnp.dot(lhs, k_bd, preferred_element_type=F32)


def ctx_call(ctx, mod, norm_w, w_n, b_n, w_t, b_t):
    bsz = ctx.shape[0]
    const = lambda shape: pl.BlockSpec(shape, lambda i: (0,) * len(shape))
    gate_rows = lambda width: pl.BlockSpec((N_GATE, width), lambda i: (_T_GATES // N_GATE, 0))
    return pl.pallas_call(
        _ctx_body,
        grid=(bsz,),
        in_specs=[pl.BlockSpec((1, CTX_LEN, D_MODEL), lambda i: (i, 0, 0)),
                  const(mod.shape), const((1, D_MODEL)),
                  const((D_MODEL, D_QK)), const((1, D_QK)),
                  const((D_MLSTM, D_MODEL)), const((D_MLSTM, 1)),
                  gate_rows(D_MODEL), gate_rows(1)],
        out_specs=[pl.BlockSpec((1, N_DIR, N_HEAD_GROUPS, V_AUG, HEADS_PER_STEP * DK),
                                lambda i: (i, 0, 0, 0, 0)),
                   pl.BlockSpec((1, N_GATE, 1), lambda i: (i, 0, 0))],
        out_shape=[jax.ShapeDtypeStruct((bsz, N_DIR, N_HEAD_GROUPS, V_AUG, HEADS_PER_STEP * DK), F32),
                   jax.ShapeDtypeStruct((bsz, N_GATE, 1), F32)],
        compiler_params=pltpu.CompilerParams(dimension_semantics=("arbitrary",),
                                             vmem_limit_bytes=VMEM_LIMIT_BYTES),
        name="ctx_state",
    )(ctx, mod, norm_w, w_n, b_n, w_t, b_t, w_t, b_t)


def _inproj_body(x_ref, mod_ref, nw_ref, nwm_ref, w_ref, b_ref, wt_ref, bt_ref, cw_ref, wco_ref,
                 k_ref, qt_ref, vt_ref, gt_ref, so_ref, ya_ref, gb_ref):
    x = x_ref[0]
    mod = mod_ref[pl.ds(pl.program_id(0), 1), :]
    hb = _modulated_norm(x, nw_ref[...], mod[:, _SH1:_SC1], mod[:, _SC1:_G1]).astype(BF16)
    tile = hb.shape[0]

    def seg(a, b):
        return jnp.dot(hb, w_ref[:, a:b], preferred_element_type=F32) + b_ref[:, a:b]

    k_ref[0] = (seg(_K0, _O0) * (DK ** -0.5)).astype(BF16)
    so_ref[0] = (_sigmoid(seg(_O0, _XIN0)) * nwm_ref[...]).astype(BF16)

    zt = lax.dot_general(wt_ref[:_T_END, :], hb, (((1,), (1,)), ((), ())),
                         preferred_element_type=F32) + bt_ref[:_T_END, :]
    for i in range(tile // CHUNK):
        lanes = slice(i * CHUNK, (i + 1) * CHUNK)
        vt_ref[0, i] = zt[_T_V:_T_Q, lanes].astype(BF16)
        qt_ref[0, i] = zt[_T_Q:_T_GATES, lanes].astype(BF16)
        gt_ref[0, i] = zt[_T_GATES:_T_END, lanes]

    u = seg(_XIN0, _GC0) * seg(_GC0, _GB0)
    col = jnp.bitwise_and(lax.broadcasted_iota(jnp.int32, (tile, 1), 0), GRID_W - 1)
    u_prev = jnp.where(col != 0, pltpu.roll(u, 1, axis=0), 0.0)
    u_next = jnp.where(col != GRID_W - 1, pltpu.roll(u, tile - 1, axis=0), 0.0)
    a = cw_ref[0:1, :] * u_prev + cw_ref[1:2, :] * u + cw_ref[2:3, :] * u_next
    ya = jnp.dot((seg(_GB0, _MA0) * a).astype(BF16), wco_ref[...], preferred_element_type=F32)
    ya_ref[0] = (_sigmoid(seg(_MA0, _MB0)) * ya).astype(BF16)
    gb_ref[0] = _sigmoid(seg(_MB0, _G0)).astype(BF16)


def inproj_call(x, mod, norm_w, mlstm_norm_w, w_n, b_n, w_t, b_t, conv_w, w_conv_out):
    bsz, t, _ = x.shape
    tile = TOKEN_TILE
    tok = lambda width: pl.BlockSpec((1, tile, width), lambda i, j: (i, j, 0))
    tok_t = lambda rows: pl.BlockSpec((1, tile // CHUNK, rows, CHUNK), lambda i, j: (i, j, 0, 0))
    seq = lambda width: jax.ShapeDtypeStruct((bsz, t, width), BF16)
    seq_t = lambda rows, dt: jax.ShapeDtypeStruct((bsz, t // CHUNK, rows, CHUNK), dt)
    resident = [mod, norm_w, mlstm_norm_w, w_n, b_n, w_t, b_t, conv_w, w_conv_out]
    return pl.pallas_call(
        _inproj_body,
        grid=(bsz, t // tile),
        in_specs=[tok(D_MODEL)] + [_resident(a.shape) for a in resident],
        out_specs=[tok(D_QK), tok_t(D_QK), tok_t(D_MLSTM), tok_t(N_GATE), tok(D_MLSTM),
                   tok(D_MODEL), tok(D_MODEL)],
        out_shape=[seq(D_QK), seq_t(D_QK, BF16), seq_t(D_MLSTM, BF16), seq_t(N_GATE, F32), seq(D_MLSTM),
                   seq(D_MODEL), seq(D_MODEL)],
        compiler_params=pltpu.CompilerParams(dimension_semantics=("arbitrary", "arbitrary"),
                                             vmem_limit_bytes=VMEM_LIMIT_BYTES),
        name="inproj_conv",
    )(x, *resident)


SERIES_ROW0 = GATES_PER_GROUP // 2
N_SERIES = N_DIR * HEADS_PER_STEP
N_SPLIT = 3
LOG2E = 1.4426950408889634
_WK, _DECAY, _WINTER, _EXPNEG, _KEYS = 0, 1, 2, 3, 4
_COLM = _KEYS + N_SPLIT
N_TABLES = _COLM + N_SPLIT * N_SERIES


def _running_max_rows(x, reverse):
    n = x.shape[0]
    row = lax.broadcasted_iota(jnp.int32, x.shape, 0)
    shift = 1
    while shift < n:
        if reverse:
            moved = jnp.where(row < n - shift, pltpu.roll(x, n - shift, axis=0), -jnp.inf)
        else:
            moved = jnp.where(row >= shift, pltpu.roll(x, shift, axis=0), -jnp.inf)
        x = jnp.maximum(x, moved)
        shift *= 2
    return x


def _gate_tables(g_ref, m0_ref, tab_ref, n_chunks):
    step_rows = N_HEAD_GROUPS * GATES_PER_GROUP
    rows = n_chunks * step_rows
    g = g_ref[0].reshape(rows, CHUNK)
    lf = _log_sigmoid(g)
    b = _cumsum_lanes(lf, _tri(CHUNK, BF16, lower=False))
    ig = pltpu.roll(g, SERIES_ROW0, axis=0)
    in_tile = lambda idx: jnp.bitwise_and(idx, GATES_PER_GROUP - 1)
    r_in_tile = in_tile(lax.broadcasted_iota(jnp.int32, (rows, 1), 0))
    fwd = r_in_tile < SERIES_ROW0 + HEADS_PER_STEP
    e = b - lf
    tot = b[:, CHUNK - 1:CHUNK]
    col = jnp.where(fwd, b, -e)
    key = ig + jnp.where(fwd, -b, e)
    g_end = jnp.where(fwd, tot + key, key)
    g_max = jnp.max(g_end, axis=1, keepdims=True)

    fwd_step = fwd[:step_rows]
    tile = lambda a, c: a[c * step_rows:(c + 1) * step_rows]
    m = m0_ref[0]
    m_in_steps, m_out_steps = [], []
    for i in range(n_chunks):
        back = n_chunks - 1 - i
        m_in_steps.append(m)
        m = jnp.maximum(jnp.where(fwd_step, tile(tot, i), tile(tot, back)) + m,
                        jnp.where(fwd_step, tile(g_max, i), tile(g_max, back)))
        m_out_steps.append(m)
    by_chunk = lambda steps: jnp.concatenate(
        [jnp.where(fwd_step, steps[c], steps[n_chunks - 1 - c]) for c in range(n_chunks)], axis=0)
    m_in, m_out = by_chunk(m_in_steps), by_chunk(m_out_steps)

    decay = jnp.broadcast_to(jnp.exp(tot + m_in - m_out), (rows, CHUNK))
    lane = lax.broadcasted_iota(jnp.int32, (rows, CHUNK), 1)
    inter = jnp.where(fwd, col, tot + col) + m_in
    key_t = jnp.transpose(key)
    fwd_lane = in_tile(lax.broadcasted_iota(jnp.int32, (1, rows), 1)) < SERIES_ROW0 + HEADS_PER_STEP
    key_max = jnp.transpose(jnp.where(fwd_lane, _running_max_rows(key_t, reverse=False),
                                      _running_max_rows(key_t, reverse=True)))
    m_t = jnp.maximum(col + key_max, inter)
    tables = {_WK: jnp.exp(g_end - m_out),
              _DECAY: jnp.where(lane < DK, decay, pltpu.roll(decay, rows - 1, axis=0)),
              _WINTER: jnp.exp(inter - m_t), _EXPNEG: jnp.exp(-m_t)}
    for x, part in enumerate(_split3(key * LOG2E)):
        tables[_KEYS + x] = part.astype(F32)
    for x, part in enumerate(_split3((col - m_t) * LOG2E)):
        for sidx in range(N_SERIES):
            tables[_COLM + x * N_SERIES + sidx] = jnp.where(r_in_tile == SERIES_ROW0 + sidx,
                                                            part.astype(F32), 0.0)
    for idx, a in tables.items():
        tab_ref[idx] = a.reshape(n_chunks, N_HEAD_GROUPS, GATES_PER_GROUP, CHUNK)


def _mlstm_body(qt_ref, k_ref, vt_ref, so_ref, g_ref, c0_ref, m0_ref, o_ref,
                tab_ref, u_ref, s_ref, st_ref, *, n_chunks):
    hp = HEADS_PER_STEP
    s_i = lax.broadcasted_iota(jnp.int32, (CHUNK, CHUNK), 0)
    t_i = lax.broadcasted_iota(jnp.int32, (CHUNK, CHUNK), 1)
    masks = (s_i <= t_i, s_i >= t_i)
    ones_rows = jnp.where(lax.broadcasted_iota(jnp.int32, (V_AUG - DV, CHUNK), 0) == 0,
                          1.0, 0.0).astype(BF16)
    kq_lane = lax.broadcasted_iota(jnp.int32, (CHUNK, hp * DK), 1)
    kq_row = lax.broadcasted_iota(jnp.int32, (hp * DK, CHUNK), 0)
    state_lane = lax.broadcasted_iota(jnp.int32, (V_AUG, hp * DK), 1)
    series = lambda tab8, d, j: tab8[SERIES_ROW0 + d * hp + j:SERIES_ROW0 + d * hp + j + 1]
    split_rows = N_SPLIT * GATES_PER_GROUP
    ones_split = jnp.ones((split_rows, CHUNK), F32)
    pick_r = jnp.bitwise_and(lax.broadcasted_iota(jnp.int32, (split_rows, N_DIR * CHUNK), 0),
                             GATES_PER_GROUP - 1)
    pick_d = jnp.where(lax.broadcasted_iota(jnp.int32, (split_rows, N_DIR * CHUNK), 1) >= CHUNK, 1, 0)
    pick_series = [jnp.where(pick_r == SERIES_ROW0 + pick_d * hp + j, 1.0, 0.0) for j in range(hp)]

    def chunk_rows(c):
        return pl.ds(pl.multiple_of(c * CHUNK, CHUNK), CHUNK)

    def values_t(c, j):
        return jnp.concatenate([vt_ref[0, c, j * DV:(j + 1) * DV, :], ones_rows], axis=0)

    grp = pl.program_id(1)

    @pl.when(grp == 0)
    def _():
        _gate_tables(g_ref, m0_ref, tab_ref, n_chunks)

    def block_diag_q(qt2):
        zero = jnp.zeros_like(qt2)
        return jnp.concatenate([jnp.where(kq_row < DK, qt2, zero),
                                jnp.where(kq_row >= DK, qt2, zero)], axis=1)

    def chunk_step(c, carry):
        k2 = k_ref[0, chunk_rows(c), :]
        zero = jnp.zeros_like(k2)
        k_bd = jnp.concatenate([jnp.where(kq_lane < DK, k2, zero),
                                jnp.where(kq_lane >= DK, k2, zero)], axis=0)
        wk8 = tab_ref[_WK, c, grp]
        vb = [values_t(c, j) for j in range(hp)]

        def key_weights(d, j):
            tile = jnp.broadcast_to(series(wk8, d, j), (BF16_SUBLANES, CHUNK)).astype(BF16)
            return pltpu.repeat(tile, V_AUG // BF16_SUBLANES, axis=0)

        lhs = jnp.concatenate(
            [jnp.concatenate([vb[j] * key_weights(d, j) for j in range(hp)], axis=1)
             for d in range(N_DIR)], axis=0)
        u_ref[c] = jnp.dot(lhs, k_bd, preferred_element_type=F32)

        qk_t = jnp.dot(k2, block_diag_q(qt_ref[0, c]), preferred_element_type=F32)
        key_side = jnp.concatenate([tab_ref[_KEYS + x, c, grp] for x in range(N_SPLIT)] + [ones_split],
                                   axis=0).astype(BF16)
        for j in range(hp):
            query_side = jnp.concatenate(
                [pick_series[j]] +
                [jnp.concatenate([tab_ref[_COLM + x * N_SERIES + d * hp + j, c, grp] for d in range(N_DIR)],
                                 axis=1)
                 for x in range(N_SPLIT)], axis=0).astype(BF16)
            log_d = lax.dot_general(key_side, query_side, (((0,), (0,)), ((), ())),
                                    preferred_element_type=F32)
            for d in range(N_DIR):
                decay = jnp.exp2(jnp.where(masks[d], log_d[:, d * CHUNK:(d + 1) * CHUNK], -jnp.inf))
                st_ref[c, d, j] = (qk_t[:, j * CHUNK:(j + 1) * CHUNK] * decay).astype(BF16)
        return carry

    lax.fori_loop(0, n_chunks, chunk_step, 0, unroll=CHUNK_UNROLL)

    def scan_step(i, carry):
        cf, cb = carry
        back = n_chunks - 1 - i
        s_ref[i, 0, 0] = cf.astype(BF16)
        s_ref[i, 0, 1] = pltpu.roll(cf, DK, axis=1).astype(BF16)
        s_ref[back, 1, 0] = cb.astype(BF16)
        s_ref[back, 1, 1] = pltpu.roll(cb, DK, axis=1).astype(BF16)
        cf = tab_ref[_DECAY, i, grp][SERIES_ROW0:SERIES_ROW0 + 1] * cf + u_ref[i, :V_AUG]
        cb = tab_ref[_DECAY, back, grp][SERIES_ROW0 + hp:SERIES_ROW0 + hp + 1] * cb + u_ref[back, V_AUG:]
        return cf, cb

    lax.fori_loop(0, n_chunks, scan_step, (c0_ref[0, 0, 0], c0_ref[0, 1, 0]), unroll=CHUNK_UNROLL)

    def output_step(c, carry):
        rows = chunk_rows(c)
        qt2 = qt_ref[0, c].astype(F32)
        w_inter8, exp_neg8 = tab_ref[_WINTER, c, grp], tab_ref[_EXPNEG, c, grp]
        no_query = jnp.zeros((DK, CHUNK), BF16)
        for j in range(hp):
            state = jnp.where(state_lane < DK, s_ref[c, 0, j], s_ref[c, 1, 1 - j])
            q_h = qt2[j * DK:(j + 1) * DK]
            qw = [(q_h * series(w_inter8, d, j)).astype(BF16) for d in range(N_DIR)]
            rhs = jnp.concatenate(
                [jnp.concatenate([st_ref[c, d, j] for d in range(N_DIR)], axis=1),
                 jnp.concatenate([qw[0], no_query], axis=1),
                 jnp.concatenate([no_query, qw[1]], axis=1)], axis=0)
            lhs = jnp.concatenate([values_t(c, j), state], axis=1)
            n_all = jnp.dot(lhs, rhs, preferred_element_type=F32)
            h_t = None
            for d in range(N_DIR):
                num = n_all[:, d * CHUNK:(d + 1) * CHUNK]
                r = 1.0 / jnp.maximum(jnp.abs(num[DV:DV + 1]), series(exp_neg8, d, j))
                part = num[:DV] * r
                h_t = part if h_t is None else h_t + part
            hn_t = h_t * lax.rsqrt(jnp.mean(h_t * h_t, axis=0, keepdims=True) + EPS)
            o_ref[0, rows, j * DV:(j + 1) * DV] = (
                jnp.transpose(hn_t) * so_ref[0, rows, j * DV:(j + 1) * DV].astype(F32)).astype(BF16)
        return carry

    lax.fori_loop(0, n_chunks, output_step, 0, unroll=CHUNK_UNROLL)


def mlstm_call(qt, k, vt, so, gates, c0, m0):
    bsz, t, _ = k.shape
    n_chunks = t // CHUNK
    hp = HEADS_PER_STEP
    seq = lambda width: pl.BlockSpec((1, t, width), lambda i, j: (i, 0, j))
    seq_t = lambda rows: pl.BlockSpec((1, n_chunks, rows, CHUNK), lambda i, j: (i, 0, j, 0))
    return pl.pallas_call(
        functools.partial(_mlstm_body, n_chunks=n_chunks),
        grid=(bsz, N_HEAD_GROUPS),
        in_specs=[seq_t(hp * DK), seq(hp * DK), seq_t(hp * DV), seq(hp * DV),
                  pl.BlockSpec((1, n_chunks, N_GATE, CHUNK), lambda i, j: (i, 0, 0, 0)),
                  pl.BlockSpec((1, N_DIR, 1, V_AUG, hp * DK), lambda i, j: (i, 0, j, 0, 0)),
                  pl.BlockSpec((1, N_GATE, 1), lambda i, j: (i, 0, 0))],
        out_specs=seq(hp * DV),
        out_shape=jax.ShapeDtypeStruct((bsz, t, D_MLSTM), BF16),
        scratch_shapes=[pltpu.VMEM((N_TABLES, n_chunks, N_HEAD_GROUPS, GATES_PER_GROUP, CHUNK), F32),
                        pltpu.VMEM((n_chunks, N_DIR * V_AUG, hp * DK), F32),
                        pltpu.VMEM((n_chunks, N_DIR, 2, V_AUG, hp * DK), BF16),
                        pltpu.VMEM((n_chunks, N_DIR, hp, CHUNK, CHUNK), BF16)],
        compiler_params=pltpu.CompilerParams(dimension_semantics=("arbitrary", "arbitrary"),
                                             vmem_limit_bytes=VMEM_LIMIT_BYTES),
        name="mlstm",
    )(qt, k, vt, so, gates, c0, m0)


def _out_body(x_ref, hs_ref, ya_ref, gb_ref, mod_ref, nw2_ref, fnw_ref, wmo_ref, wo_ref, w1_ref, w2_ref, o_ref):
    dot = functools.partial(jnp.dot, preferred_element_type=F32)
    mod = mod_ref[pl.ds(pl.program_id(0), 1), :]
    yb = dot(hs_ref[0], wmo_ref[...])
    merged = ya_ref[0].astype(F32) + gb_ref[0].astype(F32) * yb
    x1 = x_ref[0] + mod[:, _G1:_SH2] * dot(merged.astype(BF16), wo_ref[...])
    hm = _modulated_norm(x1, nw2_ref[...], mod[:, _SH2:_SC2], mod[:, _SC2:_G2]).astype(BF16)
    step = D_FF // FF_SPLIT
    ff = None
    for s in range(FF_SPLIT):
        a = jnp.maximum(dot(hm, w1_ref[:, s * step:(s + 1) * step]), 0.0)
        part = dot((a * a).astype(BF16), w2_ref[s * step:(s + 1) * step, :])
        ff = part if ff is None else ff + part
    x2 = x1 + mod[:, _G2:] * ff
    y = x2 * lax.rsqrt(jnp.mean(x2 * x2, axis=-1, keepdims=True) + EPS)
    o_ref[0] = y * fnw_ref[...]


def out_call(x, hs, ya, gb, mod, norm2_w, final_norm_w, w_mlstm_out, w_out, w_ff1, w_ff2):
    bsz, t, _ = x.shape
    tile = TOKEN_TILE
    tok = pl.BlockSpec((1, tile, D_MODEL), lambda i, j: (i, j, 0))
    resident = [mod, norm2_w, final_norm_w, w_mlstm_out, w_out, w_ff1, w_ff2]
    return pl.pallas_call(
        _out_body,
        grid=(bsz, t // tile),
        in_specs=[tok, tok, tok, tok] + [_resident(a.shape) for a in resident],
        out_specs=tok,
        out_shape=jax.ShapeDtypeStruct((bsz, t, D_MODEL), F32),
        compiler_params=pltpu.CompilerParams(dimension_semantics=("arbitrary", "arbitrary"),
                                             vmem_limit_bytes=VMEM_LIMIT_BYTES),
        name="merge_out_mlp",
    )(x, hs, ya, gb, *resident)


def _layer(x, ctx, mod, norm1_w, w_in, b_in, conv_w, mlstm_norm_w, w_conv_out, w_mlstm_out,
           w_out, norm2_w, w_ff1, w_ff2, final_norm_w):
    nw1 = norm1_w.reshape(1, D_MODEL)
    w_n, w_t = projection_weights(w_in.T)
    b_n = jnp.concatenate([b_in[_REF_K:_REF_V], b_in[_REF_O:]]).reshape(1, -1)
    b_t = jnp.concatenate([b_in[_REF_V:_REF_IG], b_in[_REF_Q:_REF_O], _group_gates(b_in[_REF_IG:_REF_Q]),
                           jnp.zeros((_T_ROWS - _T_END,), F32)]).reshape(-1, 1)

    c0, m0 = ctx_call(ctx, mod, nw1, w_n, b_n, w_t, b_t)
    k, qt, vt, gates, so, ya, gb = inproj_call(x, mod, nw1, mlstm_norm_w.reshape(1, D_MLSTM), w_n, b_n, w_t, b_t,
                                               conv_w, w_conv_out.astype(BF16))
    hs = mlstm_call(qt, k, vt, so, gates, c0, m0)
    return out_call(x, hs, ya, gb, mod, norm2_w.reshape(1, D_MODEL), final_norm_w.reshape(1, D_MODEL),
                    w_mlstm_out.astype(BF16), w_out.astype(BF16), w_ff1.astype(BF16), w_ff2.astype(BF16))


def kernel(x, c, ctx, c_ctx, w_mod, b_mod, norm1_w, w_in, b_in, conv_w, mlstm_norm_w, w_conv_out,
           w_mlstm_out, w_out, norm2_w, w_ff1, w_ff2, final_norm_w):
    depth = w_mod.shape[0]
    assert depth == 1, "the context stream is only advanced through its mLSTM state (single layer)"
    cvecs = jnp.concatenate([c, c_ctx[None, :]], axis=0)
    mod = adaln_call(cvecs, w_mod[0], b_mod[0])
    return _layer(x, ctx, mod, norm1_w[0], w_in[0], b_in[0], conv_w[0],
                  mlstm_norm_w[0], w_conv_out[0], w_mlstm_out[0], w_out[0], norm2_w[0], w_ff1[0],
                  w_ff2[0], final_norm_w)
```

```python
import functools

import jax
import jax.numpy as jnp
from jax import lax
from jax.experimental import pallas as pl
from jax.experimental.pallas import tpu as pltpu

D_MODEL = 1024
CTX_LEN = 256
GRID_W = 64
D_CONV = 1024
N_HEADS = 8
DK = 64
DV = 128
D_MLSTM = N_HEADS * DV
D_QK = N_HEADS * DK
D_FF = 4 * D_MODEL
N_DIR = 2
N_GATE = 2 * N_DIR * N_HEADS
EPS = 1e-6

CHUNK = 128
HEADS_PER_STEP = 2
assert HEADS_PER_STEP == 2 and HEADS_PER_STEP * DK == 128, "a head pair shares one 128-lane tile"
N_HEAD_GROUPS = N_HEADS // HEADS_PER_STEP
GATES_PER_GROUP = N_GATE // N_HEAD_GROUPS
SUBLANES = 8
BF16_SUBLANES = 16
V_AUG = DV + BF16_SUBLANES
CHUNK_UNROLL = 16
TOKEN_TILE = 512
CTX_BATCH = 4
FF_SPLIT = 4

VMEM_LIMIT_BYTES = 56 * 1024 * 1024

_REF_K, _REF_V, _REF_IG, _REF_FG, _REF_Q, _REF_O = 0, 512, 1536, 1552, 1568, 2080
D_IN = 8224
_K0, _O0, _XIN0, _GC0, _GB0, _MA0, _MB0, _G0 = 0, 512, 1536, 2560, 3584, 4608, 5632, 6656
_T_V, _T_Q, _T_GATES, _T_END, _T_ROWS = 0, 1024, 1536, 1568, 2048
_SH1, _SC1, _G1, _SH2, _SC2, _G2 = (i * D_MODEL for i in range(6))

F32 = jnp.float32
BF16 = jnp.bfloat16


def _sigmoid(x):
    return 1.0 / (1.0 + jnp.exp(-x))


def _log_sigmoid(x):
    return jnp.minimum(x, 0.0) - jnp.log(1.0 + jnp.exp(-jnp.abs(x)))


def _split3(x):
    hi = x.astype(BF16)
    r1 = x - hi.astype(F32)
    mid = r1.astype(BF16)
    lo = (r1 - mid.astype(F32)).astype(BF16)
    return hi, mid, lo


def _cumsum_lanes(x, tri_t):
    hi, mid, lo = _split3(x)
    dot = functools.partial(jnp.dot, preferred_element_type=F32)
    return dot(hi, tri_t) + dot(mid, tri_t) + dot(lo, tri_t)


def _tri(n, dtype, lower):
    r = lax.broadcasted_iota(jnp.int32, (n, n), 0)
    c = lax.broadcasted_iota(jnp.int32, (n, n), 1)
    return jnp.where((r >= c) if lower else (r <= c), 1.0, 0.0).astype(dtype)


def _modulated_norm(x, norm_w, shift, scale):
    y = x * lax.rsqrt(jnp.mean(x * x, axis=-1, keepdims=True) + EPS)
    return (y * norm_w) * (1.0 + scale) + shift


def _resident(shape):
    nd = len(shape)
    return pl.BlockSpec(shape, lambda *_: (0,) * nd, pipeline_mode=pl.Buffered(1))


def _adaln_body(c_ref, w_ref, b_ref, o_ref):
    c = c_ref[...]
    s = c * _sigmoid(c)
    dot = functools.partial(jnp.dot, preferred_element_type=F32)
    w = w_ref[...]
    s_hi, w_hi = s.astype(BF16), w.astype(BF16)
    s_lo, w_lo = (s - s_hi.astype(F32)).astype(BF16), (w - w_hi.astype(F32)).astype(BF16)
    o_ref[...] = (dot(s_hi, w_hi) + dot(s_hi, w_lo) + dot(s_lo, w_hi)) + b_ref[...]


def adaln_call(cvecs, w_mod, b_mod):
    n = cvecs.shape[0]
    n_out = w_mod.shape[1]
    tile = 1024
    return pl.pallas_call(
        _adaln_body,
        grid=(n_out // tile,),
        in_specs=[pl.BlockSpec((n, D_MODEL), lambda j: (0, 0)),
                  pl.BlockSpec((D_MODEL, tile), lambda j: (0, j)),
                  pl.BlockSpec((1, tile), lambda j: (0, j))],
        out_specs=pl.BlockSpec((n, tile), lambda j: (0, j)),
        out_shape=jax.ShapeDtypeStruct((n, n_out), F32),
        name="adaln",
    )(cvecs, w_mod, b_mod.reshape(1, n_out))


W_BLOCK = 512


def _transpose_cast_body(w_ref, o_ref):
    o_ref[...] = jnp.transpose(w_ref[...]).astype(BF16)


def _row_cast_body(starts_ref, w_ref, perm_ref, o_ref):
    del starts_ref
    o_ref[...] = w_ref[...].astype(BF16)

    @pl.when(pl.program_id(0) == pl.num_programs(0) - 1)
    def _():
        gates = w_ref[0:N_GATE, :].astype(BF16)
        o_ref[0:N_GATE, :] = jnp.dot(perm_ref[...], gates, preferred_element_type=F32).astype(BF16)


def _group_gates(g):
    lead = g.shape[:-1]
    g = g.reshape(*lead, 2, N_DIR, N_HEAD_GROUPS, HEADS_PER_STEP)
    g = jnp.moveaxis(g, -2, -4)
    return g.reshape(*lead, N_GATE)


def projection_weights(w_in_t):
    n_rest = (D_IN - _REF_O) // W_BLOCK
    blk = W_BLOCK // SUBLANES
    w_n = pl.pallas_call(
        _transpose_cast_body,
        grid=(1 + n_rest,),
        in_specs=[pl.BlockSpec((pl.Element(W_BLOCK), pl.Element(D_MODEL)),
                               lambda i: (SUBLANES * jnp.where(i == 0, _REF_K // SUBLANES,
                                                               _REF_O // SUBLANES + (i - 1) * blk), 0))],
        out_specs=pl.BlockSpec((D_MODEL, W_BLOCK), lambda i: (0, i)),
        out_shape=jax.ShapeDtypeStruct((D_MODEL, (1 + n_rest) * W_BLOCK), BF16),
        name="w_in_normal",
    )(w_in_t)
    starts = jnp.asarray([_REF_V // SUBLANES, _REF_V // SUBLANES + blk, _REF_Q // SUBLANES, _REF_IG // SUBLANES],
                         jnp.int32)
    gate_perm = jnp.eye(N_GATE, dtype=BF16)[_group_gates(jnp.arange(N_GATE))]
    w_t = pl.pallas_call(
        _row_cast_body,
        grid_spec=pltpu.PrefetchScalarGridSpec(
            num_scalar_prefetch=1,
            grid=(_T_ROWS // W_BLOCK,),
            in_specs=[pl.BlockSpec((pl.Element(W_BLOCK), pl.Element(D_MODEL)),
                                   lambda i, starts_ref: (SUBLANES * starts_ref[i], 0)),
                      pl.BlockSpec((N_GATE, N_GATE), lambda i, starts_ref: (0, 0))],
            out_specs=pl.BlockSpec((W_BLOCK, D_MODEL), lambda i, starts_ref: (i, 0))),
        out_shape=jax.ShapeDtypeStruct((_T_ROWS, D_MODEL), BF16),
        name="w_in_transposed",
    )(starts, w_in_t, gate_perm)
    return w_n, w_t


def _ctx_body(ctx_ref, mod_ref, nw_ref, wk_ref, bk_ref, wv_ref, bv_ref, wg_ref, bg_ref, c_ref, m_ref):
    hp = HEADS_PER_STEP
    nb = ctx_ref.shape[0]
    last = mod_ref.shape[0] - 1
    x = ctx_ref[...].reshape(nb * CTX_LEN, D_MODEL)
    hb = _modulated_norm(x, nw_ref[...], mod_ref[last:, _SH1:_SC1], mod_ref[last:, _SC1:_G1]).astype(BF16)
    nt = functools.partial(lax.dot_general, dimension_numbers=(((1,), (1,)), ((), ())),
                           preferred_element_type=F32)
    k_all = ((jnp.dot(hb, wk_ref[...], preferred_element_type=F32) + bk_ref[...]) * (DK ** -0.5)).astype(BF16)
    vt_all = (nt(wv_ref[...], hb) + bv_ref[...]).astype(BF16)
    g_all = nt(wg_ref[...], hb) + bg_ref[...]

    tri_t = _tri(CTX_LEN, BF16, lower=False)
    r_in_tile = jnp.bitwise_and(lax.broadcasted_iota(jnp.int32, (N_GATE, 1), 0), GATES_PER_GROUP - 1)
    fwd = r_in_tile < GATES_PER_GROUP // 2 + hp
    ones_rows = jnp.where(lax.broadcasted_iota(jnp.int32, (V_AUG - DV, CTX_LEN), 0) == 0, 1.0, 0.0)
    lane = lax.broadcasted_iota(jnp.int32, (CTX_LEN, hp * DK), 1)
    for bi in range(nb):
        tokens = slice(bi * CTX_LEN, (bi + 1) * CTX_LEN)
        g, vt, k = g_all[:, tokens], vt_all[:, tokens], k_all[tokens]
        lf = _log_sigmoid(g)
        b = _cumsum_lanes(lf, tri_t)
        ig = pltpu.roll(g, GATES_PER_GROUP // 2, axis=0)
        tot = b[:, CTX_LEN - 1:CTX_LEN]
        g_end = ig + jnp.where(fwd, tot - b, b - lf)
        m = jnp.maximum(tot, jnp.max(g_end, axis=1, keepdims=True))
        m_ref[bi] = m
        wk = jnp.exp(g_end - m)
        for grp in range(N_HEAD_GROUPS):
            k2 = k[:, grp * hp * DK:(grp + 1) * hp * DK]
            zero = jnp.zeros_like(k2)
            k_bd = jnp.concatenate([jnp.where(lane < DK, k2, zero), jnp.where(lane >= DK, k2, zero)], axis=0)
            vf = [jnp.concatenate([vt[(grp * hp + j) * DV:(grp * hp + j + 1) * DV].astype(F32), ones_rows],
                                  axis=0) for j in range(hp)]
            for d in range(N_DIR):
                row0 = grp * GATES_PER_GROUP + GATES_PER_GROUP // 2 + d * hp
                lhs = jnp.concatenate([(vf[j] * wk[row0 + j:row0 + j + 1]).astype(BF16) for j in range(hp)],
                                      axis=1)
                c_ref[bi, d, grp] = jnp.dot(lhs, k_bd, preferred_element_type=F32)


def ctx_call(ctx, mod, norm_w, w_n, b_n, w_t, b_t):
    bsz = ctx.shape[0]
    nb = CTX_BATCH
    const = lambda shape: pl.BlockSpec(shape, lambda i: (0,) * len(shape))
    gate_rows = lambda width: pl.BlockSpec((N_GATE, width), lambda i: (_T_GATES // N_GATE, 0))
    return pl.pallas_call(
        _ctx_body,
        grid=(bsz // nb,),
        in_specs=[pl.BlockSpec((nb, CTX_LEN, D_MODEL), lambda i: (i, 0, 0)),
                  const(mod.shape), const((1, D_MODEL)),
                  const((D_MODEL, D_QK)), const((1, D_QK)),
                  const((D_MLSTM, D_MODEL)), const((D_MLSTM, 1)),
                  gate_rows(D_MODEL), gate_rows(1)],
        out_specs=[pl.BlockSpec((nb, N_DIR, N_HEAD_GROUPS, V_AUG, HEADS_PER_STEP * DK),
                                lambda i: (i, 0, 0, 0, 0)),
                   pl.BlockSpec((nb, N_GATE, 1), lambda i: (i, 0, 0))],
        out_shape=[jax.ShapeDtypeStruct((bsz, N_DIR, N_HEAD_GROUPS, V_AUG, HEADS_PER_STEP * DK), F32),
                   jax.ShapeDtypeStruct((bsz, N_GATE, 1), F32)],
        compiler_params=pltpu.CompilerParams(dimension_semantics=("arbitrary",),
                                             vmem_limit_bytes=VMEM_LIMIT_BYTES),
        name="ctx_state",
    )(ctx, mod, norm_w, w_n, b_n, w_t, b_t, w_t, b_t)


def _inproj_body(x_ref, mod_ref, nw_ref, nwm_ref, w_ref, b_ref, wt_ref, bt_ref, cw_ref, wco_ref,
                 k_ref, qt_ref, vt_ref, gt_ref, so_ref, ya_ref, gb_ref):
    x = x_ref[0]
    mod = mod_ref[pl.ds(pl.program_id(0), 1), :]
    hb = _modulated_norm(x, nw_ref[...], mod[:, _SH1:_SC1], mod[:, _SC1:_G1]).astype(BF16)
    tile = hb.shape[0]

    def seg(a, b):
        return jnp.dot(hb, w_ref[:, a:b], preferred_element_type=F32) + b_ref[:, a:b]

    k_ref[0] = (seg(_K0, _O0) * (DK ** -0.5)).astype(BF16)
    so_ref[0] = (_sigmoid(seg(_O0, _XIN0)) * nwm_ref[...]).astype(BF16)

    zt = lax.dot_general(wt_ref[:_T_END, :], hb, (((1,), (1,)), ((), ())),
                         preferred_element_type=F32) + bt_ref[:_T_END, :]
    for i in range(tile // CHUNK):
        lanes = slice(i * CHUNK, (i + 1) * CHUNK)
        vt_ref[0, i] = zt[_T_V:_T_Q, lanes].astype(BF16)
        qt_ref[0, i] = zt[_T_Q:_T_GATES, lanes].astype(BF16)
        gt_ref[0, i] = zt[_T_GATES:_T_END, lanes]

    u = seg(_XIN0, _GC0) * seg(_GC0, _GB0)
    col = jnp.bitwise_and(lax.broadcasted_iota(jnp.int32, (tile, 1), 0), GRID_W - 1)
    u_prev = jnp.where(col != 0, pltpu.roll(u, 1, axis=0), 0.0)
    u_next = jnp.where(col != GRID_W - 1, pltpu.roll(u, tile - 1, axis=0), 0.0)
    a = cw_ref[0:1, :] * u_prev + cw_ref[1:2, :] * u + cw_ref[2:3, :] * u_next
    ya = jnp.dot((seg(_GB0, _MA0) * a).astype(BF16), wco_ref[...], preferred_element_type=F32)
    ya_ref[0] = (_sigmoid(seg(_MA0, _MB0)) * ya).astype(BF16)
    gb_ref[0] = _sigmoid(seg(_MB0, _G0)).astype(BF16)


def inproj_call(x, mod, norm_w, mlstm_norm_w, w_n, b_n, w_t, b_t, conv_w, w_conv_out):
    bsz, t, _ = x.shape
    tile = TOKEN_TILE
    tok = lambda width: pl.BlockSpec((1, tile, width), lambda i, j: (i, j, 0))
    tok_t = lambda rows: pl.BlockSpec((1, tile // CHUNK, rows, CHUNK), lambda i, j: (i, j, 0, 0))
    seq = lambda width: jax.ShapeDtypeStruct((bsz, t, width), BF16)
    seq_t = lambda rows, dt: jax.ShapeDtypeStruct((bsz, t // CHUNK, rows, CHUNK), dt)
    resident = [mod, norm_w, mlstm_norm_w, w_n, b_n, w_t, b_t, conv_w, w_conv_out]
    return pl.pallas_call(
        _inproj_body,
        grid=(bsz, t // tile),
        in_specs=[tok(D_MODEL)] + [_resident(a.shape) for a in resident],
        out_specs=[tok(D_QK), tok_t(D_QK), tok_t(D_MLSTM), tok_t(N_GATE), tok(D_MLSTM),
                   tok(D_MODEL), tok(D_MODEL)],
        out_shape=[seq(D_QK), seq_t(D_QK, BF16), seq_t(D_MLSTM, BF16), seq_t(N_GATE, F32), seq(D_MLSTM),
                   seq(D_MODEL), seq(D_MODEL)],
        compiler_params=pltpu.CompilerParams(dimension_semantics=("arbitrary", "arbitrary"),
                                             vmem_limit_bytes=VMEM_LIMIT_BYTES),
        name="inproj_conv",
    )(x, *resident)


SERIES_ROW0 = GATES_PER_GROUP // 2
N_SERIES = N_DIR * HEADS_PER_STEP
N_SPLIT = 3
LOG2E = 1.4426950408889634
_WK, _DECAY, _WINTER, _EXPNEG, _KEYS = 0, 1, 2, 3, 4
_COLM = _KEYS + N_SPLIT
N_TABLES = _COLM + N_SPLIT * N_SERIES


def _running_max_rows(x, reverse):
    n = x.shape[0]
    row = lax.broadcasted_iota(jnp.int32, x.shape, 0)
    shift = 1
    while shift < n:
        if reverse:
            moved = jnp.where(row < n - shift, pltpu.roll(x, n - shift, axis=0), -jnp.inf)
        else:
            moved = jnp.where(row >= shift, pltpu.roll(x, shift, axis=0), -jnp.inf)
        x = jnp.maximum(x, moved)
        shift *= 2
    return x


def _gate_tables(g_ref, m0_ref, tab_ref, n_chunks):
    step_rows = N_HEAD_GROUPS * GATES_PER_GROUP
    rows = n_chunks * step_rows
    g = g_ref[0].reshape(rows, CHUNK)
    lf = _log_sigmoid(g)
    b = _cumsum_lanes(lf, _tri(CHUNK, BF16, lower=False))
    ig = pltpu.roll(g, SERIES_ROW0, axis=0)
    in_tile = lambda idx: jnp.bitwise_and(idx, GATES_PER_GROUP - 1)
    r_in_tile = in_tile(lax.broadcasted_iota(jnp.int32, (rows, 1), 0))
    fwd = r_in_tile < SERIES_ROW0 + HEADS_PER_STEP
    e = b - lf
    tot = b[:, CHUNK - 1:CHUNK]
    col = jnp.where(fwd, b, -e)
    key = ig + jnp.where(fwd, -b, e)
    g_end = jnp.where(fwd, tot + key, key)
    g_max = jnp.max(g_end, axis=1, keepdims=True)

    fwd_step = fwd[:step_rows]
    tile = lambda a, c: a[c * step_rows:(c + 1) * step_rows]
    m = m0_ref[0]
    m_in_steps, m_out_steps = [], []
    for i in range(n_chunks):
        back = n_chunks - 1 - i
        m_in_steps.append(m)
        m = jnp.maximum(jnp.where(fwd_step, tile(tot, i), tile(tot, back)) + m,
                        jnp.where(fwd_step, tile(g_max, i), tile(g_max, back)))
        m_out_steps.append(m)
    by_chunk = lambda steps: jnp.concatenate(
        [jnp.where(fwd_step, steps[c], steps[n_chunks - 1 - c]) for c in range(n_chunks)], axis=0)
    m_in, m_out = by_chunk(m_in_steps), by_chunk(m_out_steps)

    decay = jnp.broadcast_to(jnp.exp(tot + m_in - m_out), (rows, CHUNK))
    lane = lax.broadcasted_iota(jnp.int32, (rows, CHUNK), 1)
    inter = jnp.where(fwd, col, tot + col) + m_in
    key_t = jnp.transpose(key)
    fwd_lane = in_tile(lax.broadcasted_iota(jnp.int32, (1, rows), 1)) < SERIES_ROW0 + HEADS_PER_STEP
    key_max = jnp.transpose(jnp.where(fwd_lane, _running_max_rows(key_t, reverse=False),
                                      _running_max_rows(key_t, reverse=True)))
    m_t = jnp.maximum(col + key_max, inter)
    tables = {_WK: jnp.exp(g_end - m_out),
              _DECAY: jnp.where(lane < DK, decay, pltpu.roll(decay, rows - 1, axis=0)),
              _WINTER: jnp.exp(inter - m_t), _EXPNEG: jnp.exp(-m_t)}
    for x, part in enumerate(_split3(key * LOG2E)):
        tables[_KEYS + x] = part.astype(F32)
    for x, part in enumerate(_split3((col - m_t) * LOG2E)):
        for sidx in range(N_SERIES):
            tables[_COLM + x * N_SERIES + sidx] = jnp.where(r_in_tile == SERIES_ROW0 + sidx,
                                                            part.astype(F32), 0.0)
    for idx, a in tables.items():
        tab_ref[idx] = a.reshape(n_chunks, N_HEAD_GROUPS, GATES_PER_GROUP, CHUNK)


def _mlstm_body(qt_ref, k_ref, vt_ref, so_ref, g_ref, c0_ref, m0_ref, o_ref,
                tab_ref, u_ref, s_ref, st_ref, *, n_chunks):
    hp = HEADS_PER_STEP
    s_i = lax.broadcasted_iota(jnp.int32, (CHUNK, CHUNK), 0)
    t_i = lax.broadcasted_iota(jnp.int32, (CHUNK, CHUNK), 1)
    masks = (s_i <= t_i, s_i >= t_i)
    ones_rows = jnp.where(lax.broadcasted_iota(jnp.int32, (V_AUG - DV, CHUNK), 0) == 0,
                          1.0, 0.0).astype(BF16)
    kq_lane = lax.broadcasted_iota(jnp.int32, (CHUNK, hp * DK), 1)
    kq_row = lax.broadcasted_iota(jnp.int32, (hp * DK, CHUNK), 0)
    state_lane = lax.broadcasted_iota(jnp.int32, (V_AUG, hp * DK), 1)
    series = lambda tab8, d, j: tab8[SERIES_ROW0 + d * hp + j:SERIES_ROW0 + d * hp + j + 1]
    split_rows = N_SPLIT * GATES_PER_GROUP
    ones_split = jnp.ones((split_rows, CHUNK), F32)
    pick_r = jnp.bitwise_and(lax.broadcasted_iota(jnp.int32, (split_rows, N_DIR * CHUNK), 0),
                             GATES_PER_GROUP - 1)
    pick_d = jnp.where(lax.broadcasted_iota(jnp.int32, (split_rows, N_DIR * CHUNK), 1) >= CHUNK, 1, 0)
    pick_series = [jnp.where(pick_r == SERIES_ROW0 + pick_d * hp + j, 1.0, 0.0) for j in range(hp)]

    def chunk_rows(c):
        return pl.ds(pl.multiple_of(c * CHUNK, CHUNK), CHUNK)

    def values_t(c, j):
        return jnp.concatenate([vt_ref[0, c, j * DV:(j + 1) * DV, :], ones_rows], axis=0)

    grp = pl.program_id(1)

    @pl.when(grp == 0)
    def _():
        _gate_tables(g_ref, m0_ref, tab_ref, n_chunks)

    def block_diag_q(qt2):
        zero = jnp.zeros_like(qt2)
        return jnp.concatenate([jnp.where(kq_row < DK, qt2, zero),
                                jnp.where(kq_row >= DK, qt2, zero)], axis=1)

    def chunk_step(c, carry):
        k2 = k_ref[0, chunk_rows(c), :]
        zero = jnp.zeros_like(k2)
        k_bd = jnp.concatenate([jnp.where(kq_lane < DK, k2, zero),
                                jnp.where(kq_lane >= DK, k2, zero)], axis=0)
        wk8 = tab_ref[_WK, c, grp]
        vb = [values_t(c, j) for j in range(hp)]

        def key_weights(d, j):
            tile = jnp.broadcast_to(series(wk8, d, j), (BF16_SUBLANES, CHUNK)).astype(BF16)
            return jnp.concatenate([tile] * (V_AUG // BF16_SUBLANES), axis=0)

        lhs = jnp.concatenate(
            [jnp.concatenate([vb[j] * key_weights(d, j) for j in range(hp)], axis=1)
             for d in range(N_DIR)], axis=0)
        u_ref[c] = jnp.dot(lhs, k_bd, preferred_element_type=F32)

        qk_t = jnp.dot(k2, block_diag_q(qt_ref[0, c]), preferred_element_type=F32)
        key_side = jnp.concatenate([tab_ref[_KEYS + x, c, grp] for x in range(N_SPLIT)] + [ones_split],
                                   axis=0).astype(BF16)
        for j in range(hp):
            query_side = jnp.concatenate(
                [pick_series[j]] +
                [jnp.concatenate([tab_ref[_COLM + x * N_SERIES + d * hp + j, c, grp] for d in range(N_DIR)],
                                 axis=1)
                 for x in range(N_SPLIT)], axis=0).astype(BF16)
            log_d = lax.dot_general(key_side, query_side, (((0,), (0,)), ((), ())),
                                    preferred_element_type=F32)
            for d in range(N_DIR):
                decay = jnp.exp2(jnp.where(masks[d], log_d[:, d * CHUNK:(d + 1) * CHUNK], -jnp.inf))
                st_ref[c, d, j] = (qk_t[:, j * CHUNK:(j + 1) * CHUNK] * decay).astype(BF16)
        return carry

    lax.fori_loop(0, n_chunks, chunk_step, 0, unroll=CHUNK_UNROLL)

    def scan_step(i, carry):
        cf, cb = carry
        back = n_chunks - 1 - i
        s_ref[i, 0, 0] = cf.astype(BF16)
        s_ref[i, 0, 1] = pltpu.roll(cf, DK, axis=1).astype(BF16)
        s_ref[back, 1, 0] = cb.astype(BF16)
        s_ref[back, 1, 1] = pltpu.roll(cb, DK, axis=1).astype(BF16)
        cf = tab_ref[_DECAY, i, grp][SERIES_ROW0:SERIES_ROW0 + 1] * cf + u_ref[i, :V_AUG]
        cb = tab_ref[_DECAY, back, grp][SERIES_ROW0 + hp:SERIES_ROW0 + hp + 1] * cb + u_ref[back, V_AUG:]
        return cf, cb

    lax.fori_loop(0, n_chunks, scan_step, (c0_ref[0, 0, 0], c0_ref[0, 1, 0]), unroll=CHUNK_UNROLL)

    def output_step(c, carry):
        rows = chunk_rows(c)
        qt2 = qt_ref[0, c].astype(F32)
        w_inter8, exp_neg8 = tab_ref[_WINTER, c, grp], tab_ref[_EXPNEG, c, grp]
        no_query = jnp.zeros((DK, CHUNK), BF16)
        for j in range(hp):
            state = jnp.where(state_lane < DK, s_ref[c, 0, j], s_ref[c, 1, 1 - j])
            q_h = qt2[j * DK:(j + 1) * DK]
            qw = [(q_h * series(w_inter8, d, j)).astype(BF16) for d in range(N_DIR)]
            rhs = jnp.concatenate(
                [jnp.concatenate([st_ref[c, d, j] for d in range(N_DIR)], axis=1),
                 jnp.concatenate([qw[0], no_query], axis=1),
                 jnp.concatenate([no_query, qw[1]], axis=1)], axis=0)
            lhs = jnp.concatenate([values_t(c, j), state], axis=1)
            n_all = jnp.dot(lhs, rhs, preferred_element_type=F32)
            h_t = None
            for d in range(N_DIR):
                num = n_all[:, d * CHUNK:(d + 1) * CHUNK]
                r = 1.0 / jnp.maximum(jnp.abs(num[DV:DV + 1]), series(exp_neg8, d, j))
                part = num[:DV] * r
                h_t = part if h_t is None else h_t + part
            hn_t = h_t * lax.rsqrt(jnp.mean(h_t * h_t, axis=0, keepdims=True) + EPS)
            o_ref[0, rows, j * DV:(j + 1) * DV] = (
                jnp.transpose(hn_t) * so_ref[0, rows, j * DV:(j + 1) * DV].astype(F32)).astype(BF16)
        return carry

    lax.fori_loop(0, n_chunks, output_step, 0, unroll=CHUNK_UNROLL)


def mlstm_call(qt, k, vt, so, gates, c0, m0):
    bsz, t, _ = k.shape
    n_chunks = t // CHUNK
    hp = HEADS_PER_STEP
    seq = lambda width: pl.BlockSpec((1, t, width), lambda i, j: (i, 0, j))
    seq_t = lambda rows: pl.BlockSpec((1, n_chunks, rows, CHUNK), lambda i, j: (i, 0, j, 0))
    return pl.pallas_call(
        functools.partial(_mlstm_body, n_chunks=n_chunks),
        grid=(bsz, N_HEAD_GROUPS),
        in_specs=[seq_t(hp * DK), seq(hp * DK), seq_t(hp * DV), seq(hp * DV),
                  pl.BlockSpec((1, n_chunks, N_GATE, CHUNK), lambda i, j: (i, 0, 0, 0)),
                  pl.BlockSpec((1, N_DIR, 1, V_AUG, hp * DK), lambda i, j: (i, 0, j, 0, 0)),
                  pl.BlockSpec((1, N_GATE, 1), lambda i, j: (i, 0, 0))],
        out_specs=seq(hp * DV),
        out_shape=jax.ShapeDtypeStruct((bsz, t, D_MLSTM), BF16),
        scratch_shapes=[pltpu.VMEM((N_TABLES, n_chunks, N_HEAD_GROUPS, GATES_PER_GROUP, CHUNK), F32),
                        pltpu.VMEM((n_chunks, N_DIR * V_AUG, hp * DK), F32),
                        pltpu.VMEM((n_chunks, N_DIR, 2, V_AUG, hp * DK), BF16),
                        pltpu.VMEM((n_chunks, N_DIR, hp, CHUNK, CHUNK), BF16)],
        compiler_params=pltpu.CompilerParams(dimension_semantics=("arbitrary", "arbitrary"),
                                             vmem_limit_bytes=VMEM_LIMIT_BYTES),
        name="mlstm",
    )(qt, k, vt, so, gates, c0, m0)


def _out_body(x_ref, hs_ref, ya_ref, gb_ref, mod_ref, nw2_ref, fnw_ref, wmo_ref, wo_ref, w1_ref, w2_ref, o_ref):
    dot = functools.partial(jnp.dot, preferred_element_type=F32)
    mod = mod_ref[pl.ds(pl.program_id(0), 1), :]
    yb = dot(hs_ref[0], wmo_ref[...])
    merged = ya_ref[0].astype(F32) + gb_ref[0].astype(F32) * yb
    x1 = x_ref[0] + mod[:, _G1:_SH2] * dot(merged.astype(BF16), wo_ref[...])
    hm = _modulated_norm(x1, nw2_ref[...], mod[:, _SH2:_SC2], mod[:, _SC2:_G2]).astype(BF16)
    step = D_FF // FF_SPLIT
    ff = None
    for s in range(FF_SPLIT):
        a = jnp.maximum(dot(hm, w1_ref[:, s * step:(s + 1) * step]), 0.0)
        part = dot((a * a).astype(BF16), w2_ref[s * step:(s + 1) * step, :])
        ff = part if ff is None else ff + part
    x2 = x1 + mod[:, _G2:] * ff
    y = x2 * lax.rsqrt(jnp.mean(x2 * x2, axis=-1, keepdims=True) + EPS)
    o_ref[0] = y * fnw_ref[...]


def out_call(x, hs, ya, gb, mod, norm2_w, final_norm_w, w_mlstm_out, w_out, w_ff1, w_ff2):
    bsz, t, _ = x.shape
    tile = TOKEN_TILE
    tok = pl.BlockSpec((1, tile, D_MODEL), lambda i, j: (i, j, 0))
    resident = [mod, norm2_w, final_norm_w, w_mlstm_out, w_out, w_ff1, w_ff2]
    return pl.pallas_call(
        _out_body,
        grid=(bsz, t // tile),
        in_specs=[tok, tok, tok, tok] + [_resident(a.shape) for a in resident],
        out_specs=tok,
        out_shape=jax.ShapeDtypeStruct((bsz, t, D_MODEL), F32),
        compiler_params=pltpu.CompilerParams(dimension_semantics=("arbitrary", "arbitrary"),
                                             vmem_limit_bytes=VMEM_LIMIT_BYTES),
        name="merge_out_mlp",
    )(x, hs, ya, gb, *resident)


def _layer(x, ctx, mod, norm1_w, w_in, b_in, conv_w, mlstm_norm_w, w_conv_out, w_mlstm_out,
           w_out, norm2_w, w_ff1, w_ff2, final_norm_w):
    nw1 = norm1_w.reshape(1, D_MODEL)
    w_n, w_t = projection_weights(w_in.T)
    b_n = jnp.concatenate([b_in[_REF_K:_REF_V], b_in[_REF_O:]]).reshape(1, -1)
    b_t = jnp.concatenate([b_in[_REF_V:_REF_IG], b_in[_REF_Q:_REF_O], _group_gates(b_in[_REF_IG:_REF_Q]),
                           jnp.zeros((_T_ROWS - _T_END,), F32)]).reshape(-1, 1)

    c0, m0 = ctx_call(ctx, mod, nw1, w_n, b_n, w_t, b_t)
    k, qt, vt, gates, so, ya, gb = inproj_call(x, mod, nw1, mlstm_norm_w.reshape(1, D_MLSTM), w_n, b_n, w_t, b_t,
                                               conv_w, w_conv_out.astype(BF16))
    hs = mlstm_call(qt, k, vt, so, gates, c0, m0)
    return out_call(x, hs, ya, gb, mod, norm2_w.reshape(1, D_MODEL), final_norm_w.reshape(1, D_MODEL),
                    w_mlstm_out.astype(BF16), w_out.astype(BF16), w_ff1.astype(BF16), w_ff2.astype(BF16))


def kernel(x, c, ctx, c_ctx, w_mod, b_mod, norm1_w, w_in, b_in, conv_w, mlstm_norm_w, w_conv_out,
           w_mlstm_out, w_out, norm2_w, w_ff1, w_ff2, final_norm_w):
    depth = w_mod.shape[0]
    assert depth == 1, "the context stream is only advanced through its mLSTM state (single layer)"
    cvecs = jnp.concatenate([c, c_ctx[None, :]], axis=0)
    mod = adaln_call(cvecs, w_mod[0], b_mod[0])
    return _layer(x, ctx, mod, norm1_w[0], w_in[0], b_in[0], conv_w[0],
                  mlstm_norm_w[0], w_conv_out[0], w_mlstm_out[0], w_out[0], norm2_w[0], w_ff1[0],
                  w_ff2[0], final_norm_w)
```

```python
import functools

import jax
import jax.numpy as jnp
from jax import lax
from jax.experimental import pallas as pl
from jax.experimental.pallas import tpu as pltpu

D_MODEL = 1024
CTX_LEN = 256
GRID_W = 64
D_CONV = 1024
N_HEADS = 8
DK = 64
DV = 128
D_MLSTM = N_HEADS * DV
D_QK = N_HEADS * DK
D_FF = 4 * D_MODEL
N_DIR = 2
N_GATE = 2 * N_DIR * N_HEADS
EPS = 1e-6

CHUNK = 128
HEADS_PER_STEP = 2
assert HEADS_PER_STEP == 2 and HEADS_PER_STEP * DK == 128, "a head pair shares one 128-lane tile"
N_HEAD_GROUPS = N_HEADS // HEADS_PER_STEP
GATES_PER_GROUP = N_GATE // N_HEAD_GROUPS
SUBLANES = 8
BF16_SUBLANES = 16
V_AUG = DV + BF16_SUBLANES
TOKEN_TILE = 512
CTX_BATCH = 4
FF_SPLIT = 4

VMEM_LIMIT_BYTES = 56 * 1024 * 1024

_REF_K, _REF_V, _REF_IG, _REF_FG, _REF_Q, _REF_O = 0, 512, 1536, 1552, 1568, 2080
D_IN = 8224
_K0, _O0, _XIN0, _GC0, _GB0, _MA0, _MB0, _G0 = 0, 512, 1536, 2560, 3584, 4608, 5632, 6656
_T_V, _T_Q, _T_GATES, _T_END, _T_ROWS = 0, 1024, 1536, 1568, 2048
_SH1, _SC1, _G1, _SH2, _SC2, _G2 = (i * D_MODEL for i in range(6))

F32 = jnp.float32
BF16 = jnp.bfloat16


def _sigmoid(x):
    return 1.0 / (1.0 + jnp.exp(-x))


def _log_sigmoid(x):
    return jnp.minimum(x, 0.0) - jnp.log(1.0 + jnp.exp(-jnp.abs(x)))


def _split3(x):
    hi = x.astype(BF16)
    r1 = x - hi.astype(F32)
    mid = r1.astype(BF16)
    lo = (r1 - mid.astype(F32)).astype(BF16)
    return hi, mid, lo


def _cumsum_lanes(x, tri_t):
    hi, mid, lo = _split3(x)
    dot = functools.partial(jnp.dot, preferred_element_type=F32)
    return dot(hi, tri_t) + dot(mid, tri_t) + dot(lo, tri_t)


def _tri(n, dtype, lower):
    r = lax.broadcasted_iota(jnp.int32, (n, n), 0)
    c = lax.broadcasted_iota(jnp.int32, (n, n), 1)
    return jnp.where((r >= c) if lower else (r <= c), 1.0, 0.0).astype(dtype)


def _modulated_norm(x, norm_w, shift, scale):
    y = x * lax.rsqrt(jnp.mean(x * x, axis=-1, keepdims=True) + EPS)
    return (y * norm_w) * (1.0 + scale) + shift


def _resident(shape):
    nd = len(shape)
    return pl.BlockSpec(shape, lambda *_: (0,) * nd, pipeline_mode=pl.Buffered(1))


def _adaln_body(c_ref, w_ref, b_ref, o_ref):
    c = c_ref[...]
    s = c * _sigmoid(c)
    dot = functools.partial(jnp.dot, preferred_element_type=F32)
    w = w_ref[...]
    s_hi, w_hi = s.astype(BF16), w.astype(BF16)
    s_lo, w_lo = (s - s_hi.astype(F32)).astype(BF16), (w - w_hi.astype(F32)).astype(BF16)
    o_ref[...] = (dot(s_hi, w_hi) + dot(s_hi, w_lo) + dot(s_lo, w_hi)) + b_ref[...]


def adaln_call(cvecs, w_mod, b_mod):
    n = cvecs.shape[0]
    n_out = w_mod.shape[1]
    tile = 1024
    return pl.pallas_call(
        _adaln_body,
        grid=(n_out // tile,),
        in_specs=[pl.BlockSpec((n, D_MODEL), lambda j: (0, 0)),
                  pl.BlockSpec((D_MODEL, tile), lambda j: (0, j)),
                  pl.BlockSpec((1, tile), lambda j: (0, j))],
        out_specs=pl.BlockSpec((n, tile), lambda j: (0, j)),
        out_shape=jax.ShapeDtypeStruct((n, n_out), F32),
        name="adaln",
    )(cvecs, w_mod, b_mod.reshape(1, n_out))


W_BLOCK = 512


def _transpose_cast_body(w_ref, o_ref):
    o_ref[...] = jnp.transpose(w_ref[...]).astype(BF16)


def _row_cast_body(starts_ref, w_ref, perm_ref, o_ref):
    del starts_ref
    o_ref[...] = w_ref[...].astype(BF16)

    @pl.when(pl.program_id(0) == pl.num_programs(0) - 1)
    def _():
        gates = w_ref[0:N_GATE, :].astype(BF16)
        o_ref[0:N_GATE, :] = jnp.dot(perm_ref[...], gates, preferred_element_type=F32).astype(BF16)


def _group_gates(g):
    lead = g.shape[:-1]
    g = g.reshape(*lead, 2, N_DIR, N_HEAD_GROUPS, HEADS_PER_STEP)
    g = jnp.moveaxis(g, -2, -4)
    return g.reshape(*lead, N_GATE)


def projection_weights(w_in_t):
    n_rest = (D_IN - _REF_O) // W_BLOCK
    blk = W_BLOCK // SUBLANES
    w_n = pl.pallas_call(
        _transpose_cast_body,
        grid=(1 + n_rest,),
        in_specs=[pl.BlockSpec((pl.Element(W_BLOCK), pl.Element(D_MODEL)),
                               lambda i: (SUBLANES * jnp.where(i == 0, _REF_K // SUBLANES,
                                                               _REF_O // SUBLANES + (i - 1) * blk), 0))],
        out_specs=pl.BlockSpec((D_MODEL, W_BLOCK), lambda i: (0, i)),
        out_shape=jax.ShapeDtypeStruct((D_MODEL, (1 + n_rest) * W_BLOCK), BF16),
        name="w_in_normal",
    )(w_in_t)
    starts = jnp.asarray([_REF_V // SUBLANES, _REF_V // SUBLANES + blk, _REF_Q // SUBLANES, _REF_IG // SUBLANES],
                         jnp.int32)
    gate_perm = jnp.eye(N_GATE, dtype=BF16)[_group_gates(jnp.arange(N_GATE))]
    w_t = pl.pallas_call(
        _row_cast_body,
        grid_spec=pltpu.PrefetchScalarGridSpec(
            num_scalar_prefetch=1,
            grid=(_T_ROWS // W_BLOCK,),
            in_specs=[pl.BlockSpec((pl.Element(W_BLOCK), pl.Element(D_MODEL)),
                                   lambda i, starts_ref: (SUBLANES * starts_ref[i], 0)),
                      pl.BlockSpec((N_GATE, N_GATE), lambda i, starts_ref: (0, 0))],
            out_specs=pl.BlockSpec((W_BLOCK, D_MODEL), lambda i, starts_ref: (i, 0))),
        out_shape=jax.ShapeDtypeStruct((_T_ROWS, D_MODEL), BF16),
        name="w_in_transposed",
    )(starts, w_in_t, gate_perm)
    return w_n, w_t


def _ctx_body(ctx_ref, mod_ref, nw_ref, wk_ref, bk_ref, wv_ref, bv_ref, wg_ref, bg_ref, c_ref, m_ref):
    hp = HEADS_PER_STEP
    nb = ctx_ref.shape[0]
    last = mod_ref.shape[0] - 1
    x = ctx_ref[...].reshape(nb * CTX_LEN, D_MODEL)
    hb = _modulated_norm(x, nw_ref[...], mod_ref[last:, _SH1:_SC1], mod_ref[last:, _SC1:_G1]).astype(BF16)
    nt = functools.partial(lax.dot_general, dimension_numbers=(((1,), (1,)), ((), ())),
                           preferred_element_type=F32)
    k_all = ((jnp.dot(hb, wk_ref[...], preferred_element_type=F32) + bk_ref[...]) * (DK ** -0.5)).astype(BF16)
    vt_all = (nt(wv_ref[...], hb) + bv_ref[...]).astype(BF16)
    g_all = nt(wg_ref[...], hb) + bg_ref[...]

    tri_t = _tri(CTX_LEN, BF16, lower=False)
    r_in_tile = jnp.bitwise_and(lax.broadcasted_iota(jnp.int32, (N_GATE, 1), 0), GATES_PER_GROUP - 1)
    fwd = r_in_tile < GATES_PER_GROUP // 2 + hp
    ones_rows = jnp.where(lax.broadcasted_iota(jnp.int32, (V_AUG - DV, CTX_LEN), 0) == 0, 1.0, 0.0)
    lane = lax.broadcasted_iota(jnp.int32, (CTX_LEN, hp * DK), 1)
    for bi in range(nb):
        tokens = slice(bi * CTX_LEN, (bi + 1) * CTX_LEN)
        g, vt, k = g_all[:, tokens], vt_all[:, tokens], k_all[tokens]
        lf = _log_sigmoid(g)
        b = _cumsum_lanes(lf, tri_t)
        ig = pltpu.roll(g, GATES_PER_GROUP // 2, axis=0)
        tot = b[:, CTX_LEN - 1:CTX_LEN]
        g_end = ig + jnp.where(fwd, tot - b, b - lf)
        m = jnp.maximum(tot, jnp.max(g_end, axis=1, keepdims=True))
        m_ref[bi] = m
        wk = jnp.exp(g_end - m)
        for grp in range(N_HEAD_GROUPS):
            k2 = k[:, grp * hp * DK:(grp + 1) * hp * DK]
            zero = jnp.zeros_like(k2)
            k_bd = jnp.concatenate([jnp.where(lane < DK, k2, zero), jnp.where(lane >= DK, k2, zero)], axis=0)
            vf = [jnp.concatenate([vt[(grp * hp + j) * DV:(grp * hp + j + 1) * DV].astype(F32), ones_rows],
                                  axis=0) for j in range(hp)]
            for d in range(N_DIR):
                row0 = grp * GATES_PER_GROUP + GATES_PER_GROUP // 2 + d * hp
                lhs = jnp.concatenate([(vf[j] * wk[row0 + j:row0 + j + 1]).astype(BF16) for j in range(hp)],
                                      axis=1)
                c_ref[bi, d, grp] = jnp.dot(lhs, k_bd, preferred_element_type=F32)


def ctx_call(ctx, mod, norm_w, w_n, b_n, w_t, b_t):
    bsz = ctx.shape[0]
    nb = CTX_BATCH
    const = lambda shape: pl.BlockSpec(shape, lambda i: (0,) * len(shape))
    gate_rows = lambda width: pl.BlockSpec((N_GATE, width), lambda i: (_T_GATES // N_GATE, 0))
    return pl.pallas_call(
        _ctx_body,
        grid=(bsz // nb,),
        in_specs=[pl.BlockSpec((nb, CTX_LEN, D_MODEL), lambda i: (i, 0, 0)),
                  const(mod.shape), const((1, D_MODEL)),
                  const((D_MODEL, D_QK)), const((1, D_QK)),
                  const((D_MLSTM, D_MODEL)), const((D_MLSTM, 1)),
                  gate_rows(D_MODEL), gate_rows(1)],
        out_specs=[pl.BlockSpec((nb, N_DIR, N_HEAD_GROUPS, V_AUG, HEADS_PER_STEP * DK),
                                lambda i: (i, 0, 0, 0, 0)),
                   pl.BlockSpec((nb, N_GATE, 1), lambda i: (i, 0, 0))],
        out_shape=[jax.ShapeDtypeStruct((bsz, N_DIR, N_HEAD_GROUPS, V_AUG, HEADS_PER_STEP * DK), F32),
                   jax.ShapeDtypeStruct((bsz, N_GATE, 1), F32)],
        compiler_params=pltpu.CompilerParams(dimension_semantics=("arbitrary",),
                                             vmem_limit_bytes=VMEM_LIMIT_BYTES),
        name="ctx_state",
    )(ctx, mod, norm_w, w_n, b_n, w_t, b_t, w_t, b_t)


def _inproj_body(x_ref, mod_ref, nw_ref, nwm_ref, w_ref, b_ref, wt_ref, bt_ref, cw_ref, wco_ref,
                 k_ref, qt_ref, vt_ref, gt_ref, so_ref, ya_ref, gb_ref):
    x = x_ref[0]
    mod = mod_ref[pl.ds(pl.program_id(0), 1), :]
    hb = _modulated_norm(x, nw_ref[...], mod[:, _SH1:_SC1], mod[:, _SC1:_G1]).astype(BF16)
    tile = hb.shape[0]

    def seg(a, b):
        return jnp.dot(hb, w_ref[:, a:b], preferred_element_type=F32) + b_ref[:, a:b]

    k_ref[0] = (seg(_K0, _O0) * (DK ** -0.5)).astype(BF16)
    so_ref[0] = (_sigmoid(seg(_O0, _XIN0)) * nwm_ref[...]).astype(BF16)

    zt = lax.dot_general(wt_ref[:_T_END, :], hb, (((1,), (1,)), ((), ())),
                         preferred_element_type=F32) + bt_ref[:_T_END, :]
    for i in range(tile // CHUNK):
        lanes = slice(i * CHUNK, (i + 1) * CHUNK)
        vt_ref[0, i] = zt[_T_V:_T_Q, lanes].astype(BF16)
        qt_ref[0, i] = zt[_T_Q:_T_GATES, lanes].astype(BF16)
        gt_ref[0, i] = zt[_T_GATES:_T_END, lanes]

    u = seg(_XIN0, _GC0) * seg(_GC0, _GB0)
    col = jnp.bitwise_and(lax.broadcasted_iota(jnp.int32, (tile, 1), 0), GRID_W - 1)
    u_prev = jnp.where(col != 0, pltpu.roll(u, 1, axis=0), 0.0)
    u_next = jnp.where(col != GRID_W - 1, pltpu.roll(u, tile - 1, axis=0), 0.0)
    a = cw_ref[0:1, :] * u_prev + cw_ref[1:2, :] * u + cw_ref[2:3, :] * u_next
    ya = jnp.dot((seg(_GB0, _MA0) * a).astype(BF16), wco_ref[...], preferred_element_type=F32)
    ya_ref[0] = (_sigmoid(seg(_MA0, _MB0)) * ya).astype(BF16)
    gb_ref[0] = _sigmoid(seg(_MB0, _G0)).astype(BF16)


def inproj_call(x, mod, norm_w, mlstm_norm_w, w_n, b_n, w_t, b_t, conv_w, w_conv_out):
    bsz, t, _ = x.shape
    tile = TOKEN_TILE
    tok = lambda width: pl.BlockSpec((1, tile, width), lambda i, j: (i, j, 0))
    tok_t = lambda rows: pl.BlockSpec((1, tile // CHUNK, rows, CHUNK), lambda i, j: (i, j, 0, 0))
    seq = lambda width: jax.ShapeDtypeStruct((bsz, t, width), BF16)
    seq_t = lambda rows, dt: jax.ShapeDtypeStruct((bsz, t // CHUNK, rows, CHUNK), dt)
    resident = [mod, norm_w, mlstm_norm_w, w_n, b_n, w_t, b_t, conv_w, w_conv_out]
    return pl.pallas_call(
        _inproj_body,
        grid=(bsz, t // tile),
        in_specs=[tok(D_MODEL)] + [_resident(a.shape) for a in resident],
        out_specs=[tok(D_QK), tok_t(D_QK), tok_t(D_MLSTM), tok_t(N_GATE), tok(D_MLSTM),
                   tok(D_MODEL), tok(D_MODEL)],
        out_shape=[seq(D_QK), seq_t(D_QK, BF16), seq_t(D_MLSTM, BF16), seq_t(N_GATE, F32), seq(D_MLSTM),
                   seq(D_MODEL), seq(D_MODEL)],
        compiler_params=pltpu.CompilerParams(dimension_semantics=("arbitrary", "arbitrary"),
                                             vmem_limit_bytes=VMEM_LIMIT_BYTES),
        name="inproj_conv",
    )(x, *resident)


SERIES_ROW0 = GATES_PER_GROUP // 2
N_SERIES = N_DIR * HEADS_PER_STEP
N_SPLIT = 3
LOG2E = 1.4426950408889634
_WK, _DECAY, _WINTER, _EXPNEG, _KEYS = 0, 1, 2, 3, 4
_COLM = _KEYS + N_SPLIT
N_TABLES = _COLM + N_SPLIT * N_SERIES


def _running_max_rows(x, reverse):
    n = x.shape[0]
    row = lax.broadcasted_iota(jnp.int32, x.shape, 0)
    shift = 1
    while shift < n:
        if reverse:
            moved = jnp.where(row < n - shift, pltpu.roll(x, n - shift, axis=0), -jnp.inf)
        else:
            moved = jnp.where(row >= shift, pltpu.roll(x, shift, axis=0), -jnp.inf)
        x = jnp.maximum(x, moved)
        shift *= 2
    return x


def _gate_tables(g_ref, m0_ref, tab_ref, n_chunks):
    step_rows = N_HEAD_GROUPS * GATES_PER_GROUP
    rows = n_chunks * step_rows
    g = g_ref[0].reshape(rows, CHUNK)
    lf = _log_sigmoid(g)
    b = _cumsum_lanes(lf, _tri(CHUNK, BF16, lower=False))
    ig = pltpu.roll(g, SERIES_ROW0, axis=0)
    in_tile = lambda idx: jnp.bitwise_and(idx, GATES_PER_GROUP - 1)
    r_in_tile = in_tile(lax.broadcasted_iota(jnp.int32, (rows, 1), 0))
    fwd = r_in_tile < SERIES_ROW0 + HEADS_PER_STEP
    e = b - lf
    tot = b[:, CHUNK - 1:CHUNK]
    col = jnp.where(fwd, b, -e)
    key = ig + jnp.where(fwd, -b, e)
    g_end = jnp.where(fwd, tot + key, key)
    g_max = jnp.max(g_end, axis=1, keepdims=True)

    fwd_step = fwd[:step_rows]
    tile = lambda a, c: a[c * step_rows:(c + 1) * step_rows]
    m = m0_ref[0]
    m_in_steps, m_out_steps = [], []
    for i in range(n_chunks):
        back = n_chunks - 1 - i
        m_in_steps.append(m)
        m = jnp.maximum(jnp.where(fwd_step, tile(tot, i), tile(tot, back)) + m,
                        jnp.where(fwd_step, tile(g_max, i), tile(g_max, back)))
        m_out_steps.append(m)
    by_chunk = lambda steps: jnp.concatenate(
        [jnp.where(fwd_step, steps[c], steps[n_chunks - 1 - c]) for c in range(n_chunks)], axis=0)
    m_in, m_out = by_chunk(m_in_steps), by_chunk(m_out_steps)

    decay = jnp.broadcast_to(jnp.exp(tot + m_in - m_out), (rows, CHUNK))
    lane = lax.broadcasted_iota(jnp.int32, (rows, CHUNK), 1)
    inter = jnp.where(fwd, col, tot + col) + m_in
    key_t = jnp.transpose(key)
    fwd_lane = in_tile(lax.broadcasted_iota(jnp.int32, (1, rows), 1)) < SERIES_ROW0 + HEADS_PER_STEP
    key_max = jnp.transpose(jnp.where(fwd_lane, _running_max_rows(key_t, reverse=False),
                                      _running_max_rows(key_t, reverse=True)))
    m_t = jnp.maximum(col + key_max, inter)
    tables = {_WK: jnp.exp(g_end - m_out),
              _DECAY: jnp.where(lane < DK, decay, pltpu.roll(decay, rows - 1, axis=0)),
              _WINTER: jnp.exp(inter - m_t), _EXPNEG: jnp.exp(-m_t)}
    for x, part in enumerate(_split3(key * LOG2E)):
        tables[_KEYS + x] = part.astype(F32)
    for x, part in enumerate(_split3((col - m_t) * LOG2E)):
        for sidx in range(N_SERIES):
            tables[_COLM + x * N_SERIES + sidx] = jnp.where(r_in_tile == SERIES_ROW0 + sidx,
                                                            part.astype(F32), 0.0)
    for idx, a in tables.items():
        tab_ref[idx] = a.reshape(n_chunks, N_HEAD_GROUPS, GATES_PER_GROUP, CHUNK)


def _mlstm_body(qt_ref, k_ref, vt_ref, so_ref, g_ref, c0_ref, m0_ref, o_ref,
                tab_ref, u_ref, s_ref, st_ref, causal_ref, *, n_chunks):
    hp = HEADS_PER_STEP
    ones_rows = jnp.where(lax.broadcasted_iota(jnp.int32, (V_AUG - DV, CHUNK), 0) == 0,
                          1.0, 0.0).astype(BF16)
    low_lanes = lax.broadcasted_iota(jnp.int32, (1, hp * DK), 1) < DK
    series = lambda tab8, d, j: tab8[SERIES_ROW0 + d * hp + j:SERIES_ROW0 + d * hp + j + 1]
    split_rows = N_SPLIT * GATES_PER_GROUP
    ones_split = jnp.ones((split_rows, CHUNK), F32)
    pick_r = jnp.bitwise_and(lax.broadcasted_iota(jnp.int32, (split_rows, N_DIR * CHUNK), 0),
                             GATES_PER_GROUP - 1)
    pick_d = jnp.where(lax.broadcasted_iota(jnp.int32, (split_rows, N_DIR * CHUNK), 1) >= CHUNK, 1, 0)
    pick_series = [jnp.where(pick_r == SERIES_ROW0 + pick_d * hp + j, 1.0, 0.0) for j in range(hp)]

    def chunk_rows(c):
        return slice(c * CHUNK, (c + 1) * CHUNK)

    def for_each_chunk(body, carry):
        for c in range(n_chunks):
            carry = body(c, carry)
        return carry

    def values_t(c, j):
        return jnp.concatenate([vt_ref[0, c, j * DV:(j + 1) * DV, :], ones_rows], axis=0)

    grp = pl.program_id(1)

    @pl.when(grp == 0)
    def _():
        _gate_tables(g_ref, m0_ref, tab_ref, n_chunks)
        s_i = lax.broadcasted_iota(jnp.int32, (CHUNK, CHUNK), 0)
        t_i = lax.broadcasted_iota(jnp.int32, (CHUNK, CHUNK), 1)
        causal_ref[0] = jnp.where(s_i <= t_i, 0.0, -jnp.inf)
        causal_ref[1] = jnp.where(s_i >= t_i, 0.0, -jnp.inf)

    def block_diag_q(qt2):
        zero = jnp.zeros((DK, CHUNK), qt2.dtype)
        return jnp.concatenate([jnp.concatenate([qt2[:DK], zero], axis=0),
                                jnp.concatenate([zero, qt2[DK:]], axis=0)], axis=1)

    def chunk_step(c, carry):
        k2 = k_ref[0, chunk_rows(c), :]
        zero = jnp.zeros_like(k2)
        k_bd = jnp.concatenate([jnp.where(low_lanes, k2, zero),
                                jnp.where(low_lanes, zero, k2)], axis=0)
        wk8 = tab_ref[_WK, c, grp]
        vb = [values_t(c, j) for j in range(hp)]

        def key_weights(d, j):
            tile = jnp.broadcast_to(series(wk8, d, j), (BF16_SUBLANES, CHUNK)).astype(BF16)
            return jnp.concatenate([tile] * (V_AUG // BF16_SUBLANES), axis=0)

        lhs = jnp.concatenate(
            [jnp.concatenate([vb[j] * key_weights(d, j) for j in range(hp)], axis=1)
             for d in range(N_DIR)], axis=0)
        u_ref[c] = jnp.dot(lhs, k_bd, preferred_element_type=F32)

        qk_t = jnp.dot(k2, block_diag_q(qt_ref[0, c]), preferred_element_type=F32)
        key_side = jnp.concatenate([tab_ref[_KEYS + x, c, grp] for x in range(N_SPLIT)] + [ones_split],
                                   axis=0).astype(BF16)
        for j in range(hp):
            query_side = jnp.concatenate(
                [pick_series[j]] +
                [jnp.concatenate([tab_ref[_COLM + x * N_SERIES + d * hp + j, c, grp] for d in range(N_DIR)],
                                 axis=1)
                 for x in range(N_SPLIT)], axis=0).astype(BF16)
            log_d = lax.dot_general(key_side, query_side, (((0,), (0,)), ((), ())),
                                    preferred_element_type=F32)
            for d in range(N_DIR):
                decay = jnp.exp2(log_d[:, d * CHUNK:(d + 1) * CHUNK] + causal_ref[d])
                st_ref[c, d, j] = (qk_t[:, j * CHUNK:(j + 1) * CHUNK] * decay).astype(BF16)
        return carry

    for_each_chunk(chunk_step, 0)

    def scan_step(i, carry):
        cf, cb = carry
        back = n_chunks - 1 - i
        s_ref[i, 0, 0] = cf.astype(BF16)
        s_ref[i, 0, 1] = pltpu.roll(cf, DK, axis=1).astype(BF16)
        s_ref[back, 1, 0] = cb.astype(BF16)
        s_ref[back, 1, 1] = pltpu.roll(cb, DK, axis=1).astype(BF16)
        cf = tab_ref[_DECAY, i, grp][SERIES_ROW0:SERIES_ROW0 + 1] * cf + u_ref[i, :V_AUG]
        cb = tab_ref[_DECAY, back, grp][SERIES_ROW0 + hp:SERIES_ROW0 + hp + 1] * cb + u_ref[back, V_AUG:]
        return cf, cb

    for_each_chunk(scan_step, (c0_ref[0, 0, 0], c0_ref[0, 1, 0]))

    def output_step(c, carry):
        rows = chunk_rows(c)
        qt2 = qt_ref[0, c].astype(F32)
        w_inter8, exp_neg8 = tab_ref[_WINTER, c, grp], tab_ref[_EXPNEG, c, grp]
        no_query = jnp.zeros((DK, CHUNK), BF16)
        for j in range(hp):
            state = jnp.where(low_lanes, s_ref[c, 0, j], s_ref[c, 1, 1 - j])
            q_h = qt2[j * DK:(j + 1) * DK]
            qw = [(q_h * series(w_inter8, d, j)).astype(BF16) for d in range(N_DIR)]
            rhs = jnp.concatenate(
                [jnp.concatenate([st_ref[c, d, j] for d in range(N_DIR)], axis=1),
                 jnp.concatenate([qw[0], no_query], axis=1),
                 jnp.concatenate([no_query, qw[1]], axis=1)], axis=0)
            lhs = jnp.concatenate([values_t(c, j), state], axis=1)
            n_all = jnp.dot(lhs, rhs, preferred_element_type=F32)
            h_t = None
            for d in range(N_DIR):
                num = n_all[:, d * CHUNK:(d + 1) * CHUNK]
                r = 1.0 / jnp.maximum(jnp.abs(num[DV:DV + 1]), series(exp_neg8, d, j))
                part = num[:DV] * r
                h_t = part if h_t is None else h_t + part
            hn_t = h_t * lax.rsqrt(jnp.mean(h_t * h_t, axis=0, keepdims=True) + EPS)
            o_ref[0, rows, j * DV:(j + 1) * DV] = (
                jnp.transpose(hn_t) * so_ref[0, rows, j * DV:(j + 1) * DV].astype(F32)).astype(BF16)
        return carry

    for_each_chunk(output_step, 0)


def mlstm_call(qt, k, vt, so, gates, c0, m0):
    bsz, t, _ = k.shape
    n_chunks = t // CHUNK
    hp = HEADS_PER_STEP
    seq = lambda width: pl.BlockSpec((1, t, width), lambda i, j: (i, 0, j))
    seq_t = lambda rows: pl.BlockSpec((1, n_chunks, rows, CHUNK), lambda i, j: (i, 0, j, 0))
    return pl.pallas_call(
        functools.partial(_mlstm_body, n_chunks=n_chunks),
        grid=(bsz, N_HEAD_GROUPS),
        in_specs=[seq_t(hp * DK), seq(hp * DK), seq_t(hp * DV), seq(hp * DV),
                  pl.BlockSpec((1, n_chunks, N_GATE, CHUNK), lambda i, j: (i, 0, 0, 0)),
                  pl.BlockSpec((1, N_DIR, 1, V_AUG, hp * DK), lambda i, j: (i, 0, j, 0, 0)),
                  pl.BlockSpec((1, N_GATE, 1), lambda i, j: (i, 0, 0))],
        out_specs=seq(hp * DV),
        out_shape=jax.ShapeDtypeStruct((bsz, t, D_MLSTM), BF16),
        scratch_shapes=[pltpu.VMEM((N_TABLES, n_chunks, N_HEAD_GROUPS, GATES_PER_GROUP, CHUNK), F32),
                        pltpu.VMEM((n_chunks, N_DIR * V_AUG, hp * DK), F32),
                        pltpu.VMEM((n_chunks, N_DIR, 2, V_AUG, hp * DK), BF16),
                        pltpu.VMEM((n_chunks, N_DIR, hp, CHUNK, CHUNK), BF16),
                        pltpu.VMEM((N_DIR, CHUNK, CHUNK), F32)],
        compiler_params=pltpu.CompilerParams(dimension_semantics=("arbitrary", "arbitrary"),
                                             vmem_limit_bytes=VMEM_LIMIT_BYTES),
        name="mlstm",
    )(qt, k, vt, so, gates, c0, m0)


def _out_body(x_ref, hs_ref, ya_ref, gb_ref, mod_ref, nw2_ref, fnw_ref, wmo_ref, wo_ref, w1_ref, w2_ref, o_ref):
    dot = functools.partial(jnp.dot, preferred_element_type=F32)
    mod = mod_ref[pl.ds(pl.program_id(0), 1), :]
    yb = dot(hs_ref[0], wmo_ref[...])
    merged = ya_ref[0].astype(F32) + gb_ref[0].astype(F32) * yb
    x1 = x_ref[0] + mod[:, _G1:_SH2] * dot(merged.astype(BF16), wo_ref[...])
    hm = _modulated_norm(x1, nw2_ref[...], mod[:, _SH2:_SC2], mod[:, _SC2:_G2]).astype(BF16)
    step = D_FF // FF_SPLIT
    ff = None
    for s in range(FF_SPLIT):
        a = jnp.maximum(dot(hm, w1_ref[:, s * step:(s + 1) * step]), 0.0)
        part = dot((a * a).astype(BF16), w2_ref[s * step:(s + 1) * step, :])
        ff = part if ff is None else ff + part
    x2 = x1 + mod[:, _G2:] * ff
    y = x2 * lax.rsqrt(jnp.mean(x2 * x2, axis=-1, keepdims=True) + EPS)
    o_ref[0] = y * fnw_ref[...]


def out_call(x, hs, ya, gb, mod, norm2_w, final_norm_w, w_mlstm_out, w_out, w_ff1, w_ff2):
    bsz, t, _ = x.shape
    tile = TOKEN_TILE
    tok = pl.BlockSpec((1, tile, D_MODEL), lambda i, j: (i, j, 0))
    resident = [mod, norm2_w, final_norm_w, w_mlstm_out, w_out, w_ff1, w_ff2]
    return pl.pallas_call(
        _out_body,
        grid=(bsz, t // tile),
        in_specs=[tok, tok, tok, tok] + [_resident(a.shape) for a in resident],
        out_specs=tok,
        out_shape=jax.ShapeDtypeStruct((bsz, t, D_MODEL), F32),
        compiler_params=pltpu.CompilerParams(dimension_semantics=("arbitrary", "arbitrary"),
                                             vmem_limit_bytes=VMEM_LIMIT_BYTES),
        name="merge_out_mlp",
    )(x, hs, ya, gb, *resident)


def _layer(x, ctx, mod, norm1_w, w_in, b_in, conv_w, mlstm_norm_w, w_conv_out, w_mlstm_out,
           w_out, norm2_w, w_ff1, w_ff2, final_norm_w):
    nw1 = norm1_w.reshape(1, D_MODEL)
    w_n, w_t = projection_weights(w_in.T)
    b_n = jnp.concatenate([b_in[_REF_K:_REF_V], b_in[_REF_O:]]).reshape(1, -1)
    b_t = jnp.concatenate([b_in[_REF_V:_REF_IG], b_in[_REF_Q:_REF_O], _group_gates(b_in[_REF_IG:_REF_Q]),
                           jnp.zeros((_T_ROWS - _T_END,), F32)]).reshape(-1, 1)

    c0, m0 = ctx_call(ctx, mod, nw1, w_n, b_n, w_t, b_t)
    k, qt, vt, gates, so, ya, gb = inproj_call(x, mod, nw1, mlstm_norm_w.reshape(1, D_MLSTM), w_n, b_n, w_t, b_t,
                                               conv_w, w_conv_out.astype(BF16))
    hs = mlstm_call(qt, k, vt, so, gates, c0, m0)
    return out_call(x, hs, ya, gb, mod, norm2_w.reshape(1, D_MODEL), final_norm_w.reshape(1, D_MODEL),
                    w_mlstm_out.astype(BF16), w_out.astype(BF16), w_ff1.astype(BF16), w_ff2.astype(BF16))


def kernel(x, c, ctx, c_ctx, w_mod, b_mod, norm1_w, w_in, b_in, conv_w, mlstm_norm_w, w_conv_out,
           w_mlstm_out, w_out, norm2_w, w_ff1, w_ff2, final_norm_w):
    depth = w_mod.shape[0]
    assert depth == 1, "the context stream is only advanced through its mLSTM state (single layer)"
    cvecs = jnp.concatenate([c, c_ctx[None, :]], axis=0)
    mod = adaln_call(cvecs, w_mod[0], b_mod[0])
    return _layer(x, ctx, mod, norm1_w[0], w_in[0], b_in[0], conv_w[0],
                  mlstm_norm_w[0], w_conv_out[0], w_mlstm_out[0], w_out[0], norm2_w[0], w_ff1[0],
                  w_ff2[0], final_norm_w)
```

```python
import functools

import jax
import jax.numpy as jnp
from jax import lax
from jax.experimental import pallas as pl
from jax.experimental.pallas import tpu as pltpu

D_MODEL = 1024
CTX_LEN = 256
GRID_W = 64
D_CONV = 1024
N_HEADS = 8
DK = 64
DV = 128
D_MLSTM = N_HEADS * DV
D_QK = N_HEADS * DK
D_FF = 4 * D_MODEL
N_DIR = 2
N_GATE = 2 * N_DIR * N_HEADS
EPS = 1e-6

CHUNK = 128
HEADS_PER_STEP = 2
assert HEADS_PER_STEP == 2 and HEADS_PER_STEP * DK == 128, "a head pair shares one 128-lane tile"
N_HEAD_GROUPS = N_HEADS // HEADS_PER_STEP
GATES_PER_GROUP = N_GATE // N_HEAD_GROUPS
SUBLANES = 8
BF16_SUBLANES = 16
V_AUG = DV + BF16_SUBLANES
TOKEN_TILE = 512
CTX_BATCH = 4

VMEM_LIMIT_BYTES = 56 * 1024 * 1024

_REF_K, _REF_V, _REF_IG, _REF_FG, _REF_Q, _REF_O = 0, 512, 1536, 1552, 1568, 2080
D_IN = 8224
_K0, _O0, _XIN0, _GC0, _GB0, _MA0, _MB0, _G0 = 0, 512, 1536, 2560, 3584, 4608, 5632, 6656
_T_V, _T_Q, _T_GATES, _T_END, _T_ROWS = 0, 1024, 1536, 1568, 2048
_SH1, _SC1, _G1, _SH2, _SC2, _G2 = (i * D_MODEL for i in range(6))

F32 = jnp.float32
BF16 = jnp.bfloat16


def _sigmoid(x):
    return 1.0 / (1.0 + jnp.exp(-x))


def _log_sigmoid(x):
    return jnp.minimum(x, 0.0) - jnp.log(1.0 + jnp.exp(-jnp.abs(x)))


def _split3(x):
    hi = x.astype(BF16)
    r1 = x - hi.astype(F32)
    mid = r1.astype(BF16)
    lo = (r1 - mid.astype(F32)).astype(BF16)
    return hi, mid, lo


def _cumsum_lanes(x, tri_t):
    hi, mid, lo = _split3(x)
    dot = functools.partial(jnp.dot, preferred_element_type=F32)
    return dot(hi, tri_t) + dot(mid, tri_t) + dot(lo, tri_t)


def _tri(n, dtype, lower):
    r = lax.broadcasted_iota(jnp.int32, (n, n), 0)
    c = lax.broadcasted_iota(jnp.int32, (n, n), 1)
    return jnp.where((r >= c) if lower else (r <= c), 1.0, 0.0).astype(dtype)


def _modulated_norm(x, norm_w, shift, scale):
    y = x * lax.rsqrt(jnp.mean(x * x, axis=-1, keepdims=True) + EPS)
    return (y * norm_w) * (1.0 + scale) + shift


def _resident(shape):
    nd = len(shape)
    return pl.BlockSpec(shape, lambda *_: (0,) * nd, pipeline_mode=pl.Buffered(1))


def _adaln_body(c_ref, w_ref, b_ref, o_ref):
    c = c_ref[...]
    s = c * _sigmoid(c)
    dot = functools.partial(jnp.dot, preferred_element_type=F32)
    w = w_ref[...]
    s_hi, w_hi = s.astype(BF16), w.astype(BF16)
    s_lo, w_lo = (s - s_hi.astype(F32)).astype(BF16), (w - w_hi.astype(F32)).astype(BF16)
    o_ref[...] = (dot(s_hi, w_hi) + dot(s_hi, w_lo) + dot(s_lo, w_hi)) + b_ref[...]


def adaln_call(cvecs, w_mod, b_mod):
    n = cvecs.shape[0]
    n_out = w_mod.shape[1]
    tile = 1024
    return pl.pallas_call(
        _adaln_body,
        grid=(n_out // tile,),
        in_specs=[pl.BlockSpec((n, D_MODEL), lambda j: (0, 0)),
                  pl.BlockSpec((D_MODEL, tile), lambda j: (0, j)),
                  pl.BlockSpec((1, tile), lambda j: (0, j))],
        out_specs=pl.BlockSpec((n, tile), lambda j: (0, j)),
        out_shape=jax.ShapeDtypeStruct((n, n_out), F32),
        name="adaln",
    )(cvecs, w_mod, b_mod.reshape(1, n_out))


W_BLOCK = 512


def _transpose_cast_body(w_ref, o_ref):
    o_ref[...] = jnp.transpose(w_ref[...]).astype(BF16)


def _row_cast_body(starts_ref, w_ref, perm_ref, o_ref):
    del starts_ref
    o_ref[...] = w_ref[...].astype(BF16)

    @pl.when(pl.program_id(0) == pl.num_programs(0) - 1)
    def _():
        gates = w_ref[0:N_GATE, :].astype(BF16)
        o_ref[0:N_GATE, :] = jnp.dot(perm_ref[...], gates, preferred_element_type=F32).astype(BF16)


def _group_gates(g):
    lead = g.shape[:-1]
    g = g.reshape(*lead, 2, N_DIR, N_HEAD_GROUPS, HEADS_PER_STEP)
    g = jnp.moveaxis(g, -2, -4)
    return g.reshape(*lead, N_GATE)


def projection_weights(w_in_t):
    n_rest = (D_IN - _REF_O) // W_BLOCK
    blk = W_BLOCK // SUBLANES
    w_n = pl.pallas_call(
        _transpose_cast_body,
        grid=(1 + n_rest,),
        in_specs=[pl.BlockSpec((pl.Element(W_BLOCK), pl.Element(D_MODEL)),
                               lambda i: (SUBLANES * jnp.where(i == 0, _REF_K // SUBLANES,
                                                               _REF_O // SUBLANES + (i - 1) * blk), 0))],
        out_specs=pl.BlockSpec((D_MODEL, W_BLOCK), lambda i: (0, i)),
        out_shape=jax.ShapeDtypeStruct((D_MODEL, (1 + n_rest) * W_BLOCK), BF16),
        name="w_in_normal",
    )(w_in_t)
    starts = jnp.asarray([_REF_V // SUBLANES, _REF_V // SUBLANES + blk, _REF_Q // SUBLANES, _REF_IG // SUBLANES],
                         jnp.int32)
    gate_perm = jnp.eye(N_GATE, dtype=BF16)[_group_gates(jnp.arange(N_GATE))]
    w_t = pl.pallas_call(
        _row_cast_body,
        grid_spec=pltpu.PrefetchScalarGridSpec(
            num_scalar_prefetch=1,
            grid=(_T_ROWS // W_BLOCK,),
            in_specs=[pl.BlockSpec((pl.Element(W_BLOCK), pl.Element(D_MODEL)),
                                   lambda i, starts_ref: (SUBLANES * starts_ref[i], 0)),
                      pl.BlockSpec((N_GATE, N_GATE), lambda i, starts_ref: (0, 0))],
            out_specs=pl.BlockSpec((W_BLOCK, D_MODEL), lambda i, starts_ref: (i, 0))),
        out_shape=jax.ShapeDtypeStruct((_T_ROWS, D_MODEL), BF16),
        name="w_in_transposed",
    )(starts, w_in_t, gate_perm)
    return w_n, w_t


def _ctx_body(ctx_ref, mod_ref, nw_ref, wk_ref, bk_ref, wv_ref, bv_ref, wg_ref, bg_ref, c_ref, m_ref):
    hp = HEADS_PER_STEP
    nb = ctx_ref.shape[0]
    last = mod_ref.shape[0] - 1
    x = ctx_ref[...].reshape(nb * CTX_LEN, D_MODEL)
    hb = _modulated_norm(x, nw_ref[...], mod_ref[last:, _SH1:_SC1], mod_ref[last:, _SC1:_G1]).astype(BF16)
    nt = functools.partial(lax.dot_general, dimension_numbers=(((1,), (1,)), ((), ())),
                           preferred_element_type=F32)
    k_all = ((jnp.dot(hb, wk_ref[...], preferred_element_type=F32) + bk_ref[...]) * (DK ** -0.5)).astype(BF16)
    vt_all = (nt(wv_ref[...], hb) + bv_ref[...]).astype(BF16)
    g_all = nt(wg_ref[...], hb) + bg_ref[...]

    tri_t = _tri(CTX_LEN, BF16, lower=False)
    r_in_tile = jnp.bitwise_and(lax.broadcasted_iota(jnp.int32, (N_GATE, 1), 0), GATES_PER_GROUP - 1)
    fwd = r_in_tile < GATES_PER_GROUP // 2 + hp
    ones_rows = jnp.where(lax.broadcasted_iota(jnp.int32, (V_AUG - DV, CTX_LEN), 0) == 0, 1.0, 0.0)
    lane = lax.broadcasted_iota(jnp.int32, (CTX_LEN, hp * DK), 1)
    for bi in range(nb):
        tokens = slice(bi * CTX_LEN, (bi + 1) * CTX_LEN)
        g, vt, k = g_all[:, tokens], vt_all[:, tokens], k_all[tokens]
        lf = _log_sigmoid(g)
        b = _cumsum_lanes(lf, tri_t)
        ig = pltpu.roll(g, GATES_PER_GROUP // 2, axis=0)
        tot = b[:, CTX_LEN - 1:CTX_LEN]
        g_end = ig + jnp.where(fwd, tot - b, b - lf)
        m = jnp.maximum(tot, jnp.max(g_end, axis=1, keepdims=True))
        m_ref[bi] = m
        wk = jnp.exp(g_end - m)
        for grp in range(N_HEAD_GROUPS):
            k2 = k[:, grp * hp * DK:(grp + 1) * hp * DK]
            zero = jnp.zeros_like(k2)
            k_bd = jnp.concatenate([jnp.where(lane < DK, k2, zero), jnp.where(lane >= DK, k2, zero)], axis=0)
            vf = [jnp.concatenate([vt[(grp * hp + j) * DV:(grp * hp + j + 1) * DV].astype(F32), ones_rows],
                                  axis=0) for j in range(hp)]
            for d in range(N_DIR):
                row0 = grp * GATES_PER_GROUP + GATES_PER_GROUP // 2 + d * hp
                lhs = jnp.concatenate([(vf[j] * wk[row0 + j:row0 + j + 1]).astype(BF16) for j in range(hp)],
                                      axis=1)
                c_ref[bi, d, grp] = jnp.dot(lhs, k_bd, preferred_element_type=F32)


def ctx_call(ctx, mod, norm_w, w_n, b_n, w_t, b_t):
    bsz = ctx.shape[0]
    nb = CTX_BATCH
    const = lambda shape: pl.BlockSpec(shape, lambda i: (0,) * len(shape))
    gate_rows = lambda width: pl.BlockSpec((N_GATE, width), lambda i: (_T_GATES // N_GATE, 0))
    return pl.pallas_call(
        _ctx_body,
        grid=(bsz // nb,),
        in_specs=[pl.BlockSpec((nb, CTX_LEN, D_MODEL), lambda i: (i, 0, 0)),
                  const(mod.shape), const((1, D_MODEL)),
                  const((D_MODEL, D_QK)), const((1, D_QK)),
                  const((D_MLSTM, D_MODEL)), const((D_MLSTM, 1)),
                  gate_rows(D_MODEL), gate_rows(1)],
        out_specs=[pl.BlockSpec((nb, N_DIR, N_HEAD_GROUPS, V_AUG, HEADS_PER_STEP * DK),
                                lambda i: (i, 0, 0, 0, 0)),
                   pl.BlockSpec((nb, N_GATE, 1), lambda i: (i, 0, 0))],
        out_shape=[jax.ShapeDtypeStruct((bsz, N_DIR, N_HEAD_GROUPS, V_AUG, HEADS_PER_STEP * DK), F32),
                   jax.ShapeDtypeStruct((bsz, N_GATE, 1), F32)],
        compiler_params=pltpu.CompilerParams(dimension_semantics=("arbitrary",),
                                             vmem_limit_bytes=VMEM_LIMIT_BYTES),
        name="ctx_state",
    )(ctx, mod, norm_w, w_n, b_n, w_t, b_t, w_t, b_t)


def _inproj_body(x_ref, mod_ref, nw_ref, nwm_ref, w_ref, b_ref, wt_ref, bt_ref, cw_ref, wco_ref,
                 k_ref, qt_ref, vt_ref, gt_ref, so_ref, ya_ref, gb_ref):
    x = x_ref[0]
    mod = mod_ref[pl.ds(pl.program_id(0), 1), :]
    hb = _modulated_norm(x, nw_ref[...], mod[:, _SH1:_SC1], mod[:, _SC1:_G1]).astype(BF16)
    tile = hb.shape[0]

    def seg(a, b):
        return jnp.dot(hb, w_ref[:, a:b], preferred_element_type=F32) + b_ref[:, a:b]

    k_ref[0] = (seg(_K0, _O0) * (DK ** -0.5)).astype(BF16)
    so_ref[0] = (_sigmoid(seg(_O0, _XIN0)) * nwm_ref[...]).astype(BF16)

    zt = lax.dot_general(wt_ref[:_T_END, :], hb, (((1,), (1,)), ((), ())),
                         preferred_element_type=F32) + bt_ref[:_T_END, :]
    for i in range(tile // CHUNK):
        lanes = slice(i * CHUNK, (i + 1) * CHUNK)
        vt_ref[0, i] = zt[_T_V:_T_Q, lanes].astype(BF16)
        qt_ref[0, i] = zt[_T_Q:_T_GATES, lanes].astype(BF16)
        gt_ref[0, i] = zt[_T_GATES:_T_END, lanes]

    u = seg(_XIN0, _GC0) * seg(_GC0, _GB0)
    col = jnp.bitwise_and(lax.broadcasted_iota(jnp.int32, (tile, 1), 0), GRID_W - 1)
    u_prev = jnp.where(col != 0, pltpu.roll(u, 1, axis=0), 0.0)
    u_next = jnp.where(col != GRID_W - 1, pltpu.roll(u, tile - 1, axis=0), 0.0)
    a = cw_ref[0:1, :] * u_prev + cw_ref[1:2, :] * u + cw_ref[2:3, :] * u_next
    ya = jnp.dot((seg(_GB0, _MA0) * a).astype(BF16), wco_ref[...], preferred_element_type=F32)
    ya_ref[0] = (_sigmoid(seg(_MA0, _MB0)) * ya).astype(BF16)
    gb_ref[0] = _sigmoid(seg(_MB0, _G0)).astype(BF16)


def inproj_call(x, mod, norm_w, mlstm_norm_w, w_n, b_n, w_t, b_t, conv_w, w_conv_out):
    bsz, t, _ = x.shape
    tile = TOKEN_TILE
    tok = lambda width: pl.BlockSpec((1, tile, width), lambda i, j: (i, j, 0))
    tok_t = lambda rows: pl.BlockSpec((1, tile // CHUNK, rows, CHUNK), lambda i, j: (i, j, 0, 0))
    seq = lambda width: jax.ShapeDtypeStruct((bsz, t, width), BF16)
    seq_t = lambda rows, dt: jax.ShapeDtypeStruct((bsz, t // CHUNK, rows, CHUNK), dt)
    resident = [mod, norm_w, mlstm_norm_w, w_n, b_n, w_t, b_t, conv_w, w_conv_out]
    return pl.pallas_call(
        _inproj_body,
        grid=(bsz, t // tile),
        in_specs=[tok(D_MODEL)] + [_resident(a.shape) for a in resident],
        out_specs=[tok(D_QK), tok_t(D_QK), tok_t(D_MLSTM), tok_t(N_GATE), tok(D_MLSTM),
                   tok(D_MODEL), tok(D_MODEL)],
        out_shape=[seq(D_QK), seq_t(D_QK, BF16), seq_t(D_MLSTM, BF16), seq_t(N_GATE, F32), seq(D_MLSTM),
                   seq(D_MODEL), seq(D_MODEL)],
        compiler_params=pltpu.CompilerParams(dimension_semantics=("arbitrary", "arbitrary"),
                                             vmem_limit_bytes=VMEM_LIMIT_BYTES),
        name="inproj_conv",
    )(x, *resident)


SERIES_ROW0 = GATES_PER_GROUP // 2
N_SERIES = N_DIR * HEADS_PER_STEP
N_SPLIT = 3
LOG2E = 1.4426950408889634
_WK, _DECAY, _WINTER, _EXPNEG, _KEYS = 0, 1, 2, 3, 4
_COLM = _KEYS + N_SPLIT
N_TABLES = _COLM + N_SPLIT * N_SERIES


def _running_max_rows(x, reverse):
    n = x.shape[0]
    row = lax.broadcasted_iota(jnp.int32, x.shape, 0)
    shift = 1
    while shift < n:
        if reverse:
            moved = jnp.where(row < n - shift, pltpu.roll(x, n - shift, axis=0), -jnp.inf)
        else:
            moved = jnp.where(row >= shift, pltpu.roll(x, shift, axis=0), -jnp.inf)
        x = jnp.maximum(x, moved)
        shift *= 2
    return x


def _gate_tables(g_ref, m0_ref, tab_ref, n_chunks):
    step_rows = N_HEAD_GROUPS * GATES_PER_GROUP
    rows = n_chunks * step_rows
    g = g_ref[0].reshape(rows, CHUNK)
    lf = _log_sigmoid(g)
    b = _cumsum_lanes(lf, _tri(CHUNK, BF16, lower=False))
    ig = pltpu.roll(g, SERIES_ROW0, axis=0)
    in_tile = lambda idx: jnp.bitwise_and(idx, GATES_PER_GROUP - 1)
    r_in_tile = in_tile(lax.broadcasted_iota(jnp.int32, (rows, 1), 0))
    fwd = r_in_tile < SERIES_ROW0 + HEADS_PER_STEP
    e = b - lf
    tot = b[:, CHUNK - 1:CHUNK]
    col = jnp.where(fwd, b, -e)
    key = ig + jnp.where(fwd, -b, e)
    g_end = jnp.where(fwd, tot + key, key)
    g_max = jnp.max(g_end, axis=1, keepdims=True)

    fwd_step = fwd[:step_rows]
    tile = lambda a, c: a[c * step_rows:(c + 1) * step_rows]
    m = m0_ref[0]
    m_in_steps, m_out_steps = [], []
    for i in range(n_chunks):
        back = n_chunks - 1 - i
        m_in_steps.append(m)
        m = jnp.maximum(jnp.where(fwd_step, tile(tot, i), tile(tot, back)) + m,
                        jnp.where(fwd_step, tile(g_max, i), tile(g_max, back)))
        m_out_steps.append(m)
    by_chunk = lambda steps: jnp.concatenate(
        [jnp.where(fwd_step, steps[c], steps[n_chunks - 1 - c]) for c in range(n_chunks)], axis=0)
    m_in, m_out = by_chunk(m_in_steps), by_chunk(m_out_steps)

    decay = jnp.broadcast_to(jnp.exp(tot + m_in - m_out), (rows, CHUNK))
    lane = lax.broadcasted_iota(jnp.int32, (rows, CHUNK), 1)
    inter = jnp.where(fwd, col, tot + col) + m_in
    key_t = jnp.transpose(key)
    fwd_lane = in_tile(lax.broadcasted_iota(jnp.int32, (1, rows), 1)) < SERIES_ROW0 + HEADS_PER_STEP
    key_max = jnp.transpose(jnp.where(fwd_lane, _running_max_rows(key_t, reverse=False),
                                      _running_max_rows(key_t, reverse=True)))
    m_t = jnp.maximum(col + key_max, inter)
    tables = {_WK: jnp.exp(g_end - m_out),
              _DECAY: jnp.where(lane < DK, decay, pltpu.roll(decay, rows - 1, axis=0)),
              _WINTER: jnp.exp(inter - m_t), _EXPNEG: jnp.exp(-m_t)}
    for x, part in enumerate(_split3(key * LOG2E)):
        tables[_KEYS + x] = part.astype(F32)
    for x, part in enumerate(_split3((col - m_t) * LOG2E)):
        for sidx in range(N_SERIES):
            tables[_COLM + x * N_SERIES + sidx] = jnp.where(r_in_tile == SERIES_ROW0 + sidx,
                                                            part.astype(F32), 0.0)
    for idx, a in tables.items():
        tab_ref[idx] = a.reshape(n_chunks, N_HEAD_GROUPS, GATES_PER_GROUP, CHUNK)


def _mlstm_body(qt_ref, k_ref, vt_ref, so_ref, g_ref, c0_ref, m0_ref, o_ref,
                tab_ref, u_ref, s_ref, st_ref, causal_ref, *, n_chunks):
    hp = HEADS_PER_STEP
    ones_rows = jnp.where(lax.broadcasted_iota(jnp.int32, (V_AUG - DV, CHUNK), 0) == 0,
                          1.0, 0.0).astype(BF16)
    low_lanes = lax.broadcasted_iota(jnp.int32, (1, hp * DK), 1) < DK
    series = lambda tab8, d, j: tab8[SERIES_ROW0 + d * hp + j:SERIES_ROW0 + d * hp + j + 1]
    split_rows = N_SPLIT * GATES_PER_GROUP
    ones_split = jnp.ones((split_rows, CHUNK), F32)
    pick_r = jnp.bitwise_and(lax.broadcasted_iota(jnp.int32, (split_rows, N_DIR * CHUNK), 0),
                             GATES_PER_GROUP - 1)
    pick_d = jnp.where(lax.broadcasted_iota(jnp.int32, (split_rows, N_DIR * CHUNK), 1) >= CHUNK, 1, 0)
    pick_series = [jnp.where(pick_r == SERIES_ROW0 + pick_d * hp + j, 1.0, 0.0) for j in range(hp)]

    def chunk_rows(c):
        return slice(c * CHUNK, (c + 1) * CHUNK)

    def for_each_chunk(body, carry):
        for c in range(n_chunks):
            carry = body(c, carry)
        return carry

    def values_t(c, j):
        return jnp.concatenate([vt_ref[0, c, j * DV:(j + 1) * DV, :], ones_rows], axis=0)

    grp = pl.program_id(1)

    @pl.when(grp == 0)
    def _():
        _gate_tables(g_ref, m0_ref, tab_ref, n_chunks)
        s_i = lax.broadcasted_iota(jnp.int32, (CHUNK, CHUNK), 0)
        t_i = lax.broadcasted_iota(jnp.int32, (CHUNK, CHUNK), 1)
        causal_ref[0] = jnp.where(s_i <= t_i, 0.0, -jnp.inf)
        causal_ref[1] = jnp.where(s_i >= t_i, 0.0, -jnp.inf)

    def block_diag_q(qt2):
        zero = jnp.zeros((DK, CHUNK), qt2.dtype)
        return jnp.concatenate([jnp.concatenate([qt2[:DK], zero], axis=0),
                                jnp.concatenate([zero, qt2[DK:]], axis=0)], axis=1)

    def chunk_step(c, carry):
        k2 = k_ref[0, chunk_rows(c), :]
        zero = jnp.zeros_like(k2)
        k_bd = jnp.concatenate([jnp.where(low_lanes, k2, zero),
                                jnp.where(low_lanes, zero, k2)], axis=0)
        wk8 = tab_ref[_WK, c, grp]
        vb = [values_t(c, j) for j in range(hp)]

        def key_weights(d, j):
            tile = jnp.broadcast_to(series(wk8, d, j), (BF16_SUBLANES, CHUNK)).astype(BF16)
            return jnp.concatenate([tile] * (V_AUG // BF16_SUBLANES), axis=0)

        lhs = jnp.concatenate(
            [jnp.concatenate([vb[j] * key_weights(d, j) for j in range(hp)], axis=1)
             for d in range(N_DIR)], axis=0)
        u_ref[c] = jnp.dot(lhs, k_bd, preferred_element_type=F32)

        qk_t = jnp.dot(k2, block_diag_q(qt_ref[0, c]), preferred_element_type=F32)
        key_side = jnp.concatenate([tab_ref[_KEYS + x, c, grp] for x in range(N_SPLIT)] + [ones_split],
                                   axis=0).astype(BF16)
        for j in range(hp):
            query_side = jnp.concatenate(
                [pick_series[j]] +
                [jnp.concatenate([tab_ref[_COLM + x * N_SERIES + d * hp + j, c, grp] for d in range(N_DIR)],
                                 axis=1)
                 for x in range(N_SPLIT)], axis=0).astype(BF16)
            log_d = lax.dot_general(key_side, query_side, (((0,), (0,)), ((), ())),
                                    preferred_element_type=F32)
            for d in range(N_DIR):
                decay = jnp.exp2(log_d[:, d * CHUNK:(d + 1) * CHUNK] + causal_ref[d])
                st_ref[c, d, j] = (qk_t[:, j * CHUNK:(j + 1) * CHUNK] * decay).astype(BF16)
        return carry

    for_each_chunk(chunk_step, 0)

    def scan_step(i, carry):
        cf, cb = carry
        back = n_chunks - 1 - i
        s_ref[i, 0, 0] = cf.astype(BF16)
        s_ref[i, 0, 1] = pltpu.roll(cf, DK, axis=1).astype(BF16)
        s_ref[back, 1, 0] = cb.astype(BF16)
        s_ref[back, 1, 1] = pltpu.roll(cb, DK, axis=1).astype(BF16)
        cf = tab_ref[_DECAY, i, grp][SERIES_ROW0:SERIES_ROW0 + 1] * cf + u_ref[i, :V_AUG]
        cb = tab_ref[_DECAY, back, grp][SERIES_ROW0 + hp:SERIES_ROW0 + hp + 1] * cb + u_ref[back, V_AUG:]
        return cf, cb

    for_each_chunk(scan_step, (c0_ref[0, 0, 0], c0_ref[0, 1, 0]))

    def output_step(c, carry):
        rows = chunk_rows(c)
        qt2 = qt_ref[0, c].astype(F32)
        w_inter8, exp_neg8 = tab_ref[_WINTER, c, grp], tab_ref[_EXPNEG, c, grp]
        no_query = jnp.zeros((DK, CHUNK), BF16)
        for j in range(hp):
            state = jnp.where(low_lanes, s_ref[c, 0, j], s_ref[c, 1, 1 - j])
            q_h = qt2[j * DK:(j + 1) * DK]
            qw = [(q_h * series(w_inter8, d, j)).astype(BF16) for d in range(N_DIR)]
            rhs = jnp.concatenate(
                [jnp.concatenate([st_ref[c, d, j] for d in range(N_DIR)], axis=1),
                 jnp.concatenate([qw[0], no_query], axis=1),
                 jnp.concatenate([no_query, qw[1]], axis=1)], axis=0)
            lhs = jnp.concatenate([values_t(c, j), state], axis=1)
            n_all = jnp.dot(lhs, rhs, preferred_element_type=F32)
            h_t = None
            for d in range(N_DIR):
                num = n_all[:, d * CHUNK:(d + 1) * CHUNK]
                r = 1.0 / jnp.maximum(jnp.abs(num[DV:DV + 1]), series(exp_neg8, d, j))
                part = num[:DV] * r
                h_t = part if h_t is None else h_t + part
            hn_t = h_t * lax.rsqrt(jnp.mean(h_t * h_t, axis=0, keepdims=True) + EPS)
            o_ref[0, rows, j * DV:(j + 1) * DV] = (
                jnp.transpose(hn_t) * so_ref[0, rows, j * DV:(j + 1) * DV].astype(F32)).astype(BF16)
        return carry

    for_each_chunk(output_step, 0)


def mlstm_call(qt, k, vt, so, gates, c0, m0):
    bsz, t, _ = k.shape
    n_chunks = t // CHUNK
    hp = HEADS_PER_STEP
    seq = lambda width: pl.BlockSpec((1, t, width), lambda i, j: (i, 0, j))
    seq_t = lambda rows: pl.BlockSpec((1, n_chunks, rows, CHUNK), lambda i, j: (i, 0, j, 0))
    return pl.pallas_call(
        functools.partial(_mlstm_body, n_chunks=n_chunks),
        grid=(bsz, N_HEAD_GROUPS),
        in_specs=[seq_t(hp * DK), seq(hp * DK), seq_t(hp * DV), seq(hp * DV),
                  pl.BlockSpec((1, n_chunks, N_GATE, CHUNK), lambda i, j: (i, 0, 0, 0)),
                  pl.BlockSpec((1, N_DIR, 1, V_AUG, hp * DK), lambda i, j: (i, 0, j, 0, 0)),
                  pl.BlockSpec((1, N_GATE, 1), lambda i, j: (i, 0, 0))],
        out_specs=seq(hp * DV),
        out_shape=jax.ShapeDtypeStruct((bsz, t, D_MLSTM), BF16),
        scratch_shapes=[pltpu.VMEM((N_TABLES, n_chunks, N_HEAD_GROUPS, GATES_PER_GROUP, CHUNK), F32),
                        pltpu.VMEM((n_chunks, N_DIR * V_AUG, hp * DK), F32),
                        pltpu.VMEM((n_chunks, N_DIR, 2, V_AUG, hp * DK), BF16),
                        pltpu.VMEM((n_chunks, N_DIR, hp, CHUNK, CHUNK), BF16),
                        pltpu.VMEM((N_DIR, CHUNK, CHUNK), F32)],
        compiler_params=pltpu.CompilerParams(dimension_semantics=("arbitrary", "arbitrary"),
                                             vmem_limit_bytes=VMEM_LIMIT_BYTES),
        name="mlstm",
    )(qt, k, vt, so, gates, c0, m0)


def _out_body(x_ref, hs_ref, ya_ref, gb_ref, mod_ref, nw2_ref, fnw_ref, wmo_ref, wo_ref, w1_ref, w2_ref, o_ref):
    dot = functools.partial(jnp.dot, preferred_element_type=F32)
    mod = mod_ref[pl.ds(pl.program_id(0), 1), :]
    yb = dot(hs_ref[0], wmo_ref[...])
    merged = ya_ref[0].astype(F32) + gb_ref[0].astype(F32) * yb
    x1 = x_ref[0] + mod[:, _G1:_SH2] * dot(merged.astype(BF16), wo_ref[...])
    hm = _modulated_norm(x1, nw2_ref[...], mod[:, _SH2:_SC2], mod[:, _SC2:_G2]).astype(BF16)
    a = jnp.maximum(dot(hm, w1_ref[...]), 0.0)
    x2 = x1 + mod[:, _G2:] * dot((a * a).astype(BF16), w2_ref[...])
    y = x2 * lax.rsqrt(jnp.mean(x2 * x2, axis=-1, keepdims=True) + EPS)
    o_ref[0] = y * fnw_ref[...]


def out_call(x, hs, ya, gb, mod, norm2_w, final_norm_w, w_mlstm_out, w_out, w_ff1, w_ff2):
    bsz, t, _ = x.shape
    tile = TOKEN_TILE
    tok = pl.BlockSpec((1, tile, D_MODEL), lambda i, j: (i, j, 0))
    resident = [mod, norm2_w, final_norm_w, w_mlstm_out, w_out, w_ff1, w_ff2]
    return pl.pallas_call(
        _out_body,
        grid=(bsz, t // tile),
        in_specs=[tok, tok, tok, tok] + [_resident(a.shape) for a in resident],
        out_specs=tok,
        out_shape=jax.ShapeDtypeStruct((bsz, t, D_MODEL), F32),
        compiler_params=pltpu.CompilerParams(dimension_semantics=("arbitrary", "arbitrary"),
                                             vmem_limit_bytes=VMEM_LIMIT_BYTES),
        name="merge_out_mlp",
    )(x, hs, ya, gb, *resident)


def _layer(x, ctx, mod, norm1_w, w_in, b_in, conv_w, mlstm_norm_w, w_conv_out, w_mlstm_out,
           w_out, norm2_w, w_ff1, w_ff2, final_norm_w):
    nw1 = norm1_w.reshape(1, D_MODEL)
    w_n, w_t = projection_weights(w_in.T)
    b_n = jnp.concatenate([b_in[_REF_K:_REF_V], b_in[_REF_O:]]).reshape(1, -1)
    b_t = jnp.concatenate([b_in[_REF_V:_REF_IG], b_in[_REF_Q:_REF_O], _group_gates(b_in[_REF_IG:_REF_Q]),
                           jnp.zeros((_T_ROWS - _T_END,), F32)]).reshape(-1, 1)

    c0, m0 = ctx_call(ctx, mod, nw1, w_n, b_n, w_t, b_t)
    k, qt, vt, gates, so, ya, gb = inproj_call(x, mod, nw1, mlstm_norm_w.reshape(1, D_MLSTM), w_n, b_n, w_t, b_t,
                                               conv_w, w_conv_out.astype(BF16))
    hs = mlstm_call(qt, k, vt, so, gates, c0, m0)
    return out_call(x, hs, ya, gb, mod, norm2_w.reshape(1, D_MODEL), final_norm_w.reshape(1, D_MODEL),
                    w_mlstm_out.astype(BF16), w_out.astype(BF16), w_ff1.astype(BF16), w_ff2.astype(BF16))


def kernel(x, c, ctx, c_ctx, w_mod, b_mod, norm1_w, w_in, b_in, conv_w, mlstm_norm_w, w_conv_out,
           w_mlstm_out, w_out, norm2_w, w_ff1, w_ff2, final_norm_w):
    depth = w_mod.shape[0]
    assert depth == 1, "the context stream is only advanced through its mLSTM state (single layer)"
    cvecs = jnp.concatenate([c, c_ctx[None, :]], axis=0)
    mod = adaln_call(cvecs, w_mod[0], b_mod[0])
    return _layer(x, ctx, mod, norm1_w[0], w_in[0], b_in[0], conv_w[0],
                  mlstm_norm_w[0], w_conv_out[0], w_mlstm_out[0], w_out[0], norm2_w[0], w_ff1[0],
                  w_ff2[0], final_norm_w)
```

```python
import functools

import jax
import jax.numpy as jnp
from jax import lax
from jax.experimental import pallas as pl
from jax.experimental.pallas import tpu as pltpu

D_MODEL = 1024
CTX_LEN = 256
GRID_W = 64
N_HEADS = 8
DK = 64
DV = 128
D_MLSTM = N_HEADS * DV
D_QK = N_HEADS * DK
N_DIR = 2
N_GATE = 2 * N_DIR * N_HEADS
EPS = 1e-6

CHUNK = 128
HEADS_PER_STEP = 2
assert HEADS_PER_STEP == 2 and HEADS_PER_STEP * DK == 128, "a head pair shares one 128-lane tile"
N_HEAD_GROUPS = N_HEADS // HEADS_PER_STEP
GATES_PER_GROUP = N_GATE // N_HEAD_GROUPS
SUBLANES = 8
BF16_SUBLANES = 16
V_AUG = DV + BF16_SUBLANES
TOKEN_TILE = 512
ADALN_TILE = 1024
CTX_BATCH = 4

VMEM_LIMIT_BYTES = 56 * 1024 * 1024

_REF_K, _REF_V, _REF_IG, _REF_FG, _REF_Q, _REF_O = 0, 512, 1536, 1552, 1568, 2080
D_IN = 8224
_K0, _O0, _XIN0, _GC0, _GB0, _MA0, _MB0, _G0 = 0, 512, 1536, 2560, 3584, 4608, 5632, 6656
_T_V, _T_Q, _T_GATES, _T_END, _T_ROWS = 0, 1024, 1536, 1568, 2048
_SH1, _SC1, _G1, _SH2, _SC2, _G2 = (i * D_MODEL for i in range(6))

F32 = jnp.float32
BF16 = jnp.bfloat16


def _sigmoid(x):
    return 1.0 / (1.0 + jnp.exp(-x))


def _log_sigmoid(x):
    return jnp.minimum(x, 0.0) - jnp.log(1.0 + jnp.exp(-jnp.abs(x)))


def _split3(x):
    hi = x.astype(BF16)
    r1 = x - hi.astype(F32)
    mid = r1.astype(BF16)
    lo = (r1 - mid.astype(F32)).astype(BF16)
    return hi, mid, lo


def _cumsum_lanes(x):
    n = x.shape[1]
    r = lax.broadcasted_iota(jnp.int32, (n, n), 0)
    c = lax.broadcasted_iota(jnp.int32, (n, n), 1)
    tri_t = jnp.where(r <= c, 1.0, 0.0).astype(BF16)
    hi, mid, lo = _split3(x)
    dot = functools.partial(jnp.dot, preferred_element_type=F32)
    return dot(hi, tri_t) + dot(mid, tri_t) + dot(lo, tri_t)


def _modulated_norm(x, norm_w, shift, scale):
    y = x * lax.rsqrt(jnp.mean(x * x, axis=-1, keepdims=True) + EPS)
    return (y * norm_w) * (1.0 + scale) + shift


def _resident(shape):
    nd = len(shape)
    return pl.BlockSpec(shape, lambda *_: (0,) * nd, pipeline_mode=pl.Buffered(1))


def _adaln_body(c_ref, w_ref, b_ref, o_ref):
    c = c_ref[...]
    s = c * _sigmoid(c)
    dot = functools.partial(jnp.dot, preferred_element_type=F32)
    w = w_ref[...]
    s_hi, w_hi = s.astype(BF16), w.astype(BF16)
    s_lo, w_lo = (s - s_hi.astype(F32)).astype(BF16), (w - w_hi.astype(F32)).astype(BF16)
    o_ref[...] = (dot(s_hi, w_hi) + dot(s_hi, w_lo) + dot(s_lo, w_hi)) + b_ref[...]


def adaln_call(cvecs, w_mod, b_mod):
    n = cvecs.shape[0]
    n_out = w_mod.shape[1]
    tile = ADALN_TILE
    return pl.pallas_call(
        _adaln_body,
        grid=(n_out // tile,),
        in_specs=[pl.BlockSpec((n, D_MODEL), lambda j: (0, 0)),
                  pl.BlockSpec((D_MODEL, tile), lambda j: (0, j)),
                  pl.BlockSpec((1, tile), lambda j: (0, j))],
        out_specs=pl.BlockSpec((n, tile), lambda j: (0, j)),
        out_shape=jax.ShapeDtypeStruct((n, n_out), F32),
        name="adaln",
    )(cvecs, w_mod, b_mod.reshape(1, n_out))


W_BLOCK = 512


def _transpose_cast_body(w_ref, o_ref):
    o_ref[...] = jnp.transpose(w_ref[...]).astype(BF16)


def _row_cast_body(starts_ref, w_ref, perm_ref, o_ref):
    del starts_ref
    o_ref[...] = w_ref[...].astype(BF16)

    @pl.when(pl.program_id(0) == pl.num_programs(0) - 1)
    def _():
        gates = w_ref[0:N_GATE, :].astype(BF16)
        o_ref[0:N_GATE, :] = jnp.dot(perm_ref[...], gates, preferred_element_type=F32).astype(BF16)


def _group_gates(g):
    lead = g.shape[:-1]
    g = g.reshape(*lead, 2, N_DIR, N_HEAD_GROUPS, HEADS_PER_STEP)
    g = jnp.moveaxis(g, -2, -4)
    return g.reshape(*lead, N_GATE)


def projection_weights(w_in_t):
    n_rest = (D_IN - _REF_O) // W_BLOCK
    blk = W_BLOCK // SUBLANES
    w_n = pl.pallas_call(
        _transpose_cast_body,
        grid=(1 + n_rest,),
        in_specs=[pl.BlockSpec((pl.Element(W_BLOCK), pl.Element(D_MODEL)),
                               lambda i: (SUBLANES * jnp.where(i == 0, _REF_K // SUBLANES,
                                                               _REF_O // SUBLANES + (i - 1) * blk), 0))],
        out_specs=pl.BlockSpec((D_MODEL, W_BLOCK), lambda i: (0, i)),
        out_shape=jax.ShapeDtypeStruct((D_MODEL, (1 + n_rest) * W_BLOCK), BF16),
        name="w_in_normal",
    )(w_in_t)
    starts = jnp.asarray([_REF_V // SUBLANES, _REF_V // SUBLANES + blk, _REF_Q // SUBLANES, _REF_IG // SUBLANES],
                         jnp.int32)
    gate_perm = jnp.eye(N_GATE, dtype=BF16)[_group_gates(jnp.arange(N_GATE))]
    w_t = pl.pallas_call(
        _row_cast_body,
        grid_spec=pltpu.PrefetchScalarGridSpec(
            num_scalar_prefetch=1,
            grid=(_T_ROWS // W_BLOCK,),
            in_specs=[pl.BlockSpec((pl.Element(W_BLOCK), pl.Element(D_MODEL)),
                                   lambda i, starts_ref: (SUBLANES * starts_ref[i], 0)),
                      pl.BlockSpec((N_GATE, N_GATE), lambda i, starts_ref: (0, 0))],
            out_specs=pl.BlockSpec((W_BLOCK, D_MODEL), lambda i, starts_ref: (i, 0))),
        out_shape=jax.ShapeDtypeStruct((_T_ROWS, D_MODEL), BF16),
        name="w_in_transposed",
    )(starts, w_in_t, gate_perm)
    return w_n, w_t


def _ctx_body(ctx_ref, mod_ref, nw_ref, wk_ref, bk_ref, wv_ref, bv_ref, wg_ref, bg_ref, c_ref, m_ref):
    hp = HEADS_PER_STEP
    nb = ctx_ref.shape[0]
    last = mod_ref.shape[0] - 1
    x = ctx_ref[...].reshape(nb * CTX_LEN, D_MODEL)
    hb = _modulated_norm(x, nw_ref[...], mod_ref[last:, _SH1:_SC1], mod_ref[last:, _SC1:_G1]).astype(BF16)
    nt = functools.partial(lax.dot_general, dimension_numbers=(((1,), (1,)), ((), ())),
                           preferred_element_type=F32)
    k_all = ((jnp.dot(hb, wk_ref[...], preferred_element_type=F32) + bk_ref[...]) * (DK ** -0.5)).astype(BF16)
    vt_all = (nt(wv_ref[...], hb) + bv_ref[...]).astype(BF16)
    g_all = nt(wg_ref[...], hb) + bg_ref[...]

    r_in_tile = jnp.bitwise_and(lax.broadcasted_iota(jnp.int32, (N_GATE, 1), 0), GATES_PER_GROUP - 1)
    fwd = r_in_tile < GATES_PER_GROUP // 2 + hp
    ones_rows = jnp.where(lax.broadcasted_iota(jnp.int32, (V_AUG - DV, CTX_LEN), 0) == 0, 1.0, 0.0)
    lane = lax.broadcasted_iota(jnp.int32, (CTX_LEN, hp * DK), 1)
    for bi in range(nb):
        tokens = slice(bi * CTX_LEN, (bi + 1) * CTX_LEN)
        g, vt, k = g_all[:, tokens], vt_all[:, tokens], k_all[tokens]
        lf = _log_sigmoid(g)
        b = _cumsum_lanes(lf)
        ig = pltpu.roll(g, GATES_PER_GROUP // 2, axis=0)
        tot = b[:, CTX_LEN - 1:CTX_LEN]
        g_end = ig + jnp.where(fwd, tot - b, b - lf)
        m = jnp.maximum(tot, jnp.max(g_end, axis=1, keepdims=True))
        m_ref[bi] = m
        wk = jnp.exp(g_end - m)
        for grp in range(N_HEAD_GROUPS):
            k2 = k[:, grp * hp * DK:(grp + 1) * hp * DK]
            zero = jnp.zeros_like(k2)
            k_bd = jnp.concatenate([jnp.where(lane < DK, k2, zero), jnp.where(lane >= DK, k2, zero)], axis=0)
            vf = [jnp.concatenate([vt[(grp * hp + j) * DV:(grp * hp + j + 1) * DV].astype(F32), ones_rows],
                                  axis=0) for j in range(hp)]
            for d in range(N_DIR):
                row0 = grp * GATES_PER_GROUP + GATES_PER_GROUP // 2 + d * hp
                lhs = jnp.concatenate([(vf[j] * wk[row0 + j:row0 + j + 1]).astype(BF16) for j in range(hp)],
                                      axis=1)
                c_ref[bi, d, grp] = jnp.dot(lhs, k_bd, preferred_element_type=F32)


def ctx_call(ctx, mod, norm_w, w_n, b_n, w_t, b_t):
    bsz = ctx.shape[0]
    nb = CTX_BATCH
    const = lambda shape: pl.BlockSpec(shape, lambda i: (0,) * len(shape))
    gate_rows = lambda width: pl.BlockSpec((N_GATE, width), lambda i: (_T_GATES // N_GATE, 0))
    return pl.pallas_call(
        _ctx_body,
        grid=(bsz // nb,),
        in_specs=[pl.BlockSpec((nb, CTX_LEN, D_MODEL), lambda i: (i, 0, 0)),
                  const(mod.shape), const((1, D_MODEL)),
                  const((D_MODEL, D_QK)), const((1, D_QK)),
                  const((D_MLSTM, D_MODEL)), const((D_MLSTM, 1)),
                  gate_rows(D_MODEL), gate_rows(1)],
        out_specs=[pl.BlockSpec((nb, N_DIR, N_HEAD_GROUPS, V_AUG, HEADS_PER_STEP * DK),
                                lambda i: (i, 0, 0, 0, 0)),
                   pl.BlockSpec((nb, N_GATE, 1), lambda i: (i, 0, 0))],
        out_shape=[jax.ShapeDtypeStruct((bsz, N_DIR, N_HEAD_GROUPS, V_AUG, HEADS_PER_STEP * DK), F32),
                   jax.ShapeDtypeStruct((bsz, N_GATE, 1), F32)],
        compiler_params=pltpu.CompilerParams(dimension_semantics=("arbitrary",),
                                             vmem_limit_bytes=VMEM_LIMIT_BYTES),
        name="ctx_state",
    )(ctx, mod, norm_w, w_n, b_n, w_t, b_t, w_t, b_t)


def _inproj_body(x_ref, mod_ref, nw_ref, nwm_ref, w_ref, b_ref, wt_ref, bt_ref, cw_ref, wco_ref,
                 k_ref, qt_ref, vt_ref, gt_ref, so_ref, ya_ref, gb_ref):
    x = x_ref[0]
    mod = mod_ref[pl.ds(pl.program_id(0), 1), :]
    hb = _modulated_norm(x, nw_ref[...], mod[:, _SH1:_SC1], mod[:, _SC1:_G1]).astype(BF16)
    tile = hb.shape[0]

    def seg(a, b):
        return jnp.dot(hb, w_ref[:, a:b], preferred_element_type=F32) + b_ref[:, a:b]

    k_ref[0] = (seg(_K0, _O0) * (DK ** -0.5)).astype(BF16)
    so_ref[0] = (_sigmoid(seg(_O0, _XIN0)) * nwm_ref[...]).astype(BF16)

    zt = lax.dot_general(wt_ref[:_T_END, :], hb, (((1,), (1,)), ((), ())),
                         preferred_element_type=F32) + bt_ref[:_T_END, :]
    for i in range(tile // CHUNK):
        lanes = slice(i * CHUNK, (i + 1) * CHUNK)
        vt_ref[0, i] = zt[_T_V:_T_Q, lanes].astype(BF16)
        qt_ref[0, i] = zt[_T_Q:_T_GATES, lanes].astype(BF16)
        gt_ref[0, i] = zt[_T_GATES:_T_END, lanes]

    u = seg(_XIN0, _GC0) * seg(_GC0, _GB0)
    col = jnp.bitwise_and(lax.broadcasted_iota(jnp.int32, (tile, 1), 0), GRID_W - 1)
    u_prev = jnp.where(col != 0, pltpu.roll(u, 1, axis=0), 0.0)
    u_next = jnp.where(col != GRID_W - 1, pltpu.roll(u, tile - 1, axis=0), 0.0)
    a = cw_ref[0:1, :] * u_prev + cw_ref[1:2, :] * u + cw_ref[2:3, :] * u_next
    ya = jnp.dot((seg(_GB0, _MA0) * a).astype(BF16), wco_ref[...], preferred_element_type=F32)
    ya_ref[0] = (_sigmoid(seg(_MA0, _MB0)) * ya).astype(BF16)
    gb_ref[0] = _sigmoid(seg(_MB0, _G0)).astype(BF16)


def inproj_call(x, mod, norm_w, mlstm_norm_w, w_n, b_n, w_t, b_t, conv_w, w_conv_out):
    bsz, t, _ = x.shape
    tile = TOKEN_TILE
    tok = lambda width: pl.BlockSpec((1, tile, width), lambda i, j: (i, j, 0))
    tok_t = lambda rows: pl.BlockSpec((1, tile // CHUNK, rows, CHUNK), lambda i, j: (i, j, 0, 0))
    seq = lambda width: jax.ShapeDtypeStruct((bsz, t, width), BF16)
    seq_t = lambda rows, dt: jax.ShapeDtypeStruct((bsz, t // CHUNK, rows, CHUNK), dt)
    resident = [mod, norm_w, mlstm_norm_w, w_n, b_n, w_t, b_t, conv_w, w_conv_out]
    return pl.pallas_call(
        _inproj_body,
        grid=(bsz, t // tile),
        in_specs=[tok(D_MODEL)] + [_resident(a.shape) for a in resident],
        out_specs=[tok(D_QK), tok_t(D_QK), tok_t(D_MLSTM), tok_t(N_GATE), tok(D_MLSTM),
                   tok(D_MODEL), tok(D_MODEL)],
        out_shape=[seq(D_QK), seq_t(D_QK, BF16), seq_t(D_MLSTM, BF16), seq_t(N_GATE, F32), seq(D_MLSTM),
                   seq(D_MODEL), seq(D_MODEL)],
        compiler_params=pltpu.CompilerParams(dimension_semantics=("arbitrary", "arbitrary"),
                                             vmem_limit_bytes=VMEM_LIMIT_BYTES),
        name="inproj_conv",
    )(x, *resident)


SERIES_ROW0 = GATES_PER_GROUP // 2
N_SERIES = N_DIR * HEADS_PER_STEP
N_SPLIT = 3
LOG2E = 1.4426950408889634
_WK, _DECAY, _WINTER, _EXPNEG, _KEYS = 0, 1, 2, 3, 4
_COLM = _KEYS + N_SPLIT
N_TABLES = _COLM + N_SPLIT * N_SERIES


def _running_max_rows(x, reverse):
    n = x.shape[0]
    row = lax.broadcasted_iota(jnp.int32, x.shape, 0)
    shift = 1
    while shift < n:
        if reverse:
            moved = jnp.where(row < n - shift, pltpu.roll(x, n - shift, axis=0), -jnp.inf)
        else:
            moved = jnp.where(row >= shift, pltpu.roll(x, shift, axis=0), -jnp.inf)
        x = jnp.maximum(x, moved)
        shift *= 2
    return x


def _gate_tables(g_ref, m0_ref, tab_ref, n_chunks):
    step_rows = N_HEAD_GROUPS * GATES_PER_GROUP
    rows = n_chunks * step_rows
    g = g_ref[0].reshape(rows, CHUNK)
    lf = _log_sigmoid(g)
    b = _cumsum_lanes(lf)
    ig = pltpu.roll(g, SERIES_ROW0, axis=0)
    in_tile = lambda idx: jnp.bitwise_and(idx, GATES_PER_GROUP - 1)
    r_in_tile = in_tile(lax.broadcasted_iota(jnp.int32, (rows, 1), 0))
    fwd = r_in_tile < SERIES_ROW0 + HEADS_PER_STEP
    e = b - lf
    tot = b[:, CHUNK - 1:CHUNK]
    col = jnp.where(fwd, b, -e)
    key = ig + jnp.where(fwd, -b, e)
    g_end = jnp.where(fwd, tot + key, key)
    g_max = jnp.max(g_end, axis=1, keepdims=True)

    fwd_step = fwd[:step_rows]
    tile = lambda a, c: a[c * step_rows:(c + 1) * step_rows]
    m = m0_ref[0]
    m_in_steps, m_out_steps = [], []
    for i in range(n_chunks):
        back = n_chunks - 1 - i
        m_in_steps.append(m)
        m = jnp.maximum(jnp.where(fwd_step, tile(tot, i), tile(tot, back)) + m,
                        jnp.where(fwd_step, tile(g_max, i), tile(g_max, back)))
        m_out_steps.append(m)
    by_chunk = lambda steps: jnp.concatenate(
        [jnp.where(fwd_step, steps[c], steps[n_chunks - 1 - c]) for c in range(n_chunks)], axis=0)
    m_in, m_out = by_chunk(m_in_steps), by_chunk(m_out_steps)

    decay = jnp.broadcast_to(jnp.exp(tot + m_in - m_out), (rows, CHUNK))
    lane = lax.broadcasted_iota(jnp.int32, (rows, CHUNK), 1)
    inter = jnp.where(fwd, col, tot + col) + m_in
    key_t = jnp.transpose(key)
    fwd_lane = in_tile(lax.broadcasted_iota(jnp.int32, (1, rows), 1)) < SERIES_ROW0 + HEADS_PER_STEP
    key_max = jnp.transpose(jnp.where(fwd_lane, _running_max_rows(key_t, reverse=False),
                                      _running_max_rows(key_t, reverse=True)))
    m_t = jnp.maximum(col + key_max, inter)
    tables = {_WK: jnp.exp(g_end - m_out),
              _DECAY: jnp.where(lane < DK, decay, pltpu.roll(decay, rows - 1, axis=0)),
              _WINTER: jnp.exp(inter - m_t), _EXPNEG: jnp.exp(-m_t)}
    for x, part in enumerate(_split3(key * LOG2E)):
        tables[_KEYS + x] = part.astype(F32)
    for x, part in enumerate(_split3((col - m_t) * LOG2E)):
        for sidx in range(N_SERIES):
            tables[_COLM + x * N_SERIES + sidx] = jnp.where(r_in_tile == SERIES_ROW0 + sidx,
                                                            part.astype(F32), 0.0)
    for idx, a in tables.items():
        tab_ref[idx] = a.reshape(n_chunks, N_HEAD_GROUPS, GATES_PER_GROUP, CHUNK)


def _mlstm_body(qt_ref, k_ref, vt_ref, so_ref, g_ref, c0_ref, m0_ref, o_ref,
                tab_ref, u_ref, s_ref, st_ref, causal_ref, *, n_chunks):
    hp = HEADS_PER_STEP
    ones_rows = jnp.where(lax.broadcasted_iota(jnp.int32, (V_AUG - DV, CHUNK), 0) == 0,
                          1.0, 0.0).astype(BF16)
    low_lanes = lax.broadcasted_iota(jnp.int32, (1, hp * DK), 1) < DK
    series = lambda tab8, d, j: tab8[SERIES_ROW0 + d * hp + j:SERIES_ROW0 + d * hp + j + 1]
    split_rows = N_SPLIT * GATES_PER_GROUP
    ones_split = jnp.ones((split_rows, CHUNK), F32)
    pick_r = jnp.bitwise_and(lax.broadcasted_iota(jnp.int32, (split_rows, N_DIR * CHUNK), 0),
                             GATES_PER_GROUP - 1)
    pick_d = jnp.where(lax.broadcasted_iota(jnp.int32, (split_rows, N_DIR * CHUNK), 1) >= CHUNK, 1, 0)
    pick_series = [jnp.where(pick_r == SERIES_ROW0 + pick_d * hp + j, 1.0, 0.0) for j in range(hp)]

    def chunk_rows(c):
        return slice(c * CHUNK, (c + 1) * CHUNK)

    def for_each_chunk(body, carry):
        for c in range(n_chunks):
            carry = body(c, carry)
        return carry

    def values_t(c, j):
        return jnp.concatenate([vt_ref[0, c, j * DV:(j + 1) * DV, :], ones_rows], axis=0)

    grp = pl.program_id(1)

    @pl.when(grp == 0)
    def _():
        _gate_tables(g_ref, m0_ref, tab_ref, n_chunks)
        s_i = lax.broadcasted_iota(jnp.int32, (CHUNK, CHUNK), 0)
        t_i = lax.broadcasted_iota(jnp.int32, (CHUNK, CHUNK), 1)
        causal_ref[0] = jnp.where(s_i <= t_i, 0.0, -jnp.inf)
        causal_ref[1] = jnp.where(s_i >= t_i, 0.0, -jnp.inf)

    def block_diag_q(qt2):
        zero = jnp.zeros((DK, CHUNK), qt2.dtype)
        return jnp.concatenate([jnp.concatenate([qt2[:DK], zero], axis=0),
                                jnp.concatenate([zero, qt2[DK:]], axis=0)], axis=1)

    def chunk_step(c, carry):
        k2 = k_ref[0, chunk_rows(c), :]
        zero = jnp.zeros_like(k2)
        k_bd = jnp.concatenate([jnp.where(low_lanes, k2, zero),
                                jnp.where(low_lanes, zero, k2)], axis=0)
        wk8 = tab_ref[_WK, c, grp]
        vb = [values_t(c, j) for j in range(hp)]

        def key_weights(d, j):
            tile = jnp.broadcast_to(series(wk8, d, j), (BF16_SUBLANES, CHUNK)).astype(BF16)
            return jnp.concatenate([tile] * (V_AUG // BF16_SUBLANES), axis=0)

        lhs = jnp.concatenate(
            [jnp.concatenate([vb[j] * key_weights(d, j) for j in range(hp)], axis=1)
             for d in range(N_DIR)], axis=0)
        u_ref[c] = jnp.dot(lhs, k_bd, preferred_element_type=F32)

        qk_t = jnp.dot(k2, block_diag_q(qt_ref[0, c]), preferred_element_type=F32)
        key_side = jnp.concatenate([tab_ref[_KEYS + x, c, grp] for x in range(N_SPLIT)] + [ones_split],
                                   axis=0).astype(BF16)
        for j in range(hp):
            query_side = jnp.concatenate(
                [pick_series[j]] +
                [jnp.concatenate([tab_ref[_COLM + x * N_SERIES + d * hp + j, c, grp] for d in range(N_DIR)],
                                 axis=1)
                 for x in range(N_SPLIT)], axis=0).astype(BF16)
            log_d = lax.dot_general(key_side, query_side, (((0,), (0,)), ((), ())),
                                    preferred_element_type=F32)
            for d in range(N_DIR):
                decay = jnp.exp2(log_d[:, d * CHUNK:(d + 1) * CHUNK] + causal_ref[d])
                st_ref[c, d, j] = (qk_t[:, j * CHUNK:(j + 1) * CHUNK] * decay).astype(BF16)
        return carry

    for_each_chunk(chunk_step, 0)

    def scan_step(i, carry):
        cf, cb = carry
        back = n_chunks - 1 - i
        s_ref[i, 0, 0] = cf.astype(BF16)
        s_ref[i, 0, 1] = pltpu.roll(cf, DK, axis=1).astype(BF16)
        s_ref[back, 1, 0] = cb.astype(BF16)
        s_ref[back, 1, 1] = pltpu.roll(cb, DK, axis=1).astype(BF16)
        cf = tab_ref[_DECAY, i, grp][SERIES_ROW0:SERIES_ROW0 + 1] * cf + u_ref[i, :V_AUG]
        cb = tab_ref[_DECAY, back, grp][SERIES_ROW0 + hp:SERIES_ROW0 + hp + 1] * cb + u_ref[back, V_AUG:]
        return cf, cb

    for_each_chunk(scan_step, (c0_ref[0, 0, 0], c0_ref[0, 1, 0]))

    def output_step(c, carry):
        rows = chunk_rows(c)
        qt2 = qt_ref[0, c].astype(F32)
        w_inter8, exp_neg8 = tab_ref[_WINTER, c, grp], tab_ref[_EXPNEG, c, grp]
        no_query = jnp.zeros((DK, CHUNK), BF16)
        for j in range(hp):
            state = jnp.where(low_lanes, s_ref[c, 0, j], s_ref[c, 1, 1 - j])
            q_h = qt2[j * DK:(j + 1) * DK]
            qw = [(q_h * series(w_inter8, d, j)).astype(BF16) for d in range(N_DIR)]
            rhs = jnp.concatenate(
                [jnp.concatenate([st_ref[c, d, j] for d in range(N_DIR)], axis=1),
                 jnp.concatenate([qw[0], no_query], axis=1),
                 jnp.concatenate([no_query, qw[1]], axis=1)], axis=0)
            lhs = jnp.concatenate([values_t(c, j), state], axis=1)
            n_all = jnp.dot(lhs, rhs, preferred_element_type=F32)
            h_t = None
            for d in range(N_DIR):
                num = n_all[:, d * CHUNK:(d + 1) * CHUNK]
                r = 1.0 / jnp.maximum(jnp.abs(num[DV:DV + 1]), series(exp_neg8, d, j))
                part = num[:DV] * r
                h_t = part if h_t is None else h_t + part
            hn_t = h_t * lax.rsqrt(jnp.mean(h_t * h_t, axis=0, keepdims=True) + EPS)
            o_ref[0, rows, j * DV:(j + 1) * DV] = (
                jnp.transpose(hn_t) * so_ref[0, rows, j * DV:(j + 1) * DV].astype(F32)).astype(BF16)
        return carry

    for_each_chunk(output_step, 0)


def mlstm_call(qt, k, vt, so, gates, c0, m0):
    bsz, t, _ = k.shape
    n_chunks = t // CHUNK
    hp = HEADS_PER_STEP
    seq = lambda width: pl.BlockSpec((1, t, width), lambda i, j: (i, 0, j))
    seq_t = lambda rows: pl.BlockSpec((1, n_chunks, rows, CHUNK), lambda i, j: (i, 0, j, 0))
    return pl.pallas_call(
        functools.partial(_mlstm_body, n_chunks=n_chunks),
        grid=(bsz, N_HEAD_GROUPS),
        in_specs=[seq_t(hp * DK), seq(hp * DK), seq_t(hp * DV), seq(hp * DV),
                  pl.BlockSpec((1, n_chunks, N_GATE, CHUNK), lambda i, j: (i, 0, 0, 0)),
                  pl.BlockSpec((1, N_DIR, 1, V_AUG, hp * DK), lambda i, j: (i, 0, j, 0, 0)),
                  pl.BlockSpec((1, N_GATE, 1), lambda i, j: (i, 0, 0))],
        out_specs=seq(hp * DV),
        out_shape=jax.ShapeDtypeStruct((bsz, t, D_MLSTM), BF16),
        scratch_shapes=[pltpu.VMEM((N_TABLES, n_chunks, N_HEAD_GROUPS, GATES_PER_GROUP, CHUNK), F32),
                        pltpu.VMEM((n_chunks, N_DIR * V_AUG, hp * DK), F32),
                        pltpu.VMEM((n_chunks, N_DIR, 2, V_AUG, hp * DK), BF16),
                        pltpu.VMEM((n_chunks, N_DIR, hp, CHUNK, CHUNK), BF16),
                        pltpu.VMEM((N_DIR, CHUNK, CHUNK), F32)],
        compiler_params=pltpu.CompilerParams(dimension_semantics=("arbitrary", "arbitrary"),
                                             vmem_limit_bytes=VMEM_LIMIT_BYTES),
        name="mlstm",
    )(qt, k, vt, so, gates, c0, m0)


def _out_body(x_ref, hs_ref, ya_ref, gb_ref, mod_ref, nw2_ref, fnw_ref, wmo_ref, wo_ref, w1_ref, w2_ref, o_ref):
    dot = functools.partial(jnp.dot, preferred_element_type=F32)
    mod = mod_ref[pl.ds(pl.program_id(0), 1), :]
    yb = dot(hs_ref[0], wmo_ref[...])
    merged = ya_ref[0].astype(F32) + gb_ref[0].astype(F32) * yb
    x1 = x_ref[0] + mod[:, _G1:_SH2] * dot(merged.astype(BF16), wo_ref[...])
    hm = _modulated_norm(x1, nw2_ref[...], mod[:, _SH2:_SC2], mod[:, _SC2:_G2]).astype(BF16)
    a = jnp.maximum(dot(hm, w1_ref[...]), 0.0)
    x2 = x1 + mod[:, _G2:] * dot((a * a).astype(BF16), w2_ref[...])
    y = x2 * lax.rsqrt(jnp.mean(x2 * x2, axis=-1, keepdims=True) + EPS)
    o_ref[0] = y * fnw_ref[...]


def out_call(x, hs, ya, gb, mod, norm2_w, final_norm_w, w_mlstm_out, w_out, w_ff1, w_ff2):
    bsz, t, _ = x.shape
    tile = TOKEN_TILE
    tok = pl.BlockSpec((1, tile, D_MODEL), lambda i, j: (i, j, 0))
    resident = [mod, norm2_w, final_norm_w, w_mlstm_out, w_out, w_ff1, w_ff2]
    return pl.pallas_call(
        _out_body,
        grid=(bsz, t // tile),
        in_specs=[tok, tok, tok, tok] + [_resident(a.shape) for a in resident],
        out_specs=tok,
        out_shape=jax.ShapeDtypeStruct((bsz, t, D_MODEL), F32),
        compiler_params=pltpu.CompilerParams(dimension_semantics=("arbitrary", "arbitrary"),
                                             vmem_limit_bytes=VMEM_LIMIT_BYTES),
        name="merge_out_mlp",
    )(x, hs, ya, gb, *resident)


def _layer(x, ctx, mod, norm1_w, w_in, b_in, conv_w, mlstm_norm_w, w_conv_out, w_mlstm_out,
           w_out, norm2_w, w_ff1, w_ff2, final_norm_w):
    nw1 = norm1_w.reshape(1, D_MODEL)
    w_n, w_t = projection_weights(w_in.T)
    b_n = jnp.concatenate([b_in[_REF_K:_REF_V], b_in[_REF_O:]]).reshape(1, -1)
    b_t = jnp.concatenate([b_in[_REF_V:_REF_IG], b_in[_REF_Q:_REF_O], _group_gates(b_in[_REF_IG:_REF_Q]),
                           jnp.zeros((_T_ROWS - _T_END,), F32)]).reshape(-1, 1)

    c0, m0 = ctx_call(ctx, mod, nw1, w_n, b_n, w_t, b_t)
    k, qt, vt, gates, so, ya, gb = inproj_call(x, mod, nw1, mlstm_norm_w.reshape(1, D_MLSTM), w_n, b_n, w_t, b_t,
                                               conv_w, w_conv_out.astype(BF16))
    hs = mlstm_call(qt, k, vt, so, gates, c0, m0)
    return out_call(x, hs, ya, gb, mod, norm2_w.reshape(1, D_MODEL), final_norm_w.reshape(1, D_MODEL),
                    w_mlstm_out.astype(BF16), w_out.astype(BF16), w_ff1.astype(BF16), w_ff2.astype(BF16))


def kernel(x, c, ctx, c_ctx, w_mod, b_mod, norm1_w, w_in, b_in, conv_w, mlstm_norm_w, w_conv_out,
           w_mlstm_out, w_out, norm2_w, w_ff1, w_ff2, final_norm_w):
    depth = w_mod.shape[0]
    assert depth == 1, "the context stream is only advanced through its mLSTM state (single layer)"
    cvecs = jnp.concatenate([c, c_ctx[None, :]], axis=0)
    mod = adaln_call(cvecs, w_mod[0], b_mod[0])
    return _layer(x, ctx, mod, norm1_w[0], w_in[0], b_in[0], conv_w[0],
                  mlstm_norm_w[0], w_conv_out[0], w_mlstm_out[0], w_out[0], norm2_w[0], w_ff1[0],
                  w_ff2[0], final_norm_w)
```

```python
import functools

import jax
import jax.numpy as jnp
from jax import lax
from jax.experimental import pallas as pl
from jax.experimental.pallas import tpu as pltpu

D_MODEL = 1024
CTX_LEN = 256
GRID_W = 64
N_HEADS = 8
DK = 64
DV = 128
D_MLSTM = N_HEADS * DV
D_QK = N_HEADS * DK
N_DIR = 2
N_GATE = 2 * N_DIR * N_HEADS
EPS = 1e-6

CHUNK = 128
HEADS_PER_STEP = 2
assert HEADS_PER_STEP == 2 and HEADS_PER_STEP * DK == 128, "a head pair shares one 128-lane tile"
N_HEAD_GROUPS = N_HEADS // HEADS_PER_STEP
GATES_PER_GROUP = N_GATE // N_HEAD_GROUPS
SUBLANES = 8
BF16_SUBLANES = 16
V_AUG = DV + BF16_SUBLANES
TOKEN_TILE = 512
ADALN_TILE = 2048
CTX_BATCH = 4

VMEM_LIMIT_BYTES = 56 * 1024 * 1024

_REF_K, _REF_V, _REF_IG, _REF_FG, _REF_Q, _REF_O = 0, 512, 1536, 1552, 1568, 2080
D_IN = 8224
_K0, _O0, _XIN0, _GC0, _GB0, _MA0, _MB0, _G0 = 0, 512, 1536, 2560, 3584, 4608, 5632, 6656
_T_V, _T_Q, _T_GATES, _T_END, _T_ROWS = 0, 1024, 1536, 1568, 2048
_SH1, _SC1, _G1, _SH2, _SC2, _G2 = (i * D_MODEL for i in range(6))

F32 = jnp.float32
BF16 = jnp.bfloat16


def _sigmoid(x):
    return 1.0 / (1.0 + jnp.exp(-x))


def _log_sigmoid(x):
    return jnp.minimum(x, 0.0) - jnp.log(1.0 + jnp.exp(-jnp.abs(x)))


def _split3(x):
    hi = x.astype(BF16)
    r1 = x - hi.astype(F32)
    mid = r1.astype(BF16)
    lo = (r1 - mid.astype(F32)).astype(BF16)
    return hi, mid, lo


def _cumsum_lanes(x):
    n = x.shape[1]
    r = lax.broadcasted_iota(jnp.int32, (n, n), 0)
    c = lax.broadcasted_iota(jnp.int32, (n, n), 1)
    tri_t = jnp.where(r <= c, 1.0, 0.0).astype(BF16)
    hi, mid, lo = _split3(x)
    dot = functools.partial(jnp.dot, preferred_element_type=F32)
    return dot(hi, tri_t) + dot(mid, tri_t) + dot(lo, tri_t)


def _modulated_norm(x, norm_w, shift, scale):
    y = x * lax.rsqrt(jnp.mean(x * x, axis=-1, keepdims=True) + EPS)
    return (y * norm_w) * (1.0 + scale) + shift


def _resident(shape):
    nd = len(shape)
    return pl.BlockSpec(shape, lambda *_: (0,) * nd, pipeline_mode=pl.Buffered(1))


def _adaln_body(c_ref, w_ref, b_ref, o_ref):
    c = c_ref[...]
    s = c * _sigmoid(c)
    dot = functools.partial(jnp.dot, preferred_element_type=F32)
    w = w_ref[...]
    s_hi, w_hi = s.astype(BF16), w.astype(BF16)
    s_lo, w_lo = (s - s_hi.astype(F32)).astype(BF16), (w - w_hi.astype(F32)).astype(BF16)
    o_ref[...] = (dot(s_hi, w_hi) + dot(s_hi, w_lo) + dot(s_lo, w_hi)) + b_ref[...]


def adaln_call(cvecs, w_mod, b_mod):
    n = cvecs.shape[0]
    n_out = w_mod.shape[1]
    tile = ADALN_TILE
    return pl.pallas_call(
        _adaln_body,
        grid=(n_out // tile,),
        in_specs=[pl.BlockSpec((n, D_MODEL), lambda j: (0, 0)),
                  pl.BlockSpec((D_MODEL, tile), lambda j: (0, j)),
                  pl.BlockSpec((1, tile), lambda j: (0, j))],
        out_specs=pl.BlockSpec((n, tile), lambda j: (0, j)),
        out_shape=jax.ShapeDtypeStruct((n, n_out), F32),
        compiler_params=pltpu.CompilerParams(dimension_semantics=("arbitrary",),
                                             vmem_limit_bytes=VMEM_LIMIT_BYTES),
        name="adaln",
    )(cvecs, w_mod, b_mod.reshape(1, n_out))


W_BLOCK = 512


def _transpose_cast_body(w_ref, o_ref):
    o_ref[...] = jnp.transpose(w_ref[...]).astype(BF16)


def _row_cast_body(starts_ref, w_ref, perm_ref, o_ref):
    del starts_ref
    o_ref[...] = w_ref[...].astype(BF16)

    @pl.when(pl.program_id(0) == pl.num_programs(0) - 1)
    def _():
        gates = w_ref[0:N_GATE, :].astype(BF16)
        o_ref[0:N_GATE, :] = jnp.dot(perm_ref[...], gates, preferred_element_type=F32).astype(BF16)


def _group_gates(g):
    lead = g.shape[:-1]
    g = g.reshape(*lead, 2, N_DIR, N_HEAD_GROUPS, HEADS_PER_STEP)
    g = jnp.moveaxis(g, -2, -4)
    return g.reshape(*lead, N_GATE)


def projection_weights(w_in_t):
    n_rest = (D_IN - _REF_O) // W_BLOCK
    blk = W_BLOCK // SUBLANES
    w_n = pl.pallas_call(
        _transpose_cast_body,
        grid=(1 + n_rest,),
        in_specs=[pl.BlockSpec((pl.Element(W_BLOCK), pl.Element(D_MODEL)),
                               lambda i: (SUBLANES * jnp.where(i == 0, _REF_K // SUBLANES,
                                                               _REF_O // SUBLANES + (i - 1) * blk), 0))],
        out_specs=pl.BlockSpec((D_MODEL, W_BLOCK), lambda i: (0, i)),
        out_shape=jax.ShapeDtypeStruct((D_MODEL, (1 + n_rest) * W_BLOCK), BF16),
        name="w_in_normal",
    )(w_in_t)
    starts = jnp.asarray([_REF_V // SUBLANES, _REF_V // SUBLANES + blk, _REF_Q // SUBLANES, _REF_IG // SUBLANES],
                         jnp.int32)
    gate_perm = jnp.eye(N_GATE, dtype=BF16)[_group_gates(jnp.arange(N_GATE))]
    w_t = pl.pallas_call(
        _row_cast_body,
        grid_spec=pltpu.PrefetchScalarGridSpec(
            num_scalar_prefetch=1,
            grid=(_T_ROWS // W_BLOCK,),
            in_specs=[pl.BlockSpec((pl.Element(W_BLOCK), pl.Element(D_MODEL)),
                                   lambda i, starts_ref: (SUBLANES * starts_ref[i], 0)),
                      pl.BlockSpec((N_GATE, N_GATE), lambda i, starts_ref: (0, 0))],
            out_specs=pl.BlockSpec((W_BLOCK, D_MODEL), lambda i, starts_ref: (i, 0))),
        out_shape=jax.ShapeDtypeStruct((_T_ROWS, D_MODEL), BF16),
        name="w_in_transposed",
    )(starts, w_in_t, gate_perm)
    return w_n, w_t


def _ctx_body(ctx_ref, mod_ref, nw_ref, wk_ref, bk_ref, wv_ref, bv_ref, wg_ref, bg_ref, c_ref, m_ref):
    hp = HEADS_PER_STEP
    nb = ctx_ref.shape[0]
    last = mod_ref.shape[0] - 1
    x = ctx_ref[...].reshape(nb * CTX_LEN, D_MODEL)
    hb = _modulated_norm(x, nw_ref[...], mod_ref[last:, _SH1:_SC1], mod_ref[last:, _SC1:_G1]).astype(BF16)
    nt = functools.partial(lax.dot_general, dimension_numbers=(((1,), (1,)), ((), ())),
                           preferred_element_type=F32)
    k_all = ((jnp.dot(hb, wk_ref[...], preferred_element_type=F32) + bk_ref[...]) * (DK ** -0.5)).astype(BF16)
    vt_all = (nt(wv_ref[...], hb) + bv_ref[...]).astype(BF16)
    g_all = nt(wg_ref[...], hb) + bg_ref[...]

    r_in_tile = jnp.bitwise_and(lax.broadcasted_iota(jnp.int32, (N_GATE, 1), 0), GATES_PER_GROUP - 1)
    fwd = r_in_tile < GATES_PER_GROUP // 2 + hp
    ones_rows = jnp.where(lax.broadcasted_iota(jnp.int32, (V_AUG - DV, CTX_LEN), 0) == 0, 1.0, 0.0)
    lane = lax.broadcasted_iota(jnp.int32, (CTX_LEN, hp * DK), 1)
    for bi in range(nb):
        tokens = slice(bi * CTX_LEN, (bi + 1) * CTX_LEN)
        g, vt, k = g_all[:, tokens], vt_all[:, tokens], k_all[tokens]
        lf = _log_sigmoid(g)
        b = _cumsum_lanes(lf)
        ig = pltpu.roll(g, GATES_PER_GROUP // 2, axis=0)
        tot = b[:, CTX_LEN - 1:CTX_LEN]
        g_end = ig + jnp.where(fwd, tot - b, b - lf)
        m = jnp.maximum(tot, jnp.max(g_end, axis=1, keepdims=True))
        m_ref[bi] = m
        wk = jnp.exp(g_end - m)
        for grp in range(N_HEAD_GROUPS):
            k2 = k[:, grp * hp * DK:(grp + 1) * hp * DK]
            zero = jnp.zeros_like(k2)
            k_bd = jnp.concatenate([jnp.where(lane < DK, k2, zero), jnp.where(lane >= DK, k2, zero)], axis=0)
            vf = [jnp.concatenate([vt[(grp * hp + j) * DV:(grp * hp + j + 1) * DV].astype(F32), ones_rows],
                                  axis=0) for j in range(hp)]
            for d in range(N_DIR):
                row0 = grp * GATES_PER_GROUP + GATES_PER_GROUP // 2 + d * hp
                lhs = jnp.concatenate([(vf[j] * wk[row0 + j:row0 + j + 1]).astype(BF16) for j in range(hp)],
                                      axis=1)
                c_ref[bi, d, grp] = jnp.dot(lhs, k_bd, preferred_element_type=F32)


def ctx_call(ctx, mod, norm_w, w_n, b_n, w_t, b_t):
    bsz = ctx.shape[0]
    nb = CTX_BATCH
    const = lambda shape: pl.BlockSpec(shape, lambda i: (0,) * len(shape))
    gate_rows = lambda width: pl.BlockSpec((N_GATE, width), lambda i: (_T_GATES // N_GATE, 0))
    return pl.pallas_call(
        _ctx_body,
        grid=(bsz // nb,),
        in_specs=[pl.BlockSpec((nb, CTX_LEN, D_MODEL), lambda i: (i, 0, 0)),
                  const(mod.shape), const((1, D_MODEL)),
                  const((D_MODEL, D_QK)), const((1, D_QK)),
                  const((D_MLSTM, D_MODEL)), const((D_MLSTM, 1)),
                  gate_rows(D_MODEL), gate_rows(1)],
        out_specs=[pl.BlockSpec((nb, N_DIR, N_HEAD_GROUPS, V_AUG, HEADS_PER_STEP * DK),
                                lambda i: (i, 0, 0, 0, 0)),
                   pl.BlockSpec((nb, N_GATE, 1), lambda i: (i, 0, 0))],
        out_shape=[jax.ShapeDtypeStruct((bsz, N_DIR, N_HEAD_GROUPS, V_AUG, HEADS_PER_STEP * DK), F32),
                   jax.ShapeDtypeStruct((bsz, N_GATE, 1), F32)],
        compiler_params=pltpu.CompilerParams(dimension_semantics=("arbitrary",),
                                             vmem_limit_bytes=VMEM_LIMIT_BYTES),
        name="ctx_state",
    )(ctx, mod, norm_w, w_n, b_n, w_t, b_t, w_t, b_t)


def _inproj_body(x_ref, mod_ref, nw_ref, nwm_ref, w_ref, b_ref, wt_ref, bt_ref, cw_ref, wco_ref,
                 k_ref, qt_ref, vt_ref, gt_ref, so_ref, ya_ref, gb_ref):
    x = x_ref[0]
    mod = mod_ref[pl.ds(pl.program_id(0), 1), :]
    hb = _modulated_norm(x, nw_ref[...], mod[:, _SH1:_SC1], mod[:, _SC1:_G1]).astype(BF16)
    tile = hb.shape[0]

    def seg(a, b):
        return jnp.dot(hb, w_ref[:, a:b], preferred_element_type=F32) + b_ref[:, a:b]

    k_ref[0] = (seg(_K0, _O0) * (DK ** -0.5)).astype(BF16)
    so_ref[0] = (_sigmoid(seg(_O0, _XIN0)) * nwm_ref[...]).astype(BF16)

    zt = lax.dot_general(wt_ref[:_T_END, :], hb, (((1,), (1,)), ((), ())),
                         preferred_element_type=F32) + bt_ref[:_T_END, :]
    for i in range(tile // CHUNK):
        lanes = slice(i * CHUNK, (i + 1) * CHUNK)
        vt_ref[0, i] = zt[_T_V:_T_Q, lanes].astype(BF16)
        qt_ref[0, i] = zt[_T_Q:_T_GATES, lanes].astype(BF16)
        gt_ref[0, i] = zt[_T_GATES:_T_END, lanes]

    u = seg(_XIN0, _GC0) * seg(_GC0, _GB0)
    col = jnp.bitwise_and(lax.broadcasted_iota(jnp.int32, (tile, 1), 0), GRID_W - 1)
    u_prev = jnp.where(col != 0, pltpu.roll(u, 1, axis=0), 0.0)
    u_next = jnp.where(col != GRID_W - 1, pltpu.roll(u, tile - 1, axis=0), 0.0)
    a = cw_ref[0:1, :] * u_prev + cw_ref[1:2, :] * u + cw_ref[2:3, :] * u_next
    ya = jnp.dot((seg(_GB0, _MA0) * a).astype(BF16), wco_ref[...], preferred_element_type=F32)
    ya_ref[0] = (_sigmoid(seg(_MA0, _MB0)) * ya).astype(BF16)
    gb_ref[0] = _sigmoid(seg(_MB0, _G0)).astype(BF16)


def inproj_call(x, mod, norm_w, mlstm_norm_w, w_n, b_n, w_t, b_t, conv_w, w_conv_out):
    bsz, t, _ = x.shape
    tile = TOKEN_TILE
    tok = lambda width: pl.BlockSpec((1, tile, width), lambda i, j: (i, j, 0))
    tok_t = lambda rows: pl.BlockSpec((1, tile // CHUNK, rows, CHUNK), lambda i, j: (i, j, 0, 0))
    seq = lambda width: jax.ShapeDtypeStruct((bsz, t, width), BF16)
    seq_t = lambda rows, dt: jax.ShapeDtypeStruct((bsz, t // CHUNK, rows, CHUNK), dt)
    resident = [mod, norm_w, mlstm_norm_w, w_n, b_n, w_t, b_t, conv_w, w_conv_out]
    return pl.pallas_call(
        _inproj_body,
        grid=(bsz, t // tile),
        in_specs=[tok(D_MODEL)] + [_resident(a.shape) for a in resident],
        out_specs=[tok(D_QK), tok_t(D_QK), tok_t(D_MLSTM), tok_t(N_GATE), tok(D_MLSTM),
                   tok(D_MODEL), tok(D_MODEL)],
        out_shape=[seq(D_QK), seq_t(D_QK, BF16), seq_t(D_MLSTM, BF16), seq_t(N_GATE, F32), seq(D_MLSTM),
                   seq(D_MODEL), seq(D_MODEL)],
        compiler_params=pltpu.CompilerParams(dimension_semantics=("arbitrary", "arbitrary"),
                                             vmem_limit_bytes=VMEM_LIMIT_BYTES),
        name="inproj_conv",
    )(x, *resident)


SERIES_ROW0 = GATES_PER_GROUP // 2
N_SERIES = N_DIR * HEADS_PER_STEP
N_SPLIT = 3
LOG2E = 1.4426950408889634
_WK, _DECAY, _WINTER, _EXPNEG, _KEYS = 0, 1, 2, 3, 4
_COLM = _KEYS + N_SPLIT
N_TABLES = _COLM + N_SPLIT * N_SERIES


def _running_max_rows(x, reverse):
    n = x.shape[0]
    row = lax.broadcasted_iota(jnp.int32, x.shape, 0)
    shift = 1
    while shift < n:
        if reverse:
            moved = jnp.where(row < n - shift, pltpu.roll(x, n - shift, axis=0), -jnp.inf)
        else:
            moved = jnp.where(row >= shift, pltpu.roll(x, shift, axis=0), -jnp.inf)
        x = jnp.maximum(x, moved)
        shift *= 2
    return x


def _gate_tables(g_ref, m0_ref, tab_ref, n_chunks):
    step_rows = N_HEAD_GROUPS * GATES_PER_GROUP
    rows = n_chunks * step_rows
    g = g_ref[0].reshape(rows, CHUNK)
    lf = _log_sigmoid(g)
    b = _cumsum_lanes(lf)
    ig = pltpu.roll(g, SERIES_ROW0, axis=0)
    in_tile = lambda idx: jnp.bitwise_and(idx, GATES_PER_GROUP - 1)
    r_in_tile = in_tile(lax.broadcasted_iota(jnp.int32, (rows, 1), 0))
    fwd = r_in_tile < SERIES_ROW0 + HEADS_PER_STEP
    e = b - lf
    tot = b[:, CHUNK - 1:CHUNK]
    col = jnp.where(fwd, b, -e)
    key = ig + jnp.where(fwd, -b, e)
    g_end = jnp.where(fwd, tot + key, key)
    g_max = jnp.max(g_end, axis=1, keepdims=True)

    fwd_step = fwd[:step_rows]
    tile = lambda a, c: a[c * step_rows:(c + 1) * step_rows]
    m = m0_ref[0]
    m_in_steps, m_out_steps = [], []
    for i in range(n_chunks):
        back = n_chunks - 1 - i
        m_in_steps.append(m)
        m = jnp.maximum(jnp.where(fwd_step, tile(tot, i), tile(tot, back)) + m,
                        jnp.where(fwd_step, tile(g_max, i), tile(g_max, back)))
        m_out_steps.append(m)
    by_chunk = lambda steps: jnp.concatenate(
        [jnp.where(fwd_step, steps[c], steps[n_chunks - 1 - c]) for c in range(n_chunks)], axis=0)
    m_in, m_out = by_chunk(m_in_steps), by_chunk(m_out_steps)

    decay = jnp.broadcast_to(jnp.exp(tot + m_in - m_out), (rows, CHUNK))
    lane = lax.broadcasted_iota(jnp.int32, (rows, CHUNK), 1)
    inter = jnp.where(fwd, col, tot + col) + m_in
    key_t = jnp.transpose(key)
    fwd_lane = in_tile(lax.broadcasted_iota(jnp.int32, (1, rows), 1)) < SERIES_ROW0 + HEADS_PER_STEP
    key_max = jnp.transpose(jnp.where(fwd_lane, _running_max_rows(key_t, reverse=False),
                                      _running_max_rows(key_t, reverse=True)))
    m_t = jnp.maximum(col + key_max, inter)
    tables = {_WK: jnp.exp(g_end - m_out),
              _DECAY: jnp.where(lane < DK, decay, pltpu.roll(decay, rows - 1, axis=0)),
              _WINTER: jnp.exp(inter - m_t), _EXPNEG: jnp.exp(-m_t)}
    for x, part in enumerate(_split3(key * LOG2E)):
        tables[_KEYS + x] = part.astype(F32)
    for x, part in enumerate(_split3((col - m_t) * LOG2E)):
        for sidx in range(N_SERIES):
            tables[_COLM + x * N_SERIES + sidx] = jnp.where(r_in_tile == SERIES_ROW0 + sidx,
                                                            part.astype(F32), 0.0)
    for idx, a in tables.items():
        tab_ref[idx] = a.reshape(n_chunks, N_HEAD_GROUPS, GATES_PER_GROUP, CHUNK)


def _mlstm_body(qt_ref, k_ref, vt_ref, so_ref, g_ref, c0_ref, m0_ref, o_ref,
                tab_ref, u_ref, s_ref, st_ref, causal_ref, *, n_chunks):
    hp = HEADS_PER_STEP
    ones_rows = jnp.where(lax.broadcasted_iota(jnp.int32, (V_AUG - DV, CHUNK), 0) == 0,
                          1.0, 0.0).astype(BF16)
    low_lanes = lax.broadcasted_iota(jnp.int32, (1, hp * DK), 1) < DK
    series = lambda tab8, d, j: tab8[SERIES_ROW0 + d * hp + j:SERIES_ROW0 + d * hp + j + 1]
    split_rows = N_SPLIT * GATES_PER_GROUP
    ones_split = jnp.ones((split_rows, CHUNK), F32)
    pick_r = jnp.bitwise_and(lax.broadcasted_iota(jnp.int32, (split_rows, N_DIR * CHUNK), 0),
                             GATES_PER_GROUP - 1)
    pick_d = jnp.where(lax.broadcasted_iota(jnp.int32, (split_rows, N_DIR * CHUNK), 1) >= CHUNK, 1, 0)
    pick_series = [jnp.where(pick_r == SERIES_ROW0 + pick_d * hp + j, 1.0, 0.0) for j in range(hp)]

    def chunk_rows(c):
        return slice(c * CHUNK, (c + 1) * CHUNK)

    def for_each_chunk(body, carry):
        for c in range(n_chunks):
            carry = body(c, carry)
        return carry

    def values_t(c, j):
        return jnp.concatenate([vt_ref[0, c, j * DV:(j + 1) * DV, :], ones_rows], axis=0)

    grp = pl.program_id(1)

    @pl.when(grp == 0)
    def _():
        _gate_tables(g_ref, m0_ref, tab_ref, n_chunks)
        s_i = lax.broadcasted_iota(jnp.int32, (CHUNK, CHUNK), 0)
        t_i = lax.broadcasted_iota(jnp.int32, (CHUNK, CHUNK), 1)
        causal_ref[0] = jnp.where(s_i <= t_i, 0.0, -jnp.inf)
        causal_ref[1] = jnp.where(s_i >= t_i, 0.0, -jnp.inf)

    def block_diag_q(qt2):
        zero = jnp.zeros((DK, CHUNK), qt2.dtype)
        return jnp.concatenate([jnp.concatenate([qt2[:DK], zero], axis=0),
                                jnp.concatenate([zero, qt2[DK:]], axis=0)], axis=1)

    def chunk_step(c, carry):
        k2 = k_ref[0, chunk_rows(c), :]
        zero = jnp.zeros_like(k2)
        k_bd = jnp.concatenate([jnp.where(low_lanes, k2, zero),
                                jnp.where(low_lanes, zero, k2)], axis=0)
        wk8 = tab_ref[_WK, c, grp]
        vb = [values_t(c, j) for j in range(hp)]

        def key_weights(d, j):
            tile = jnp.broadcast_to(series(wk8, d, j), (BF16_SUBLANES, CHUNK)).astype(BF16)
            return jnp.concatenate([tile] * (V_AUG // BF16_SUBLANES), axis=0)

        lhs = jnp.concatenate(
            [jnp.concatenate([vb[j] * key_weights(d, j) for j in range(hp)], axis=1)
             for d in range(N_DIR)], axis=0)
        u_ref[c] = jnp.dot(lhs, k_bd, preferred_element_type=F32)

        qk_t = jnp.dot(k2, block_diag_q(qt_ref[0, c]), preferred_element_type=F32)
        key_side = jnp.concatenate([tab_ref[_KEYS + x, c, grp] for x in range(N_SPLIT)] + [ones_split],
                                   axis=0).astype(BF16)
        for j in range(hp):
            query_side = jnp.concatenate(
                [pick_series[j]] +
                [jnp.concatenate([tab_ref[_COLM + x * N_SERIES + d * hp + j, c, grp] for d in range(N_DIR)],
                                 axis=1)
                 for x in range(N_SPLIT)], axis=0).astype(BF16)
            log_d = lax.dot_general(key_side, query_side, (((0,), (0,)), ((), ())),
                                    preferred_element_type=F32)
            for d in range(N_DIR):
                decay = jnp.exp2(log_d[:, d * CHUNK:(d + 1) * CHUNK] + causal_ref[d])
                st_ref[c, d, j] = (qk_t[:, j * CHUNK:(j + 1) * CHUNK] * decay).astype(BF16)
        return carry

    for_each_chunk(chunk_step, 0)

    def scan_step(i, carry):
        cf, cb = carry
        back = n_chunks - 1 - i
        s_ref[i, 0, 0] = cf.astype(BF16)
        s_ref[i, 0, 1] = pltpu.roll(cf, DK, axis=1).astype(BF16)
        s_ref[back, 1, 0] = cb.astype(BF16)
        s_ref[back, 1, 1] = pltpu.roll(cb, DK, axis=1).astype(BF16)
        cf = tab_ref[_DECAY, i, grp][SERIES_ROW0:SERIES_ROW0 + 1] * cf + u_ref[i, :V_AUG]
        cb = tab_ref[_DECAY, back, grp][SERIES_ROW0 + hp:SERIES_ROW0 + hp + 1] * cb + u_ref[back, V_AUG:]
        return cf, cb

    for_each_chunk(scan_step, (c0_ref[0, 0, 0], c0_ref[0, 1, 0]))

    def output_step(c, carry):
        rows = chunk_rows(c)
        qt2 = qt_ref[0, c].astype(F32)
        w_inter8, exp_neg8 = tab_ref[_WINTER, c, grp], tab_ref[_EXPNEG, c, grp]
        no_query = jnp.zeros((DK, CHUNK), BF16)
        for j in range(hp):
            state = jnp.where(low_lanes, s_ref[c, 0, j], s_ref[c, 1, 1 - j])
            q_h = qt2[j * DK:(j + 1) * DK]
            qw = [(q_h * series(w_inter8, d, j)).astype(BF16) for d in range(N_DIR)]
            rhs = jnp.concatenate(
                [jnp.concatenate([st_ref[c, d, j] for d in range(N_DIR)], axis=1),
                 jnp.concatenate([qw[0], no_query], axis=1),
                 jnp.concatenate([no_query, qw[1]], axis=1)], axis=0)
            lhs = jnp.concatenate([values_t(c, j), state], axis=1)
            n_all = jnp.dot(lhs, rhs, preferred_element_type=F32)
            h_t = None
            for d in range(N_DIR):
                num = n_all[:, d * CHUNK:(d + 1) * CHUNK]
                r = 1.0 / jnp.maximum(jnp.abs(num[DV:DV + 1]), series(exp_neg8, d, j))
                part = num[:DV] * r
                h_t = part if h_t is None else h_t + part
            hn_t = h_t * lax.rsqrt(jnp.mean(h_t * h_t, axis=0, keepdims=True) + EPS)
            o_ref[0, rows, j * DV:(j + 1) * DV] = (
                jnp.transpose(hn_t) * so_ref[0, rows, j * DV:(j + 1) * DV].astype(F32)).astype(BF16)
        return carry

    for_each_chunk(output_step, 0)


def mlstm_call(qt, k, vt, so, gates, c0, m0):
    bsz, t, _ = k.shape
    n_chunks = t // CHUNK
    hp = HEADS_PER_STEP
    seq = lambda width: pl.BlockSpec((1, t, width), lambda i, j: (i, 0, j))
    seq_t = lambda rows: pl.BlockSpec((1, n_chunks, rows, CHUNK), lambda i, j: (i, 0, j, 0))
    return pl.pallas_call(
        functools.partial(_mlstm_body, n_chunks=n_chunks),
        grid=(bsz, N_HEAD_GROUPS),
        in_specs=[seq_t(hp * DK), seq(hp * DK), seq_t(hp * DV), seq(hp * DV),
                  pl.BlockSpec((1, n_chunks, N_GATE, CHUNK), lambda i, j: (i, 0, 0, 0)),
                  pl.BlockSpec((1, N_DIR, 1, V_AUG, hp * DK), lambda i, j: (i, 0, j, 0, 0)),
                  pl.BlockSpec((1, N_GATE, 1), lambda i, j: (i, 0, 0))],
        out_specs=seq(hp * DV),
        out_shape=jax.ShapeDtypeStruct((bsz, t, D_MLSTM), BF16),
        scratch_shapes=[pltpu.VMEM((N_TABLES, n_chunks, N_HEAD_GROUPS, GATES_PER_GROUP, CHUNK), F32),
                        pltpu.VMEM((n_chunks, N_DIR * V_AUG, hp * DK), F32),
                        pltpu.VMEM((n_chunks, N_DIR, 2, V_AUG, hp * DK), BF16),
                        pltpu.VMEM((n_chunks, N_DIR, hp, CHUNK, CHUNK), BF16),
                        pltpu.VMEM((N_DIR, CHUNK, CHUNK), F32)],
        compiler_params=pltpu.CompilerParams(dimension_semantics=("arbitrary", "arbitrary"),
                                             vmem_limit_bytes=VMEM_LIMIT_BYTES),
        name="mlstm",
    )(qt, k, vt, so, gates, c0, m0)


def _out_body(x_ref, hs_ref, ya_ref, gb_ref, mod_ref, nw2_ref, fnw_ref, wmo_ref, wo_ref, w1_ref, w2_ref, o_ref):
    dot = functools.partial(jnp.dot, preferred_element_type=F32)
    mod = mod_ref[pl.ds(pl.program_id(0), 1), :]
    yb = dot(hs_ref[0], wmo_ref[...])
    merged = ya_ref[0].astype(F32) + gb_ref[0].astype(F32) * yb
    x1 = x_ref[0] + mod[:, _G1:_SH2] * dot(merged.astype(BF16), wo_ref[...])
    hm = _modulated_norm(x1, nw2_ref[...], mod[:, _SH2:_SC2], mod[:, _SC2:_G2]).astype(BF16)
    a = jnp.maximum(dot(hm, w1_ref[...]), 0.0)
    x2 = x1 + mod[:, _G2:] * dot((a * a).astype(BF16), w2_ref[...])
    y = x2 * lax.rsqrt(jnp.mean(x2 * x2, axis=-1, keepdims=True) + EPS)
    o_ref[0] = y * fnw_ref[...]


def out_call(x, hs, ya, gb, mod, norm2_w, final_norm_w, w_mlstm_out, w_out, w_ff1, w_ff2):
    bsz, t, _ = x.shape
    tile = TOKEN_TILE
    tok = pl.BlockSpec((1, tile, D_MODEL), lambda i, j: (i, j, 0))
    resident = [mod, norm2_w, final_norm_w, w_mlstm_out, w_out, w_ff1, w_ff2]
    return pl.pallas_call(
        _out_body,
        grid=(bsz, t // tile),
        in_specs=[tok, tok, tok, tok] + [_resident(a.shape) for a in resident],
        out_specs=tok,
        out_shape=jax.ShapeDtypeStruct((bsz, t, D_MODEL), F32),
        compiler_params=pltpu.CompilerParams(dimension_semantics=("arbitrary", "arbitrary"),
                                             vmem_limit_bytes=VMEM_LIMIT_BYTES),
        name="merge_out_mlp",
    )(x, hs, ya, gb, *resident)


def _layer(x, ctx, mod, norm1_w, w_in, b_in, conv_w, mlstm_norm_w, w_conv_out, w_mlstm_out,
           w_out, norm2_w, w_ff1, w_ff2, final_norm_w):
    nw1 = norm1_w.reshape(1, D_MODEL)
    w_n, w_t = projection_weights(w_in.T)
    b_n = jnp.concatenate([b_in[_REF_K:_REF_V], b_in[_REF_O:]]).reshape(1, -1)
    b_t = jnp.concatenate([b_in[_REF_V:_REF_IG], b_in[_REF_Q:_REF_O], _group_gates(b_in[_REF_IG:_REF_Q]),
                           jnp.zeros((_T_ROWS - _T_END,), F32)]).reshape(-1, 1)

    c0, m0 = ctx_call(ctx, mod, nw1, w_n, b_n, w_t, b_t)
    k, qt, vt, gates, so, ya, gb = inproj_call(x, mod, nw1, mlstm_norm_w.reshape(1, D_MLSTM), w_n, b_n, w_t, b_t,
                                               conv_w, w_conv_out.astype(BF16))
    hs = mlstm_call(qt, k, vt, so, gates, c0, m0)
    return out_call(x, hs, ya, gb, mod, norm2_w.reshape(1, D_MODEL), final_norm_w.reshape(1, D_MODEL),
                    w_mlstm_out.astype(BF16), w_out.astype(BF16), w_ff1.astype(BF16), w_ff2.astype(BF16))


def kernel(x, c, ctx, c_ctx, w_mod, b_mod, norm1_w, w_in, b_in, conv_w, mlstm_norm_w, w_conv_out,
           w_mlstm_out, w_out, norm2_w, w_ff1, w_ff2, final_norm_w):
    depth = w_mod.shape[0]
    assert depth == 1, "the context stream is only advanced through its mLSTM state (single layer)"
    cvecs = jnp.concatenate([c, c_ctx[None, :]], axis=0)
    mod = adaln_call(cvecs, w_mod[0], b_mod[0])
    return _layer(x, ctx, mod, norm1_w[0], w_in[0], b_in[0], conv_w[0],
                  mlstm_norm_w[0], w_conv_out[0], w_mlstm_out[0], w_out[0], norm2_w[0], w_ff1[0],
                  w_ff2[0], final_norm_w)
```

```python
import functools

import jax
import jax.numpy as jnp
from jax import lax
from jax.experimental import pallas as pl
from jax.experimental.pallas import tpu as pltpu

D_MODEL = 1024
CTX_LEN = 256
GRID_W = 64
N_HEADS = 8
DK = 64
DV = 128
D_MLSTM = N_HEADS * DV
D_QK = N_HEADS * DK
N_DIR = 2
N_GATE = 2 * N_DIR * N_HEADS
EPS = 1e-6

CHUNK = 128
HEADS_PER_STEP = 2
assert HEADS_PER_STEP == 2 and HEADS_PER_STEP * DK == 128, "a head pair shares one 128-lane tile"
N_HEAD_GROUPS = N_HEADS // HEADS_PER_STEP
GATES_PER_GROUP = N_GATE // N_HEAD_GROUPS
SUBLANES = 8
BF16_SUBLANES = 16
V_AUG = DV + BF16_SUBLANES
TOKEN_TILE = 512
ADALN_TILE = 1024
CTX_BATCH = 4

VMEM_LIMIT_BYTES = 56 * 1024 * 1024

_REF_K, _REF_V, _REF_IG, _REF_FG, _REF_Q, _REF_O = 0, 512, 1536, 1552, 1568, 2080
D_IN = 8224
_K0, _O0, _XIN0, _GC0, _GB0, _MA0, _MB0, _G0 = 0, 512, 1536, 2560, 3584, 4608, 5632, 6656
_T_V, _T_Q, _T_GATES, _T_END, _T_ROWS = 0, 1024, 1536, 1568, 2048
_SH1, _SC1, _G1, _SH2, _SC2, _G2 = (i * D_MODEL for i in range(6))

F32 = jnp.float32
BF16 = jnp.bfloat16


def _sigmoid(x):
    return 0.5 * jnp.tanh(0.5 * x) + 0.5


def _log_sigmoid(x):
    return jnp.minimum(x, 0.0) - jnp.log(1.0 + jnp.exp(-jnp.abs(x)))


def _split3(x):
    hi = x.astype(BF16)
    r1 = x - hi.astype(F32)
    mid = r1.astype(BF16)
    lo = (r1 - mid.astype(F32)).astype(BF16)
    return hi, mid, lo


def _cumsum_lanes(x):
    n = x.shape[1]
    r = lax.broadcasted_iota(jnp.int32, (n, n), 0)
    c = lax.broadcasted_iota(jnp.int32, (n, n), 1)
    tri_t = jnp.where(r <= c, 1.0, 0.0).astype(BF16)
    hi, mid, lo = _split3(x)
    dot = functools.partial(jnp.dot, preferred_element_type=F32)
    return dot(hi, tri_t) + dot(mid, tri_t) + dot(lo, tri_t)


def _modulated_norm(x, norm_w, shift, scale):
    y = x * lax.rsqrt(jnp.mean(x * x, axis=-1, keepdims=True) + EPS)
    return (y * norm_w) * (1.0 + scale) + shift


def _resident(shape):
    nd = len(shape)
    return pl.BlockSpec(shape, lambda *_: (0,) * nd, pipeline_mode=pl.Buffered(1))


def _adaln_body(c_ref, w_ref, b_ref, o_ref):
    c = c_ref[...]
    s = c * _sigmoid(c)
    dot = functools.partial(jnp.dot, preferred_element_type=F32)
    w = w_ref[...]
    s_hi, w_hi = s.astype(BF16), w.astype(BF16)
    s_lo, w_lo = (s - s_hi.astype(F32)).astype(BF16), (w - w_hi.astype(F32)).astype(BF16)
    o_ref[...] = (dot(s_hi, w_hi) + dot(s_hi, w_lo) + dot(s_lo, w_hi)) + b_ref[...]


def adaln_call(cvecs, w_mod, b_mod):
    n = cvecs.shape[0]
    n_out = w_mod.shape[1]
    tile = ADALN_TILE
    return pl.pallas_call(
        _adaln_body,
        grid=(n_out // tile,),
        in_specs=[pl.BlockSpec((n, D_MODEL), lambda j: (0, 0)),
                  pl.BlockSpec((D_MODEL, tile), lambda j: (0, j)),
                  pl.BlockSpec((1, tile), lambda j: (0, j))],
        out_specs=pl.BlockSpec((n, tile), lambda j: (0, j)),
        out_shape=jax.ShapeDtypeStruct((n, n_out), F32),
        name="adaln",
    )(cvecs, w_mod, b_mod.reshape(1, n_out))


W_BLOCK = 512


def _transpose_cast_body(w_ref, o_ref):
    o_ref[...] = jnp.transpose(w_ref[...]).astype(BF16)


def _row_cast_body(starts_ref, w_ref, perm_ref, o_ref):
    del starts_ref
    o_ref[...] = w_ref[...].astype(BF16)

    @pl.when(pl.program_id(0) == pl.num_programs(0) - 1)
    def _():
        gates = w_ref[0:N_GATE, :].astype(BF16)
        o_ref[0:N_GATE, :] = jnp.dot(perm_ref[...], gates, preferred_element_type=F32).astype(BF16)


def _group_gates(g):
    lead = g.shape[:-1]
    g = g.reshape(*lead, 2, N_DIR, N_HEAD_GROUPS, HEADS_PER_STEP)
    g = jnp.moveaxis(g, -2, -4)
    return g.reshape(*lead, N_GATE)


def projection_weights(w_in_t):
    n_rest = (D_IN - _REF_O) // W_BLOCK
    blk = W_BLOCK // SUBLANES
    w_n = pl.pallas_call(
        _transpose_cast_body,
        grid=(1 + n_rest,),
        in_specs=[pl.BlockSpec((pl.Element(W_BLOCK), pl.Element(D_MODEL)),
                               lambda i: (SUBLANES * jnp.where(i == 0, _REF_K // SUBLANES,
                                                               _REF_O // SUBLANES + (i - 1) * blk), 0))],
        out_specs=pl.BlockSpec((D_MODEL, W_BLOCK), lambda i: (0, i)),
        out_shape=jax.ShapeDtypeStruct((D_MODEL, (1 + n_rest) * W_BLOCK), BF16),
        name="w_in_normal",
    )(w_in_t)
    starts = jnp.asarray([_REF_V // SUBLANES, _REF_V // SUBLANES + blk, _REF_Q // SUBLANES, _REF_IG // SUBLANES],
                         jnp.int32)
    gate_perm = jnp.eye(N_GATE, dtype=BF16)[_group_gates(jnp.arange(N_GATE))]
    w_t = pl.pallas_call(
        _row_cast_body,
        grid_spec=pltpu.PrefetchScalarGridSpec(
            num_scalar_prefetch=1,
            grid=(_T_ROWS // W_BLOCK,),
            in_specs=[pl.BlockSpec((pl.Element(W_BLOCK), pl.Element(D_MODEL)),
                                   lambda i, starts_ref: (SUBLANES * starts_ref[i], 0)),
                      pl.BlockSpec((N_GATE, N_GATE), lambda i, starts_ref: (0, 0))],
            out_specs=pl.BlockSpec((W_BLOCK, D_MODEL), lambda i, starts_ref: (i, 0))),
        out_shape=jax.ShapeDtypeStruct((_T_ROWS, D_MODEL), BF16),
        name="w_in_transposed",
    )(starts, w_in_t, gate_perm)
    return w_n, w_t


def _ctx_body(ctx_ref, mod_ref, nw_ref, wk_ref, bk_ref, wv_ref, bv_ref, wg_ref, bg_ref, c_ref, m_ref):
    hp = HEADS_PER_STEP
    nb = ctx_ref.shape[0]
    last = mod_ref.shape[0] - 1
    x = ctx_ref[...].reshape(nb * CTX_LEN, D_MODEL)
    hb = _modulated_norm(x, nw_ref[...], mod_ref[last:, _SH1:_SC1], mod_ref[last:, _SC1:_G1]).astype(BF16)
    nt = functools.partial(lax.dot_general, dimension_numbers=(((1,), (1,)), ((), ())),
                           preferred_element_type=F32)
    k_all = ((jnp.dot(hb, wk_ref[...], preferred_element_type=F32) + bk_ref[...]) * (DK ** -0.5)).astype(BF16)
    vt_all = (nt(wv_ref[...], hb) + bv_ref[...]).astype(BF16)
    g_all = nt(wg_ref[...], hb) + bg_ref[...]

    r_in_tile = jnp.bitwise_and(lax.broadcasted_iota(jnp.int32, (N_GATE, 1), 0), GATES_PER_GROUP - 1)
    fwd = r_in_tile < GATES_PER_GROUP // 2 + hp
    ones_rows = jnp.where(lax.broadcasted_iota(jnp.int32, (V_AUG - DV, CTX_LEN), 0) == 0, 1.0, 0.0)
    lane = lax.broadcasted_iota(jnp.int32, (CTX_LEN, hp * DK), 1)
    for bi in range(nb):
        tokens = slice(bi * CTX_LEN, (bi + 1) * CTX_LEN)
        g, vt, k = g_all[:, tokens], vt_all[:, tokens], k_all[tokens]
        lf = _log_sigmoid(g)
        b = _cumsum_lanes(lf)
        ig = pltpu.roll(g, GATES_PER_GROUP // 2, axis=0)
        tot = b[:, CTX_LEN - 1:CTX_LEN]
        g_end = ig + jnp.where(fwd, tot - b, b - lf)
        m = jnp.maximum(tot, jnp.max(g_end, axis=1, keepdims=True))
        m_ref[bi] = m
        wk = jnp.exp(g_end - m)
        for grp in range(N_HEAD_GROUPS):
            k2 = k[:, grp * hp * DK:(grp + 1) * hp * DK]
            zero = jnp.zeros_like(k2)
            k_bd = jnp.concatenate([jnp.where(lane < DK, k2, zero), jnp.where(lane >= DK, k2, zero)], axis=0)
            vf = [jnp.concatenate([vt[(grp * hp + j) * DV:(grp * hp + j + 1) * DV].astype(F32), ones_rows],
                                  axis=0) for j in range(hp)]
            for d in range(N_DIR):
                row0 = grp * GATES_PER_GROUP + GATES_PER_GROUP // 2 + d * hp
                lhs = jnp.concatenate([(vf[j] * wk[row0 + j:row0 + j + 1]).astype(BF16) for j in range(hp)],
                                      axis=1)
                c_ref[bi, d, grp] = jnp.dot(lhs, k_bd, preferred_element_type=F32)


def ctx_call(ctx, mod, norm_w, w_n, b_n, w_t, b_t):
    bsz = ctx.shape[0]
    nb = CTX_BATCH
    const = lambda shape: pl.BlockSpec(shape, lambda i: (0,) * len(shape))
    gate_rows = lambda width: pl.BlockSpec((N_GATE, width), lambda i: (_T_GATES // N_GATE, 0))
    return pl.pallas_call(
        _ctx_body,
        grid=(bsz // nb,),
        in_specs=[pl.BlockSpec((nb, CTX_LEN, D_MODEL), lambda i: (i, 0, 0)),
                  const(mod.shape), const((1, D_MODEL)),
                  const((D_MODEL, D_QK)), const((1, D_QK)),
                  const((D_MLSTM, D_MODEL)), const((D_MLSTM, 1)),
                  gate_rows(D_MODEL), gate_rows(1)],
        out_specs=[pl.BlockSpec((nb, N_DIR, N_HEAD_GROUPS, V_AUG, HEADS_PER_STEP * DK),
                                lambda i: (i, 0, 0, 0, 0)),
                   pl.BlockSpec((nb, N_GATE, 1), lambda i: (i, 0, 0))],
        out_shape=[jax.ShapeDtypeStruct((bsz, N_DIR, N_HEAD_GROUPS, V_AUG, HEADS_PER_STEP * DK), F32),
                   jax.ShapeDtypeStruct((bsz, N_GATE, 1), F32)],
        compiler_params=pltpu.CompilerParams(dimension_semantics=("arbitrary",),
                                             vmem_limit_bytes=VMEM_LIMIT_BYTES),
        name="ctx_state",
    )(ctx, mod, norm_w, w_n, b_n, w_t, b_t, w_t, b_t)


def _inproj_body(x_ref, mod_ref, nw_ref, nwm_ref, w_ref, b_ref, wt_ref, bt_ref, cw_ref, wco_ref,
                 k_ref, qt_ref, vt_ref, gt_ref, so_ref, ya_ref, gb_ref):
    x = x_ref[0]
    mod = mod_ref[pl.ds(pl.program_id(0), 1), :]
    hb = _modulated_norm(x, nw_ref[...], mod[:, _SH1:_SC1], mod[:, _SC1:_G1]).astype(BF16)
    tile = hb.shape[0]

    def seg(a, b):
        return jnp.dot(hb, w_ref[:, a:b], preferred_element_type=F32) + b_ref[:, a:b]

    k_ref[0] = (seg(_K0, _O0) * (DK ** -0.5)).astype(BF16)
    so_ref[0] = (_sigmoid(seg(_O0, _XIN0)) * nwm_ref[...]).astype(BF16)

    zt = lax.dot_general(wt_ref[:_T_END, :], hb, (((1,), (1,)), ((), ())),
                         preferred_element_type=F32) + bt_ref[:_T_END, :]
    for i in range(tile // CHUNK):
        lanes = slice(i * CHUNK, (i + 1) * CHUNK)
        vt_ref[0, i] = zt[_T_V:_T_Q, lanes].astype(BF16)
        qt_ref[0, i] = zt[_T_Q:_T_GATES, lanes].astype(BF16)
        gt_ref[0, i] = zt[_T_GATES:_T_END, lanes]

    u = seg(_XIN0, _GC0) * seg(_GC0, _GB0)
    col = jnp.bitwise_and(lax.broadcasted_iota(jnp.int32, (tile, 1), 0), GRID_W - 1)
    u_prev = jnp.where(col != 0, pltpu.roll(u, 1, axis=0), 0.0)
    u_next = jnp.where(col != GRID_W - 1, pltpu.roll(u, tile - 1, axis=0), 0.0)
    a = cw_ref[0:1, :] * u_prev + cw_ref[1:2, :] * u + cw_ref[2:3, :] * u_next
    ya = jnp.dot((seg(_GB0, _MA0) * a).astype(BF16), wco_ref[...], preferred_element_type=F32)
    ya_ref[0] = (_sigmoid(seg(_MA0, _MB0)) * ya).astype(BF16)
    gb_ref[0] = _sigmoid(seg(_MB0, _G0)).astype(BF16)


def inproj_call(x, mod, norm_w, mlstm_norm_w, w_n, b_n, w_t, b_t, conv_w, w_conv_out):
    bsz, t, _ = x.shape
    tile = TOKEN_TILE
    tok = lambda width: pl.BlockSpec((1, tile, width), lambda i, j: (i, j, 0))
    tok_t = lambda rows: pl.BlockSpec((1, tile // CHUNK, rows, CHUNK), lambda i, j: (i, j, 0, 0))
    seq = lambda width: jax.ShapeDtypeStruct((bsz, t, width), BF16)
    seq_t = lambda rows, dt: jax.ShapeDtypeStruct((bsz, t // CHUNK, rows, CHUNK), dt)
    resident = [mod, norm_w, mlstm_norm_w, w_n, b_n, w_t, b_t, conv_w, w_conv_out]
    return pl.pallas_call(
        _inproj_body,
        grid=(bsz, t // tile),
        in_specs=[tok(D_MODEL)] + [_resident(a.shape) for a in resident],
        out_specs=[tok(D_QK), tok_t(D_QK), tok_t(D_MLSTM), tok_t(N_GATE), tok(D_MLSTM),
                   tok(D_MODEL), tok(D_MODEL)],
        out_shape=[seq(D_QK), seq_t(D_QK, BF16), seq_t(D_MLSTM, BF16), seq_t(N_GATE, F32), seq(D_MLSTM),
                   seq(D_MODEL), seq(D_MODEL)],
        compiler_params=pltpu.CompilerParams(dimension_semantics=("arbitrary", "arbitrary"),
                                             vmem_limit_bytes=VMEM_LIMIT_BYTES),
        name="inproj_conv",
    )(x, *resident)


SERIES_ROW0 = GATES_PER_GROUP // 2
N_SERIES = N_DIR * HEADS_PER_STEP
N_SPLIT = 3
LOG2E = 1.4426950408889634
_WK, _DECAY, _WINTER, _EXPNEG, _KEYS = 0, 1, 2, 3, 4
_COLM = _KEYS + N_SPLIT
N_TABLES = _COLM + N_SPLIT * N_SERIES


def _running_max_rows(x, reverse):
    n = x.shape[0]
    row = lax.broadcasted_iota(jnp.int32, x.shape, 0)
    shift = 1
    while shift < n:
        if reverse:
            moved = jnp.where(row < n - shift, pltpu.roll(x, n - shift, axis=0), -jnp.inf)
        else:
            moved = jnp.where(row >= shift, pltpu.roll(x, shift, axis=0), -jnp.inf)
        x = jnp.maximum(x, moved)
        shift *= 2
    return x


def _gate_tables(g_ref, m0_ref, tab_ref, n_chunks):
    step_rows = N_HEAD_GROUPS * GATES_PER_GROUP
    rows = n_chunks * step_rows
    g = g_ref[0].reshape(rows, CHUNK)
    lf = _log_sigmoid(g)
    b = _cumsum_lanes(lf)
    ig = pltpu.roll(g, SERIES_ROW0, axis=0)
    in_tile = lambda idx: jnp.bitwise_and(idx, GATES_PER_GROUP - 1)
    r_in_tile = in_tile(lax.broadcasted_iota(jnp.int32, (rows, 1), 0))
    fwd = r_in_tile < SERIES_ROW0 + HEADS_PER_STEP
    e = b - lf
    tot = b[:, CHUNK - 1:CHUNK]
    col = jnp.where(fwd, b, -e)
    key = ig + jnp.where(fwd, -b, e)
    g_end = jnp.where(fwd, tot + key, key)
    g_max = jnp.max(g_end, axis=1, keepdims=True)

    fwd_step = fwd[:step_rows]
    tile = lambda a, c: a[c * step_rows:(c + 1) * step_rows]
    m = m0_ref[0]
    m_in_steps, m_out_steps = [], []
    for i in range(n_chunks):
        back = n_chunks - 1 - i
        m_in_steps.append(m)
        m = jnp.maximum(jnp.where(fwd_step, tile(tot, i), tile(tot, back)) + m,
                        jnp.where(fwd_step, tile(g_max, i), tile(g_max, back)))
        m_out_steps.append(m)
    by_chunk = lambda steps: jnp.concatenate(
        [jnp.where(fwd_step, steps[c], steps[n_chunks - 1 - c]) for c in range(n_chunks)], axis=0)
    m_in, m_out = by_chunk(m_in_steps), by_chunk(m_out_steps)

    decay = jnp.broadcast_to(jnp.exp(tot + m_in - m_out), (rows, CHUNK))
    lane = lax.broadcasted_iota(jnp.int32, (rows, CHUNK), 1)
    inter = jnp.where(fwd, col, tot + col) + m_in
    key_t = jnp.transpose(key)
    fwd_lane = in_tile(lax.broadcasted_iota(jnp.int32, (1, rows), 1)) < SERIES_ROW0 + HEADS_PER_STEP
    key_max = jnp.transpose(jnp.where(fwd_lane, _running_max_rows(key_t, reverse=False),
                                      _running_max_rows(key_t, reverse=True)))
    m_t = jnp.maximum(col + key_max, inter)
    tables = {_WK: jnp.exp(g_end - m_out),
              _DECAY: jnp.where(lane < DK, decay, pltpu.roll(decay, rows - 1, axis=0)),
              _WINTER: jnp.exp(inter - m_t), _EXPNEG: jnp.exp(-m_t)}
    for x, part in enumerate(_split3(key * LOG2E)):
        tables[_KEYS + x] = part.astype(F32)
    for x, part in enumerate(_split3((col - m_t) * LOG2E)):
        for sidx in range(N_SERIES):
            tables[_COLM + x * N_SERIES + sidx] = jnp.where(r_in_tile == SERIES_ROW0 + sidx,
                                                            part.astype(F32), 0.0)
    for idx, a in tables.items():
        tab_ref[idx] = a.reshape(n_chunks, N_HEAD_GROUPS, GATES_PER_GROUP, CHUNK)


def _mlstm_body(qt_ref, k_ref, vt_ref, so_ref, g_ref, c0_ref, m0_ref, o_ref,
                tab_ref, u_ref, s_ref, st_ref, causal_ref, *, n_chunks):
    hp = HEADS_PER_STEP
    ones_rows = jnp.where(lax.broadcasted_iota(jnp.int32, (V_AUG - DV, CHUNK), 0) == 0,
                          1.0, 0.0).astype(BF16)
    low_lanes = lax.broadcasted_iota(jnp.int32, (1, hp * DK), 1) < DK
    series = lambda tab8, d, j: tab8[SERIES_ROW0 + d * hp + j:SERIES_ROW0 + d * hp + j + 1]
    split_rows = N_SPLIT * GATES_PER_GROUP
    ones_split = jnp.ones((split_rows, CHUNK), F32)
    pick_r = jnp.bitwise_and(lax.broadcasted_iota(jnp.int32, (split_rows, N_DIR * CHUNK), 0),
                             GATES_PER_GROUP - 1)
    pick_d = jnp.where(lax.broadcasted_iota(jnp.int32, (split_rows, N_DIR * CHUNK), 1) >= CHUNK, 1, 0)
    pick_series = [jnp.where(pick_r == SERIES_ROW0 + pick_d * hp + j, 1.0, 0.0) for j in range(hp)]

    def chunk_rows(c):
        return slice(c * CHUNK, (c + 1) * CHUNK)

    def for_each_chunk(body, carry):
        for c in range(n_chunks):
            carry = body(c, carry)
        return carry

    def values_t(c, j):
        return jnp.concatenate([vt_ref[0, c, j * DV:(j + 1) * DV, :], ones_rows], axis=0)

    grp = pl.program_id(1)

    @pl.when(grp == 0)
    def _():
        _gate_tables(g_ref, m0_ref, tab_ref, n_chunks)
        s_i = lax.broadcasted_iota(jnp.int32, (CHUNK, CHUNK), 0)
        t_i = lax.broadcasted_iota(jnp.int32, (CHUNK, CHUNK), 1)
        causal_ref[0] = jnp.where(s_i <= t_i, 0.0, -jnp.inf)
        causal_ref[1] = jnp.where(s_i >= t_i, 0.0, -jnp.inf)

    def block_diag_q(qt2):
        zero = jnp.zeros((DK, CHUNK), qt2.dtype)
        return jnp.concatenate([jnp.concatenate([qt2[:DK], zero], axis=0),
                                jnp.concatenate([zero, qt2[DK:]], axis=0)], axis=1)

    def chunk_step(c, carry):
        k2 = k_ref[0, chunk_rows(c), :]
        zero = jnp.zeros_like(k2)
        k_bd = jnp.concatenate([jnp.where(low_lanes, k2, zero),
                                jnp.where(low_lanes, zero, k2)], axis=0)
        wk8 = tab_ref[_WK, c, grp]
        vb = [values_t(c, j) for j in range(hp)]

        def key_weights(d, j):
            tile = jnp.broadcast_to(series(wk8, d, j), (BF16_SUBLANES, CHUNK)).astype(BF16)
            return jnp.concatenate([tile] * (V_AUG // BF16_SUBLANES), axis=0)

        lhs = jnp.concatenate(
            [jnp.concatenate([vb[j] * key_weights(d, j) for j in range(hp)], axis=1)
             for d in range(N_DIR)], axis=0)
        u_ref[c] = jnp.dot(lhs, k_bd, preferred_element_type=F32)

        qk_t = jnp.dot(k2, block_diag_q(qt_ref[0, c]), preferred_element_type=F32)
        key_side = jnp.concatenate([tab_ref[_KEYS + x, c, grp] for x in range(N_SPLIT)] + [ones_split],
                                   axis=0).astype(BF16)
        for j in range(hp):
            query_side = jnp.concatenate(
                [pick_series[j]] +
                [jnp.concatenate([tab_ref[_COLM + x * N_SERIES + d * hp + j, c, grp] for d in range(N_DIR)],
                                 axis=1)
                 for x in range(N_SPLIT)], axis=0).astype(BF16)
            log_d = lax.dot_general(key_side, query_side, (((0,), (0,)), ((), ())),
                                    preferred_element_type=F32)
            for d in range(N_DIR):
                decay = jnp.exp2(log_d[:, d * CHUNK:(d + 1) * CHUNK] + causal_ref[d])
                st_ref[c, d, j] = (qk_t[:, j * CHUNK:(j + 1) * CHUNK] * decay).astype(BF16)
        return carry

    for_each_chunk(chunk_step, 0)

    def scan_step(i, carry):
        cf, cb = carry
        back = n_chunks - 1 - i
        s_ref[i, 0, 0] = cf.astype(BF16)
        s_ref[i, 0, 1] = pltpu.roll(cf, DK, axis=1).astype(BF16)
        s_ref[back, 1, 0] = cb.astype(BF16)
        s_ref[back, 1, 1] = pltpu.roll(cb, DK, axis=1).astype(BF16)
        cf = tab_ref[_DECAY, i, grp][SERIES_ROW0:SERIES_ROW0 + 1] * cf + u_ref[i, :V_AUG]
        cb = tab_ref[_DECAY, back, grp][SERIES_ROW0 + hp:SERIES_ROW0 + hp + 1] * cb + u_ref[back, V_AUG:]
        return cf, cb

    for_each_chunk(scan_step, (c0_ref[0, 0, 0], c0_ref[0, 1, 0]))

    def output_step(c, carry):
        rows = chunk_rows(c)
        qt2 = qt_ref[0, c].astype(F32)
        w_inter8, exp_neg8 = tab_ref[_WINTER, c, grp], tab_ref[_EXPNEG, c, grp]
        no_query = jnp.zeros((DK, CHUNK), BF16)
        for j in range(hp):
            state = jnp.where(low_lanes, s_ref[c, 0, j], s_ref[c, 1, 1 - j])
            q_h = qt2[j * DK:(j + 1) * DK]
            qw = [(q_h * series(w_inter8, d, j)).astype(BF16) for d in range(N_DIR)]
            rhs = jnp.concatenate(
                [jnp.concatenate([st_ref[c, d, j] for d in range(N_DIR)], axis=1),
                 jnp.concatenate([qw[0], no_query], axis=1),
                 jnp.concatenate([no_query, qw[1]], axis=1)], axis=0)
            lhs = jnp.concatenate([values_t(c, j), state], axis=1)
            n_all = jnp.dot(lhs, rhs, preferred_element_type=F32)
            h_t = None
            for d in range(N_DIR):
                num = n_all[:, d * CHUNK:(d + 1) * CHUNK]
                r = 1.0 / jnp.maximum(jnp.abs(num[DV:DV + 1]), series(exp_neg8, d, j))
                part = num[:DV] * r
                h_t = part if h_t is None else h_t + part
            hn_t = h_t * lax.rsqrt(jnp.mean(h_t * h_t, axis=0, keepdims=True) + EPS)
            o_ref[0, rows, j * DV:(j + 1) * DV] = (
                jnp.transpose(hn_t) * so_ref[0, rows, j * DV:(j + 1) * DV].astype(F32)).astype(BF16)
        return carry

    for_each_chunk(output_step, 0)


def mlstm_call(qt, k, vt, so, gates, c0, m0):
    bsz, t, _ = k.shape
    n_chunks = t // CHUNK
    hp = HEADS_PER_STEP
    seq = lambda width: pl.BlockSpec((1, t, width), lambda i, j: (i, 0, j))
    seq_t = lambda rows: pl.BlockSpec((1, n_chunks, rows, CHUNK), lambda i, j: (i, 0, j, 0))
    return pl.pallas_call(
        functools.partial(_mlstm_body, n_chunks=n_chunks),
        grid=(bsz, N_HEAD_GROUPS),
        in_specs=[seq_t(hp * DK), seq(hp * DK), seq_t(hp * DV), seq(hp * DV),
                  pl.BlockSpec((1, n_chunks, N_GATE, CHUNK), lambda i, j: (i, 0, 0, 0)),
                  pl.BlockSpec((1, N_DIR, 1, V_AUG, hp * DK), lambda i, j: (i, 0, j, 0, 0)),
                  pl.BlockSpec((1, N_GATE, 1), lambda i, j: (i, 0, 0))],
        out_specs=seq(hp * DV),
        out_shape=jax.ShapeDtypeStruct((bsz, t, D_MLSTM), BF16),
        scratch_shapes=[pltpu.VMEM((N_TABLES, n_chunks, N_HEAD_GROUPS, GATES_PER_GROUP, CHUNK), F32),
                        pltpu.VMEM((n_chunks, N_DIR * V_AUG, hp * DK), F32),
                        pltpu.VMEM((n_chunks, N_DIR, 2, V_AUG, hp * DK), BF16),
                        pltpu.VMEM((n_chunks, N_DIR, hp, CHUNK, CHUNK), BF16),
                        pltpu.VMEM((N_DIR, CHUNK, CHUNK), F32)],
        compiler_params=pltpu.CompilerParams(dimension_semantics=("arbitrary", "arbitrary"),
                                             vmem_limit_bytes=VMEM_LIMIT_BYTES),
        name="mlstm",
    )(qt, k, vt, so, gates, c0, m0)


def _out_body(x_ref, hs_ref, ya_ref, gb_ref, mod_ref, nw2_ref, fnw_ref, wmo_ref, wo_ref, w1_ref, w2_ref, o_ref):
    dot = functools.partial(jnp.dot, preferred_element_type=F32)
    mod = mod_ref[pl.ds(pl.program_id(0), 1), :]
    yb = dot(hs_ref[0], wmo_ref[...])
    merged = ya_ref[0].astype(F32) + gb_ref[0].astype(F32) * yb
    x1 = x_ref[0] + mod[:, _G1:_SH2] * dot(merged.astype(BF16), wo_ref[...])
    hm = _modulated_norm(x1, nw2_ref[...], mod[:, _SH2:_SC2], mod[:, _SC2:_G2]).astype(BF16)
    a = jnp.maximum(dot(hm, w1_ref[...]), 0.0)
    x2 = x1 + mod[:, _G2:] * dot((a * a).astype(BF16), w2_ref[...])
    y = x2 * lax.rsqrt(jnp.mean(x2 * x2, axis=-1, keepdims=True) + EPS)
    o_ref[0] = y * fnw_ref[...]


def out_call(x, hs, ya, gb, mod, norm2_w, final_norm_w, w_mlstm_out, w_out, w_ff1, w_ff2):
    bsz, t, _ = x.shape
    tile = TOKEN_TILE
    tok = pl.BlockSpec((1, tile, D_MODEL), lambda i, j: (i, j, 0))
    resident = [mod, norm2_w, final_norm_w, w_mlstm_out, w_out, w_ff1, w_ff2]
    return pl.pallas_call(
        _out_body,
        grid=(bsz, t // tile),
        in_specs=[tok, tok, tok, tok] + [_resident(a.shape) for a in resident],
        out_specs=tok,
        out_shape=jax.ShapeDtypeStruct((bsz, t, D_MODEL), F32),
        compiler_params=pltpu.CompilerParams(dimension_semantics=("arbitrary", "arbitrary"),
                                             vmem_limit_bytes=VMEM_LIMIT_BYTES),
        name="merge_out_mlp",
    )(x, hs, ya, gb, *resident)


def _layer(x, ctx, mod, norm1_w, w_in, b_in, conv_w, mlstm_norm_w, w_conv_out, w_mlstm_out,
           w_out, norm2_w, w_ff1, w_ff2, final_norm_w):
    nw1 = norm1_w.reshape(1, D_MODEL)
    w_n, w_t = projection_weights(w_in.T)
    b_n = jnp.concatenate([b_in[_REF_K:_REF_V], b_in[_REF_O:]]).reshape(1, -1)
    b_t = jnp.concatenate([b_in[_REF_V:_REF_IG], b_in[_REF_Q:_REF_O], _group_gates(b_in[_REF_IG:_REF_Q]),
                           jnp.zeros((_T_ROWS - _T_END,), F32)]).reshape(-1, 1)

    c0, m0 = ctx_call(ctx, mod, nw1, w_n, b_n, w_t, b_t)
    k, qt, vt, gates, so, ya, gb = inproj_call(x, mod, nw1, mlstm_norm_w.reshape(1, D_MLSTM), w_n, b_n, w_t, b_t,
                                               conv_w, w_conv_out.astype(BF16))
    hs = mlstm_call(qt, k, vt, so, gates, c0, m0)
    return out_call(x, hs, ya, gb, mod, norm2_w.reshape(1, D_MODEL), final_norm_w.reshape(1, D_MODEL),
                    w_mlstm_out.astype(BF16), w_out.astype(BF16), w_ff1.astype(BF16), w_ff2.astype(BF16))


def kernel(x, c, ctx, c_ctx, w_mod, b_mod, norm1_w, w_in, b_in, conv_w, mlstm_norm_w, w_conv_out,
           w_mlstm_out, w_out, norm2_w, w_ff1, w_ff2, final_norm_w):
    depth = w_mod.shape[0]
    assert depth == 1, "the context stream is only advanced through its mLSTM state (single layer)"
    cvecs = jnp.concatenate([c, c_ctx[None, :]], axis=0)
    mod = adaln_call(cvecs, w_mod[0], b_mod[0])
    return _layer(x, ctx, mod, norm1_w[0], w_in[0], b_in[0], conv_w[0],
                  mlstm_norm_w[0], w_conv_out[0], w_mlstm_out[0], w_out[0], norm2_w[0], w_ff1[0],
                  w_ff2[0], final_norm_w)
```

```python
import functools

import jax
import jax.numpy as jnp
from jax import lax
from jax.experimental import pallas as pl
from jax.experimental.pallas import tpu as pltpu

D_MODEL = 1024
CTX_LEN = 256
GRID_W = 64
N_HEADS = 8
DK = 64
DV = 128
D_MLSTM = N_HEADS * DV
D_QK = N_HEADS * DK
N_DIR = 2
N_GATE = 2 * N_DIR * N_HEADS
EPS = 1e-6

CHUNK = 128
HEADS_PER_STEP = 2
assert HEADS_PER_STEP == 2 and HEADS_PER_STEP * DK == 128, "a head pair shares one 128-lane tile"
N_HEAD_GROUPS = N_HEADS // HEADS_PER_STEP
GATES_PER_GROUP = N_GATE // N_HEAD_GROUPS
SUBLANES = 8
BF16_SUBLANES = 16
V_AUG = DV + BF16_SUBLANES
TOKEN_TILE = 512
ADALN_TILE = 1024
CTX_BATCH = 4

VMEM_LIMIT_BYTES = 56 * 1024 * 1024

_REF_K, _REF_V, _REF_IG, _REF_FG, _REF_Q, _REF_O = 0, 512, 1536, 1552, 1568, 2080
D_IN = 8224
_K0, _O0, _XIN0, _GC0, _GB0, _MA0, _MB0, _G0 = 0, 512, 1536, 2560, 3584, 4608, 5632, 6656
_T_V, _T_Q, _T_GATES, _T_END, _T_ROWS = 0, 1024, 1536, 1568, 2048
_SH1, _SC1, _G1, _SH2, _SC2, _G2 = (i * D_MODEL for i in range(6))

F32 = jnp.float32
BF16 = jnp.bfloat16


def _sigmoid(x):
    return 0.5 * jnp.tanh(0.5 * x) + 0.5


def _log_sigmoid(x):
    return jnp.minimum(x, 0.0) - jnp.log(1.0 + jnp.exp(-jnp.abs(x)))


def _split3(x):
    hi = x.astype(BF16)
    r1 = x - hi.astype(F32)
    mid = r1.astype(BF16)
    lo = (r1 - mid.astype(F32)).astype(BF16)
    return hi, mid, lo


def _cumsum_lanes(x):
    n = x.shape[1]
    r = lax.broadcasted_iota(jnp.int32, (n, n), 0)
    c = lax.broadcasted_iota(jnp.int32, (n, n), 1)
    tri_t = jnp.where(r <= c, 1.0, 0.0).astype(BF16)
    hi, mid, lo = _split3(x)
    dot = functools.partial(jnp.dot, preferred_element_type=F32)
    return dot(hi, tri_t) + dot(mid, tri_t) + dot(lo, tri_t)


def _modulated_norm(x, norm_w, shift, scale):
    y = x * lax.rsqrt(jnp.mean(x * x, axis=-1, keepdims=True) + EPS)
    return y * (norm_w * (1.0 + scale)) + shift


def _resident(shape):
    nd = len(shape)
    return pl.BlockSpec(shape, lambda *_: (0,) * nd, pipeline_mode=pl.Buffered(1))


def _adaln_body(c_ref, w_ref, b_ref, o_ref):
    c = c_ref[...]
    s = c * _sigmoid(c)
    dot = functools.partial(jnp.dot, preferred_element_type=F32)
    w = w_ref[...]
    s_hi, w_hi = s.astype(BF16), w.astype(BF16)
    s_lo, w_lo = (s - s_hi.astype(F32)).astype(BF16), (w - w_hi.astype(F32)).astype(BF16)
    o_ref[...] = (dot(s_hi, w_hi) + dot(s_hi, w_lo) + dot(s_lo, w_hi)) + b_ref[...]


def adaln_call(cvecs, w_mod, b_mod):
    n = cvecs.shape[0]
    n_out = w_mod.shape[1]
    tile = ADALN_TILE
    return pl.pallas_call(
        _adaln_body,
        grid=(n_out // tile,),
        in_specs=[pl.BlockSpec((n, D_MODEL), lambda j: (0, 0)),
                  pl.BlockSpec((D_MODEL, tile), lambda j: (0, j)),
                  pl.BlockSpec((1, tile), lambda j: (0, j))],
        out_specs=pl.BlockSpec((n, tile), lambda j: (0, j)),
        out_shape=jax.ShapeDtypeStruct((n, n_out), F32),
        name="adaln",
    )(cvecs, w_mod, b_mod.reshape(1, n_out))


W_BLOCK = 512


def _transpose_cast_body(w_ref, o_ref):
    o_ref[...] = jnp.transpose(w_ref[...]).astype(BF16)


def _row_cast_body(starts_ref, w_ref, perm_ref, o_ref):
    del starts_ref
    o_ref[...] = w_ref[...].astype(BF16)

    @pl.when(pl.program_id(0) == pl.num_programs(0) - 1)
    def _():
        gates = w_ref[0:N_GATE, :].astype(BF16)
        o_ref[0:N_GATE, :] = jnp.dot(perm_ref[...], gates, preferred_element_type=F32).astype(BF16)


def _group_gates(g):
    lead = g.shape[:-1]
    g = g.reshape(*lead, 2, N_DIR, N_HEAD_GROUPS, HEADS_PER_STEP)
    g = jnp.moveaxis(g, -2, -4)
    return g.reshape(*lead, N_GATE)


def projection_weights(w_in_t):
    n_rest = (D_IN - _REF_O) // W_BLOCK
    blk = W_BLOCK // SUBLANES
    w_n = pl.pallas_call(
        _transpose_cast_body,
        grid=(1 + n_rest,),
        in_specs=[pl.BlockSpec((pl.Element(W_BLOCK), pl.Element(D_MODEL)),
                               lambda i: (SUBLANES * jnp.where(i == 0, _REF_K // SUBLANES,
                                                               _REF_O // SUBLANES + (i - 1) * blk), 0))],
        out_specs=pl.BlockSpec((D_MODEL, W_BLOCK), lambda i: (0, i)),
        out_shape=jax.ShapeDtypeStruct((D_MODEL, (1 + n_rest) * W_BLOCK), BF16),
        name="w_in_normal",
    )(w_in_t)
    starts = jnp.asarray([_REF_V // SUBLANES, _REF_V // SUBLANES + blk, _REF_Q // SUBLANES, _REF_IG // SUBLANES],
                         jnp.int32)
    gate_perm = jnp.eye(N_GATE, dtype=BF16)[_group_gates(jnp.arange(N_GATE))]
    w_t = pl.pallas_call(
        _row_cast_body,
        grid_spec=pltpu.PrefetchScalarGridSpec(
            num_scalar_prefetch=1,
            grid=(_T_ROWS // W_BLOCK,),
            in_specs=[pl.BlockSpec((pl.Element(W_BLOCK), pl.Element(D_MODEL)),
                                   lambda i, starts_ref: (SUBLANES * starts_ref[i], 0)),
                      pl.BlockSpec((N_GATE, N_GATE), lambda i, starts_ref: (0, 0))],
            out_specs=pl.BlockSpec((W_BLOCK, D_MODEL), lambda i, starts_ref: (i, 0))),
        out_shape=jax.ShapeDtypeStruct((_T_ROWS, D_MODEL), BF16),
        name="w_in_transposed",
    )(starts, w_in_t, gate_perm)
    return w_n, w_t


def _ctx_body(ctx_ref, mod_ref, nw_ref, wk_ref, bk_ref, wv_ref, bv_ref, wg_ref, bg_ref, c_ref, m_ref):
    hp = HEADS_PER_STEP
    nb = ctx_ref.shape[0]
    last = mod_ref.shape[0] - 1
    x = ctx_ref[...].reshape(nb * CTX_LEN, D_MODEL)
    hb = _modulated_norm(x, nw_ref[...], mod_ref[last:, _SH1:_SC1], mod_ref[last:, _SC1:_G1]).astype(BF16)
    nt = functools.partial(lax.dot_general, dimension_numbers=(((1,), (1,)), ((), ())),
                           preferred_element_type=F32)
    k_all = ((jnp.dot(hb, wk_ref[...], preferred_element_type=F32) + bk_ref[...]) * (DK ** -0.5)).astype(BF16)
    vt_all = (nt(wv_ref[...], hb) + bv_ref[...]).astype(BF16)
    g_all = nt(wg_ref[...], hb) + bg_ref[...]

    r_in_tile = jnp.bitwise_and(lax.broadcasted_iota(jnp.int32, (N_GATE, 1), 0), GATES_PER_GROUP - 1)
    fwd = r_in_tile < GATES_PER_GROUP // 2 + hp
    ones_rows = jnp.where(lax.broadcasted_iota(jnp.int32, (V_AUG - DV, CTX_LEN), 0) == 0, 1.0, 0.0)
    lane = lax.broadcasted_iota(jnp.int32, (CTX_LEN, hp * DK), 1)
    for bi in range(nb):
        tokens = slice(bi * CTX_LEN, (bi + 1) * CTX_LEN)
        g, vt, k = g_all[:, tokens], vt_all[:, tokens], k_all[tokens]
        lf = _log_sigmoid(g)
        b = _cumsum_lanes(lf)
        ig = pltpu.roll(g, GATES_PER_GROUP // 2, axis=0)
        tot = b[:, CTX_LEN - 1:CTX_LEN]
        g_end = ig + jnp.where(fwd, tot - b, b - lf)
        m = jnp.maximum(tot, jnp.max(g_end, axis=1, keepdims=True))
        m_ref[bi] = m
        wk = jnp.exp(g_end - m)
        for grp in range(N_HEAD_GROUPS):
            k2 = k[:, grp * hp * DK:(grp + 1) * hp * DK]
            zero = jnp.zeros_like(k2)
            k_bd = jnp.concatenate([jnp.where(lane < DK, k2, zero), jnp.where(lane >= DK, k2, zero)], axis=0)
            vf = [jnp.concatenate([vt[(grp * hp + j) * DV:(grp * hp + j + 1) * DV].astype(F32), ones_rows],
                                  axis=0) for j in range(hp)]
            for d in range(N_DIR):
                row0 = grp * GATES_PER_GROUP + GATES_PER_GROUP // 2 + d * hp
                lhs = jnp.concatenate([(vf[j] * wk[row0 + j:row0 + j + 1]).astype(BF16) for j in range(hp)],
                                      axis=1)
                c_ref[bi, d, grp] = jnp.dot(lhs, k_bd, preferred_element_type=F32)


def ctx_call(ctx, mod, norm_w, w_n, b_n, w_t, b_t):
    bsz = ctx.shape[0]
    nb = CTX_BATCH
    const = lambda shape: pl.BlockSpec(shape, lambda i: (0,) * len(shape))
    gate_rows = lambda width: pl.BlockSpec((N_GATE, width), lambda i: (_T_GATES // N_GATE, 0))
    return pl.pallas_call(
        _ctx_body,
        grid=(bsz // nb,),
        in_specs=[pl.BlockSpec((nb, CTX_LEN, D_MODEL), lambda i: (i, 0, 0)),
                  const(mod.shape), const((1, D_MODEL)),
                  const((D_MODEL, D_QK)), const((1, D_QK)),
                  const((D_MLSTM, D_MODEL)), const((D_MLSTM, 1)),
                  gate_rows(D_MODEL), gate_rows(1)],
        out_specs=[pl.BlockSpec((nb, N_DIR, N_HEAD_GROUPS, V_AUG, HEADS_PER_STEP * DK),
                                lambda i: (i, 0, 0, 0, 0)),
                   pl.BlockSpec((nb, N_GATE, 1), lambda i: (i, 0, 0))],
        out_shape=[jax.ShapeDtypeStruct((bsz, N_DIR, N_HEAD_GROUPS, V_AUG, HEADS_PER_STEP * DK), F32),
                   jax.ShapeDtypeStruct((bsz, N_GATE, 1), F32)],
        compiler_params=pltpu.CompilerParams(dimension_semantics=("arbitrary",),
                                             vmem_limit_bytes=VMEM_LIMIT_BYTES),
        name="ctx_state",
    )(ctx, mod, norm_w, w_n, b_n, w_t, b_t, w_t, b_t)


def _inproj_body(x_ref, mod_ref, nw_ref, nwm_ref, w_ref, b_ref, wt_ref, bt_ref, cw_ref, wco_ref,
                 k_ref, qt_ref, vt_ref, gt_ref, so_ref, ya_ref, gb_ref):
    x = x_ref[0]
    mod = mod_ref[pl.ds(pl.program_id(0), 1), :]
    hb = _modulated_norm(x, nw_ref[...], mod[:, _SH1:_SC1], mod[:, _SC1:_G1]).astype(BF16)
    tile = hb.shape[0]

    def seg(a, b):
        return jnp.dot(hb, w_ref[:, a:b], preferred_element_type=F32) + b_ref[:, a:b]

    k_ref[0] = (seg(_K0, _O0) * (DK ** -0.5)).astype(BF16)
    so_ref[0] = (_sigmoid(seg(_O0, _XIN0)) * nwm_ref[...]).astype(BF16)

    zt = lax.dot_general(wt_ref[:_T_END, :], hb, (((1,), (1,)), ((), ())),
                         preferred_element_type=F32) + bt_ref[:_T_END, :]
    for i in range(tile // CHUNK):
        lanes = slice(i * CHUNK, (i + 1) * CHUNK)
        vt_ref[0, i] = zt[_T_V:_T_Q, lanes].astype(BF16)
        qt_ref[0, i] = zt[_T_Q:_T_GATES, lanes].astype(BF16)
        gt_ref[0, i] = zt[_T_GATES:_T_END, lanes]

    u = seg(_XIN0, _GC0) * seg(_GC0, _GB0)
    col = jnp.bitwise_and(lax.broadcasted_iota(jnp.int32, (tile, 1), 0), GRID_W - 1)
    u_prev = jnp.where(col != 0, pltpu.roll(u, 1, axis=0), 0.0)
    u_next = jnp.where(col != GRID_W - 1, pltpu.roll(u, tile - 1, axis=0), 0.0)
    a = cw_ref[0:1, :] * u_prev + cw_ref[1:2, :] * u + cw_ref[2:3, :] * u_next
    ya = jnp.dot((seg(_GB0, _MA0) * a).astype(BF16), wco_ref[...], preferred_element_type=F32)
    ya_ref[0] = (_sigmoid(seg(_MA0, _MB0)) * ya).astype(BF16)
    gb_ref[0] = _sigmoid(seg(_MB0, _G0)).astype(BF16)


def inproj_call(x, mod, norm_w, mlstm_norm_w, w_n, b_n, w_t, b_t, conv_w, w_conv_out):
    bsz, t, _ = x.shape
    tile = TOKEN_TILE
    tok = lambda width: pl.BlockSpec((1, tile, width), lambda i, j: (i, j, 0))
    tok_t = lambda rows: pl.BlockSpec((1, tile // CHUNK, rows, CHUNK), lambda i, j: (i, j, 0, 0))
    seq = lambda width: jax.ShapeDtypeStruct((bsz, t, width), BF16)
    seq_t = lambda rows, dt: jax.ShapeDtypeStruct((bsz, t // CHUNK, rows, CHUNK), dt)
    resident = [mod, norm_w, mlstm_norm_w, w_n, b_n, w_t, b_t, conv_w, w_conv_out]
    return pl.pallas_call(
        _inproj_body,
        grid=(bsz, t // tile),
        in_specs=[tok(D_MODEL)] + [_resident(a.shape) for a in resident],
        out_specs=[tok(D_QK), tok_t(D_QK), tok_t(D_MLSTM), tok_t(N_GATE), tok(D_MLSTM),
                   tok(D_MODEL), tok(D_MODEL)],
        out_shape=[seq(D_QK), seq_t(D_QK, BF16), seq_t(D_MLSTM, BF16), seq_t(N_GATE, F32), seq(D_MLSTM),
                   seq(D_MODEL), seq(D_MODEL)],
        compiler_params=pltpu.CompilerParams(dimension_semantics=("arbitrary", "arbitrary"),
                                             vmem_limit_bytes=VMEM_LIMIT_BYTES),
        name="inproj_conv",
    )(x, *resident)


SERIES_ROW0 = GATES_PER_GROUP // 2
N_SERIES = N_DIR * HEADS_PER_STEP
N_SPLIT = 3
LOG2E = 1.4426950408889634
_WK, _DECAY, _WINTER, _EXPNEG, _KEYS = 0, 1, 2, 3, 4
_COLM = _KEYS + N_SPLIT
N_TABLES = _COLM + N_SPLIT * N_SERIES


def _running_max_rows(x, reverse):
    n = x.shape[0]
    row = lax.broadcasted_iota(jnp.int32, x.shape, 0)
    shift = 1
    while shift < n:
        if reverse:
            moved = jnp.where(row < n - shift, pltpu.roll(x, n - shift, axis=0), -jnp.inf)
        else:
            moved = jnp.where(row >= shift, pltpu.roll(x, shift, axis=0), -jnp.inf)
        x = jnp.maximum(x, moved)
        shift *= 2
    return x


def _gate_tables(g_ref, m0_ref, tab_ref, n_chunks):
    step_rows = N_HEAD_GROUPS * GATES_PER_GROUP
    rows = n_chunks * step_rows
    g = g_ref[0].reshape(rows, CHUNK)
    lf = _log_sigmoid(g)
    b = _cumsum_lanes(lf)
    ig = pltpu.roll(g, SERIES_ROW0, axis=0)
    in_tile = lambda idx: jnp.bitwise_and(idx, GATES_PER_GROUP - 1)
    r_in_tile = in_tile(lax.broadcasted_iota(jnp.int32, (rows, 1), 0))
    fwd = r_in_tile < SERIES_ROW0 + HEADS_PER_STEP
    e = b - lf
    tot = b[:, CHUNK - 1:CHUNK]
    col = jnp.where(fwd, b, -e)
    key = ig + jnp.where(fwd, -b, e)
    g_end = jnp.where(fwd, tot + key, key)
    g_max = jnp.max(g_end, axis=1, keepdims=True)

    fwd_step = fwd[:step_rows]
    tile = lambda a, c: a[c * step_rows:(c + 1) * step_rows]
    m = m0_ref[0]
    m_in_steps, m_out_steps = [], []
    for i in range(n_chunks):
        back = n_chunks - 1 - i
        m_in_steps.append(m)
        m = jnp.maximum(jnp.where(fwd_step, tile(tot, i), tile(tot, back)) + m,
                        jnp.where(fwd_step, tile(g_max, i), tile(g_max, back)))
        m_out_steps.append(m)
    by_chunk = lambda steps: jnp.concatenate(
        [jnp.where(fwd_step, steps[c], steps[n_chunks - 1 - c]) for c in range(n_chunks)], axis=0)
    m_in, m_out = by_chunk(m_in_steps), by_chunk(m_out_steps)

    decay = jnp.broadcast_to(jnp.exp(tot + m_in - m_out), (rows, CHUNK))
    lane = lax.broadcasted_iota(jnp.int32, (rows, CHUNK), 1)
    inter = jnp.where(fwd, col, tot + col) + m_in
    key_t = jnp.transpose(key)
    fwd_lane = in_tile(lax.broadcasted_iota(jnp.int32, (1, rows), 1)) < SERIES_ROW0 + HEADS_PER_STEP
    key_max = jnp.transpose(jnp.where(fwd_lane, _running_max_rows(key_t, reverse=False),
                                      _running_max_rows(key_t, reverse=True)))
    m_t = jnp.maximum(col + key_max, inter)
    tables = {_WK: jnp.exp(g_end - m_out),
              _DECAY: jnp.where(lane < DK, decay, pltpu.roll(decay, rows - 1, axis=0)),
              _WINTER: jnp.exp(inter - m_t), _EXPNEG: jnp.exp(-m_t)}
    for x, part in enumerate(_split3(key * LOG2E)):
        tables[_KEYS + x] = part.astype(F32)
    for x, part in enumerate(_split3((col - m_t) * LOG2E)):
        for sidx in range(N_SERIES):
            tables[_COLM + x * N_SERIES + sidx] = jnp.where(r_in_tile == SERIES_ROW0 + sidx,
                                                            part.astype(F32), 0.0)
    for idx, a in tables.items():
        tab_ref[idx] = a.reshape(n_chunks, N_HEAD_GROUPS, GATES_PER_GROUP, CHUNK)


def _mlstm_body(qt_ref, k_ref, vt_ref, so_ref, g_ref, c0_ref, m0_ref, o_ref,
                tab_ref, u_ref, s_ref, st_ref, causal_ref, *, n_chunks):
    hp = HEADS_PER_STEP
    ones_rows = jnp.where(lax.broadcasted_iota(jnp.int32, (V_AUG - DV, CHUNK), 0) == 0,
                          1.0, 0.0).astype(BF16)
    low_lanes = lax.broadcasted_iota(jnp.int32, (1, hp * DK), 1) < DK
    series = lambda tab8, d, j: tab8[SERIES_ROW0 + d * hp + j:SERIES_ROW0 + d * hp + j + 1]
    split_rows = N_SPLIT * GATES_PER_GROUP
    ones_split = jnp.ones((split_rows, CHUNK), F32)
    pick_r = jnp.bitwise_and(lax.broadcasted_iota(jnp.int32, (split_rows, N_DIR * CHUNK), 0),
                             GATES_PER_GROUP - 1)
    pick_d = jnp.where(lax.broadcasted_iota(jnp.int32, (split_rows, N_DIR * CHUNK), 1) >= CHUNK, 1, 0)
    pick_series = [jnp.where(pick_r == SERIES_ROW0 + pick_d * hp + j, 1.0, 0.0) for j in range(hp)]

    def chunk_rows(c):
        return slice(c * CHUNK, (c + 1) * CHUNK)

    def for_each_chunk(body, carry):
        for c in range(n_chunks):
            carry = body(c, carry)
        return carry

    def values_t(c, j):
        return jnp.concatenate([vt_ref[0, c, j * DV:(j + 1) * DV, :], ones_rows], axis=0)

    grp = pl.program_id(1)

    @pl.when(grp == 0)
    def _():
        _gate_tables(g_ref, m0_ref, tab_ref, n_chunks)
        s_i = lax.broadcasted_iota(jnp.int32, (CHUNK, CHUNK), 0)
        t_i = lax.broadcasted_iota(jnp.int32, (CHUNK, CHUNK), 1)
        causal_ref[0] = jnp.where(s_i <= t_i, 0.0, -jnp.inf)
        causal_ref[1] = jnp.where(s_i >= t_i, 0.0, -jnp.inf)

    def block_diag_q(qt2):
        zero = jnp.zeros((DK, CHUNK), qt2.dtype)
        return jnp.concatenate([jnp.concatenate([qt2[:DK], zero], axis=0),
                                jnp.concatenate([zero, qt2[DK:]], axis=0)], axis=1)

    def chunk_step(c, carry):
        k2 = k_ref[0, chunk_rows(c), :]
        zero = jnp.zeros_like(k2)
        k_bd = jnp.concatenate([jnp.where(low_lanes, k2, zero),
                                jnp.where(low_lanes, zero, k2)], axis=0)
        wk8 = tab_ref[_WK, c, grp]
        vb = [values_t(c, j) for j in range(hp)]

        def key_weights(d, j):
            tile = jnp.broadcast_to(series(wk8, d, j), (BF16_SUBLANES, CHUNK)).astype(BF16)
            return jnp.concatenate([tile] * (V_AUG // BF16_SUBLANES), axis=0)

        lhs = jnp.concatenate(
            [jnp.concatenate([vb[j] * key_weights(d, j) for j in range(hp)], axis=1)
             for d in range(N_DIR)], axis=0)
        u_ref[c] = jnp.dot(lhs, k_bd, preferred_element_type=F32)

        qk_t = jnp.dot(k2, block_diag_q(qt_ref[0, c]), preferred_element_type=F32)
        key_side = jnp.concatenate([tab_ref[_KEYS + x, c, grp] for x in range(N_SPLIT)] + [ones_split],
                                   axis=0).astype(BF16)
        for j in range(hp):
            query_side = jnp.concatenate(
                [pick_series[j]] +
                [jnp.concatenate([tab_ref[_COLM + x * N_SERIES + d * hp + j, c, grp] for d in range(N_DIR)],
                                 axis=1)
                 for x in range(N_SPLIT)], axis=0).astype(BF16)
            log_d = lax.dot_general(key_side, query_side, (((0,), (0,)), ((), ())),
                                    preferred_element_type=F32)
            for d in range(N_DIR):
                decay = jnp.exp2(log_d[:, d * CHUNK:(d + 1) * CHUNK] + causal_ref[d])
                st_ref[c, d, j] = (qk_t[:, j * CHUNK:(j + 1) * CHUNK] * decay).astype(BF16)
        return carry

    for_each_chunk(chunk_step, 0)

    def scan_step(i, carry):
        cf, cb = carry
        back = n_chunks - 1 - i
        s_ref[i, 0, 0] = cf.astype(BF16)
        s_ref[i, 0, 1] = pltpu.roll(cf, DK, axis=1).astype(BF16)
        s_ref[back, 1, 0] = cb.astype(BF16)
        s_ref[back, 1, 1] = pltpu.roll(cb, DK, axis=1).astype(BF16)
        cf = tab_ref[_DECAY, i, grp][SERIES_ROW0:SERIES_ROW0 + 1] * cf + u_ref[i, :V_AUG]
        cb = tab_ref[_DECAY, back, grp][SERIES_ROW0 + hp:SERIES_ROW0 + hp + 1] * cb + u_ref[back, V_AUG:]
        return cf, cb

    for_each_chunk(scan_step, (c0_ref[0, 0, 0], c0_ref[0, 1, 0]))

    def output_step(c, carry):
        rows = chunk_rows(c)
        qt2 = qt_ref[0, c].astype(F32)
        w_inter8, exp_neg8 = tab_ref[_WINTER, c, grp], tab_ref[_EXPNEG, c, grp]
        no_query = jnp.zeros((DK, CHUNK), BF16)
        for j in range(hp):
            state = jnp.where(low_lanes, s_ref[c, 0, j], s_ref[c, 1, 1 - j])
            q_h = qt2[j * DK:(j + 1) * DK]
            qw = [(q_h * series(w_inter8, d, j)).astype(BF16) for d in range(N_DIR)]
            rhs = jnp.concatenate(
                [jnp.concatenate([st_ref[c, d, j] for d in range(N_DIR)], axis=1),
                 jnp.concatenate([qw[0], no_query], axis=1),
                 jnp.concatenate([no_query, qw[1]], axis=1)], axis=0)
            lhs = jnp.concatenate([values_t(c, j), state], axis=1)
            n_all = jnp.dot(lhs, rhs, preferred_element_type=F32)
            h_t = None
            for d in range(N_DIR):
                num = n_all[:, d * CHUNK:(d + 1) * CHUNK]
                r = 1.0 / jnp.maximum(jnp.abs(num[DV:DV + 1]), series(exp_neg8, d, j))
                part = num[:DV] * r
                h_t = part if h_t is None else h_t + part
            hn_t = h_t * lax.rsqrt(jnp.mean(h_t * h_t, axis=0, keepdims=True) + EPS)
            o_ref[0, rows, j * DV:(j + 1) * DV] = (
                jnp.transpose(hn_t) * so_ref[0, rows, j * DV:(j + 1) * DV].astype(F32)).astype(BF16)
        return carry

    for_each_chunk(output_step, 0)


def mlstm_call(qt, k, vt, so, gates, c0, m0):
    bsz, t, _ = k.shape
    n_chunks = t // CHUNK
    hp = HEADS_PER_STEP
    seq = lambda width: pl.BlockSpec((1, t, width), lambda i, j: (i, 0, j))
    seq_t = lambda rows: pl.BlockSpec((1, n_chunks, rows, CHUNK), lambda i, j: (i, 0, j, 0))
    return pl.pallas_call(
        functools.partial(_mlstm_body, n_chunks=n_chunks),
        grid=(bsz, N_HEAD_GROUPS),
        in_specs=[seq_t(hp * DK), seq(hp * DK), seq_t(hp * DV), seq(hp * DV),
                  pl.BlockSpec((1, n_chunks, N_GATE, CHUNK), lambda i, j: (i, 0, 0, 0)),
                  pl.BlockSpec((1, N_DIR, 1, V_AUG, hp * DK), lambda i, j: (i, 0, j, 0, 0)),
                  pl.BlockSpec((1, N_GATE, 1), lambda i, j: (i, 0, 0))],
        out_specs=seq(hp * DV),
        out_shape=jax.ShapeDtypeStruct((bsz, t, D_MLSTM), BF16),
        scratch_shapes=[pltpu.VMEM((N_TABLES, n_chunks, N_HEAD_GROUPS, GATES_PER_GROUP, CHUNK), F32),
                        pltpu.VMEM((n_chunks, N_DIR * V_AUG, hp * DK), F32),
                        pltpu.VMEM((n_chunks, N_DIR, 2, V_AUG, hp * DK), BF16),
                        pltpu.VMEM((n_chunks, N_DIR, hp, CHUNK, CHUNK), BF16),
                        pltpu.VMEM((N_DIR, CHUNK, CHUNK), F32)],
        compiler_params=pltpu.CompilerParams(dimension_semantics=("arbitrary", "arbitrary"),
                                             vmem_limit_bytes=VMEM_LIMIT_BYTES),
        name="mlstm",
    )(qt, k, vt, so, gates, c0, m0)


def _out_body(x_ref, hs_ref, ya_ref, gb_ref, mod_ref, nw2_ref, fnw_ref, wmo_ref, wo_ref, w1_ref, w2_ref, o_ref):
    dot = functools.partial(jnp.dot, preferred_element_type=F32)
    mod = mod_ref[pl.ds(pl.program_id(0), 1), :]
    yb = dot(hs_ref[0], wmo_ref[...])
    merged = ya_ref[0].astype(F32) + gb_ref[0].astype(F32) * yb
    x1 = x_ref[0] + mod[:, _G1:_SH2] * dot(merged.astype(BF16), wo_ref[...])
    hm = _modulated_norm(x1, nw2_ref[...], mod[:, _SH2:_SC2], mod[:, _SC2:_G2]).astype(BF16)
    a = jnp.maximum(dot(hm, w1_ref[...]), 0.0)
    x2 = x1 + mod[:, _G2:] * dot((a * a).astype(BF16), w2_ref[...])
    y = x2 * lax.rsqrt(jnp.mean(x2 * x2, axis=-1, keepdims=True) + EPS)
    o_ref[0] = y * fnw_ref[...]


def out_call(x, hs, ya, gb, mod, norm2_w, final_norm_w, w_mlstm_out, w_out, w_ff1, w_ff2):
    bsz, t, _ = x.shape
    tile = TOKEN_TILE
    tok = pl.BlockSpec((1, tile, D_MODEL), lambda i, j: (i, j, 0))
    resident = [mod, norm2_w, final_norm_w, w_mlstm_out, w_out, w_ff1, w_ff2]
    return pl.pallas_call(
        _out_body,
        grid=(bsz, t // tile),
        in_specs=[tok, tok, tok, tok] + [_resident(a.shape) for a in resident],
        out_specs=tok,
        out_shape=jax.ShapeDtypeStruct((bsz, t, D_MODEL), F32),
        compiler_params=pltpu.CompilerParams(dimension_semantics=("arbitrary", "arbitrary"),
                                             vmem_limit_bytes=VMEM_LIMIT_BYTES),
        name="merge_out_mlp",
    )(x, hs, ya, gb, *resident)


def _layer(x, ctx, mod, norm1_w, w_in, b_in, conv_w, mlstm_norm_w, w_conv_out, w_mlstm_out,
           w_out, norm2_w, w_ff1, w_ff2, final_norm_w):
    nw1 = norm1_w.reshape(1, D_MODEL)
    w_n, w_t = projection_weights(w_in.T)
    b_n = jnp.concatenate([b_in[_REF_K:_REF_V], b_in[_REF_O:]]).reshape(1, -1)
    b_t = jnp.concatenate([b_in[_REF_V:_REF_IG], b_in[_REF_Q:_REF_O], _group_gates(b_in[_REF_IG:_REF_Q]),
                           jnp.zeros((_T_ROWS - _T_END,), F32)]).reshape(-1, 1)

    c0, m0 = ctx_call(ctx, mod, nw1, w_n, b_n, w_t, b_t)
    k, qt, vt, gates, so, ya, gb = inproj_call(x, mod, nw1, mlstm_norm_w.reshape(1, D_MLSTM), w_n, b_n, w_t, b_t,
                                               conv_w, w_conv_out.astype(BF16))
    hs = mlstm_call(qt, k, vt, so, gates, c0, m0)
    return out_call(x, hs, ya, gb, mod, norm2_w.reshape(1, D_MODEL), final_norm_w.reshape(1, D_MODEL),
                    w_mlstm_out.astype(BF16), w_out.astype(BF16), w_ff1.astype(BF16), w_ff2.astype(BF16))


def kernel(x, c, ctx, c_ctx, w_mod, b_mod, norm1_w, w_in, b_in, conv_w, mlstm_norm_w, w_conv_out,
           w_mlstm_out, w_out, norm2_w, w_ff1, w_ff2, final_norm_w):
    depth = w_mod.shape[0]
    assert depth == 1, "the context stream is only advanced through its mLSTM state (single layer)"
    cvecs = jnp.concatenate([c, c_ctx[None, :]], axis=0)
    mod = adaln_call(cvecs, w_mod[0], b_mod[0])
    return _layer(x, ctx, mod, norm1_w[0], w_in[0], b_in[0], conv_w[0],
                  mlstm_norm_w[0], w_conv_out[0], w_mlstm_out[0], w_out[0], norm2_w[0], w_ff1[0],
                  w_ff2[0], final_norm_w)
```

```python
import functools

import jax
import jax.numpy as jnp
from jax import lax
from jax.experimental import pallas as pl
from jax.experimental.pallas import tpu as pltpu

D_MODEL = 1024
CTX_LEN = 256
GRID_W = 64
N_HEADS = 8
DK = 64
DV = 128
D_MLSTM = N_HEADS * DV
D_QK = N_HEADS * DK
N_DIR = 2
N_GATE = 2 * N_DIR * N_HEADS
EPS = 1e-6

CHUNK = 128
HEADS_PER_STEP = 2
assert HEADS_PER_STEP == 2 and HEADS_PER_STEP * DK == 128, "a head pair shares one 128-lane tile"
N_HEAD_GROUPS = N_HEADS // HEADS_PER_STEP
GATES_PER_GROUP = N_GATE // N_HEAD_GROUPS
SUBLANES = 8
BF16_SUBLANES = 16
V_AUG = BF16_SUBLANES + DV
TOKEN_TILE = 512
ADALN_TILE = 1024
CTX_BATCH = 4

VMEM_LIMIT_BYTES = 56 * 1024 * 1024

_REF_K, _REF_V, _REF_IG, _REF_FG, _REF_Q, _REF_O = 0, 512, 1536, 1552, 1568, 2080
D_IN = 8224
_K0, _O0, _XIN0, _GC0, _GB0, _MA0, _MB0, _G0 = 0, 512, 1536, 2560, 3584, 4608, 5632, 6656
_T_V, _T_Q, _T_GATES, _T_END, _T_ROWS = 0, 1024, 1536, 1568, 2048
_SH1, _SC1, _G1, _SH2, _SC2, _G2 = (i * D_MODEL for i in range(6))

F32 = jnp.float32
BF16 = jnp.bfloat16


def _sigmoid(x):
    return 0.5 * jnp.tanh(0.5 * x) + 0.5


def _log_sigmoid(x):
    return jnp.minimum(x, 0.0) - jnp.log(1.0 + jnp.exp(-jnp.abs(x)))


def _split3(x):
    hi = x.astype(BF16)
    r1 = x - hi.astype(F32)
    mid = r1.astype(BF16)
    lo = (r1 - mid.astype(F32)).astype(BF16)
    return hi, mid, lo


def _cumsum_lanes(x):
    n = x.shape[1]
    r = lax.broadcasted_iota(jnp.int32, (n, n), 0)
    c = lax.broadcasted_iota(jnp.int32, (n, n), 1)
    tri_t = jnp.where(r <= c, 1.0, 0.0).astype(BF16)
    hi, mid, lo = _split3(x)
    dot = functools.partial(jnp.dot, preferred_element_type=F32)
    return dot(hi, tri_t) + dot(mid, tri_t) + dot(lo, tri_t)


def _modulated_norm(x, norm_w, shift, scale):
    y = x * lax.rsqrt(jnp.mean(x * x, axis=-1, keepdims=True) + EPS)
    return y * (norm_w * (1.0 + scale)) + shift


def _resident(shape):
    nd = len(shape)
    return pl.BlockSpec(shape, lambda *_: (0,) * nd, pipeline_mode=pl.Buffered(1))


def _adaln_body(c_ref, w_ref, b_ref, o_ref):
    c = c_ref[...]
    s = c * _sigmoid(c)
    dot = functools.partial(jnp.dot, preferred_element_type=F32)
    w = w_ref[...]
    s_hi, w_hi = s.astype(BF16), w.astype(BF16)
    s_lo, w_lo = (s - s_hi.astype(F32)).astype(BF16), (w - w_hi.astype(F32)).astype(BF16)
    o_ref[...] = (dot(s_hi, w_hi) + dot(s_hi, w_lo) + dot(s_lo, w_hi)) + b_ref[...]


def adaln_call(cvecs, w_mod, b_mod):
    n = cvecs.shape[0]
    n_out = w_mod.shape[1]
    tile = ADALN_TILE
    return pl.pallas_call(
        _adaln_body,
        grid=(n_out // tile,),
        in_specs=[pl.BlockSpec((n, D_MODEL), lambda j: (0, 0)),
                  pl.BlockSpec((D_MODEL, tile), lambda j: (0, j)),
                  pl.BlockSpec((1, tile), lambda j: (0, j))],
        out_specs=pl.BlockSpec((n, tile), lambda j: (0, j)),
        out_shape=jax.ShapeDtypeStruct((n, n_out), F32),
        name="adaln",
    )(cvecs, w_mod, b_mod.reshape(1, n_out))


W_BLOCK = 512


def _transpose_cast_body(w_ref, o_ref):
    o_ref[...] = jnp.transpose(w_ref[...]).astype(BF16)


def _row_cast_body(starts_ref, w_ref, perm_ref, o_ref):
    del starts_ref
    o_ref[...] = w_ref[...].astype(BF16)

    @pl.when(pl.program_id(0) == pl.num_programs(0) - 1)
    def _():
        gates = w_ref[0:N_GATE, :].astype(BF16)
        o_ref[0:N_GATE, :] = jnp.dot(perm_ref[...], gates, preferred_element_type=F32).astype(BF16)


def _group_gates(g):
    lead = g.shape[:-1]
    g = g.reshape(*lead, 2, N_DIR, N_HEAD_GROUPS, HEADS_PER_STEP)
    g = jnp.moveaxis(g, -2, -4)
    return g.reshape(*lead, N_GATE)


def projection_weights(w_in_t):
    n_rest = (D_IN - _REF_O) // W_BLOCK
    blk = W_BLOCK // SUBLANES
    w_n = pl.pallas_call(
        _transpose_cast_body,
        grid=(1 + n_rest,),
        in_specs=[pl.BlockSpec((pl.Element(W_BLOCK), pl.Element(D_MODEL)),
                               lambda i: (SUBLANES * jnp.where(i == 0, _REF_K // SUBLANES,
                                                               _REF_O // SUBLANES + (i - 1) * blk), 0))],
        out_specs=pl.BlockSpec((D_MODEL, W_BLOCK), lambda i: (0, i)),
        out_shape=jax.ShapeDtypeStruct((D_MODEL, (1 + n_rest) * W_BLOCK), BF16),
        name="w_in_normal",
    )(w_in_t)
    starts = jnp.asarray([_REF_V // SUBLANES, _REF_V // SUBLANES + blk, _REF_Q // SUBLANES, _REF_IG // SUBLANES],
                         jnp.int32)
    gate_perm = jnp.eye(N_GATE, dtype=BF16)[_group_gates(jnp.arange(N_GATE))]
    w_t = pl.pallas_call(
        _row_cast_body,
        grid_spec=pltpu.PrefetchScalarGridSpec(
            num_scalar_prefetch=1,
            grid=(_T_ROWS // W_BLOCK,),
            in_specs=[pl.BlockSpec((pl.Element(W_BLOCK), pl.Element(D_MODEL)),
                                   lambda i, starts_ref: (SUBLANES * starts_ref[i], 0)),
                      pl.BlockSpec((N_GATE, N_GATE), lambda i, starts_ref: (0, 0))],
            out_specs=pl.BlockSpec((W_BLOCK, D_MODEL), lambda i, starts_ref: (i, 0))),
        out_shape=jax.ShapeDtypeStruct((_T_ROWS, D_MODEL), BF16),
        name="w_in_transposed",
    )(starts, w_in_t, gate_perm)
    return w_n, w_t


def _ctx_body(ctx_ref, mod_ref, nw_ref, wk_ref, bk_ref, wv_ref, bv_ref, wg_ref, bg_ref, c_ref, m_ref):
    hp = HEADS_PER_STEP
    nb = ctx_ref.shape[0]
    last = mod_ref.shape[0] - 1
    x = ctx_ref[...].reshape(nb * CTX_LEN, D_MODEL)
    hb = _modulated_norm(x, nw_ref[...], mod_ref[last:, _SH1:_SC1], mod_ref[last:, _SC1:_G1]).astype(BF16)
    nt = functools.partial(lax.dot_general, dimension_numbers=(((1,), (1,)), ((), ())),
                           preferred_element_type=F32)
    k_all = ((jnp.dot(hb, wk_ref[...], preferred_element_type=F32) + bk_ref[...]) * (DK ** -0.5)).astype(BF16)
    vt_all = (nt(wv_ref[...], hb) + bv_ref[...]).astype(BF16)
    g_all = nt(wg_ref[...], hb) + bg_ref[...]

    r_in_tile = jnp.bitwise_and(lax.broadcasted_iota(jnp.int32, (N_GATE, 1), 0), GATES_PER_GROUP - 1)
    fwd = r_in_tile < GATES_PER_GROUP // 2 + hp
    ones_rows = jnp.where(lax.broadcasted_iota(jnp.int32, (V_AUG - DV, CTX_LEN), 0) == 0, 1.0, 0.0)
    lane = lax.broadcasted_iota(jnp.int32, (CTX_LEN, hp * DK), 1)
    for bi in range(nb):
        tokens = slice(bi * CTX_LEN, (bi + 1) * CTX_LEN)
        g, vt, k = g_all[:, tokens], vt_all[:, tokens], k_all[tokens]
        lf = _log_sigmoid(g)
        b = _cumsum_lanes(lf)
        ig = pltpu.roll(g, GATES_PER_GROUP // 2, axis=0)
        tot = b[:, CTX_LEN - 1:CTX_LEN]
        g_end = ig + jnp.where(fwd, tot - b, b - lf)
        m = jnp.maximum(tot, jnp.max(g_end, axis=1, keepdims=True))
        m_ref[bi] = m
        wk = jnp.exp(g_end - m)
        for grp in range(N_HEAD_GROUPS):
            k2 = k[:, grp * hp * DK:(grp + 1) * hp * DK]
            zero = jnp.zeros_like(k2)
            k_bd = jnp.concatenate([jnp.where(lane < DK, k2, zero), jnp.where(lane >= DK, k2, zero)], axis=0)
            vf = [jnp.concatenate([ones_rows, vt[(grp * hp + j) * DV:(grp * hp + j + 1) * DV].astype(F32)],
                                  axis=0) for j in range(hp)]
            for d in range(N_DIR):
                row0 = grp * GATES_PER_GROUP + GATES_PER_GROUP // 2 + d * hp
                lhs = jnp.concatenate([(vf[j] * wk[row0 + j:row0 + j + 1]).astype(BF16) for j in range(hp)],
                                      axis=1)
                c_ref[bi, d, grp] = jnp.dot(lhs, k_bd, preferred_element_type=F32)


def ctx_call(ctx, mod, norm_w, w_n, b_n, w_t, b_t):
    bsz = ctx.shape[0]
    nb = CTX_BATCH
    const = lambda shape: pl.BlockSpec(shape, lambda i: (0,) * len(shape))
    gate_rows = lambda width: pl.BlockSpec((N_GATE, width), lambda i: (_T_GATES // N_GATE, 0))
    return pl.pallas_call(
        _ctx_body,
        grid=(bsz // nb,),
        in_specs=[pl.BlockSpec((nb, CTX_LEN, D_MODEL), lambda i: (i, 0, 0)),
                  const(mod.shape), const((1, D_MODEL)),
                  const((D_MODEL, D_QK)), const((1, D_QK)),
                  const((D_MLSTM, D_MODEL)), const((D_MLSTM, 1)),
                  gate_rows(D_MODEL), gate_rows(1)],
        out_specs=[pl.BlockSpec((nb, N_DIR, N_HEAD_GROUPS, V_AUG, HEADS_PER_STEP * DK),
                                lambda i: (i, 0, 0, 0, 0)),
                   pl.BlockSpec((nb, N_GATE, 1), lambda i: (i, 0, 0))],
        out_shape=[jax.ShapeDtypeStruct((bsz, N_DIR, N_HEAD_GROUPS, V_AUG, HEADS_PER_STEP * DK), F32),
                   jax.ShapeDtypeStruct((bsz, N_GATE, 1), F32)],
        compiler_params=pltpu.CompilerParams(dimension_semantics=("arbitrary",),
                                             vmem_limit_bytes=VMEM_LIMIT_BYTES),
        name="ctx_state",
    )(ctx, mod, norm_w, w_n, b_n, w_t, b_t, w_t, b_t)


def _inproj_body(x_ref, mod_ref, nw_ref, nwm_ref, w_ref, b_ref, wt_ref, bt_ref, cw_ref, wco_ref,
                 k_ref, qt_ref, vt_ref, gt_ref, so_ref, ya_ref, gb_ref):
    x = x_ref[0]
    mod = mod_ref[pl.ds(pl.program_id(0), 1), :]
    hb = _modulated_norm(x, nw_ref[...], mod[:, _SH1:_SC1], mod[:, _SC1:_G1]).astype(BF16)
    tile = hb.shape[0]

    def seg(a, b):
        return jnp.dot(hb, w_ref[:, a:b], preferred_element_type=F32) + b_ref[:, a:b]

    k_ref[0] = (seg(_K0, _O0) * (DK ** -0.5)).astype(BF16)
    so_ref[0] = (_sigmoid(seg(_O0, _XIN0)) * nwm_ref[...]).astype(BF16)

    zt = lax.dot_general(wt_ref[:_T_END, :], hb, (((1,), (1,)), ((), ())),
                         preferred_element_type=F32) + bt_ref[:_T_END, :]
    for i in range(tile // CHUNK):
        lanes = slice(i * CHUNK, (i + 1) * CHUNK)
        vt_ref[0, i] = zt[_T_V:_T_Q, lanes].astype(BF16)
        qt_ref[0, i] = zt[_T_Q:_T_GATES, lanes].astype(BF16)
        gt_ref[0, i] = zt[_T_GATES:_T_END, lanes]

    u = seg(_XIN0, _GC0) * seg(_GC0, _GB0)
    col = jnp.bitwise_and(lax.broadcasted_iota(jnp.int32, (tile, 1), 0), GRID_W - 1)
    u_prev = jnp.where(col != 0, pltpu.roll(u, 1, axis=0), 0.0)
    u_next = jnp.where(col != GRID_W - 1, pltpu.roll(u, tile - 1, axis=0), 0.0)
    a = cw_ref[0:1, :] * u_prev + cw_ref[1:2, :] * u + cw_ref[2:3, :] * u_next
    ya = jnp.dot((seg(_GB0, _MA0) * a).astype(BF16), wco_ref[...], preferred_element_type=F32)
    ya_ref[0] = (_sigmoid(seg(_MA0, _MB0)) * ya).astype(BF16)
    gb_ref[0] = _sigmoid(seg(_MB0, _G0)).astype(BF16)


def inproj_call(x, mod, norm_w, mlstm_norm_w, w_n, b_n, w_t, b_t, conv_w, w_conv_out):
    bsz, t, _ = x.shape
    tile = TOKEN_TILE
    tok = lambda width: pl.BlockSpec((1, tile, width), lambda i, j: (i, j, 0))
    tok_t = lambda rows: pl.BlockSpec((1, tile // CHUNK, rows, CHUNK), lambda i, j: (i, j, 0, 0))
    seq = lambda width: jax.ShapeDtypeStruct((bsz, t, width), BF16)
    seq_t = lambda rows, dt: jax.ShapeDtypeStruct((bsz, t // CHUNK, rows, CHUNK), dt)
    resident = [mod, norm_w, mlstm_norm_w, w_n, b_n, w_t, b_t, conv_w, w_conv_out]
    return pl.pallas_call(
        _inproj_body,
        grid=(bsz, t // tile),
        in_specs=[tok(D_MODEL)] + [_resident(a.shape) for a in resident],
        out_specs=[tok(D_QK), tok_t(D_QK), tok_t(D_MLSTM), tok_t(N_GATE), tok(D_MLSTM),
                   tok(D_MODEL), tok(D_MODEL)],
        out_shape=[seq(D_QK), seq_t(D_QK, BF16), seq_t(D_MLSTM, BF16), seq_t(N_GATE, F32), seq(D_MLSTM),
                   seq(D_MODEL), seq(D_MODEL)],
        compiler_params=pltpu.CompilerParams(dimension_semantics=("arbitrary", "arbitrary"),
                                             vmem_limit_bytes=VMEM_LIMIT_BYTES),
        name="inproj_conv",
    )(x, *resident)


SERIES_ROW0 = GATES_PER_GROUP // 2
N_SERIES = N_DIR * HEADS_PER_STEP
N_SPLIT = 3
LOG2E = 1.4426950408889634
_WK, _DECAY, _WINTER, _EXPNEG, _KEYS = 0, 1, 2, 3, 4
_COLM = _KEYS + N_SPLIT
N_TABLES = _COLM + N_SPLIT * N_SERIES


def _running_max_rows(x, reverse):
    n = x.shape[0]
    row = lax.broadcasted_iota(jnp.int32, x.shape, 0)
    shift = 1
    while shift < n:
        if reverse:
            moved = jnp.where(row < n - shift, pltpu.roll(x, n - shift, axis=0), -jnp.inf)
        else:
            moved = jnp.where(row >= shift, pltpu.roll(x, shift, axis=0), -jnp.inf)
        x = jnp.maximum(x, moved)
        shift *= 2
    return x


def _gate_tables(g_ref, m0_ref, tab_ref, n_chunks):
    step_rows = N_HEAD_GROUPS * GATES_PER_GROUP
    rows = n_chunks * step_rows
    g = g_ref[0].reshape(rows, CHUNK)
    lf = _log_sigmoid(g)
    b = _cumsum_lanes(lf)
    ig = pltpu.roll(g, SERIES_ROW0, axis=0)
    in_tile = lambda idx: jnp.bitwise_and(idx, GATES_PER_GROUP - 1)
    r_in_tile = in_tile(lax.broadcasted_iota(jnp.int32, (rows, 1), 0))
    fwd = r_in_tile < SERIES_ROW0 + HEADS_PER_STEP
    e = b - lf
    tot = b[:, CHUNK - 1:CHUNK]
    col = jnp.where(fwd, b, -e)
    key = ig + jnp.where(fwd, -b, e)
    g_end = jnp.where(fwd, tot + key, key)
    g_max = jnp.max(g_end, axis=1, keepdims=True)

    fwd_step = fwd[:step_rows]
    tile = lambda a, c: a[c * step_rows:(c + 1) * step_rows]
    m = m0_ref[0]
    m_in_steps, m_out_steps = [], []
    for i in range(n_chunks):
        back = n_chunks - 1 - i
        m_in_steps.append(m)
        m = jnp.maximum(jnp.where(fwd_step, tile(tot, i), tile(tot, back)) + m,
                        jnp.where(fwd_step, tile(g_max, i), tile(g_max, back)))
        m_out_steps.append(m)
    by_chunk = lambda steps: jnp.concatenate(
        [jnp.where(fwd_step, steps[c], steps[n_chunks - 1 - c]) for c in range(n_chunks)], axis=0)
    m_in, m_out = by_chunk(m_in_steps), by_chunk(m_out_steps)

    decay = jnp.broadcast_to(jnp.exp(tot + m_in - m_out), (rows, CHUNK))
    lane = lax.broadcasted_iota(jnp.int32, (rows, CHUNK), 1)
    inter = jnp.where(fwd, col, tot + col) + m_in
    key_t = jnp.transpose(key)
    fwd_lane = in_tile(lax.broadcasted_iota(jnp.int32, (1, rows), 1)) < SERIES_ROW0 + HEADS_PER_STEP
    key_max = jnp.transpose(jnp.where(fwd_lane, _running_max_rows(key_t, reverse=False),
                                      _running_max_rows(key_t, reverse=True)))
    m_t = jnp.maximum(col + key_max, inter)
    tables = {_WK: jnp.exp(g_end - m_out),
              _DECAY: jnp.where(lane < DK, decay, pltpu.roll(decay, rows - 1, axis=0)),
              _WINTER: jnp.exp(inter - m_t), _EXPNEG: jnp.exp(-m_t)}
    for x, part in enumerate(_split3(key * LOG2E)):
        tables[_KEYS + x] = part.astype(F32)
    for x, part in enumerate(_split3((col - m_t) * LOG2E)):
        for sidx in range(N_SERIES):
            tables[_COLM + x * N_SERIES + sidx] = jnp.where(r_in_tile == SERIES_ROW0 + sidx,
                                                            part.astype(F32), 0.0)
    for idx, a in tables.items():
        tab_ref[idx] = a.reshape(n_chunks, N_HEAD_GROUPS, GATES_PER_GROUP, CHUNK)


def _mlstm_body(qt_ref, k_ref, vt_ref, so_ref, g_ref, c0_ref, m0_ref, o_ref,
                tab_ref, u_ref, s_ref, st_ref, causal_ref, *, n_chunks):
    hp = HEADS_PER_STEP
    ones_rows = jnp.where(lax.broadcasted_iota(jnp.int32, (V_AUG - DV, CHUNK), 0) == 0,
                          1.0, 0.0).astype(BF16)
    low_lanes = lax.broadcasted_iota(jnp.int32, (1, hp * DK), 1) < DK
    series = lambda tab8, d, j: tab8[SERIES_ROW0 + d * hp + j:SERIES_ROW0 + d * hp + j + 1]
    split_rows = N_SPLIT * GATES_PER_GROUP
    ones_split = jnp.ones((split_rows, CHUNK), F32)
    pick_r = jnp.bitwise_and(lax.broadcasted_iota(jnp.int32, (split_rows, N_DIR * CHUNK), 0),
                             GATES_PER_GROUP - 1)
    pick_d = jnp.where(lax.broadcasted_iota(jnp.int32, (split_rows, N_DIR * CHUNK), 1) >= CHUNK, 1, 0)
    pick_series = [jnp.where(pick_r == SERIES_ROW0 + pick_d * hp + j, 1.0, 0.0) for j in range(hp)]

    def chunk_rows(c):
        return slice(c * CHUNK, (c + 1) * CHUNK)

    def for_each_chunk(body, carry):
        for c in range(n_chunks):
            carry = body(c, carry)
        return carry

    def values_t(c, j):
        return jnp.concatenate([ones_rows, vt_ref[0, c, j * DV:(j + 1) * DV, :]], axis=0)

    grp = pl.program_id(1)

    @pl.when(grp == 0)
    def _():
        _gate_tables(g_ref, m0_ref, tab_ref, n_chunks)
        s_i = lax.broadcasted_iota(jnp.int32, (CHUNK, CHUNK), 0)
        t_i = lax.broadcasted_iota(jnp.int32, (CHUNK, CHUNK), 1)
        causal_ref[0] = jnp.where(s_i <= t_i, 0.0, -jnp.inf)
        causal_ref[1] = jnp.where(s_i >= t_i, 0.0, -jnp.inf)

    def block_diag_q(qt2):
        zero = jnp.zeros((DK, CHUNK), qt2.dtype)
        return jnp.concatenate([jnp.concatenate([qt2[:DK], zero], axis=0),
                                jnp.concatenate([zero, qt2[DK:]], axis=0)], axis=1)

    def chunk_step(c, carry):
        k2 = k_ref[0, chunk_rows(c), :]
        zero = jnp.zeros_like(k2)
        k_bd = jnp.concatenate([jnp.where(low_lanes, k2, zero),
                                jnp.where(low_lanes, zero, k2)], axis=0)
        wk8 = tab_ref[_WK, c, grp]
        vb = [values_t(c, j) for j in range(hp)]

        def key_weights(d, j):
            tile = jnp.broadcast_to(series(wk8, d, j), (BF16_SUBLANES, CHUNK)).astype(BF16)
            return jnp.concatenate([tile] * (V_AUG // BF16_SUBLANES), axis=0)

        lhs = jnp.concatenate(
            [jnp.concatenate([vb[j] * key_weights(d, j) for j in range(hp)], axis=1)
             for d in range(N_DIR)], axis=0)
        u_ref[c] = jnp.dot(lhs, k_bd, preferred_element_type=F32)

        qk_t = jnp.dot(k2, block_diag_q(qt_ref[0, c]), preferred_element_type=F32)
        key_side = jnp.concatenate([tab_ref[_KEYS + x, c, grp] for x in range(N_SPLIT)] + [ones_split],
                                   axis=0).astype(BF16)
        for j in range(hp):
            query_side = jnp.concatenate(
                [pick_series[j]] +
                [jnp.concatenate([tab_ref[_COLM + x * N_SERIES + d * hp + j, c, grp] for d in range(N_DIR)],
                                 axis=1)
                 for x in range(N_SPLIT)], axis=0).astype(BF16)
            log_d = lax.dot_general(key_side, query_side, (((0,), (0,)), ((), ())),
                                    preferred_element_type=F32)
            for d in range(N_DIR):
                decay = jnp.exp2(log_d[:, d * CHUNK:(d + 1) * CHUNK] + causal_ref[d])
                st_ref[c, d, j] = (qk_t[:, j * CHUNK:(j + 1) * CHUNK] * decay).astype(BF16)
        return carry

    for_each_chunk(chunk_step, 0)

    def scan_step(i, carry):
        cf, cb = carry
        back = n_chunks - 1 - i
        s_ref[i, 0, 0] = cf.astype(BF16)
        s_ref[i, 0, 1] = pltpu.roll(cf, DK, axis=1).astype(BF16)
        s_ref[back, 1, 0] = cb.astype(BF16)
        s_ref[back, 1, 1] = pltpu.roll(cb, DK, axis=1).astype(BF16)
        cf = tab_ref[_DECAY, i, grp][SERIES_ROW0:SERIES_ROW0 + 1] * cf + u_ref[i, :V_AUG]
        cb = tab_ref[_DECAY, back, grp][SERIES_ROW0 + hp:SERIES_ROW0 + hp + 1] * cb + u_ref[back, V_AUG:]
        return cf, cb

    for_each_chunk(scan_step, (c0_ref[0, 0, 0], c0_ref[0, 1, 0]))

    def output_step(c, carry):
        rows = chunk_rows(c)
        qt2 = qt_ref[0, c].astype(F32)
        w_inter8, exp_neg8 = tab_ref[_WINTER, c, grp], tab_ref[_EXPNEG, c, grp]
        no_query = jnp.zeros((DK, CHUNK), BF16)
        for j in range(hp):
            state = jnp.where(low_lanes, s_ref[c, 0, j], s_ref[c, 1, 1 - j])
            q_h = qt2[j * DK:(j + 1) * DK]
            qw = [(q_h * series(w_inter8, d, j)).astype(BF16) for d in range(N_DIR)]
            rhs = jnp.concatenate(
                [jnp.concatenate([st_ref[c, d, j] for d in range(N_DIR)], axis=1),
                 jnp.concatenate([qw[0], no_query], axis=1),
                 jnp.concatenate([no_query, qw[1]], axis=1)], axis=0)
            lhs = jnp.concatenate([values_t(c, j), state], axis=1)
            n_all = jnp.dot(lhs, rhs, preferred_element_type=F32)
            h_t = None
            for d in range(N_DIR):
                num = n_all[:, d * CHUNK:(d + 1) * CHUNK]
                r = 1.0 / jnp.maximum(jnp.abs(num[:1]), series(exp_neg8, d, j))
                part = num[V_AUG - DV:] * r
                h_t = part if h_t is None else h_t + part
            hn_t = h_t * lax.rsqrt(jnp.mean(h_t * h_t, axis=0, keepdims=True) + EPS)
            o_ref[0, rows, j * DV:(j + 1) * DV] = (
                jnp.transpose(hn_t) * so_ref[0, rows, j * DV:(j + 1) * DV].astype(F32)).astype(BF16)
        return carry

    for_each_chunk(output_step, 0)


def mlstm_call(qt, k, vt, so, gates, c0, m0):
    bsz, t, _ = k.shape
    n_chunks = t // CHUNK
    hp = HEADS_PER_STEP
    seq = lambda width: pl.BlockSpec((1, t, width), lambda i, j: (i, 0, j))
    seq_t = lambda rows: pl.BlockSpec((1, n_chunks, rows, CHUNK), lambda i, j: (i, 0, j, 0))
    return pl.pallas_call(
        functools.partial(_mlstm_body, n_chunks=n_chunks),
        grid=(bsz, N_HEAD_GROUPS),
        in_specs=[seq_t(hp * DK), seq(hp * DK), seq_t(hp * DV), seq(hp * DV),
                  pl.BlockSpec((1, n_chunks, N_GATE, CHUNK), lambda i, j: (i, 0, 0, 0)),
                  pl.BlockSpec((1, N_DIR, 1, V_AUG, hp * DK), lambda i, j: (i, 0, j, 0, 0)),
                  pl.BlockSpec((1, N_GATE, 1), lambda i, j: (i, 0, 0))],
        out_specs=seq(hp * DV),
        out_shape=jax.ShapeDtypeStruct((bsz, t, D_MLSTM), BF16),
        scratch_shapes=[pltpu.VMEM((N_TABLES, n_chunks, N_HEAD_GROUPS, GATES_PER_GROUP, CHUNK), F32),
                        pltpu.VMEM((n_chunks, N_DIR * V_AUG, hp * DK), F32),
                        pltpu.VMEM((n_chunks, N_DIR, 2, V_AUG, hp * DK), BF16),
                        pltpu.VMEM((n_chunks, N_DIR, hp, CHUNK, CHUNK), BF16),
                        pltpu.VMEM((N_DIR, CHUNK, CHUNK), F32)],
        compiler_params=pltpu.CompilerParams(dimension_semantics=("arbitrary", "arbitrary"),
                                             vmem_limit_bytes=VMEM_LIMIT_BYTES),
        name="mlstm",
    )(qt, k, vt, so, gates, c0, m0)


def _out_body(x_ref, hs_ref, ya_ref, gb_ref, mod_ref, nw2_ref, fnw_ref, wmo_ref, wo_ref, w1_ref, w2_ref, o_ref):
    dot = functools.partial(jnp.dot, preferred_element_type=F32)
    mod = mod_ref[pl.ds(pl.program_id(0), 1), :]
    yb = dot(hs_ref[0], wmo_ref[...])
    merged = ya_ref[0].astype(F32) + gb_ref[0].astype(F32) * yb
    x1 = x_ref[0] + mod[:, _G1:_SH2] * dot(merged.astype(BF16), wo_ref[...])
    hm = _modulated_norm(x1, nw2_ref[...], mod[:, _SH2:_SC2], mod[:, _SC2:_G2]).astype(BF16)
    a = jnp.maximum(dot(hm, w1_ref[...]), 0.0)
    x2 = x1 + mod[:, _G2:] * dot((a * a).astype(BF16), w2_ref[...])
    y = x2 * lax.rsqrt(jnp.mean(x2 * x2, axis=-1, keepdims=True) + EPS)
    o_ref[0] = y * fnw_ref[...]


def out_call(x, hs, ya, gb, mod, norm2_w, final_norm_w, w_mlstm_out, w_out, w_ff1, w_ff2):
    bsz, t, _ = x.shape
    tile = TOKEN_TILE
    tok = pl.BlockSpec((1, tile, D_MODEL), lambda i, j: (i, j, 0))
    resident = [mod, norm2_w, final_norm_w, w_mlstm_out, w_out, w_ff1, w_ff2]
    return pl.pallas_call(
        _out_body,
        grid=(bsz, t // tile),
        in_specs=[tok, tok, tok, tok] + [_resident(a.shape) for a in resident],
        out_specs=tok,
        out_shape=jax.ShapeDtypeStruct((bsz, t, D_MODEL), F32),
        compiler_params=pltpu.CompilerParams(dimension_semantics=("arbitrary", "arbitrary"),
                                             vmem_limit_bytes=VMEM_LIMIT_BYTES),
        name="merge_out_mlp",
    )(x, hs, ya, gb, *resident)


def _layer(x, ctx, mod, norm1_w, w_in, b_in, conv_w, mlstm_norm_w, w_conv_out, w_mlstm_out,
           w_out, norm2_w, w_ff1, w_ff2, final_norm_w):
    nw1 = norm1_w.reshape(1, D_MODEL)
    w_n, w_t = projection_weights(w_in.T)
    b_n = jnp.concatenate([b_in[_REF_K:_REF_V], b_in[_REF_O:]]).reshape(1, -1)
    b_t = jnp.concatenate([b_in[_REF_V:_REF_IG], b_in[_REF_Q:_REF_O], _group_gates(b_in[_REF_IG:_REF_Q]),
                           jnp.zeros((_T_ROWS - _T_END,), F32)]).reshape(-1, 1)

    c0, m0 = ctx_call(ctx, mod, nw1, w_n, b_n, w_t, b_t)
    k, qt, vt, gates, so, ya, gb = inproj_call(x, mod, nw1, mlstm_norm_w.reshape(1, D_MLSTM), w_n, b_n, w_t, b_t,
                                               conv_w, w_conv_out.astype(BF16))
    hs = mlstm_call(qt, k, vt, so, gates, c0, m0)
    return out_call(x, hs, ya, gb, mod, norm2_w.reshape(1, D_MODEL), final_norm_w.reshape(1, D_MODEL),
                    w_mlstm_out.astype(BF16), w_out.astype(BF16), w_ff1.astype(BF16), w_ff2.astype(BF16))


def kernel(x, c, ctx, c_ctx, w_mod, b_mod, norm1_w, w_in, b_in, conv_w, mlstm_norm_w, w_conv_out,
           w_mlstm_out, w_out, norm2_w, w_ff1, w_ff2, final_norm_w):
    depth = w_mod.shape[0]
    assert depth == 1, "the context stream is only advanced through its mLSTM state (single layer)"
    cvecs = jnp.concatenate([c, c_ctx[None, :]], axis=0)
    mod = adaln_call(cvecs, w_mod[0], b_mod[0])
    return _layer(x, ctx, mod, norm1_w[0], w_in[0], b_in[0], conv_w[0],
                  mlstm_norm_w[0], w_conv_out[0], w_mlstm_out[0], w_out[0], norm2_w[0], w_ff1[0],
                  w_ff2[0], final_norm_w)
```

```python
import functools

import jax
import jax.numpy as jnp
from jax import lax
from jax.experimental import pallas as pl
from jax.experimental.pallas import tpu as pltpu

D_MODEL = 1024
CTX_LEN = 256
GRID_W = 64
N_HEADS = 8
DK = 64
DV = 128
D_MLSTM = N_HEADS * DV
D_QK = N_HEADS * DK
N_DIR = 2
N_GATE = 2 * N_DIR * N_HEADS
EPS = 1e-6

CHUNK = 128
HEADS_PER_STEP = 2
assert HEADS_PER_STEP == 2 and HEADS_PER_STEP * DK == 128, "a head pair shares one 128-lane tile"
N_HEAD_GROUPS = N_HEADS // HEADS_PER_STEP
GATES_PER_GROUP = N_GATE // N_HEAD_GROUPS
SUBLANES = 8
BF16_SUBLANES = 16
V_AUG = BF16_SUBLANES + DV
TOKEN_TILE = 512
ADALN_TILE = 1024
CTX_BATCH = 4

VMEM_LIMIT_BYTES = 56 * 1024 * 1024

_REF_K, _REF_V, _REF_IG, _REF_FG, _REF_Q, _REF_O = 0, 512, 1536, 1552, 1568, 2080
D_IN = 8224
_K0, _O0, _XIN0, _GC0, _GB0, _MA0, _MB0, _G0 = 0, 512, 1536, 2560, 3584, 4608, 5632, 6656
_T_V, _T_Q, _T_GATES, _T_END, _T_ROWS = 0, 1024, 1536, 1568, 2048
_SH1, _SC1, _G1, _SH2, _SC2, _G2 = (i * D_MODEL for i in range(6))

F32 = jnp.float32
BF16 = jnp.bfloat16


def _sigmoid(x):
    return 0.5 * jnp.tanh(0.5 * x) + 0.5


def _log_sigmoid(x):
    return jnp.minimum(x, 0.0) - jnp.log(1.0 + jnp.exp(-jnp.abs(x)))


def _split3(x):
    hi = x.astype(BF16)
    r1 = x - hi.astype(F32)
    mid = r1.astype(BF16)
    lo = (r1 - mid.astype(F32)).astype(BF16)
    return hi, mid, lo


def _cumsum_lanes(x):
    n = x.shape[1]
    r = lax.broadcasted_iota(jnp.int32, (n, n), 0)
    c = lax.broadcasted_iota(jnp.int32, (n, n), 1)
    tri_t = jnp.where(r <= c, 1.0, 0.0).astype(BF16)
    hi, mid, lo = _split3(x)
    dot = functools.partial(jnp.dot, preferred_element_type=F32)
    return dot(hi, tri_t) + dot(mid, tri_t) + dot(lo, tri_t)


def _modulated_norm(x, norm_w, shift, scale):
    y = x * lax.rsqrt(jnp.mean(x * x, axis=-1, keepdims=True) + EPS)
    return y * (norm_w * (1.0 + scale)) + shift


def _resident(shape):
    nd = len(shape)
    return pl.BlockSpec(shape, lambda *_: (0,) * nd, pipeline_mode=pl.Buffered(1))


def _adaln_body(c_ref, w_ref, b_ref, o_ref):
    c = c_ref[...]
    s = c * _sigmoid(c)
    dot = functools.partial(jnp.dot, preferred_element_type=F32)
    w = w_ref[...]
    s_hi, w_hi = s.astype(BF16), w.astype(BF16)
    s_lo, w_lo = (s - s_hi.astype(F32)).astype(BF16), (w - w_hi.astype(F32)).astype(BF16)
    o_ref[...] = (dot(s_hi, w_hi) + dot(s_hi, w_lo) + dot(s_lo, w_hi)) + b_ref[...]


def adaln_call(cvecs, w_mod, b_mod):
    n = cvecs.shape[0]
    n_out = w_mod.shape[1]
    tile = ADALN_TILE
    return pl.pallas_call(
        _adaln_body,
        grid=(n_out // tile,),
        in_specs=[pl.BlockSpec((n, D_MODEL), lambda j: (0, 0)),
                  pl.BlockSpec((D_MODEL, tile), lambda j: (0, j)),
                  pl.BlockSpec((1, tile), lambda j: (0, j))],
        out_specs=pl.BlockSpec((n, tile), lambda j: (0, j)),
        out_shape=jax.ShapeDtypeStruct((n, n_out), F32),
        name="adaln",
    )(cvecs, w_mod, b_mod.reshape(1, n_out))


W_BLOCK = 512
CAST_STEPS = 8


def _transpose_cast_body(w_ref, o_ref):
    o_ref[...] = jnp.transpose(w_ref[...]).astype(BF16)


def _row_cast_body(starts_ref, w_ref, perm_ref, o_ref):
    del starts_ref
    o_ref[...] = w_ref[...].astype(BF16)

    @pl.when(pl.program_id(0) == pl.num_programs(0) - 1)
    def _():
        gates = w_ref[0:N_GATE, :].astype(BF16)
        o_ref[0:N_GATE, :] = jnp.dot(perm_ref[...], gates, preferred_element_type=F32).astype(BF16)


def _group_gates(g):
    lead = g.shape[:-1]
    g = g.reshape(*lead, 2, N_DIR, N_HEAD_GROUPS, HEADS_PER_STEP)
    g = jnp.moveaxis(g, -2, -4)
    return g.reshape(*lead, N_GATE)


def projection_weights(w_in_t):
    n_rest = (D_IN - _REF_O) // W_BLOCK
    blk = W_BLOCK // SUBLANES
    w_n = pl.pallas_call(
        _transpose_cast_body,
        grid=(1 + n_rest,),
        in_specs=[pl.BlockSpec((pl.Element(W_BLOCK), pl.Element(D_MODEL)),
                               lambda i: (SUBLANES * jnp.where(i == 0, _REF_K // SUBLANES,
                                                               _REF_O // SUBLANES + (i - 1) * blk), 0))],
        out_specs=pl.BlockSpec((D_MODEL, W_BLOCK), lambda i: (0, i)),
        out_shape=jax.ShapeDtypeStruct((D_MODEL, (1 + n_rest) * W_BLOCK), BF16),
        name="w_in_normal",
    )(w_in_t)
    starts = jnp.asarray([_REF_V // SUBLANES, _REF_V // SUBLANES + blk, _REF_Q // SUBLANES, _REF_IG // SUBLANES],
                         jnp.int32)
    gate_perm = jnp.eye(N_GATE, dtype=BF16)[_group_gates(jnp.arange(N_GATE))]
    w_t = pl.pallas_call(
        _row_cast_body,
        grid_spec=pltpu.PrefetchScalarGridSpec(
            num_scalar_prefetch=1,
            grid=(_T_ROWS // W_BLOCK,),
            in_specs=[pl.BlockSpec((pl.Element(W_BLOCK), pl.Element(D_MODEL)),
                                   lambda i, starts_ref: (SUBLANES * starts_ref[i], 0)),
                      pl.BlockSpec((N_GATE, N_GATE), lambda i, starts_ref: (0, 0))],
            out_specs=pl.BlockSpec((W_BLOCK, D_MODEL), lambda i, starts_ref: (i, 0))),
        out_shape=jax.ShapeDtypeStruct((_T_ROWS, D_MODEL), BF16),
        name="w_in_transposed",
    )(starts, w_in_t, gate_perm)
    return w_n, w_t


def _cast_body(*refs):
    n = len(refs) // 2
    for src, dst in zip(refs[:n], refs[n:]):
        dst[...] = src[...].astype(BF16)


def cast_weights(*weights):
    spec = lambda w: pl.BlockSpec((w.shape[0] // CAST_STEPS, w.shape[1]), lambda i: (i, 0))
    return pl.pallas_call(
        _cast_body,
        grid=(CAST_STEPS,),
        in_specs=[spec(w) for w in weights],
        out_specs=[spec(w) for w in weights],
        out_shape=[jax.ShapeDtypeStruct(w.shape, BF16) for w in weights],
        name="cast_weights",
    )(*weights)


def _ctx_body(ctx_ref, mod_ref, nw_ref, wk_ref, bk_ref, wv_ref, bv_ref, wg_ref, bg_ref, c_ref, m_ref):
    hp = HEADS_PER_STEP
    nb = ctx_ref.shape[0]
    last = mod_ref.shape[0] - 1
    x = ctx_ref[...].reshape(nb * CTX_LEN, D_MODEL)
    hb = _modulated_norm(x, nw_ref[...], mod_ref[last:, _SH1:_SC1], mod_ref[last:, _SC1:_G1]).astype(BF16)
    nt = functools.partial(lax.dot_general, dimension_numbers=(((1,), (1,)), ((), ())),
                           preferred_element_type=F32)
    k_all = ((jnp.dot(hb, wk_ref[...], preferred_element_type=F32) + bk_ref[...]) * (DK ** -0.5)).astype(BF16)
    vt_all = (nt(wv_ref[...], hb) + bv_ref[...]).astype(BF16)
    g_all = nt(wg_ref[...], hb) + bg_ref[...]

    r_in_tile = jnp.bitwise_and(lax.broadcasted_iota(jnp.int32, (N_GATE, 1), 0), GATES_PER_GROUP - 1)
    fwd = r_in_tile < GATES_PER_GROUP // 2 + hp
    ones_rows = jnp.where(lax.broadcasted_iota(jnp.int32, (V_AUG - DV, CTX_LEN), 0) == 0, 1.0, 0.0)
    lane = lax.broadcasted_iota(jnp.int32, (CTX_LEN, hp * DK), 1)
    for bi in range(nb):
        tokens = slice(bi * CTX_LEN, (bi + 1) * CTX_LEN)
        g, vt, k = g_all[:, tokens], vt_all[:, tokens], k_all[tokens]
        lf = _log_sigmoid(g)
        b = _cumsum_lanes(lf)
        ig = pltpu.roll(g, GATES_PER_GROUP // 2, axis=0)
        tot = b[:, CTX_LEN - 1:CTX_LEN]
        g_end = ig + jnp.where(fwd, tot - b, b - lf)
        m = jnp.maximum(tot, jnp.max(g_end, axis=1, keepdims=True))
        m_ref[bi] = m
        wk = jnp.exp(g_end - m)
        for grp in range(N_HEAD_GROUPS):
            k2 = k[:, grp * hp * DK:(grp + 1) * hp * DK]
            zero = jnp.zeros_like(k2)
            k_bd = jnp.concatenate([jnp.where(lane < DK, k2, zero), jnp.where(lane >= DK, k2, zero)], axis=0)
            vf = [jnp.concatenate([ones_rows, vt[(grp * hp + j) * DV:(grp * hp + j + 1) * DV].astype(F32)],
                                  axis=0) for j in range(hp)]
            for d in range(N_DIR):
                row0 = grp * GATES_PER_GROUP + GATES_PER_GROUP // 2 + d * hp
                lhs = jnp.concatenate([(vf[j] * wk[row0 + j:row0 + j + 1]).astype(BF16) for j in range(hp)],
                                      axis=1)
                c_ref[bi, d, grp] = jnp.dot(lhs, k_bd, preferred_element_type=F32)


def ctx_call(ctx, mod, norm_w, w_n, b_n, w_t, b_t):
    bsz = ctx.shape[0]
    nb = CTX_BATCH
    const = lambda shape: pl.BlockSpec(shape, lambda i: (0,) * len(shape))
    gate_rows = lambda width: pl.BlockSpec((N_GATE, width), lambda i: (_T_GATES // N_GATE, 0))
    return pl.pallas_call(
        _ctx_body,
        grid=(bsz // nb,),
        in_specs=[pl.BlockSpec((nb, CTX_LEN, D_MODEL), lambda i: (i, 0, 0)),
                  const(mod.shape), const((1, D_MODEL)),
                  const((D_MODEL, D_QK)), const((1, D_QK)),
                  const((D_MLSTM, D_MODEL)), const((D_MLSTM, 1)),
                  gate_rows(D_MODEL), gate_rows(1)],
        out_specs=[pl.BlockSpec((nb, N_DIR, N_HEAD_GROUPS, V_AUG, HEADS_PER_STEP * DK),
                                lambda i: (i, 0, 0, 0, 0)),
                   pl.BlockSpec((nb, N_GATE, 1), lambda i: (i, 0, 0))],
        out_shape=[jax.ShapeDtypeStruct((bsz, N_DIR, N_HEAD_GROUPS, V_AUG, HEADS_PER_STEP * DK), F32),
                   jax.ShapeDtypeStruct((bsz, N_GATE, 1), F32)],
        compiler_params=pltpu.CompilerParams(dimension_semantics=("arbitrary",),
                                             vmem_limit_bytes=VMEM_LIMIT_BYTES),
        name="ctx_state",
    )(ctx, mod, norm_w, w_n, b_n, w_t, b_t, w_t, b_t)


def _inproj_body(x_ref, mod_ref, nw_ref, nwm_ref, w_ref, b_ref, wt_ref, bt_ref, cw_ref, wco_ref,
                 k_ref, qt_ref, vt_ref, gt_ref, so_ref, ya_ref, gb_ref):
    x = x_ref[0]
    mod = mod_ref[pl.ds(pl.program_id(0), 1), :]
    hb = _modulated_norm(x, nw_ref[...], mod[:, _SH1:_SC1], mod[:, _SC1:_G1]).astype(BF16)
    tile = hb.shape[0]

    def seg(a, b):
        return jnp.dot(hb, w_ref[:, a:b], preferred_element_type=F32) + b_ref[:, a:b]

    k_ref[0] = (seg(_K0, _O0) * (DK ** -0.5)).astype(BF16)
    so_ref[0] = (_sigmoid(seg(_O0, _XIN0)) * nwm_ref[...]).astype(BF16)

    zt = lax.dot_general(wt_ref[:_T_END, :], hb, (((1,), (1,)), ((), ())),
                         preferred_element_type=F32) + bt_ref[:_T_END, :]
    for i in range(tile // CHUNK):
        lanes = slice(i * CHUNK, (i + 1) * CHUNK)
        vt_ref[0, i] = zt[_T_V:_T_Q, lanes].astype(BF16)
        qt_ref[0, i] = zt[_T_Q:_T_GATES, lanes].astype(BF16)
        gt_ref[0, i] = zt[_T_GATES:_T_END, lanes]

    u = seg(_XIN0, _GC0) * seg(_GC0, _GB0)
    col = jnp.bitwise_and(lax.broadcasted_iota(jnp.int32, (tile, 1), 0), GRID_W - 1)
    u_prev = jnp.where(col != 0, pltpu.roll(u, 1, axis=0), 0.0)
    u_next = jnp.where(col != GRID_W - 1, pltpu.roll(u, tile - 1, axis=0), 0.0)
    a = cw_ref[0:1, :] * u_prev + cw_ref[1:2, :] * u + cw_ref[2:3, :] * u_next
    ya = jnp.dot((seg(_GB0, _MA0) * a).astype(BF16), wco_ref[...], preferred_element_type=F32)
    ya_ref[0] = (_sigmoid(seg(_MA0, _MB0)) * ya).astype(BF16)
    gb_ref[0] = _sigmoid(seg(_MB0, _G0)).astype(BF16)


def inproj_call(x, mod, norm_w, mlstm_norm_w, w_n, b_n, w_t, b_t, conv_w, w_conv_out):
    bsz, t, _ = x.shape
    tile = TOKEN_TILE
    tok = lambda width: pl.BlockSpec((1, tile, width), lambda i, j: (i, j, 0))
    tok_t = lambda rows: pl.BlockSpec((1, tile // CHUNK, rows, CHUNK), lambda i, j: (i, j, 0, 0))
    seq = lambda width: jax.ShapeDtypeStruct((bsz, t, width), BF16)
    seq_t = lambda rows, dt: jax.ShapeDtypeStruct((bsz, t // CHUNK, rows, CHUNK), dt)
    resident = [mod, norm_w, mlstm_norm_w, w_n, b_n, w_t, b_t, conv_w, w_conv_out]
    return pl.pallas_call(
        _inproj_body,
        grid=(bsz, t // tile),
        in_specs=[tok(D_MODEL)] + [_resident(a.shape) for a in resident],
        out_specs=[tok(D_QK), tok_t(D_QK), tok_t(D_MLSTM), tok_t(N_GATE), tok(D_MLSTM),
                   tok(D_MODEL), tok(D_MODEL)],
        out_shape=[seq(D_QK), seq_t(D_QK, BF16), seq_t(D_MLSTM, BF16), seq_t(N_GATE, F32), seq(D_MLSTM),
                   seq(D_MODEL), seq(D_MODEL)],
        compiler_params=pltpu.CompilerParams(dimension_semantics=("arbitrary", "arbitrary"),
                                             vmem_limit_bytes=VMEM_LIMIT_BYTES),
        name="inproj_conv",
    )(x, *resident)


SERIES_ROW0 = GATES_PER_GROUP // 2
N_SERIES = N_DIR * HEADS_PER_STEP
N_SPLIT = 3
LOG2E = 1.4426950408889634
_WK, _DECAY, _WINTER, _EXPNEG, _KEYS = 0, 1, 2, 3, 4
_COLM = _KEYS + N_SPLIT
N_TABLES = _COLM + N_SPLIT * N_SERIES


def _running_max_rows(x, reverse):
    n = x.shape[0]
    row = lax.broadcasted_iota(jnp.int32, x.shape, 0)
    shift = 1
    while shift < n:
        if reverse:
            moved = jnp.where(row < n - shift, pltpu.roll(x, n - shift, axis=0), -jnp.inf)
        else:
            moved = jnp.where(row >= shift, pltpu.roll(x, shift, axis=0), -jnp.inf)
        x = jnp.maximum(x, moved)
        shift *= 2
    return x


def _gate_tables(g_ref, m0_ref, tab_ref, n_chunks):
    step_rows = N_HEAD_GROUPS * GATES_PER_GROUP
    rows = n_chunks * step_rows
    g = g_ref[0].reshape(rows, CHUNK)
    lf = _log_sigmoid(g)
    b = _cumsum_lanes(lf)
    ig = pltpu.roll(g, SERIES_ROW0, axis=0)
    in_tile = lambda idx: jnp.bitwise_and(idx, GATES_PER_GROUP - 1)
    r_in_tile = in_tile(lax.broadcasted_iota(jnp.int32, (rows, 1), 0))
    fwd = r_in_tile < SERIES_ROW0 + HEADS_PER_STEP
    e = b - lf
    tot = b[:, CHUNK - 1:CHUNK]
    col = jnp.where(fwd, b, -e)
    key = ig + jnp.where(fwd, -b, e)
    g_end = jnp.where(fwd, tot + key, key)
    g_max = jnp.max(g_end, axis=1, keepdims=True)

    fwd_step = fwd[:step_rows]
    tile = lambda a, c: a[c * step_rows:(c + 1) * step_rows]
    m = m0_ref[0]
    m_in_steps, m_out_steps = [], []
    for i in range(n_chunks):
        back = n_chunks - 1 - i
        m_in_steps.append(m)
        m = jnp.maximum(jnp.where(fwd_step, tile(tot, i), tile(tot, back)) + m,
                        jnp.where(fwd_step, tile(g_max, i), tile(g_max, back)))
        m_out_steps.append(m)
    by_chunk = lambda steps: jnp.concatenate(
        [jnp.where(fwd_step, steps[c], steps[n_chunks - 1 - c]) for c in range(n_chunks)], axis=0)
    m_in, m_out = by_chunk(m_in_steps), by_chunk(m_out_steps)

    decay = jnp.broadcast_to(jnp.exp(tot + m_in - m_out), (rows, CHUNK))
    lane = lax.broadcasted_iota(jnp.int32, (rows, CHUNK), 1)
    inter = jnp.where(fwd, col, tot + col) + m_in
    key_t = jnp.transpose(key)
    fwd_lane = in_tile(lax.broadcasted_iota(jnp.int32, (1, rows), 1)) < SERIES_ROW0 + HEADS_PER_STEP
    key_max = jnp.transpose(jnp.where(fwd_lane, _running_max_rows(key_t, reverse=False),
                                      _running_max_rows(key_t, reverse=True)))
    m_t = jnp.maximum(col + key_max, inter)
    tables = {_WK: jnp.exp(g_end - m_out),
              _DECAY: jnp.where(lane < DK, decay, pltpu.roll(decay, rows - 1, axis=0)),
              _WINTER: jnp.exp(inter - m_t), _EXPNEG: jnp.exp(-m_t)}
    for x, part in enumerate(_split3(key * LOG2E)):
        tables[_KEYS + x] = part.astype(F32)
    for x, part in enumerate(_split3((col - m_t) * LOG2E)):
        for sidx in range(N_SERIES):
            tables[_COLM + x * N_SERIES + sidx] = jnp.where(r_in_tile == SERIES_ROW0 + sidx,
                                                            part.astype(F32), 0.0)
    for idx, a in tables.items():
        tab_ref[idx] = a.reshape(n_chunks, N_HEAD_GROUPS, GATES_PER_GROUP, CHUNK)


def _mlstm_body(qt_ref, k_ref, vt_ref, so_ref, g_ref, c0_ref, m0_ref, o_ref,
                tab_ref, u_ref, s_ref, st_ref, causal_ref, *, n_chunks):
    hp = HEADS_PER_STEP
    ones_rows = jnp.where(lax.broadcasted_iota(jnp.int32, (V_AUG - DV, CHUNK), 0) == 0,
                          1.0, 0.0).astype(BF16)
    low_lanes = lax.broadcasted_iota(jnp.int32, (1, hp * DK), 1) < DK
    series = lambda tab8, d, j: tab8[SERIES_ROW0 + d * hp + j:SERIES_ROW0 + d * hp + j + 1]
    split_rows = N_SPLIT * GATES_PER_GROUP
    ones_split = jnp.ones((split_rows, CHUNK), F32)
    pick_r = jnp.bitwise_and(lax.broadcasted_iota(jnp.int32, (split_rows, N_DIR * CHUNK), 0),
                             GATES_PER_GROUP - 1)
    pick_d = jnp.where(lax.broadcasted_iota(jnp.int32, (split_rows, N_DIR * CHUNK), 1) >= CHUNK, 1, 0)
    pick_series = [jnp.where(pick_r == SERIES_ROW0 + pick_d * hp + j, 1.0, 0.0) for j in range(hp)]

    def chunk_rows(c):
        return slice(c * CHUNK, (c + 1) * CHUNK)

    def for_each_chunk(body, carry):
        for c in range(n_chunks):
            carry = body(c, carry)
        return carry

    def values_t(c, j):
        return jnp.concatenate([ones_rows, vt_ref[0, c, j * DV:(j + 1) * DV, :]], axis=0)

    grp = pl.program_id(1)

    @pl.when(grp == 0)
    def _():
        _gate_tables(g_ref, m0_ref, tab_ref, n_chunks)
        s_i = lax.broadcasted_iota(jnp.int32, (CHUNK, CHUNK), 0)
        t_i = lax.broadcasted_iota(jnp.int32, (CHUNK, CHUNK), 1)
        causal_ref[0] = jnp.where(s_i <= t_i, 0.0, -jnp.inf)
        causal_ref[1] = jnp.where(s_i >= t_i, 0.0, -jnp.inf)

    def block_diag_q(qt2):
        zero = jnp.zeros((DK, CHUNK), qt2.dtype)
        return jnp.concatenate([jnp.concatenate([qt2[:DK], zero], axis=0),
                                jnp.concatenate([zero, qt2[DK:]], axis=0)], axis=1)

    def chunk_step(c, carry):
        k2 = k_ref[0, chunk_rows(c), :]
        zero = jnp.zeros_like(k2)
        k_bd = jnp.concatenate([jnp.where(low_lanes, k2, zero),
                                jnp.where(low_lanes, zero, k2)], axis=0)
        wk8 = tab_ref[_WK, c, grp]
        vb = [values_t(c, j) for j in range(hp)]

        def key_weights(d, j):
            tile = jnp.broadcast_to(series(wk8, d, j), (BF16_SUBLANES, CHUNK)).astype(BF16)
            return jnp.concatenate([tile] * (V_AUG // BF16_SUBLANES), axis=0)

        lhs = jnp.concatenate(
            [jnp.concatenate([vb[j] * key_weights(d, j) for j in range(hp)], axis=1)
             for d in range(N_DIR)], axis=0)
        u_ref[c] = jnp.dot(lhs, k_bd, preferred_element_type=F32)

        qk_t = jnp.dot(k2, block_diag_q(qt_ref[0, c]), preferred_element_type=F32)
        key_side = jnp.concatenate([tab_ref[_KEYS + x, c, grp] for x in range(N_SPLIT)] + [ones_split],
                                   axis=0).astype(BF16)
        for j in range(hp):
            query_side = jnp.concatenate(
                [pick_series[j]] +
                [jnp.concatenate([tab_ref[_COLM + x * N_SERIES + d * hp + j, c, grp] for d in range(N_DIR)],
                                 axis=1)
                 for x in range(N_SPLIT)], axis=0).astype(BF16)
            log_d = lax.dot_general(key_side, query_side, (((0,), (0,)), ((), ())),
                                    preferred_element_type=F32)
            for d in range(N_DIR):
                decay = jnp.exp2(log_d[:, d * CHUNK:(d + 1) * CHUNK] + causal_ref[d])
                st_ref[c, d, j] = (qk_t[:, j * CHUNK:(j + 1) * CHUNK] * decay).astype(BF16)
        return carry

    for_each_chunk(chunk_step, 0)

    def scan_step(i, carry):
        cf, cb = carry
        back = n_chunks - 1 - i
        s_ref[i, 0, 0] = cf.astype(BF16)
        s_ref[i, 0, 1] = pltpu.roll(cf, DK, axis=1).astype(BF16)
        s_ref[back, 1, 0] = cb.astype(BF16)
        s_ref[back, 1, 1] = pltpu.roll(cb, DK, axis=1).astype(BF16)
        cf = tab_ref[_DECAY, i, grp][SERIES_ROW0:SERIES_ROW0 + 1] * cf + u_ref[i, :V_AUG]
        cb = tab_ref[_DECAY, back, grp][SERIES_ROW0 + hp:SERIES_ROW0 + hp + 1] * cb + u_ref[back, V_AUG:]
        return cf, cb

    for_each_chunk(scan_step, (c0_ref[0, 0, 0], c0_ref[0, 1, 0]))

    def output_step(c, carry):
        rows = chunk_rows(c)
        qt2 = qt_ref[0, c].astype(F32)
        w_inter8, exp_neg8 = tab_ref[_WINTER, c, grp], tab_ref[_EXPNEG, c, grp]
        no_query = jnp.zeros((DK, CHUNK), BF16)
        for j in range(hp):
            state = jnp.where(low_lanes, s_ref[c, 0, j], s_ref[c, 1, 1 - j])
            q_h = qt2[j * DK:(j + 1) * DK]
            qw = [(q_h * series(w_inter8, d, j)).astype(BF16) for d in range(N_DIR)]
            rhs = jnp.concatenate(
                [jnp.concatenate([st_ref[c, d, j] for d in range(N_DIR)], axis=1),
                 jnp.concatenate([qw[0], no_query], axis=1),
                 jnp.concatenate([no_query, qw[1]], axis=1)], axis=0)
            lhs = jnp.concatenate([values_t(c, j), state], axis=1)
            n_all = jnp.dot(lhs, rhs, preferred_element_type=F32)
            h_t = None
            for d in range(N_DIR):
                num = n_all[:, d * CHUNK:(d + 1) * CHUNK]
                r = 1.0 / jnp.maximum(jnp.abs(num[:1]), series(exp_neg8, d, j))
                part = num[V_AUG - DV:] * r
                h_t = part if h_t is None else h_t + part
            hn_t = h_t * lax.rsqrt(jnp.mean(h_t * h_t, axis=0, keepdims=True) + EPS)
            o_ref[0, rows, j * DV:(j + 1) * DV] = (
                jnp.transpose(hn_t) * so_ref[0, rows, j * DV:(j + 1) * DV].astype(F32)).astype(BF16)
        return carry

    for_each_chunk(output_step, 0)


def mlstm_call(qt, k, vt, so, gates, c0, m0):
    bsz, t, _ = k.shape
    n_chunks = t // CHUNK
    hp = HEADS_PER_STEP
    seq = lambda width: pl.BlockSpec((1, t, width), lambda i, j: (i, 0, j))
    seq_t = lambda rows: pl.BlockSpec((1, n_chunks, rows, CHUNK), lambda i, j: (i, 0, j, 0))
    return pl.pallas_call(
        functools.partial(_mlstm_body, n_chunks=n_chunks),
        grid=(bsz, N_HEAD_GROUPS),
        in_specs=[seq_t(hp * DK), seq(hp * DK), seq_t(hp * DV), seq(hp * DV),
                  pl.BlockSpec((1, n_chunks, N_GATE, CHUNK), lambda i, j: (i, 0, 0, 0)),
                  pl.BlockSpec((1, N_DIR, 1, V_AUG, hp * DK), lambda i, j: (i, 0, j, 0, 0)),
                  pl.BlockSpec((1, N_GATE, 1), lambda i, j: (i, 0, 0))],
        out_specs=seq(hp * DV),
        out_shape=jax.ShapeDtypeStruct((bsz, t, D_MLSTM), BF16),
        scratch_shapes=[pltpu.VMEM((N_TABLES, n_chunks, N_HEAD_GROUPS, GATES_PER_GROUP, CHUNK), F32),
                        pltpu.VMEM((n_chunks, N_DIR * V_AUG, hp * DK), F32),
                        pltpu.VMEM((n_chunks, N_DIR, 2, V_AUG, hp * DK), BF16),
                        pltpu.VMEM((n_chunks, N_DIR, hp, CHUNK, CHUNK), BF16),
                        pltpu.VMEM((N_DIR, CHUNK, CHUNK), F32)],
        compiler_params=pltpu.CompilerParams(dimension_semantics=("arbitrary", "arbitrary"),
                                             vmem_limit_bytes=VMEM_LIMIT_BYTES),
        name="mlstm",
    )(qt, k, vt, so, gates, c0, m0)


def _out_body(x_ref, hs_ref, ya_ref, gb_ref, mod_ref, nw2_ref, fnw_ref, wmo_ref, wo_ref, w1_ref, w2_ref, o_ref):
    dot = functools.partial(jnp.dot, preferred_element_type=F32)
    mod = mod_ref[pl.ds(pl.program_id(0), 1), :]
    yb = dot(hs_ref[0], wmo_ref[...])
    merged = ya_ref[0].astype(F32) + gb_ref[0].astype(F32) * yb
    x1 = x_ref[0] + mod[:, _G1:_SH2] * dot(merged.astype(BF16), wo_ref[...])
    hm = _modulated_norm(x1, nw2_ref[...], mod[:, _SH2:_SC2], mod[:, _SC2:_G2]).astype(BF16)
    a = jnp.maximum(dot(hm, w1_ref[...]), 0.0)
    x2 = x1 + mod[:, _G2:] * dot((a * a).astype(BF16), w2_ref[...])
    y = x2 * lax.rsqrt(jnp.mean(x2 * x2, axis=-1, keepdims=True) + EPS)
    o_ref[0] = y * fnw_ref[...]


def out_call(x, hs, ya, gb, mod, norm2_w, final_norm_w, w_mlstm_out, w_out, w_ff1, w_ff2):
    bsz, t, _ = x.shape
    tile = TOKEN_TILE
    tok = pl.BlockSpec((1, tile, D_MODEL), lambda i, j: (i, j, 0))
    resident = [mod, norm2_w, final_norm_w, w_mlstm_out, w_out, w_ff1, w_ff2]
    return pl.pallas_call(
        _out_body,
        grid=(bsz, t // tile),
        in_specs=[tok, tok, tok, tok] + [_resident(a.shape) for a in resident],
        out_specs=tok,
        out_shape=jax.ShapeDtypeStruct((bsz, t, D_MODEL), F32),
        compiler_params=pltpu.CompilerParams(dimension_semantics=("arbitrary", "arbitrary"),
                                             vmem_limit_bytes=VMEM_LIMIT_BYTES),
        name="merge_out_mlp",
    )(x, hs, ya, gb, *resident)


def _layer(x, ctx, mod, norm1_w, w_in, b_in, conv_w, mlstm_norm_w, w_conv_out, w_mlstm_out,
           w_out, norm2_w, w_ff1, w_ff2, final_norm_w):
    nw1 = norm1_w.reshape(1, D_MODEL)
    w_n, w_t = projection_weights(w_in.T)
    b_n = jnp.concatenate([b_in[_REF_K:_REF_V], b_in[_REF_O:]]).reshape(1, -1)
    b_t = jnp.concatenate([b_in[_REF_V:_REF_IG], b_in[_REF_Q:_REF_O], _group_gates(b_in[_REF_IG:_REF_Q]),
                           jnp.zeros((_T_ROWS - _T_END,), F32)]).reshape(-1, 1)

    c0, m0 = ctx_call(ctx, mod, nw1, w_n, b_n, w_t, b_t)
    w_conv_out, w_mlstm_out, w_out, w_ff1, w_ff2 = cast_weights(w_conv_out, w_mlstm_out, w_out, w_ff1, w_ff2)
    k, qt, vt, gates, so, ya, gb = inproj_call(x, mod, nw1, mlstm_norm_w.reshape(1, D_MLSTM), w_n, b_n, w_t, b_t,
                                               conv_w, w_conv_out)
    hs = mlstm_call(qt, k, vt, so, gates, c0, m0)
    return out_call(x, hs, ya, gb, mod, norm2_w.reshape(1, D_MODEL), final_norm_w.reshape(1, D_MODEL),
                    w_mlstm_out, w_out, w_ff1, w_ff2)


def kernel(x, c, ctx, c_ctx, w_mod, b_mod, norm1_w, w_in, b_in, conv_w, mlstm_norm_w, w_conv_out,
           w_mlstm_out, w_out, norm2_w, w_ff1, w_ff2, final_norm_w):
    depth = w_mod.shape[0]
    assert depth == 1, "the context stream is only advanced through its mLSTM state (single layer)"
    cvecs = jnp.concatenate([c, c_ctx[None, :]], axis=0)
    mod = adaln_call(cvecs, w_mod[0], b_mod[0])
    return _layer(x, ctx, mod, norm1_w[0], w_in[0], b_in[0], conv_w[0],
                  mlstm_norm_w[0], w_conv_out[0], w_mlstm_out[0], w_out[0], norm2_w[0], w_ff1[0],
                  w_ff2[0], final_norm_w)
```

```python
import functools

import jax
import jax.numpy as jnp
from jax import lax
from jax.experimental import pallas as pl
from jax.experimental.pallas import tpu as pltpu

D_MODEL = 1024
CTX_LEN = 256
GRID_W = 64
N_HEADS = 8
DK = 64
DV = 128
D_MLSTM = N_HEADS * DV
D_QK = N_HEADS * DK
N_DIR = 2
N_GATE = 2 * N_DIR * N_HEADS
EPS = 1e-6

CHUNK = 128
HEADS_PER_STEP = 2
assert HEADS_PER_STEP == 2 and HEADS_PER_STEP * DK == 128, "a head pair shares one 128-lane tile"
N_HEAD_GROUPS = N_HEADS // HEADS_PER_STEP
GATES_PER_GROUP = N_GATE // N_HEAD_GROUPS
SUBLANES = 8
BF16_SUBLANES = 16
V_AUG = BF16_SUBLANES + DV
TOKEN_TILE = 512
ADALN_TILE = 1024
CTX_BATCH = 4

VMEM_LIMIT_BYTES = 56 * 1024 * 1024

_REF_K, _REF_V, _REF_IG, _REF_FG, _REF_Q, _REF_O = 0, 512, 1536, 1552, 1568, 2080
D_IN = 8224
_K0, _O0, _XIN0, _GC0, _GB0, _MA0, _MB0, _G0 = 0, 512, 1536, 2560, 3584, 4608, 5632, 6656
_SEG_O, _SEG_MA = 0, 4
_T_V, _T_Q, _T_GATES, _T_END, _T_ROWS = 0, 1024, 1536, 1568, 2048
_SH1, _SC1, _G1, _SH2, _SC2, _G2 = (i * D_MODEL for i in range(6))

F32 = jnp.float32
BF16 = jnp.bfloat16


def _sigmoid(x):
    return 0.5 * jnp.tanh(0.5 * x) + 0.5


def _log_sigmoid(x):
    return jnp.minimum(x, 0.0) - jnp.log(1.0 + jnp.exp(-jnp.abs(x)))


def _split3(x):
    hi = x.astype(BF16)
    r1 = x - hi.astype(F32)
    mid = r1.astype(BF16)
    lo = (r1 - mid.astype(F32)).astype(BF16)
    return hi, mid, lo


def _cumsum_lanes(x):
    n = x.shape[1]
    r = lax.broadcasted_iota(jnp.int32, (n, n), 0)
    c = lax.broadcasted_iota(jnp.int32, (n, n), 1)
    tri_t = jnp.where(r <= c, 1.0, 0.0).astype(BF16)
    hi, mid, lo = _split3(x)
    dot = functools.partial(jnp.dot, preferred_element_type=F32)
    return dot(hi, tri_t) + dot(mid, tri_t) + dot(lo, tri_t)


def _modulated_norm(x, norm_w, shift, scale):
    y = x * lax.rsqrt(jnp.mean(x * x, axis=-1, keepdims=True) + EPS)
    return y * (norm_w * (1.0 + scale)) + shift


def _resident(shape):
    nd = len(shape)
    return pl.BlockSpec(shape, lambda *_: (0,) * nd, pipeline_mode=pl.Buffered(1))


def _adaln_body(c_ref, w_ref, b_ref, o_ref):
    c = c_ref[...]
    s = c * _sigmoid(c)
    dot = functools.partial(jnp.dot, preferred_element_type=F32)
    w = w_ref[...]
    s_hi, w_hi = s.astype(BF16), w.astype(BF16)
    s_lo, w_lo = (s - s_hi.astype(F32)).astype(BF16), (w - w_hi.astype(F32)).astype(BF16)
    o_ref[...] = (dot(s_hi, w_hi) + dot(s_hi, w_lo) + dot(s_lo, w_hi)) + b_ref[...]


def adaln_call(cvecs, w_mod, b_mod):
    n = cvecs.shape[0]
    n_out = w_mod.shape[1]
    tile = ADALN_TILE
    return pl.pallas_call(
        _adaln_body,
        grid=(n_out // tile,),
        in_specs=[pl.BlockSpec((n, D_MODEL), lambda j: (0, 0)),
                  pl.BlockSpec((D_MODEL, tile), lambda j: (0, j)),
                  pl.BlockSpec((1, tile), lambda j: (0, j))],
        out_specs=pl.BlockSpec((n, tile), lambda j: (0, j)),
        out_shape=jax.ShapeDtypeStruct((n, n_out), F32),
        name="adaln",
    )(cvecs, w_mod, b_mod.reshape(1, n_out))


W_BLOCK = 512


N_NORMAL_BLOCKS = 1 + (D_IN - _REF_O) // W_BLOCK
N_T_BLOCKS = _T_ROWS // W_BLOCK


def _weights_body(starts_ref, w_ref, perm_ref, n_ref, t_ref):
    del starts_ref
    i = pl.program_id(0)

    @pl.when(i < N_NORMAL_BLOCKS)
    def _():
        seg_blocks = D_MODEL // W_BLOCK
        sigmoid_fed = (((i >= 1 + _SEG_O * seg_blocks) & (i < 1 + (_SEG_O + 1) * seg_blocks))
                       | (i >= 1 + _SEG_MA * seg_blocks))
        scale = jnp.where(i == 0, DK ** -0.5, jnp.where(sigmoid_fed, 0.5, 1.0))
        n_ref[...] = (jnp.transpose(w_ref[...]) * scale).astype(BF16)

    @pl.when(i >= N_NORMAL_BLOCKS)
    def _():
        t_ref[...] = w_ref[...].astype(BF16)

    @pl.when(i == pl.num_programs(0) - 1)
    def _():
        gates = w_ref[0:N_GATE, :].astype(BF16)
        t_ref[0:N_GATE, :] = jnp.dot(perm_ref[...], gates, preferred_element_type=F32).astype(BF16)


def _group_gates(g):
    lead = g.shape[:-1]
    g = g.reshape(*lead, 2, N_DIR, N_HEAD_GROUPS, HEADS_PER_STEP)
    g = jnp.moveaxis(g, -2, -4)
    return g.reshape(*lead, N_GATE)


def projection_weights(w_in_t):
    blk = W_BLOCK // SUBLANES
    starts = ([_REF_K // SUBLANES] + [_REF_O // SUBLANES + b * blk for b in range(N_NORMAL_BLOCKS - 1)]
              + [_REF_V // SUBLANES, _REF_V // SUBLANES + blk, _REF_Q // SUBLANES, _REF_IG // SUBLANES])
    gate_perm = jnp.eye(N_GATE, dtype=BF16)[_group_gates(jnp.arange(N_GATE))]
    last_n = N_NORMAL_BLOCKS - 1
    return pl.pallas_call(
        _weights_body,
        grid_spec=pltpu.PrefetchScalarGridSpec(
            num_scalar_prefetch=1,
            grid=(N_NORMAL_BLOCKS + N_T_BLOCKS,),
            in_specs=[pl.BlockSpec((pl.Element(W_BLOCK), pl.Element(D_MODEL)),
                                   lambda i, starts_ref: (SUBLANES * starts_ref[i], 0)),
                      pl.BlockSpec((N_GATE, N_GATE), lambda i, starts_ref: (0, 0))],
            out_specs=[pl.BlockSpec((D_MODEL, W_BLOCK), lambda i, starts_ref: (0, jnp.minimum(i, last_n))),
                       pl.BlockSpec((W_BLOCK, D_MODEL),
                                    lambda i, starts_ref: (jnp.maximum(i - N_NORMAL_BLOCKS, 0), 0))]),
        out_shape=[jax.ShapeDtypeStruct((D_MODEL, N_NORMAL_BLOCKS * W_BLOCK), BF16),
                   jax.ShapeDtypeStruct((_T_ROWS, D_MODEL), BF16)],
        compiler_params=pltpu.CompilerParams(dimension_semantics=("arbitrary",)),
        name="w_in_prep",
    )(jnp.asarray(starts, jnp.int32), w_in_t, gate_perm)


def _ctx_body(ctx_ref, mod_ref, nw_ref, wk_ref, bk_ref, wv_ref, bv_ref, wg_ref, bg_ref, c_ref, m_ref):
    hp = HEADS_PER_STEP
    nb = ctx_ref.shape[0]
    last = mod_ref.shape[0] - 1
    x = ctx_ref[...].reshape(nb * CTX_LEN, D_MODEL)
    hb = _modulated_norm(x, nw_ref[...], mod_ref[last:, _SH1:_SC1], mod_ref[last:, _SC1:_G1]).astype(BF16)
    nt = functools.partial(lax.dot_general, dimension_numbers=(((1,), (1,)), ((), ())),
                           preferred_element_type=F32)
    k_all = (jnp.dot(hb, wk_ref[...], preferred_element_type=F32) + bk_ref[...]).astype(BF16)
    vt_all = (nt(wv_ref[...], hb) + bv_ref[...]).astype(BF16)
    g_all = nt(wg_ref[...], hb) + bg_ref[...]

    r_in_tile = jnp.bitwise_and(lax.broadcasted_iota(jnp.int32, (N_GATE, 1), 0), GATES_PER_GROUP - 1)
    fwd = r_in_tile < GATES_PER_GROUP // 2 + hp
    ones_rows = jnp.where(lax.broadcasted_iota(jnp.int32, (V_AUG - DV, CTX_LEN), 0) == 0, 1.0, 0.0)
    lane = lax.broadcasted_iota(jnp.int32, (CTX_LEN, hp * DK), 1)
    for bi in range(nb):
        tokens = slice(bi * CTX_LEN, (bi + 1) * CTX_LEN)
        g, vt, k = g_all[:, tokens], vt_all[:, tokens], k_all[tokens]
        lf = _log_sigmoid(g)
        b = _cumsum_lanes(lf)
        ig = pltpu.roll(g, GATES_PER_GROUP // 2, axis=0)
        tot = b[:, CTX_LEN - 1:CTX_LEN]
        g_end = ig + jnp.where(fwd, tot - b, b - lf)
        m = jnp.maximum(tot, jnp.max(g_end, axis=1, keepdims=True))
        m_ref[bi] = m
        wk = jnp.exp(g_end - m)
        for grp in range(N_HEAD_GROUPS):
            k2 = k[:, grp * hp * DK:(grp + 1) * hp * DK]
            zero = jnp.zeros_like(k2)
            k_bd = jnp.concatenate([jnp.where(lane < DK, k2, zero), jnp.where(lane >= DK, k2, zero)], axis=0)
            vf = [jnp.concatenate([ones_rows, vt[(grp * hp + j) * DV:(grp * hp + j + 1) * DV].astype(F32)],
                                  axis=0) for j in range(hp)]
            for d in range(N_DIR):
                row0 = grp * GATES_PER_GROUP + GATES_PER_GROUP // 2 + d * hp
                lhs = jnp.concatenate([(vf[j] * wk[row0 + j:row0 + j + 1]).astype(BF16) for j in range(hp)],
                                      axis=1)
                c_ref[bi, d, grp] = jnp.dot(lhs, k_bd, preferred_element_type=F32)


def ctx_call(ctx, mod, norm_w, w_n, b_n, w_t, b_t):
    bsz = ctx.shape[0]
    nb = CTX_BATCH
    const = lambda shape: pl.BlockSpec(shape, lambda i: (0,) * len(shape))
    gate_rows = lambda width: pl.BlockSpec((N_GATE, width), lambda i: (_T_GATES // N_GATE, 0))
    return pl.pallas_call(
        _ctx_body,
        grid=(bsz // nb,),
        in_specs=[pl.BlockSpec((nb, CTX_LEN, D_MODEL), lambda i: (i, 0, 0)),
                  const(mod.shape), const((1, D_MODEL)),
                  const((D_MODEL, D_QK)), const((1, D_QK)),
                  const((D_MLSTM, D_MODEL)), const((D_MLSTM, 1)),
                  gate_rows(D_MODEL), gate_rows(1)],
        out_specs=[pl.BlockSpec((nb, N_DIR, N_HEAD_GROUPS, V_AUG, HEADS_PER_STEP * DK),
                                lambda i: (i, 0, 0, 0, 0)),
                   pl.BlockSpec((nb, N_GATE, 1), lambda i: (i, 0, 0))],
        out_shape=[jax.ShapeDtypeStruct((bsz, N_DIR, N_HEAD_GROUPS, V_AUG, HEADS_PER_STEP * DK), F32),
                   jax.ShapeDtypeStruct((bsz, N_GATE, 1), F32)],
        compiler_params=pltpu.CompilerParams(dimension_semantics=("arbitrary",),
                                             vmem_limit_bytes=VMEM_LIMIT_BYTES),
        name="ctx_state",
    )(ctx, mod, norm_w, w_n, b_n, w_t, b_t, w_t, b_t)


def _inproj_body(x_ref, mod_ref, nw_ref, nwm_ref, w_ref, b_ref, wt_ref, bt_ref, cw_ref, wco_ref,
                 k_ref, qt_ref, vt_ref, gt_ref, so_ref, ya_ref, gb_ref):
    x = x_ref[0]
    mod = mod_ref[pl.ds(pl.program_id(0), 1), :]
    hb = _modulated_norm(x, nw_ref[...], mod[:, _SH1:_SC1], mod[:, _SC1:_G1]).astype(BF16)
    tile = hb.shape[0]

    def seg(a, b):
        return jnp.dot(hb, w_ref[:, a:b], preferred_element_type=F32) + b_ref[:, a:b]

    k_ref[0] = seg(_K0, _O0).astype(BF16)
    half_sig = lambda z: 0.5 * jnp.tanh(z) + 0.5
    so_ref[0] = (half_sig(seg(_O0, _XIN0)) * nwm_ref[...]).astype(BF16)

    zt = lax.dot_general(wt_ref[:_T_END, :], hb, (((1,), (1,)), ((), ())),
                         preferred_element_type=F32) + bt_ref[:_T_END, :]
    for i in range(tile // CHUNK):
        lanes = slice(i * CHUNK, (i + 1) * CHUNK)
        vt_ref[0, i] = zt[_T_V:_T_Q, lanes].astype(BF16)
        qt_ref[0, i] = zt[_T_Q:_T_GATES, lanes].astype(BF16)
        gt_ref[0, i] = zt[_T_GATES:_T_END, lanes]

    u = seg(_XIN0, _GC0) * seg(_GC0, _GB0)
    col = jnp.bitwise_and(lax.broadcasted_iota(jnp.int32, (tile, 1), 0), GRID_W - 1)
    u_prev = jnp.where(col != 0, pltpu.roll(u, 1, axis=0), 0.0)
    u_next = jnp.where(col != GRID_W - 1, pltpu.roll(u, tile - 1, axis=0), 0.0)
    a = cw_ref[0:1, :] * u_prev + cw_ref[1:2, :] * u + cw_ref[2:3, :] * u_next
    ya = jnp.dot((seg(_GB0, _MA0) * a).astype(BF16), wco_ref[...], preferred_element_type=F32)
    ya_ref[0] = (half_sig(seg(_MA0, _MB0)) * ya).astype(BF16)
    gb_ref[0] = half_sig(seg(_MB0, _G0)).astype(BF16)


def inproj_call(x, mod, norm_w, mlstm_norm_w, w_n, b_n, w_t, b_t, conv_w, w_conv_out):
    bsz, t, _ = x.shape
    tile = TOKEN_TILE
    tok = lambda width: pl.BlockSpec((1, tile, width), lambda i, j: (i, j, 0))
    tok_t = lambda rows: pl.BlockSpec((1, tile // CHUNK, rows, CHUNK), lambda i, j: (i, j, 0, 0))
    seq = lambda width: jax.ShapeDtypeStruct((bsz, t, width), BF16)
    seq_t = lambda rows, dt: jax.ShapeDtypeStruct((bsz, t // CHUNK, rows, CHUNK), dt)
    resident = [mod, norm_w, mlstm_norm_w, w_n, b_n, w_t, b_t, conv_w, w_conv_out]
    return pl.pallas_call(
        _inproj_body,
        grid=(bsz, t // tile),
        in_specs=[tok(D_MODEL)] + [_resident(a.shape) for a in resident],
        out_specs=[tok(D_QK), tok_t(D_QK), tok_t(D_MLSTM), tok_t(N_GATE), tok(D_MLSTM),
                   tok(D_MODEL), tok(D_MODEL)],
        out_shape=[seq(D_QK), seq_t(D_QK, BF16), seq_t(D_MLSTM, BF16), seq_t(N_GATE, F32), seq(D_MLSTM),
                   seq(D_MODEL), seq(D_MODEL)],
        compiler_params=pltpu.CompilerParams(dimension_semantics=("arbitrary", "arbitrary"),
                                             vmem_limit_bytes=VMEM_LIMIT_BYTES),
        name="inproj_conv",
    )(x, *resident)


SERIES_ROW0 = GATES_PER_GROUP // 2
N_SERIES = N_DIR * HEADS_PER_STEP
N_SPLIT = 3
LOG2E = 1.4426950408889634
_WK, _DECAY, _WINTER, _EXPNEG, _KEYS = 0, 1, 2, 3, 4
_COLM = _KEYS + N_SPLIT
N_TABLES = _COLM + N_SPLIT * N_SERIES


def _running_max_rows(x, reverse):
    n = x.shape[0]
    row = lax.broadcasted_iota(jnp.int32, x.shape, 0)
    shift = 1
    while shift < n:
        if reverse:
            moved = jnp.where(row < n - shift, pltpu.roll(x, n - shift, axis=0), -jnp.inf)
        else:
            moved = jnp.where(row >= shift, pltpu.roll(x, shift, axis=0), -jnp.inf)
        x = jnp.maximum(x, moved)
        shift *= 2
    return x


def _gate_tables(g_ref, m0_ref, tab_ref, n_chunks):
    step_rows = N_HEAD_GROUPS * GATES_PER_GROUP
    rows = n_chunks * step_rows
    g = g_ref[0].reshape(rows, CHUNK)
    lf = _log_sigmoid(g)
    b = _cumsum_lanes(lf)
    ig = pltpu.roll(g, SERIES_ROW0, axis=0)
    in_tile = lambda idx: jnp.bitwise_and(idx, GATES_PER_GROUP - 1)
    r_in_tile = in_tile(lax.broadcasted_iota(jnp.int32, (rows, 1), 0))
    fwd = r_in_tile < SERIES_ROW0 + HEADS_PER_STEP
    e = b - lf
    tot = b[:, CHUNK - 1:CHUNK]
    col = jnp.where(fwd, b, -e)
    key = ig + jnp.where(fwd, -b, e)
    g_end = jnp.where(fwd, tot + key, key)
    g_max = jnp.max(g_end, axis=1, keepdims=True)

    fwd_step = fwd[:step_rows]
    tile = lambda a, c: a[c * step_rows:(c + 1) * step_rows]
    m = m0_ref[0]
    m_in_steps, m_out_steps = [], []
    for i in range(n_chunks):
        back = n_chunks - 1 - i
        m_in_steps.append(m)
        m = jnp.maximum(jnp.where(fwd_step, tile(tot, i), tile(tot, back)) + m,
                        jnp.where(fwd_step, tile(g_max, i), tile(g_max, back)))
        m_out_steps.append(m)
    by_chunk = lambda steps: jnp.concatenate(
        [jnp.where(fwd_step, steps[c], steps[n_chunks - 1 - c]) for c in range(n_chunks)], axis=0)
    m_in, m_out = by_chunk(m_in_steps), by_chunk(m_out_steps)

    decay = jnp.broadcast_to(jnp.exp(tot + m_in - m_out), (rows, CHUNK))
    lane = lax.broadcasted_iota(jnp.int32, (rows, CHUNK), 1)
    inter = jnp.where(fwd, col, tot + col) + m_in
    key_t = jnp.transpose(key)
    fwd_lane = in_tile(lax.broadcasted_iota(jnp.int32, (1, rows), 1)) < SERIES_ROW0 + HEADS_PER_STEP
    key_max = jnp.transpose(jnp.where(fwd_lane, _running_max_rows(key_t, reverse=False),
                                      _running_max_rows(key_t, reverse=True)))
    m_t = jnp.maximum(col + key_max, inter)
    tables = {_WK: jnp.exp(g_end - m_out),
              _DECAY: jnp.where(lane < DK, decay, pltpu.roll(decay, rows - 1, axis=0)),
              _WINTER: jnp.exp(inter - m_t), _EXPNEG: jnp.exp(-m_t)}
    for x, part in enumerate(_split3(key * LOG2E)):
        tables[_KEYS + x] = part.astype(F32)
    for x, part in enumerate(_split3((col - m_t) * LOG2E)):
        for sidx in range(N_SERIES):
            tables[_COLM + x * N_SERIES + sidx] = jnp.where(r_in_tile == SERIES_ROW0 + sidx,
                                                            part.astype(F32), 0.0)
    for idx, a in tables.items():
        tab_ref[idx] = a.reshape(n_chunks, N_HEAD_GROUPS, GATES_PER_GROUP, CHUNK)


def _mlstm_body(qt_ref, k_ref, vt_ref, so_ref, g_ref, c0_ref, m0_ref, o_ref,
                tab_ref, u_ref, s_ref, st_ref, causal_ref, *, n_chunks):
    hp = HEADS_PER_STEP
    ones_rows = jnp.where(lax.broadcasted_iota(jnp.int32, (V_AUG - DV, CHUNK), 0) == 0,
                          1.0, 0.0).astype(BF16)
    low_lanes = lax.broadcasted_iota(jnp.int32, (1, hp * DK), 1) < DK
    series = lambda tab8, d, j: tab8[SERIES_ROW0 + d * hp + j:SERIES_ROW0 + d * hp + j + 1]
    split_rows = N_SPLIT * GATES_PER_GROUP
    ones_split = jnp.ones((split_rows, CHUNK), F32)
    pick_r = jnp.bitwise_and(lax.broadcasted_iota(jnp.int32, (split_rows, N_DIR * CHUNK), 0),
                             GATES_PER_GROUP - 1)
    pick_d = jnp.where(lax.broadcasted_iota(jnp.int32, (split_rows, N_DIR * CHUNK), 1) >= CHUNK, 1, 0)
    pick_series = [jnp.where(pick_r == SERIES_ROW0 + pick_d * hp + j, 1.0, 0.0) for j in range(hp)]

    def chunk_rows(c):
        return slice(c * CHUNK, (c + 1) * CHUNK)

    def for_each_chunk(body, carry):
        for c in range(n_chunks):
            carry = body(c, carry)
        return carry

    def values_t(c, j):
        return jnp.concatenate([ones_rows, vt_ref[0, c, j * DV:(j + 1) * DV, :]], axis=0)

    grp = pl.program_id(1)

    @pl.when(grp == 0)
    def _():
        _gate_tables(g_ref, m0_ref, tab_ref, n_chunks)
        s_i = lax.broadcasted_iota(jnp.int32, (CHUNK, CHUNK), 0)
        t_i = lax.broadcasted_iota(jnp.int32, (CHUNK, CHUNK), 1)
        causal_ref[0] = jnp.where(s_i <= t_i, 0.0, -jnp.inf)
        causal_ref[1] = jnp.where(s_i >= t_i, 0.0, -jnp.inf)

    def block_diag_q(qt2):
        zero = jnp.zeros((DK, CHUNK), qt2.dtype)
        return jnp.concatenate([jnp.concatenate([qt2[:DK], zero], axis=0),
                                jnp.concatenate([zero, qt2[DK:]], axis=0)], axis=1)

    def chunk_step(c, carry):
        k2 = k_ref[0, chunk_rows(c), :]
        zero = jnp.zeros_like(k2)
        k_bd = jnp.concatenate([jnp.where(low_lanes, k2, zero),
                                jnp.where(low_lanes, zero, k2)], axis=0)
        wk8 = tab_ref[_WK, c, grp]
        vb = [values_t(c, j) for j in range(hp)]

        def key_weights(d, j):
            tile = jnp.broadcast_to(series(wk8, d, j), (BF16_SUBLANES, CHUNK)).astype(BF16)
            return jnp.concatenate([tile] * (V_AUG // BF16_SUBLANES), axis=0)

        lhs = jnp.concatenate(
            [jnp.concatenate([vb[j] * key_weights(d, j) for j in range(hp)], axis=1)
             for d in range(N_DIR)], axis=0)
        u_ref[c] = jnp.dot(lhs, k_bd, preferred_element_type=F32)

        qk_t = jnp.dot(k2, block_diag_q(qt_ref[0, c]), preferred_element_type=F32)
        key_side = jnp.concatenate([tab_ref[_KEYS + x, c, grp] for x in range(N_SPLIT)] + [ones_split],
                                   axis=0).astype(BF16)
        for j in range(hp):
            query_side = jnp.concatenate(
                [pick_series[j]] +
                [jnp.concatenate([tab_ref[_COLM + x * N_SERIES + d * hp + j, c, grp] for d in range(N_DIR)],
                                 axis=1)
                 for x in range(N_SPLIT)], axis=0).astype(BF16)
            log_d = lax.dot_general(key_side, query_side, (((0,), (0,)), ((), ())),
                                    preferred_element_type=F32)
            for d in range(N_DIR):
                decay = jnp.exp2(log_d[:, d * CHUNK:(d + 1) * CHUNK] + causal_ref[d])
                st_ref[c, d, j] = (qk_t[:, j * CHUNK:(j + 1) * CHUNK] * decay).astype(BF16)
        return carry

    for_each_chunk(chunk_step, 0)

    def scan_step(i, carry):
        cf, cb = carry
        back = n_chunks - 1 - i
        s_ref[i, 0, 0] = cf.astype(BF16)
        s_ref[i, 0, 1] = pltpu.roll(cf, DK, axis=1).astype(BF16)
        s_ref[back, 1, 0] = cb.astype(BF16)
        s_ref[back, 1, 1] = pltpu.roll(cb, DK, axis=1).astype(BF16)
        cf = tab_ref[_DECAY, i, grp][SERIES_ROW0:SERIES_ROW0 + 1] * cf + u_ref[i, :V_AUG]
        cb = tab_ref[_DECAY, back, grp][SERIES_ROW0 + hp:SERIES_ROW0 + hp + 1] * cb + u_ref[back, V_AUG:]
        return cf, cb

    for_each_chunk(scan_step, (c0_ref[0, 0, 0], c0_ref[0, 1, 0]))

    def output_step(c, carry):
        rows = chunk_rows(c)
        qt2 = qt_ref[0, c].astype(F32)
        w_inter8, exp_neg8 = tab_ref[_WINTER, c, grp], tab_ref[_EXPNEG, c, grp]
        no_query = jnp.zeros((DK, CHUNK), BF16)
        for j in range(hp):
            state = jnp.where(low_lanes, s_ref[c, 0, j], s_ref[c, 1, 1 - j])
            q_h = qt2[j * DK:(j + 1) * DK]
            qw = [(q_h * series(w_inter8, d, j)).astype(BF16) for d in range(N_DIR)]
            rhs = jnp.concatenate(
                [jnp.concatenate([st_ref[c, d, j] for d in range(N_DIR)], axis=1),
                 jnp.concatenate([qw[0], no_query], axis=1),
                 jnp.concatenate([no_query, qw[1]], axis=1)], axis=0)
            lhs = jnp.concatenate([values_t(c, j), state], axis=1)
            n_all = jnp.dot(lhs, rhs, preferred_element_type=F32)
            h_t = None
            for d in range(N_DIR):
                num = n_all[:, d * CHUNK:(d + 1) * CHUNK]
                r = 1.0 / jnp.maximum(jnp.abs(num[:1]), series(exp_neg8, d, j))
                part = num[V_AUG - DV:] * r
                h_t = part if h_t is None else h_t + part
            hn_t = h_t * lax.rsqrt(jnp.mean(h_t * h_t, axis=0, keepdims=True) + EPS)
            o_ref[0, rows, j * DV:(j + 1) * DV] = (
                jnp.transpose(hn_t) * so_ref[0, rows, j * DV:(j + 1) * DV].astype(F32)).astype(BF16)
        return carry

    for_each_chunk(output_step, 0)


def mlstm_call(qt, k, vt, so, gates, c0, m0):
    bsz, t, _ = k.shape
    n_chunks = t // CHUNK
    hp = HEADS_PER_STEP
    seq = lambda width: pl.BlockSpec((1, t, width), lambda i, j: (i, 0, j))
    seq_t = lambda rows: pl.BlockSpec((1, n_chunks, rows, CHUNK), lambda i, j: (i, 0, j, 0))
    return pl.pallas_call(
        functools.partial(_mlstm_body, n_chunks=n_chunks),
        grid=(bsz, N_HEAD_GROUPS),
        in_specs=[seq_t(hp * DK), seq(hp * DK), seq_t(hp * DV), seq(hp * DV),
                  pl.BlockSpec((1, n_chunks, N_GATE, CHUNK), lambda i, j: (i, 0, 0, 0)),
                  pl.BlockSpec((1, N_DIR, 1, V_AUG, hp * DK), lambda i, j: (i, 0, j, 0, 0)),
                  pl.BlockSpec((1, N_GATE, 1), lambda i, j: (i, 0, 0))],
        out_specs=seq(hp * DV),
        out_shape=jax.ShapeDtypeStruct((bsz, t, D_MLSTM), BF16),
        scratch_shapes=[pltpu.VMEM((N_TABLES, n_chunks, N_HEAD_GROUPS, GATES_PER_GROUP, CHUNK), F32),
                        pltpu.VMEM((n_chunks, N_DIR * V_AUG, hp * DK), F32),
                        pltpu.VMEM((n_chunks, N_DIR, 2, V_AUG, hp * DK), BF16),
                        pltpu.VMEM((n_chunks, N_DIR, hp, CHUNK, CHUNK), BF16),
                        pltpu.VMEM((N_DIR, CHUNK, CHUNK), F32)],
        compiler_params=pltpu.CompilerParams(dimension_semantics=("arbitrary", "arbitrary"),
                                             vmem_limit_bytes=VMEM_LIMIT_BYTES),
        name="mlstm",
    )(qt, k, vt, so, gates, c0, m0)


def _out_body(x_ref, hs_ref, ya_ref, gb_ref, mod_ref, nw2_ref, fnw_ref, wmo_ref, wo_ref, w1_ref, w2_ref, o_ref):
    dot = functools.partial(jnp.dot, preferred_element_type=F32)
    mod = mod_ref[pl.ds(pl.program_id(0), 1), :]
    yb = dot(hs_ref[0], wmo_ref[...])
    merged = ya_ref[0].astype(F32) + gb_ref[0].astype(F32) * yb
    x1 = x_ref[0] + mod[:, _G1:_SH2] * dot(merged.astype(BF16), wo_ref[...])
    hm = _modulated_norm(x1, nw2_ref[...], mod[:, _SH2:_SC2], mod[:, _SC2:_G2]).astype(BF16)
    a = jnp.maximum(dot(hm, w1_ref[...]), 0.0)
    x2 = x1 + mod[:, _G2:] * dot((a * a).astype(BF16), w2_ref[...])
    y = x2 * lax.rsqrt(jnp.mean(x2 * x2, axis=-1, keepdims=True) + EPS)
    o_ref[0] = y * fnw_ref[...]


def out_call(x, hs, ya, gb, mod, norm2_w, final_norm_w, w_mlstm_out, w_out, w_ff1, w_ff2):
    bsz, t, _ = x.shape
    tile = TOKEN_TILE
    tok = pl.BlockSpec((1, tile, D_MODEL), lambda i, j: (i, j, 0))
    resident = [mod, norm2_w, final_norm_w, w_mlstm_out, w_out, w_ff1, w_ff2]
    return pl.pallas_call(
        _out_body,
        grid=(bsz, t // tile),
        in_specs=[tok, tok, tok, tok] + [_resident(a.shape) for a in resident],
        out_specs=tok,
        out_shape=jax.ShapeDtypeStruct((bsz, t, D_MODEL), F32),
        compiler_params=pltpu.CompilerParams(dimension_semantics=("arbitrary", "arbitrary"),
                                             vmem_limit_bytes=VMEM_LIMIT_BYTES),
        name="merge_out_mlp",
    )(x, hs, ya, gb, *resident)


def _layer(x, ctx, mod, norm1_w, w_in, b_in, conv_w, mlstm_norm_w, w_conv_out, w_mlstm_out,
           w_out, norm2_w, w_ff1, w_ff2, final_norm_w):
    nw1 = norm1_w.reshape(1, D_MODEL)
    w_n, w_t = projection_weights(w_in.T)
    b_rest = b_in[_REF_O:].reshape(-1, D_MODEL)
    b_rest = b_rest * jnp.where((jnp.arange(b_rest.shape[0]) == _SEG_O) | (jnp.arange(b_rest.shape[0]) >= _SEG_MA),
                                0.5, 1.0)[:, None]
    b_n = jnp.concatenate([b_in[_REF_K:_REF_V] * (DK ** -0.5), b_rest.reshape(-1)]).reshape(1, -1)
    b_t = jnp.concatenate([b_in[_REF_V:_REF_IG], b_in[_REF_Q:_REF_O], _group_gates(b_in[_REF_IG:_REF_Q]),
                           jnp.zeros((_T_ROWS - _T_END,), F32)]).reshape(-1, 1)

    c0, m0 = ctx_call(ctx, mod, nw1, w_n, b_n, w_t, b_t)
    k, qt, vt, gates, so, ya, gb = inproj_call(x, mod, nw1, mlstm_norm_w.reshape(1, D_MLSTM), w_n, b_n, w_t, b_t,
                                               conv_w, w_conv_out.astype(BF16))
    hs = mlstm_call(qt, k, vt, so, gates, c0, m0)
    return out_call(x, hs, ya, gb, mod, norm2_w.reshape(1, D_MODEL), final_norm_w.reshape(1, D_MODEL),
                    w_mlstm_out.astype(BF16), w_out.astype(BF16), w_ff1.astype(BF16), w_ff2.astype(BF16))


def kernel(x, c, ctx, c_ctx, w_mod, b_mod, norm1_w, w_in, b_in, conv_w, mlstm_norm_w, w_conv_out,
           w_mlstm_out, w_out, norm2_w, w_ff1, w_ff2, final_norm_w):
    depth = w_mod.shape[0]
    assert depth == 1, "the context stream is only advanced through its mLSTM state (single layer)"
    cvecs = jnp.concatenate([c, c_ctx[None, :]], axis=0)
    mod = adaln_call(cvecs, w_mod[0], b_mod[0])
    return _layer(x, ctx, mod, norm1_w[0], w_in[0], b_in[0], conv_w[0],
                  mlstm_norm_w[0], w_conv_out[0], w_mlstm_out[0], w_out[0], norm2_w[0], w_ff1[0],
                  w_ff2[0], final_norm_w)
```
